```python
import math
import jax, jax.numpy as jnp
from jax import lax
import numpy as np

D_MODEL = 4096
BATCH = 4
SEQ = 4096
DEPTH = 1

SSM_WIDTH = D_MODEL // 2
SSM_GROUP = 16
SSM_GROUPS = SSM_WIDTH // SSM_GROUP
SSM_STATE = 64
DT_MIN = 1e-3
DT_MAX = 1e-1
N_HEADS = 16
HEAD_DIM = 128
N_KV = 4
HPG = N_HEADS // N_KV
ATT_WIDTH = N_HEADS * HEAD_DIM
KV_WIDTH = N_KV * HEAD_DIM
L_CMP = 32
STRIDE_CMP = 16
L_SEL = 64
N_SEL = 16
WINDOW = 512
Q_BLOCK = 128
SEL_Q_BLOCK = 64
D_FF = ((8 * D_MODEL // 3 + 255) // 256) * 256
RMS_EPS = 1e-6
NEG_INF = -1e30
FORCE_SCORE = 1e9
IN_SPLITS = (SSM_WIDTH, ATT_WIDTH, KV_WIDTH, KV_WIDTH, KV_WIDTH, KV_WIDTH, KV_WIDTH, KV_WIDTH, 3 * N_HEADS, D_MODEL, D_MODEL)
IN_WIDTH = SSM_WIDTH + ATT_WIDTH + 6 * KV_WIDTH + 3 * N_HEADS + 2 * D_MODEL

kernel_name = "hybrid_s5_nsa_gated_block"


def rms_norm(x, gain):
    xf = x.astype(jnp.float32)
    y = xf * lax.rsqrt(jnp.mean(xf * xf, axis=-1, keepdims=True) + RMS_EPS)
    return (y * gain.astype(jnp.float32)).astype(x.dtype)


def _ssm_combine(e1, e2):
    a1r, a1i, b1r, b1i = e1
    a2r, a2i, b2r, b2i = e2
    return (a1r * a2r - a1i * a2i,
            a1r * a2i + a1i * a2r,
            a2r * b1r - a2i * b1i + b2r,
            a2r * b1i + a2i * b1r + b2i)


def s5_mixer(u, a_re, a_im, log_dt, b_re, b_im, c_re, c_im, d_skip, w_glu, b_glu):
    f32 = jnp.float32
    bsz, seq, _ = u.shape
    uf = u.astype(f32).reshape(bsz, seq, SSM_GROUPS, SSM_GROUP)
    ar, ai = a_re.astype(f32), a_im.astype(f32)
    dt = jnp.exp(log_dt.astype(f32))[:, None]
    decay = jnp.exp(dt * ar)
    abar_r, abar_i = decay * jnp.cos(dt * ai), decay * jnp.sin(dt * ai)
    den = ar * ar + ai * ai
    zr = ((abar_r - 1.0) * ar + abar_i * ai) / den
    zi = (abar_i * ar - (abar_r - 1.0) * ai) / den
    br, bi = b_re.astype(f32), b_im.astype(f32)
    bbar_r = zr[..., None] * br - zi[..., None] * bi
    bbar_i = zr[..., None] * bi + zi[..., None] * br
    bu_r = jnp.einsum('gpc,bsgc->bsgp', bbar_r, uf)
    bu_i = jnp.einsum('gpc,bsgc->bsgp', bbar_i, uf)
    a_shape = (1, seq, SSM_GROUPS, SSM_STATE)
    elems = (jnp.broadcast_to(abar_r, a_shape), jnp.broadcast_to(abar_i, a_shape), bu_r, bu_i)
    _, _, h_r, h_i = lax.associative_scan(_ssm_combine, elems, axis=1)
    y = (jnp.einsum('gcp,bsgp->bsgc', c_re.astype(f32), h_r)
         - jnp.einsum('gcp,bsgp->bsgc', c_im.astype(f32), h_i)
         + d_skip.astype(f32).reshape(SSM_GROUPS, SSM_GROUP) * uf)
    y = jax.nn.gelu(y.reshape(bsz, seq, SSM_WIDTH))
    y = y * jax.nn.sigmoid(y @ w_glu.astype(f32) + b_glu.astype(f32))
    return y.astype(u.dtype)


def _compress(k, pe, w1, w2):
    bsz, seq = k.shape[:2]
    n_cmp = (seq - L_CMP) // STRIDE_CMP + 1
    idx = np.arange(n_cmp)[:, None] * STRIDE_CMP + np.arange(L_CMP)[None, :]
    blk = k[:, idx] + pe[:, None, :]
    blk = jnp.moveaxis(blk, 3, 2).reshape(bsz, n_cmp, N_KV, L_CMP * HEAD_DIM)
    return jax.nn.gelu(blk @ w1) @ w2


def nsa_mixer(q, k_c, v_c, k_s, v_s, k_w, v_w, gate_logits, pe_k, w1_k, w2_k, pe_v, w1_v, w2_v):
    f32 = jnp.float32
    bsz, seq = q.shape[:2]
    q = q.reshape(bsz, seq, N_KV, HPG, HEAD_DIM) * (HEAD_DIM ** -0.5)
    k_c, v_c, k_s, v_s, k_w, v_w = [a.reshape(bsz, seq, N_KV, HEAD_DIM) for a in (k_c, v_c, k_s, v_s, k_w, v_w)]
    t = jnp.arange(seq)

    n_cmp = (seq - L_CMP) // STRIDE_CMP + 1
    kc = _compress(k_c, pe_k, w1_k, w2_k)
    vc = _compress(v_c, pe_v, w1_v, w2_v)
    blk_end = jnp.arange(n_cmp) * STRIDE_CMP + (L_CMP - 1)
    cmp_ok = blk_end[None, :] <= t[:, None]
    s = jnp.einsum('bsghd,bngd->bghsn', q, kc).astype(f32)
    p_cmp = jax.nn.softmax(jnp.where(cmp_ok, s, NEG_INF), axis=-1) * cmp_ok
    o_cmp = jnp.einsum('bghsn,bngd->bsghd', p_cmp.astype(q.dtype), vc)

    n_blk = seq // L_SEL
    n_top = min(N_SEL, n_blk)
    ci = np.arange(n_cmp)[:, None]
    sj = np.arange(n_blk)[None, :]
    overlap = ((ci * STRIDE_CMP < (sj + 1) * L_SEL) & (ci * STRIDE_CMP + L_CMP > sj * L_SEL)).astype(np.float32)
    imp = jnp.einsum('bghsn,nj->bgsj', p_cmp, jnp.asarray(overlap))
    blk = jnp.arange(n_blk)[None, :]
    cur = (t // L_SEL)[:, None]
    allowed = blk * L_SEL <= t[:, None]
    forced = (blk == 0) | (blk == cur) | (blk == cur - 1)
    score = jnp.where(forced, FORCE_SCORE, jnp.where(allowed, imp, NEG_INF))
    _, sel_idx = lax.top_k(score, n_top)

    k_blk = k_s.reshape(bsz, n_blk, L_SEL, N_KV, HEAD_DIM).transpose(0, 3, 1, 2, 4)
    v_blk = v_s.reshape(bsz, n_blk, L_SEL, N_KV, HEAD_DIM).transpose(0, 3, 1, 2, 4)
    n_qc = seq // SEL_Q_BLOCK
    q_chunks = jnp.moveaxis(q.reshape(bsz, n_qc, SEL_Q_BLOCK, N_KV, HPG, HEAD_DIM), 1, 0)
    idx_chunks = jnp.moveaxis(sel_idx.reshape(bsz, N_KV, n_qc, SEL_Q_BLOCK, n_top), 2, 0)
    t_chunks = t.reshape(n_qc, SEL_Q_BLOCK)
    b_ix = jnp.arange(bsz)[:, None, None, None]
    g_ix = jnp.arange(N_KV)[None, :, None, None]
    offs = jnp.arange(L_SEL)

    def sel_block(args):
        qc, ic, tc = args
        kg = k_blk[b_ix, g_ix, ic]
        vg = v_blk[b_ix, g_ix, ic]
        pos = ic[..., None] * L_SEL + offs
        ok = (pos <= tc[None, None, :, None, None])[:, :, None]
        sc = jnp.einsum('bqghd,bgqkld->bghqkl', qc, kg).astype(f32)
        p = jax.nn.softmax(jnp.where(ok, sc, NEG_INF), axis=(-2, -1))
        return jnp.einsum('bghqkl,bgqkld->bqghd', p.astype(qc.dtype), vg)

    o_sel = lax.map(sel_block, (q_chunks, idx_chunks, t_chunks))
    o_sel = jnp.moveaxis(o_sel, 0, 1).reshape(bsz, seq, N_KV, HPG, HEAD_DIM)

    nb = seq // Q_BLOCK
    nw = WINDOW // Q_BLOCK

    def band(a):
        ab = a.reshape(bsz, nb, Q_BLOCK, N_KV, HEAD_DIM)
        ap = jnp.pad(ab, ((0, 0), (nw, 0), (0, 0), (0, 0), (0, 0)))
        return jnp.concatenate([ap[:, j:j + nb] for j in range(nw + 1)], axis=2)

    kwb, vwb = band(k_w), band(v_w)
    qpos = t.reshape(nb, Q_BLOCK)
    kpos = (jnp.arange(nb)[:, None] - nw) * Q_BLOCK + jnp.arange((nw + 1) * Q_BLOCK)[None, :]
    diff = qpos[:, :, None] - kpos[:, None, :]
    win_ok = (diff >= 0) & (diff < WINDOW) & (kpos[:, None, :] >= 0)
    qb = q.reshape(bsz, nb, Q_BLOCK, N_KV, HPG, HEAD_DIM)
    sw = jnp.einsum('bnqghd,bnkgd->bghnqk', qb, kwb).astype(f32)
    pw = jax.nn.softmax(jnp.where(win_ok, sw, NEG_INF), axis=-1)
    o_win = jnp.einsum('bghnqk,bnkgd->bnqghd', pw.astype(q.dtype), vwb).reshape(bsz, seq, N_KV, HPG, HEAD_DIM)

    g = jax.nn.sigmoid(gate_logits.astype(f32)).reshape(bsz, seq, 3, N_KV, HPG, 1).astype(q.dtype)
    o = g[:, :, 0] * o_cmp + g[:, :, 1] * o_sel + g[:, :, 2] * o_win
    return o.reshape(bsz, seq, ATT_WIDTH)


def setup_inputs(seed: int = 0) -> dict:
    key = jax.random.key(seed)
    ks = jax.random.split(key, 32)
    f32 = jnp.float32
    L = DEPTH

    def nrm(k, shape, scale):
        return jax.random.normal(k, shape, f32) * scale

    def gain(k, width):
        return 1.0 + 0.05 * jax.random.normal(k, (L, width), f32)

    n_idx = jnp.arange(SSM_STATE, dtype=f32)
    return {
        "x": nrm(ks[0], (BATCH, SEQ, D_MODEL), 1.0),
        "norm_mix_pre": gain(ks[1], D_MODEL),
        "w_in": nrm(ks[2], (L, D_MODEL, IN_WIDTH), D_MODEL ** -0.5),
        "ssm_a_re": -0.5 * jnp.exp(0.05 * jax.random.normal(ks[3], (L, SSM_GROUPS, SSM_STATE), f32)),
        "ssm_a_im": math.pi * n_idx + 0.05 * jax.random.normal(ks[4], (L, SSM_GROUPS, SSM_STATE), f32),
        "ssm_log_dt": jax.random.uniform(ks[5], (L, SSM_GROUPS), f32, math.log(DT_MIN), math.log(DT_MAX)),
        "ssm_b_re": nrm(ks[6], (L, SSM_GROUPS, SSM_STATE, SSM_GROUP), (2 * SSM_GROUP) ** -0.5),
        "ssm_b_im": nrm(ks[7], (L, SSM_GROUPS, SSM_STATE, SSM_GROUP), (2 * SSM_GROUP) ** -0.5),
        "ssm_c_re": nrm(ks[8], (L, SSM_GROUPS, SSM_GROUP, SSM_STATE), (2 * SSM_STATE) ** -0.5),
        "ssm_c_im": nrm(ks[9], (L, SSM_GROUPS, SSM_GROUP, SSM_STATE), (2 * SSM_STATE) ** -0.5),
        "ssm_d": nrm(ks[10], (L, SSM_WIDTH), 1.0),
        "ssm_w_glu": nrm(ks[11], (L, SSM_WIDTH, SSM_WIDTH), SSM_WIDTH ** -0.5),
        "ssm_b_glu": nrm(ks[12], (L, SSM_WIDTH), 0.01),
        "cmp_pe_k": nrm(ks[13], (L, L_CMP, HEAD_DIM), 0.1),
        "cmp_w1_k": nrm(ks[14], (L, L_CMP * HEAD_DIM, HEAD_DIM), (L_CMP * HEAD_DIM) ** -0.5),
        "cmp_w2_k": nrm(ks[15], (L, HEAD_DIM, HEAD_DIM), HEAD_DIM ** -0.5),
        "cmp_pe_v": nrm(ks[16], (L, L_CMP, HEAD_DIM), 0.1),
        "cmp_w1_v": nrm(ks[17], (L, L_CMP * HEAD_DIM, HEAD_DIM), (L_CMP * HEAD_DIM) ** -0.5),
        "cmp_w2_v": nrm(ks[18], (L, HEAD_DIM, HEAD_DIM), HEAD_DIM ** -0.5),
        "w_proj_a": nrm(ks[19], (L, SSM_WIDTH, D_MODEL), SSM_WIDTH ** -0.5),
        "w_proj_b": nrm(ks[20], (L, ATT_WIDTH, D_MODEL), ATT_WIDTH ** -0.5),
        "w_out": nrm(ks[21], (L, D_MODEL, D_MODEL), D_MODEL ** -0.5),
        "norm_mix_post": gain(ks[22], D_MODEL),
        "norm_ffn_pre": gain(ks[23], D_MODEL),
        "w_ffn_gate": nrm(ks[24], (L, D_MODEL, D_FF), D_MODEL ** -0.5),
        "w_ffn_up": nrm(ks[25], (L, D_MODEL, D_FF), D_MODEL ** -0.5),
        "w_ffn_down": nrm(ks[26], (L, D_FF, D_MODEL), D_FF ** -0.5),
        "norm_ffn_post": gain(ks[27], D_MODEL),
    }


def reference(x, norm_mix_pre, w_in, ssm_a_re, ssm_a_im, ssm_log_dt, ssm_b_re, ssm_b_im, ssm_c_re, ssm_c_im,
              ssm_d, ssm_w_glu, ssm_b_glu, cmp_pe_k, cmp_w1_k, cmp_w2_k, cmp_pe_v, cmp_w1_v, cmp_w2_v,
              w_proj_a, w_proj_b, w_out, norm_mix_post, norm_ffn_pre, w_ffn_gate, w_ffn_up, w_ffn_down,
              norm_ffn_post):
    split_at = [int(v) for v in np.cumsum(IN_SPLITS)[:-1]]
    h = x
    for l in range(DEPTH):
        hn = rms_norm(h, norm_mix_pre[l])
        proj = hn @ w_in[l]
        u, q, kc, vc, ks_, vs_, kw, vw, g_nsa, g_a, g_b = jnp.split(proj, split_at, axis=-1)
        y_a = s5_mixer(u, ssm_a_re[l], ssm_a_im[l], ssm_log_dt[l], ssm_b_re[l], ssm_b_im[l],
                       ssm_c_re[l], ssm_c_im[l], ssm_d[l], ssm_w_glu[l], ssm_b_glu[l])
        y_b = nsa_mixer(q, kc, vc, ks_, vs_, kw, vw, g_nsa, cmp_pe_k[l], cmp_w1_k[l], cmp_w2_k[l],
                        cmp_pe_v[l], cmp_w1_v[l], cmp_w2_v[l])
        merged = jax.nn.sigmoid(g_a) * (y_a @ w_proj_a[l]) + jax.nn.sigmoid(g_b) * (y_b @ w_proj_b[l])
        h = h + rms_norm(merged @ w_out[l], norm_mix_post[l])
        hn = rms_norm(h, norm_ffn_pre[l])
        f = (jax.nn.silu(hn @ w_ffn_gate[l]) * (hn @ w_ffn_up[l])) @ w_ffn_down[l]
        h = h + rms_norm(f, norm_ffn_post[l])
    return h
```

```python
import functools

import jax
import jax.numpy as jnp
from jax import lax
from jax.experimental import pallas as pl
from jax.experimental.pallas import tpu as pltpu

F32 = jnp.float32
BF16 = jnp.bfloat16

D_MODEL = 4096
SSM_WIDTH = 2048
SSM_GROUP = 16
SSM_GROUPS = 128
SSM_STATE = 64
N_HEADS = 16
HEAD_DIM = 128
N_KV = 4
HPG = 4
ATT_WIDTH = 2048
KV_WIDTH = 512
L_CMP = 32
STRIDE_CMP = 16
L_SEL = 64
N_SEL = 16
WINDOW = 512
D_FF = 11008
RMS_EPS = 1e-6
NEG_INF = -1e30
FORCE_SCORE = 1e9

V7X_VMEM_LIMIT_BYTES = 56 * 1024 * 1024
LANES = 128
SUBLANES = 8

S5_GT = 16
S5_NC = S5_GT * SSM_GROUP
S5_NP = S5_GT * SSM_STATE
S5_TILES = SSM_GROUPS // S5_GT
S5_BPAD = SUBLANES


def _params(sem, vmem=V7X_VMEM_LIMIT_BYTES):
    return pltpu.CompilerParams(dimension_semantics=sem, vmem_limit_bytes=vmem)


def _rmsnorm_body(x_ref, g_ref, o_ref):
    x = x_ref[...]
    ms = jnp.mean(x * x, axis=-1, keepdims=True)
    o_ref[...] = (x * lax.rsqrt(ms + RMS_EPS) * g_ref[...]).astype(o_ref.dtype)


def _rmsnorm(x, gain, out_dtype, tm=256):
    m, d = x.shape
    tm = min(tm, m)
    return pl.pallas_call(
        _rmsnorm_body,
        out_shape=jax.ShapeDtypeStruct((m, d), out_dtype),
        grid=(m // tm,),
        in_specs=[pl.BlockSpec((tm, d), lambda i: (i, 0)), pl.BlockSpec((1, d), lambda i: (0, 0))],
        out_specs=pl.BlockSpec((tm, d), lambda i: (i, 0)),
        compiler_params=_params(("parallel",)),
        name="rmsnorm",
    )(x, gain.reshape(1, d))


def _mm_body(a_ref, w_ref, o_ref, *, scale):
    acc = jnp.dot(a_ref[...], w_ref[...], preferred_element_type=F32)
    if scale is not None:
        acc = acc * scale
    o_ref[...] = acc.astype(o_ref.dtype)


def _matmul(a, w, out_dtype, tm, tn, scale=None, name="matmul"):
    m, k = a.shape
    n = w.shape[1]
    tm, tn = min(tm, m), min(tn, n)
    assert m % tm == 0 and n % tn == 0
    return pl.pallas_call(
        functools.partial(_mm_body, scale=scale),
        out_shape=jax.ShapeDtypeStruct((m, n), out_dtype),
        grid=(m // tm, n // tn),
        in_specs=[pl.BlockSpec((tm, k), lambda i, j: (i, 0)), pl.BlockSpec((k, tn), lambda i, j: (0, j))],
        out_specs=pl.BlockSpec((tm, tn), lambda i, j: (i, j)),
        compiler_params=_params(("parallel", "arbitrary")),
        name=name,
    )(a, w)


def _s5_body(u_ref, are_ref, aim_ref, ldt_ref, bre_ref, bim_ref, cre_ref, cim_ref, d_ref, y_ref,
             abar_scr, bbar_scr, h_scr, st_scr, *, tc):
    ti = pl.program_id(1)

    @pl.when(ti == 0)
    def _():
        ar, ai = are_ref[...], aim_ref[...]
        dt = jnp.exp(ldt_ref[...])
        decay = jnp.exp(dt * ar)
        abr, abi = decay * jnp.cos(dt * ai), decay * jnp.sin(dt * ai)
        den = ar * ar + ai * ai
        zr = ((abr - 1.0) * ar + abi * ai) / den
        zi = (abi * ar - (abr - 1.0) * ai) / den
        abar_scr[0:S5_BPAD, :] = jnp.broadcast_to(abr, (S5_BPAD, S5_NP))
        abar_scr[S5_BPAD:2 * S5_BPAD, :] = jnp.broadcast_to(abi, (S5_BPAD, S5_NP))
        br, bi = bre_ref[...], bim_ref[...]
        bbar_scr[:, 0:S5_NP] = (zr * br - zi * bi).astype(BF16)
        bbar_scr[:, S5_NP:2 * S5_NP] = (zr * bi + zi * br).astype(BF16)
        st_scr[...] = jnp.zeros_like(st_scr)

    u = u_ref[...]
    h_scr[...] = jnp.dot(u.astype(BF16), bbar_scr[...], preferred_element_type=F32)
    ar = abar_scr[0:S5_BPAD, :]
    ai = abar_scr[S5_BPAD:2 * S5_BPAD, :]

    def step(t, carry):
        hr, hi = carry
        r0 = pl.multiple_of(t * S5_BPAD, S5_BPAD)
        bur = h_scr[pl.ds(r0, S5_BPAD), 0:S5_NP]
        bui = h_scr[pl.ds(r0, S5_BPAD), S5_NP:2 * S5_NP]
        nhr = ar * hr - ai * hi + bur
        nhi = ar * hi + ai * hr + bui
        h_scr[pl.ds(r0, S5_BPAD), 0:S5_NP] = nhr
        h_scr[pl.ds(r0, S5_BPAD), S5_NP:2 * S5_NP] = nhi
        return nhr, nhi

    hr, hi = lax.fori_loop(0, tc, step, (st_scr[0:S5_BPAD, :], st_scr[S5_BPAD:2 * S5_BPAD, :]), unroll=2)
    st_scr[0:S5_BPAD, :] = hr
    st_scr[S5_BPAD:2 * S5_BPAD, :] = hi

    y = (jnp.dot(h_scr[:, 0:S5_NP].astype(BF16), cre_ref[...].astype(BF16), preferred_element_type=F32)
         - jnp.dot(h_scr[:, S5_NP:2 * S5_NP].astype(BF16), cim_ref[...].astype(BF16), preferred_element_type=F32)
         + d_ref[...] * u)
    y_ref[...] = jax.nn.gelu(y)


def _s5_mixer(u_tb, a_re, a_im, log_dt, b_re, b_im, c_re, c_im, d_skip, seq, tc=128):
    tc = min(tc, seq)
    eye = jnp.eye(S5_GT, dtype=F32)

    def bdiag_b(b):
        bt = b.reshape(S5_TILES, S5_GT, SSM_STATE, SSM_GROUP).transpose(0, 1, 3, 2)
        return jnp.einsum("tgcp,gh->tgchp", bt, eye).reshape(S5_TILES, S5_NC, S5_NP)

    def bdiag_c(c):
        ct = c.reshape(S5_TILES, S5_GT, SSM_GROUP, SSM_STATE)
        return jnp.einsum("tgcp,gh->thpgc", ct, eye).reshape(S5_TILES, S5_NP, S5_NC)

    flat = lambda a: a.reshape(1, SSM_GROUPS * SSM_STATE)
    ldt = jnp.repeat(log_dt, SSM_STATE).reshape(1, SSM_GROUPS * SSM_STATE)
    rows = tc * S5_BPAD
    vec_spec = pl.BlockSpec((1, S5_NP), lambda g, t: (0, g))
    return pl.pallas_call(
        functools.partial(_s5_body, tc=tc),
        out_shape=jax.ShapeDtypeStruct((seq * S5_BPAD, SSM_WIDTH), F32),
        grid=(S5_TILES, seq // tc),
        in_specs=[
            pl.BlockSpec((rows, S5_NC), lambda g, t: (t, g)),
            vec_spec, vec_spec, vec_spec,
            pl.BlockSpec((None, S5_NC, S5_NP), lambda g, t: (g, 0, 0)),
            pl.BlockSpec((None, S5_NC, S5_NP), lambda g, t: (g, 0, 0)),
            pl.BlockSpec((None, S5_NP, S5_NC), lambda g, t: (g, 0, 0)),
            pl.BlockSpec((None, S5_NP, S5_NC), lambda g, t: (g, 0, 0)),
            pl.BlockSpec((1, S5_NC), lambda g, t: (0, g)),
        ],
        out_specs=pl.BlockSpec((rows, S5_NC), lambda g, t: (t, g)),
        scratch_shapes=[
            pltpu.VMEM((2 * S5_BPAD, S5_NP), F32),
            pltpu.VMEM((S5_NC, 2 * S5_NP), BF16),
            pltpu.VMEM((rows, 2 * S5_NP), F32),
            pltpu.VMEM((2 * S5_BPAD, S5_NP), F32),
        ],
        compiler_params=_params(("arbitrary", "arbitrary")),
        name="s5_scan",
    )(u_tb, flat(a_re), flat(a_im), ldt, bdiag_b(b_re), bdiag_b(b_im), bdiag_c(c_re), bdiag_c(c_im),
      d_skip.reshape(1, SSM_WIDTH))


def _glu_body(y_ref, w_ref, b_ref, o_ref, ybf_scr, *, tn):
    j = pl.program_id(1)

    @pl.when(j == 0)
    def _():
        ybf_scr[...] = y_ref[...].astype(BF16)

    z = jnp.dot(ybf_scr[...], w_ref[...], preferred_element_type=F32) + b_ref[...]
    yt = y_ref[:, pl.ds(pl.multiple_of(j * tn, tn), tn)]
    o_ref[...] = (yt * jax.nn.sigmoid(z)).astype(o_ref.dtype)


def _glu(y, w_bf, bias, tm=512, tn=512):
    m, k = y.shape
    tm = min(tm, m)
    return pl.pallas_call(
        functools.partial(_glu_body, tn=tn),
        out_shape=jax.ShapeDtypeStruct((m, k), BF16),
        grid=(m // tm, k // tn),
        in_specs=[pl.BlockSpec((tm, k), lambda i, j: (i, 0)),
                  pl.BlockSpec((k, tn), lambda i, j: (0, j)),
                  pl.BlockSpec((1, tn), lambda i, j: (0, j))],
        out_specs=pl.BlockSpec((tm, tn), lambda i, j: (i, j)),
        scratch_shapes=[pltpu.VMEM((tm, k), BF16)],
        compiler_params=_params(("parallel", "arbitrary")),
        name="s5_glu",
    )(y, w_bf, bias.reshape(1, k))


def _compress_body(x_ref, pe_ref, w1_ref, w2_ref, o_ref, *, ncmp):
    half = L_CMP // 2
    acc_a = jnp.zeros((ncmp, HEAD_DIM), F32)
    acc_b = jnp.zeros((ncmp, HEAD_DIM), F32)
    for r in range(half):
        xr = x_ref[pl.ds(r, ncmp, stride=STRIDE_CMP), :]
        xa = (xr + pe_ref[r:r + 1, :]).astype(BF16)
        xb = (xr + pe_ref[half + r:half + r + 1, :]).astype(BF16)
        acc_a += jnp.dot(xa, w1_ref[r * HEAD_DIM:(r + 1) * HEAD_DIM, :], preferred_element_type=F32)
        acc_b += jnp.dot(xb, w1_ref[(half + r) * HEAD_DIM:(half + r + 1) * HEAD_DIM, :],
                         preferred_element_type=F32)
    pre = acc_a + jnp.concatenate([acc_b[1:], acc_b[:1]], axis=0)
    o_ref[...] = jnp.dot(jax.nn.gelu(pre).astype(BF16), w2_ref[...], preferred_element_type=F32).astype(o_ref.dtype)


def _compress(kvc, pe, w1_bf, w2_bf, bsz, seq):
    ncmp = seq // STRIDE_CMP
    return pl.pallas_call(
        functools.partial(_compress_body, ncmp=ncmp),
        out_shape=jax.ShapeDtypeStruct((bsz, 2, N_KV, ncmp, HEAD_DIM), BF16),
        grid=(bsz, 2, N_KV),
        in_specs=[pl.BlockSpec((seq, HEAD_DIM), lambda b, w, g: (b, w * N_KV + g)),
                  pl.BlockSpec((None, L_CMP, HEAD_DIM), lambda b, w, g: (w, 0, 0)),
                  pl.BlockSpec((None, L_CMP * HEAD_DIM, HEAD_DIM), lambda b, w, g: (w, 0, 0)),
                  pl.BlockSpec((None, HEAD_DIM, HEAD_DIM), lambda b, w, g: (w, 0, 0))],
        out_specs=pl.BlockSpec((None, None, None, ncmp, HEAD_DIM), lambda b, w, g: (b, w, g, 0, 0)),
        compiler_params=_params(("parallel", "parallel", "parallel")),
        name="nsa_compress",
    )(kvc, pe, w1_bf, w2_bf)


def _cmp_attn_body(q_ref, kc_ref, vc_ref, o_ref, sel_ref, *, tq, ncmp, nblk, ntop):
    qi = pl.program_id(2)
    t0 = qi * tq
    tpos = t0 + lax.broadcasted_iota(jnp.int32, (tq, ncmp), 0)
    blk_end = lax.broadcasted_iota(jnp.int32, (tq, ncmp), 1) * STRIDE_CMP + (L_CMP - 1)
    ok = blk_end <= tpos
    okf = ok.astype(F32)
    kc = kc_ref[...]
    vc = vc_ref[...]
    psum = jnp.zeros((tq, ncmp), F32)
    for h in range(HPG):
        qh = q_ref[:, h * HEAD_DIM:(h + 1) * HEAD_DIM]
        s = lax.dot_general(qh, kc, (((1,), (1,)), ((), ())), preferred_element_type=F32)
        s = jnp.where(ok, s, NEG_INF)
        e = jnp.exp(s - jnp.max(s, axis=-1, keepdims=True))
        p = e / jnp.sum(e, axis=-1, keepdims=True) * okf
        o_ref[:, h * HEAD_DIM:(h + 1) * HEAD_DIM] = jnp.dot(p.astype(BF16), vc, preferred_element_type=F32)
        psum = psum + p

    jj = lax.broadcasted_iota(jnp.int32, (nblk, ncmp), 0)
    nn = lax.broadcasted_iota(jnp.int32, (nblk, ncmp), 1)
    ov = ((nn * STRIDE_CMP < (jj + 1) * L_SEL) & (nn * STRIDE_CMP + L_CMP > jj * L_SEL)).astype(BF16)
    p_hi = psum.astype(BF16)
    p_lo = (psum - p_hi.astype(F32)).astype(BF16)
    nt = (((1,), (1,)), ((), ()))
    imp = (lax.dot_general(ov, p_hi, nt, preferred_element_type=F32)
           + lax.dot_general(ov, p_lo, nt, preferred_element_type=F32))

    jb = lax.broadcasted_iota(jnp.int32, (nblk, tq), 0)
    tt = t0 + lax.broadcasted_iota(jnp.int32, (nblk, tq), 1)
    cur = tt // L_SEL
    allowed = jb * L_SEL <= tt
    forced = (jb == 0) | (jb == cur) | (jb == cur - 1)
    score = jnp.where(forced, FORCE_SCORE, jnp.where(allowed, imp, NEG_INF))
    cnt = jnp.zeros((nblk, tq), F32)
    for jp in range(nblk):
        row = score[jp:jp + 1, :]
        tie = jnp.where(jb > jp, 1.0, 0.0)
        cnt = cnt + jnp.where(row > score, 1.0, jnp.where(row == score, tie, 0.0))
    sel_ref[...] = (cnt < ntop).astype(sel_ref.dtype)


def _cmp_attn(q, kcv, bsz, seq, tq=512):
    tq = min(tq, seq)
    ncmp = seq // STRIDE_CMP
    nblk = seq // L_SEL
    ntop = min(N_SEL, nblk)
    nq = seq // tq
    return pl.pallas_call(
        functools.partial(_cmp_attn_body, tq=tq, ncmp=ncmp, nblk=nblk, ntop=ntop),
        out_shape=(jax.ShapeDtypeStruct((bsz * seq, ATT_WIDTH), F32),
                   jax.ShapeDtypeStruct((bsz, N_KV, nblk, seq), F32)),
        grid=(bsz, N_KV, nq),
        in_specs=[pl.BlockSpec((tq, HPG * HEAD_DIM), lambda b, g, i: (b * nq + i, g)),
                  pl.BlockSpec((None, None, None, ncmp, HEAD_DIM), lambda b, g, i: (b, 0, g, 0, 0)),
                  pl.BlockSpec((None, None, None, ncmp, HEAD_DIM), lambda b, g, i: (b, 1, g, 0, 0))],
        out_specs=(pl.BlockSpec((tq, HPG * HEAD_DIM), lambda b, g, i: (b * nq + i, g)),
                   pl.BlockSpec((None, None, nblk, tq), lambda b, g, i: (b, g, 0, i))),
        compiler_params=_params(("parallel", "parallel", "parallel")),
        name="nsa_cmp_attn_topk",
    )(q, kcv, kcv)


def _sel_attn_body(q_ref, k_ref, v_ref, sel_ref, o_ref, m_scr, l_scr, acc_scr, *, tq, nblk):
    qi = pl.program_id(2)
    tk = tq
    m_scr[...] = jnp.full_like(m_scr, NEG_INF)
    l_scr[...] = jnp.zeros_like(l_scr)
    acc_scr[...] = jnp.zeros_like(acc_scr)
    sel = sel_ref[...]
    blocks_per_tile = tk // L_SEL

    def tile(kt, diagonal):
        k0 = pl.multiple_of(kt * tk, tk)
        k = k_ref[pl.ds(k0, tk), :]
        v = v_ref[pl.ds(k0, tk), :]
        jj = lax.broadcasted_iota(jnp.int32, (nblk, tk), 0)
        cc = lax.broadcasted_iota(jnp.int32, (nblk, tk), 1)
        expand = (jj == kt * blocks_per_tile + cc // L_SEL).astype(BF16)
        ok = jnp.dot(sel, expand, preferred_element_type=F32) > 0.5
        if diagonal:
            ok = ok & (lax.broadcasted_iota(jnp.int32, (tq, tk), 1) <= lax.broadcasted_iota(jnp.int32, (tq, tk), 0))
        for h in range(HPG):
            qh = q_ref[:, h * HEAD_DIM:(h + 1) * HEAD_DIM]
            s = lax.dot_general(qh, k, (((1,), (1,)), ((), ())), preferred_element_type=F32)
            s = jnp.where(ok, s, NEG_INF)
            m_prev = m_scr[h]
            m_next = jnp.maximum(m_prev, jnp.max(s, axis=-1, keepdims=True))
            p = jnp.exp(s - pltpu.repeat(m_next, tk // LANES, axis=1))
            alpha = jnp.exp(m_prev - m_next)
            l_scr[h] = alpha * l_scr[h] + jnp.sum(p, axis=-1, keepdims=True)
            acc_scr[h] = alpha * acc_scr[h] + jnp.dot(p.astype(BF16), v, preferred_element_type=F32)
            m_scr[h] = m_next

    def full_tile(kt, carry):
        tile(kt, False)
        return carry

    lax.fori_loop(0, qi, full_tile, 0)
    tile(qi, True)
    for h in range(HPG):
        o_ref[:, h * HEAD_DIM:(h + 1) * HEAD_DIM] = acc_scr[h] / l_scr[h]


def _sel_attn(q, kv, sel, bsz, seq, tq=256):
    tq = min(tq, seq)
    nq = seq // tq
    nblk = seq // L_SEL
    return pl.pallas_call(
        functools.partial(_sel_attn_body, tq=tq, nblk=nblk),
        out_shape=jax.ShapeDtypeStruct((bsz * seq, ATT_WIDTH), F32),
        grid=(bsz, N_KV, nq),
        in_specs=[pl.BlockSpec((tq, HPG * HEAD_DIM), lambda b, g, i: (b * nq + i, g)),
                  pl.BlockSpec((seq, HEAD_DIM), lambda b, g, i: (b, g)),
                  pl.BlockSpec((seq, HEAD_DIM), lambda b, g, i: (b, N_KV + g)),
                  pl.BlockSpec((None, None, tq, nblk), lambda b, g, i: (b, g, i, 0))],
        out_specs=pl.BlockSpec((tq, HPG * HEAD_DIM), lambda b, g, i: (b * nq + i, g)),
        scratch_shapes=[pltpu.VMEM((HPG, tq, LANES), F32),
                        pltpu.VMEM((HPG, tq, LANES), F32),
                        pltpu.VMEM((HPG, tq, HEAD_DIM), F32)],
        compiler_params=_params(("parallel", "parallel", "arbitrary")),
        name="nsa_sel_attn",
    )(q, kv, kv, sel)


def _win_attn_body(q_ref, kc_ref, kp_ref, vc_ref, vp_ref, ocmp_ref, osel_ref, gate_ref, o_ref, *, tq):
    qi = pl.program_id(2)
    r = lax.broadcasted_iota(jnp.int32, (tq, tq), 0)
    c = lax.broadcasted_iota(jnp.int32, (tq, tq), 1)
    ok_cur = c <= r
    ok_prev = (c > r) & (qi > 0)
    gates = jax.nn.sigmoid(gate_ref[...])
    nt = (((1,), (1,)), ((), ()))
    for h in range(HPG):
        sl = slice(h * HEAD_DIM, (h + 1) * HEAD_DIM)
        qh = q_ref[:, sl]
        s1 = jnp.where(ok_cur, lax.dot_general(qh, kc_ref[...], nt, preferred_element_type=F32), NEG_INF)
        s0 = jnp.where(ok_prev, lax.dot_general(qh, kp_ref[...], nt, preferred_element_type=F32), NEG_INF)
        m = jnp.maximum(jnp.max(s1, axis=-1, keepdims=True), jnp.max(s0, axis=-1, keepdims=True))
        p1 = jnp.exp(s1 - m)
        p0 = jnp.exp(s0 - m)
        l = jnp.sum(p1, axis=-1, keepdims=True) + jnp.sum(p0, axis=-1, keepdims=True)
        ow = (jnp.dot(p1.astype(BF16), vc_ref[...], preferred_element_type=F32)
              + jnp.dot(p0.astype(BF16), vp_ref[...], preferred_element_type=F32)) / l
        g_cmp = gates[:, h:h + 1]
        g_sel = gates[:, HPG + h:HPG + h + 1]
        g_win = gates[:, 2 * HPG + h:2 * HPG + h + 1]
        o_ref[:, sl] = (g_cmp * ocmp_ref[:, sl] + g_sel * osel_ref[:, sl] + g_win * ow).astype(o_ref.dtype)


def _win_attn_combine(q, kv, o_cmp, o_sel, gates, bsz, seq):
    tq = WINDOW
    assert seq % tq == 0
    nq = seq // tq
    qspec = pl.BlockSpec((tq, HPG * HEAD_DIM), lambda b, g, i: (b * nq + i, g))
    cur = lambda col: pl.BlockSpec((tq, HEAD_DIM), lambda b, g, i: (b * nq + i, col * N_KV + g))
    prev = lambda col: pl.BlockSpec((tq, HEAD_DIM), lambda b, g, i: (b * nq + jnp.maximum(i - 1, 0), col * N_KV + g))
    return pl.pallas_call(
        functools.partial(_win_attn_body, tq=tq),
        out_shape=jax.ShapeDtypeStruct((bsz * seq, ATT_WIDTH), BF16),
        grid=(bsz, N_KV, nq),
        in_specs=[qspec, cur(2), prev(2), cur(3), prev(3), qspec, qspec,
                  pl.BlockSpec((None, None, tq, LANES), lambda b, g, i: (b, g, i, 0))],
        out_specs=qspec,
        compiler_params=_params(("parallel", "parallel", "parallel")),
        name="nsa_win_attn_gate",
    )(q, kv, kv, kv, kv, o_cmp, o_sel, gates)


def _merge_body(hn_ref, ya_ref, yb_ref, wga_ref, wgb_ref, wa_ref, wb_ref, o_ref):
    hn = hn_ref[...]
    ga = jnp.dot(hn, wga_ref[...], preferred_element_type=F32)
    gb = jnp.dot(hn, wgb_ref[...], preferred_element_type=F32)
    pa = jnp.dot(ya_ref[...], wa_ref[...], preferred_element_type=F32)
    pb = jnp.dot(yb_ref[...], wb_ref[...], preferred_element_type=F32)
    o_ref[...] = (jax.nn.sigmoid(ga) * pa + jax.nn.sigmoid(gb) * pb).astype(o_ref.dtype)


def _merge(hn, ya, yb, wga, wgb, wa, wb, tm=512, tn=512):
    m, d = hn.shape
    ka = ya.shape[1]
    tm = min(tm, m)
    row = lambda k: pl.BlockSpec((tm, k), lambda i, j: (i, 0))
    col = lambda k: pl.BlockSpec((k, tn), lambda i, j: (0, j))
    return pl.pallas_call(
        _merge_body,
        out_shape=jax.ShapeDtypeStruct((m, d), BF16),
        grid=(m // tm, d // tn),
        in_specs=[row(d), row(ka), row(ka), col(d), col(d), col(ka), col(ka)],
        out_specs=pl.BlockSpec((tm, tn), lambda i, j: (i, j)),
        compiler_params=_params(("parallel", "arbitrary")),
        name="mixer_merge",
    )(hn, ya, yb, wga, wgb, wa, wb)


def _rms(x, g):
    return x * lax.rsqrt(jnp.mean(x * x, axis=-1, keepdims=True) + RMS_EPS) * g


def _res_norm2_body(raw_ref, x_ref, gpost_ref, gpre_ref, h_ref, hn_ref):
    h = x_ref[...] + _rms(raw_ref[...], gpost_ref[...])
    h_ref[...] = h
    hn_ref[...] = _rms(h, gpre_ref[...]).astype(hn_ref.dtype)


def _res_norm2(raw, x, g_post, g_pre, tm=256):
    m, d = x.shape
    tm = min(tm, m)
    rows = pl.BlockSpec((tm, d), lambda i: (i, 0))
    vec = pl.BlockSpec((1, d), lambda i: (0, 0))
    return pl.pallas_call(
        _res_norm2_body,
        out_shape=(jax.ShapeDtypeStruct((m, d), F32), jax.ShapeDtypeStruct((m, d), BF16)),
        grid=(m // tm,),
        in_specs=[rows, rows, vec, vec],
        out_specs=(rows, rows),
        compiler_params=_params(("parallel",)),
        name="residual_norm_prenorm",
    )(raw, x, g_post.reshape(1, d), g_pre.reshape(1, d))


def _res_norm_body(raw_ref, x_ref, g_ref, o_ref):
    o_ref[...] = x_ref[...] + _rms(raw_ref[...], g_ref[...])


def _res_norm(raw, x, g, tm=256):
    m, d = x.shape
    tm = min(tm, m)
    rows = pl.BlockSpec((tm, d), lambda i: (i, 0))
    return pl.pallas_call(
        _res_norm_body,
        out_shape=jax.ShapeDtypeStruct((m, d), F32),
        grid=(m // tm,),
        in_specs=[rows, rows, pl.BlockSpec((1, d), lambda i: (0, 0))],
        out_specs=rows,
        compiler_params=_params(("parallel",)),
        name="residual_norm",
    )(raw, x, g.reshape(1, d))


def _swiglu_body(a_ref, wg_ref, wu_ref, o_ref):
    a = a_ref[...]
    g = jnp.dot(a, wg_ref[...], preferred_element_type=F32)
    u = jnp.dot(a, wu_ref[...], preferred_element_type=F32)
    o_ref[...] = (g * jax.nn.sigmoid(g) * u).astype(o_ref.dtype)


def _swiglu(a, wg, wu, tm=1024, tn=256):
    m, k = a.shape
    n = wg.shape[1]
    tm = min(tm, m)
    assert n % tn == 0
    return pl.pallas_call(
        _swiglu_body,
        out_shape=jax.ShapeDtypeStruct((m, n), BF16),
        grid=(m // tm, n // tn),
        in_specs=[pl.BlockSpec((tm, k), lambda i, j: (i, 0)),
                  pl.BlockSpec((k, tn), lambda i, j: (0, j)),
                  pl.BlockSpec((k, tn), lambda i, j: (0, j))],
        out_specs=pl.BlockSpec((tm, tn), lambda i, j: (i, j)),
        compiler_params=_params(("parallel", "arbitrary")),
        name="ffn_swiglu",
    )(a, wg, wu)


def kernel(x, norm_mix_pre, w_in, ssm_a_re, ssm_a_im, ssm_log_dt, ssm_b_re, ssm_b_im, ssm_c_re, ssm_c_im, ssm_d, ssm_w_glu, ssm_b_glu, cmp_pe_k, cmp_w1_k, cmp_w2_k, cmp_pe_v, cmp_w1_v, cmp_w2_v, w_proj_a, w_proj_b, w_out, norm_mix_post, norm_ffn_pre, w_ffn_gate, w_ffn_up, w_ffn_down, norm_ffn_post):
    bsz, seq, d = x.shape
    m = bsz * seq
    depth = w_in.shape[0]
    assert bsz <= S5_BPAD
    h = x.reshape(m, d)
    o_u, o_q, o_kvc, o_kv, o_gn, o_ga, o_gb = 0, 2048, 4096, 5120, 7168, 7216, 7216 + D_MODEL
    for l in range(depth):
        w = w_in[l]
        bf = lambda a: a.astype(BF16)
        w_u, w_q = bf(w[:, o_u:o_q]), bf(w[:, o_q:o_kvc])
        w_kvc, w_kv = bf(w[:, o_kvc:o_kv]), bf(w[:, o_kv:o_gn])
        w_gn = bf(jnp.pad(w[:, o_gn:o_ga], ((0, 0), (0, LANES - 3 * N_HEADS))))
        w_ga, w_gb = bf(w[:, o_ga:o_gb]), bf(w[:, o_gb:o_gb + D_MODEL])

        hn = _rmsnorm(h, norm_mix_pre[l], BF16)
        u = _matmul(hn, w_u, F32, 1024, 512, name="in_proj_u")
        q = _matmul(hn, w_q, BF16, 1024, 512, scale=HEAD_DIM ** -0.5, name="in_proj_q")
        kvc = _matmul(hn, w_kvc, F32, 1024, 512, name="in_proj_kv_cmp")
        kv = _matmul(hn, w_kv, BF16, 1024, 512, name="in_proj_kv")
        gn = _matmul(hn, w_gn, F32, 1024, LANES, name="in_proj_gates")

        u_tb = jnp.pad(u.reshape(bsz, seq, SSM_WIDTH).transpose(1, 0, 2), ((0, 0), (0, S5_BPAD - bsz), (0, 0)))
        y_tb = _s5_mixer(u_tb.reshape(seq * S5_BPAD, SSM_WIDTH), ssm_a_re[l], ssm_a_im[l], ssm_log_dt[l],
                         ssm_b_re[l], ssm_b_im[l], ssm_c_re[l], ssm_c_im[l], ssm_d[l], seq)
        y = y_tb.reshape(seq, S5_BPAD, SSM_WIDTH)[:, :bsz].transpose(1, 0, 2).reshape(m, SSM_WIDTH)
        y_a = _glu(y, bf(ssm_w_glu[l]), ssm_b_glu[l])

        pe = jnp.stack([cmp_pe_k[l], cmp_pe_v[l]])
        w1 = bf(jnp.stack([cmp_w1_k[l], cmp_w1_v[l]]))
        w2 = bf(jnp.stack([cmp_w2_k[l], cmp_w2_v[l]]))
        kcv = _compress(kvc, pe, w1, w2, bsz, seq)
        o_cmp, sel_t = _cmp_attn(q, kcv, bsz, seq)
        sel = bf(jnp.swapaxes(sel_t, 2, 3))
        o_sel = _sel_attn(q, kv, sel, bsz, seq)
        gates = gn[:, :3 * N_HEADS].reshape(bsz, seq, 3, N_KV, HPG).transpose(0, 3, 1, 2, 4)
        gates = jnp.pad(gates.reshape(bsz, N_KV, seq, 3 * HPG), ((0, 0), (0, 0), (0, 0), (0, LANES - 3 * HPG)))
        y_b = _win_attn_combine(q, kv, o_cmp, o_sel, gates, bsz, seq)

        merged = _merge(hn, y_a, y_b, w_ga, w_gb, bf(w_proj_a[l]), bf(w_proj_b[l]))
        mix = _matmul(merged, bf(w_out[l]), F32, 1024, 512, name="out_proj")
        h, hn2 = _res_norm2(mix, h, norm_mix_post[l], norm_ffn_pre[l])

        act = _swiglu(hn2, bf(w_ffn_gate[l]), bf(w_ffn_up[l]))
        f = _matmul(act, bf(w_ffn_down[l]), F32, 512, 256, name="ffn_down")
        h = _res_norm(f, h, norm_ffn_post[l])
    return h.reshape(bsz, seq, d)
```

```python
import functools
import math

import jax
import jax.numpy as jnp
from jax import lax
from jax.experimental import pallas as pl
from jax.experimental.pallas import tpu as pltpu

F32 = jnp.float32
BF16 = jnp.bfloat16

D_MODEL = 4096
SSM_WIDTH = 2048
SSM_GROUP = 16
SSM_GROUPS = 128
SSM_STATE = 64
N_HEADS = 16
HEAD_DIM = 128
N_KV = 4
HPG = 4
ATT_WIDTH = 2048
KV_WIDTH = 512
L_CMP = 32
STRIDE_CMP = 16
L_SEL = 64
N_SEL = 16
WINDOW = 512
D_FF = 11008
RMS_EPS = 1e-6
NEG_INF = -1e30
FORCE_SCORE = 1e9
LOG2E = math.log2(math.e)

V7X_VMEM_LIMIT_BYTES = 56 * 1024 * 1024
LANES = 128
SUBLANES = 8

S5_GT = 16
S5_NC = S5_GT * SSM_GROUP
S5_NP = S5_GT * SSM_STATE
S5_TILES = SSM_GROUPS // S5_GT
S5_SLABS = 2 * S5_NP // LANES

ATT_ROWS = 128
SEL_TQ, SEL_TK, SEL_RB = 512, 512, 64


def _params(sem, vmem=V7X_VMEM_LIMIT_BYTES):
    return pltpu.CompilerParams(dimension_semantics=sem, vmem_limit_bytes=vmem)


def _rmsnorm_body(x_ref, g_ref, o_ref):
    x = x_ref[...]
    ms = jnp.mean(x * x, axis=-1, keepdims=True)
    o_ref[...] = (x * lax.rsqrt(ms + RMS_EPS) * g_ref[...]).astype(o_ref.dtype)


def _rmsnorm(x, gain, out_dtype, tm=256):
    m, d = x.shape
    tm = min(tm, m)
    return pl.pallas_call(
        _rmsnorm_body,
        out_shape=jax.ShapeDtypeStruct((m, d), out_dtype),
        grid=(m // tm,),
        in_specs=[pl.BlockSpec((tm, d), lambda i: (i, 0)), pl.BlockSpec((1, d), lambda i: (0, 0))],
        out_specs=pl.BlockSpec((tm, d), lambda i: (i, 0)),
        compiler_params=_params(("parallel",)),
        name="rmsnorm",
    )(x, gain.reshape(1, d))


def _mm_body(a_ref, w_ref, o_ref):
    o_ref[...] = jnp.dot(a_ref[...], w_ref[...], preferred_element_type=F32).astype(o_ref.dtype)


def _mm_scaled_body(a_ref, w_ref, s_ref, o_ref):
    acc = jnp.dot(a_ref[...], w_ref[...], preferred_element_type=F32)
    o_ref[...] = (acc * s_ref[...]).astype(o_ref.dtype)


def _matmul(a, w, out_dtype, tm, tn, col_scale=None, name="matmul"):
    m, k = a.shape
    n = w.shape[1]
    tm, tn = min(tm, m), min(tn, n)
    assert m % tm == 0 and n % tn == 0
    in_specs = [pl.BlockSpec((tm, k), lambda i, j: (i, 0)), pl.BlockSpec((k, tn), lambda i, j: (0, j))]
    args = [a, w]
    body = _mm_body
    if col_scale is not None:
        in_specs.append(pl.BlockSpec((1, tn), lambda i, j: (0, j)))
        args.append(col_scale.reshape(1, n))
        body = _mm_scaled_body
    return pl.pallas_call(
        body,
        out_shape=jax.ShapeDtypeStruct((m, n), out_dtype),
        grid=(m // tm, n // tn),
        in_specs=in_specs,
        out_specs=pl.BlockSpec((tm, tn), lambda i, j: (i, j)),
        compiler_params=_params(("parallel", "arbitrary")),
        name=name,
    )(*args)


def _s5_body(u_ref, are_ref, aim_ref, ldt_ref, bre_ref, bim_ref, cre_ref, cim_ref, d_ref, y_ref,
             abar_scr, bbar_scr, h_scr, st_scr, *, tc, bsz, tps):
    ti = pl.program_id(1)
    half = S5_SLABS // 2

    @pl.when(ti == 0)
    def _():
        row_tile = lax.broadcasted_iota(jnp.int32, (SUBLANES, S5_NP), 0) // bsz
        abr8 = jnp.zeros((SUBLANES, S5_NP), F32)
        abi8 = jnp.zeros((SUBLANES, S5_NP), F32)
        for k in range(tps):
            cols = slice(k * S5_NP, (k + 1) * S5_NP)
            ar, ai = are_ref[:, cols], aim_ref[:, cols]
            dt = jnp.exp(ldt_ref[:, cols])
            decay = jnp.exp(dt * ar)
            abr, abi = decay * jnp.cos(dt * ai), decay * jnp.sin(dt * ai)
            den = ar * ar + ai * ai
            zr = ((abr - 1.0) * ar + abi * ai) / den
            zi = (abi * ar - (abr - 1.0) * ai) / den
            abr8 = jnp.where(row_tile == k, abr, abr8)
            abi8 = jnp.where(row_tile == k, abi, abi8)
            br, bi = bre_ref[k], bim_ref[k]
            bbar_scr[k, :, 0:S5_NP] = (zr * br - zi * bi).astype(BF16)
            bbar_scr[k, :, S5_NP:2 * S5_NP] = (zr * bi + zi * br).astype(BF16)
        abar_scr[0:SUBLANES, :] = abr8
        abar_scr[SUBLANES:2 * SUBLANES, :] = abi8
        st_scr[...] = jnp.zeros_like(st_scr)

    for k in range(tps):
        for b in range(bsz):
            ub = u_ref[b, :, k * S5_NC:(k + 1) * S5_NC]
            bu = jnp.dot(ub.astype(BF16), bbar_scr[k], preferred_element_type=F32)
            for s in range(S5_SLABS):
                h_scr[s, pl.ds(k * bsz + b, tc, stride=SUBLANES), :] = bu[:, s * LANES:(s + 1) * LANES]

    ar = abar_scr[0:SUBLANES, :]
    ai = abar_scr[SUBLANES:2 * SUBLANES, :]

    def step(t, carry):
        hr, hi = carry
        r0 = pl.multiple_of(t * SUBLANES, SUBLANES)
        bur = jnp.concatenate([h_scr[s, pl.ds(r0, SUBLANES), :] for s in range(half)], axis=1)
        bui = jnp.concatenate([h_scr[half + s, pl.ds(r0, SUBLANES), :] for s in range(half)], axis=1)
        nhr = ar * hr - ai * hi + bur
        nhi = ar * hi + ai * hr + bui
        for s in range(half):
            h_scr[s, pl.ds(r0, SUBLANES), :] = nhr[:, s * LANES:(s + 1) * LANES]
            h_scr[half + s, pl.ds(r0, SUBLANES), :] = nhi[:, s * LANES:(s + 1) * LANES]
        return nhr, nhi

    hr, hi = lax.fori_loop(0, tc, step, (st_scr[0:SUBLANES, :], st_scr[SUBLANES:2 * SUBLANES, :]), unroll=2)
    st_scr[0:SUBLANES, :] = hr
    st_scr[SUBLANES:2 * SUBLANES, :] = hi

    for k in range(tps):
        cre = cre_ref[k].astype(BF16)
        cim = cim_ref[k].astype(BF16)
        for b in range(bsz):
            rows = pl.ds(k * bsz + b, tc, stride=SUBLANES)
            hbr = jnp.concatenate([h_scr[s, rows, :] for s in range(half)], axis=1).astype(BF16)
            hbi = jnp.concatenate([h_scr[half + s, rows, :] for s in range(half)], axis=1).astype(BF16)
            cols = slice(k * S5_NC, (k + 1) * S5_NC)
            y = (jnp.dot(hbr, cre, preferred_element_type=F32) - jnp.dot(hbi, cim, preferred_element_type=F32)
                 + d_ref[:, cols] * u_ref[b, :, cols])
            y_ref[b, :, cols] = jax.nn.gelu(y)


def _s5_mixer(u3, a_re, a_im, log_dt, b_re, b_im, c_re, c_im, d_skip, tc=128):
    bsz, seq, _ = u3.shape
    assert SUBLANES % bsz == 0
    tps = min(SUBLANES // bsz, S5_TILES)
    assert tps * bsz == SUBLANES
    tc = min(tc, seq)
    eye = jnp.eye(S5_GT, dtype=F32)

    def bdiag_b(b):
        bt = b.reshape(S5_TILES, S5_GT, SSM_STATE, SSM_GROUP).transpose(0, 1, 3, 2)
        return jnp.einsum("tgcp,gh->tgchp", bt, eye).reshape(S5_TILES, S5_NC, S5_NP)

    def bdiag_c(c):
        ct = c.reshape(S5_TILES, S5_GT, SSM_GROUP, SSM_STATE)
        return jnp.einsum("tgcp,gh->thpgc", ct, eye).reshape(S5_TILES, S5_NP, S5_NC)

    flat = lambda a: a.reshape(1, SSM_GROUPS * SSM_STATE)
    ldt = jnp.repeat(log_dt, SSM_STATE).reshape(1, SSM_GROUPS * SSM_STATE)
    vec_spec = pl.BlockSpec((1, tps * S5_NP), lambda g, t: (0, g))
    b_spec = pl.BlockSpec((tps, S5_NC, S5_NP), lambda g, t: (g, 0, 0))
    c_spec = pl.BlockSpec((tps, S5_NP, S5_NC), lambda g, t: (g, 0, 0))
    return pl.pallas_call(
        functools.partial(_s5_body, tc=tc, bsz=bsz, tps=tps),
        out_shape=jax.ShapeDtypeStruct((bsz, seq, SSM_WIDTH), F32),
        grid=(S5_TILES // tps, seq // tc),
        in_specs=[
            pl.BlockSpec((bsz, tc, tps * S5_NC), lambda g, t: (0, t, g)),
            vec_spec, vec_spec, vec_spec, b_spec, b_spec, c_spec, c_spec,
            pl.BlockSpec((1, tps * S5_NC), lambda g, t: (0, g)),
        ],
        out_specs=pl.BlockSpec((bsz, tc, tps * S5_NC), lambda g, t: (0, t, g)),
        scratch_shapes=[
            pltpu.VMEM((2 * SUBLANES, S5_NP), F32),
            pltpu.VMEM((tps, S5_NC, 2 * S5_NP), BF16),
            pltpu.VMEM((S5_SLABS, tc * SUBLANES, LANES), F32),
            pltpu.VMEM((2 * SUBLANES, S5_NP), F32),
        ],
        compiler_params=_params(("arbitrary", "arbitrary")),
        name="s5_scan",
    )(u3, flat(a_re), flat(a_im), ldt, bdiag_b(b_re), bdiag_b(b_im), bdiag_c(c_re), bdiag_c(c_im),
      d_skip.reshape(1, SSM_WIDTH))


def _glu_body(y_ref, w_ref, b_ref, o_ref, ybf_scr, *, tn):
    j = pl.program_id(1)

    @pl.when(j == 0)
    def _():
        ybf_scr[...] = y_ref[...].astype(BF16)

    z = jnp.dot(ybf_scr[...], w_ref[...], preferred_element_type=F32) + b_ref[...]
    yt = y_ref[:, pl.ds(pl.multiple_of(j * tn, tn), tn)]
    o_ref[...] = (yt * jax.nn.sigmoid(z)).astype(o_ref.dtype)


def _glu(y, w_bf, bias, tm=1024, tn=1024):
    m, k = y.shape
    tm = min(tm, m)
    return pl.pallas_call(
        functools.partial(_glu_body, tn=tn),
        out_shape=jax.ShapeDtypeStruct((m, k), BF16),
        grid=(m // tm, k // tn),
        in_specs=[pl.BlockSpec((tm, k), lambda i, j: (i, 0)),
                  pl.BlockSpec((k, tn), lambda i, j: (0, j)),
                  pl.BlockSpec((1, tn), lambda i, j: (0, j))],
        out_specs=pl.BlockSpec((tm, tn), lambda i, j: (i, j)),
        scratch_shapes=[pltpu.VMEM((tm, k), BF16)],
        compiler_params=_params(("parallel", "arbitrary")),
        name="s5_glu",
    )(y, w_bf, bias.reshape(1, k))


def _compress_body(x_ref, pe_ref, w1_ref, w2_ref, o_ref, *, ncmp):
    half = L_CMP // 2
    acc_a = jnp.zeros((ncmp, HEAD_DIM), F32)
    acc_b = jnp.zeros((ncmp, HEAD_DIM), F32)
    for r in range(half):
        xr = x_ref[pl.ds(r, ncmp, stride=STRIDE_CMP), :]
        xa = (xr + pe_ref[r:r + 1, :]).astype(BF16)
        xb = (xr + pe_ref[half + r:half + r + 1, :]).astype(BF16)
        acc_a += jnp.dot(xa, w1_ref[r * HEAD_DIM:(r + 1) * HEAD_DIM, :], preferred_element_type=F32)
        acc_b += jnp.dot(xb, w1_ref[(half + r) * HEAD_DIM:(half + r + 1) * HEAD_DIM, :],
                         preferred_element_type=F32)
    pre = acc_a + jnp.concatenate([acc_b[1:], acc_b[:1]], axis=0)
    o_ref[...] = jnp.dot(jax.nn.gelu(pre).astype(BF16), w2_ref[...], preferred_element_type=F32).astype(o_ref.dtype)


def _compress(proj_f32, col0, pe, w1_bf, w2_bf, bsz, seq):
    ncmp = seq // STRIDE_CMP
    cb = col0 // HEAD_DIM
    return pl.pallas_call(
        functools.partial(_compress_body, ncmp=ncmp),
        out_shape=jax.ShapeDtypeStruct((bsz, 2, N_KV, ncmp, HEAD_DIM), BF16),
        grid=(bsz, 2, N_KV),
        in_specs=[pl.BlockSpec((seq, HEAD_DIM), lambda b, w, g: (b, cb + w * N_KV + g)),
                  pl.BlockSpec((None, L_CMP, HEAD_DIM), lambda b, w, g: (w, 0, 0)),
                  pl.BlockSpec((None, L_CMP * HEAD_DIM, HEAD_DIM), lambda b, w, g: (w, 0, 0)),
                  pl.BlockSpec((None, HEAD_DIM, HEAD_DIM), lambda b, w, g: (w, 0, 0))],
        out_specs=pl.BlockSpec((None, None, None, ncmp, HEAD_DIM), lambda b, w, g: (b, w, g, 0, 0)),
        compiler_params=_params(("parallel", "parallel", "parallel")),
        name="nsa_compress",
    )(proj_f32, pe, w1_bf, w2_bf)


def _cmp_attn_body(q_ref, kc_ref, vc_ref, o_ref, bias_ref, *, tq, ncmp, nblk, ntop):
    qi = pl.program_id(2)
    t0 = qi * tq
    tpos = t0 + lax.broadcasted_iota(jnp.int32, (tq, ncmp), 0)
    blk_end = lax.broadcasted_iota(jnp.int32, (tq, ncmp), 1) * STRIDE_CMP + (L_CMP - 1)
    ok = blk_end <= tpos
    okf = ok.astype(F32)
    kc = kc_ref[...]
    vc = vc_ref[...]
    psum = jnp.zeros((tq, ncmp), F32)
    for h in range(HPG):
        qh = q_ref[:, h * HEAD_DIM:(h + 1) * HEAD_DIM]
        s = lax.dot_general(qh, kc, (((1,), (1,)), ((), ())), preferred_element_type=F32)
        s = jnp.where(ok, s, NEG_INF)
        e = jnp.exp2(s - jnp.max(s, axis=-1, keepdims=True))
        p = e / jnp.sum(e, axis=-1, keepdims=True) * okf
        o_ref[:, h * HEAD_DIM:(h + 1) * HEAD_DIM] = jnp.dot(p.astype(BF16), vc, preferred_element_type=F32)
        psum = psum + p

    jj = lax.broadcasted_iota(jnp.int32, (nblk, ncmp), 0)
    nn = lax.broadcasted_iota(jnp.int32, (nblk, ncmp), 1)
    ov = ((nn * STRIDE_CMP < (jj + 1) * L_SEL) & (nn * STRIDE_CMP + L_CMP > jj * L_SEL)).astype(BF16)
    p_hi = psum.astype(BF16)
    p_lo = (psum - p_hi.astype(F32)).astype(BF16)
    nt = (((1,), (1,)), ((), ()))
    imp = (lax.dot_general(ov, p_hi, nt, preferred_element_type=F32)
           + lax.dot_general(ov, p_lo, nt, preferred_element_type=F32))

    jb = lax.broadcasted_iota(jnp.int32, (nblk, tq), 0)
    tt = t0 + lax.broadcasted_iota(jnp.int32, (nblk, tq), 1)
    cur = tt // L_SEL
    allowed = jb * L_SEL <= tt
    forced = (jb == 0) | (jb == cur) | (jb == cur - 1)
    score = jnp.where(forced, FORCE_SCORE, jnp.where(allowed, imp, NEG_INF))
    cnt = jnp.zeros((nblk, tq), F32)
    for jp in range(nblk):
        row = score[jp:jp + 1, :]
        tie = jnp.where(jb > jp, 1.0, 0.0)
        cnt = cnt + jnp.where(row > score, 1.0, jnp.where(row == score, tie, 0.0))
    bias_ref[...] = jnp.where(cnt < ntop, 0.0, NEG_INF).astype(bias_ref.dtype)


def _cmp_attn(qkv, kcv, bsz, seq, tq=512):
    tq = min(tq, seq)
    ncmp = seq // STRIDE_CMP
    nblk = seq // L_SEL
    ntop = min(N_SEL, nblk)
    nq = seq // tq
    return pl.pallas_call(
        functools.partial(_cmp_attn_body, tq=tq, ncmp=ncmp, nblk=nblk, ntop=ntop),
        out_shape=(jax.ShapeDtypeStruct((bsz * seq, ATT_WIDTH), F32),
                   jax.ShapeDtypeStruct((bsz, N_KV, nblk, seq), F32)),
        grid=(bsz, N_KV, nq),
        in_specs=[pl.BlockSpec((tq, HPG * HEAD_DIM), lambda b, g, i: (b * nq + i, g)),
                  pl.BlockSpec((None, None, None, ncmp, HEAD_DIM), lambda b, g, i: (b, 0, g, 0, 0)),
                  pl.BlockSpec((None, None, None, ncmp, HEAD_DIM), lambda b, g, i: (b, 1, g, 0, 0))],
        out_specs=(pl.BlockSpec((tq, HPG * HEAD_DIM), lambda b, g, i: (b * nq + i, g)),
                   pl.BlockSpec((None, None, nblk, tq), lambda b, g, i: (b, g, 0, i))),
        compiler_params=_params(("parallel", "parallel", "parallel")),
        name="nsa_cmp_attn_topk",
    )(qkv, kcv, kcv)


def _sel_attn_body(q_ref, k_ref, v_ref, bias_ref, o_ref, qa_scr, m_scr, acc_scr, s_scr, p_scr, alpha_scr,
                   *, tq, tk, rb):
    qi = pl.program_id(2)
    m_scr[...] = jnp.full_like(m_scr, NEG_INF)
    acc_scr[...] = jnp.zeros_like(acc_scr)
    for h in range(HPG):
        qa_scr[h, :, 0:HEAD_DIM] = q_ref[:, h * HEAD_DIM:(h + 1) * HEAD_DIM]
        qa_scr[h, :, HEAD_DIM:2 * HEAD_DIM] = bias_ref[...]
    nt = (((1,), (1,)), ((), ()))

    def tile(kt, diag_off):
        k0 = pl.multiple_of(kt * tk, tk)
        key_blk = kt * (tk // L_SEL) + lax.broadcasted_iota(jnp.int32, (tk, LANES), 0) // L_SEL
        onehot = jnp.where(lax.broadcasted_iota(jnp.int32, (tk, LANES), 1) == key_blk, 1.0, 0.0).astype(BF16)
        k_aug = jnp.concatenate([k_ref[pl.ds(k0, tk), :], onehot], axis=1)
        v_aug = jnp.concatenate([v_ref[pl.ds(k0, tk), :], jnp.ones((tk, LANES), BF16)], axis=1)
        for h in range(HPG):
            s_scr[h] = lax.dot_general(qa_scr[h], k_aug, nt, preferred_element_type=F32)
        for h in range(HPG):
            for r in range(tq // rb):
                rows = slice(r * rb, (r + 1) * rb)
                s = s_scr[h, rows, :]
                if diag_off is not None and diag_off + tk - 1 > r * rb:
                    col = lax.broadcasted_iota(jnp.int32, (rb, tk), 1) + diag_off
                    row = lax.broadcasted_iota(jnp.int32, (rb, tk), 0) + r * rb
                    s = jnp.where(col <= row, s, NEG_INF)
                m_prev = m_scr[h, rows, :]
                m_next = jnp.maximum(m_prev, jnp.max(s, axis=-1, keepdims=True))
                p_scr[h, rows, :] = jnp.exp2(s - pltpu.repeat(m_next, tk // LANES, axis=1)).astype(BF16)
                alpha_scr[h, rows, :] = jnp.exp2(m_prev - m_next)
                m_scr[h, rows, :] = m_next
        for h in range(HPG):
            acc_scr[h] = (pltpu.repeat(alpha_scr[h], 2, axis=1) * acc_scr[h]
                          + jnp.dot(p_scr[h], v_aug, preferred_element_type=F32))

    def full_tile(kt, carry):
        tile(kt, None)
        return carry

    n_full = qi * (tq // tk)
    lax.fori_loop(0, n_full, full_tile, 0)
    for d in range(tq // tk):
        tile(n_full + d, d * tk)
    for h in range(HPG):
        o_ref[:, h * HEAD_DIM:(h + 1) * HEAD_DIM] = (acc_scr[h, :, 0:HEAD_DIM]
                                                     / acc_scr[h, :, HEAD_DIM:2 * HEAD_DIM])


def _sel_attn(qkv, bias, bsz, seq, tq=SEL_TQ, tk=SEL_TK, rb=SEL_RB):
    tq = min(tq, seq)
    tk, rb = min(tk, tq), min(rb, tq)
    nq = seq // tq
    kcol = ATT_WIDTH // HEAD_DIM
    return pl.pallas_call(
        functools.partial(_sel_attn_body, tq=tq, tk=tk, rb=rb),
        out_shape=jax.ShapeDtypeStruct((bsz * seq, ATT_WIDTH), F32),
        grid=(bsz, N_KV, nq),
        in_specs=[pl.BlockSpec((tq, HPG * HEAD_DIM), lambda b, g, i: (b * nq + i, g)),
                  pl.BlockSpec((seq, HEAD_DIM), lambda b, g, i: (b, kcol + g)),
                  pl.BlockSpec((seq, HEAD_DIM), lambda b, g, i: (b, kcol + N_KV + g)),
                  pl.BlockSpec((None, None, tq, LANES), lambda b, g, i: (b, g, i, 0))],
        out_specs=pl.BlockSpec((tq, HPG * HEAD_DIM), lambda b, g, i: (b * nq + i, g)),
        scratch_shapes=[pltpu.VMEM((HPG, tq, 2 * HEAD_DIM), BF16),
                        pltpu.VMEM((HPG, tq, LANES), F32),
                        pltpu.VMEM((HPG, tq, 2 * HEAD_DIM), F32),
                        pltpu.VMEM((HPG, tq, tk), F32),
                        pltpu.VMEM((HPG, tq, tk), BF16),
                        pltpu.VMEM((HPG, tq, LANES), F32)],
        compiler_params=_params(("parallel", "parallel", "arbitrary")),
        name="nsa_sel_attn",
    )(qkv, qkv, qkv, bias)


def _win_attn_body(q_ref, kc_ref, kp_ref, vc_ref, vp_ref, ocmp_ref, osel_ref, gate_ref, o_ref, *, tq):
    qi = pl.program_id(2)
    nt = (((1,), (1,)), ((), ()))
    ones = jnp.ones((tq, LANES), BF16)
    vc_aug = jnp.concatenate([vc_ref[...], ones], axis=1)
    vp_aug = jnp.concatenate([vp_ref[...], ones], axis=1)
    for r in range(tq // ATT_ROWS):
        rows = slice(r * ATT_ROWS, (r + 1) * ATT_ROWS)
        n_cur, p_lo = (r + 1) * ATT_ROWS, r * ATT_ROWS
        row_c = lax.broadcasted_iota(jnp.int32, (ATT_ROWS, n_cur), 0) + r * ATT_ROWS
        ok_cur = lax.broadcasted_iota(jnp.int32, (ATT_ROWS, n_cur), 1) <= row_c
        row_p = lax.broadcasted_iota(jnp.int32, (ATT_ROWS, tq - p_lo), 0) + r * ATT_ROWS
        ok_prev = (lax.broadcasted_iota(jnp.int32, (ATT_ROWS, tq - p_lo), 1) + p_lo > row_p) & (qi > 0)
        gates = jax.nn.sigmoid(gate_ref[rows, :])
        for h in range(HPG):
            sl = slice(h * HEAD_DIM, (h + 1) * HEAD_DIM)
            qh = q_ref[rows, sl]
            s1 = jnp.where(ok_cur, lax.dot_general(qh, kc_ref[0:n_cur, :], nt, preferred_element_type=F32), NEG_INF)
            s0 = jnp.where(ok_prev, lax.dot_general(qh, kp_ref[p_lo:tq, :], nt, preferred_element_type=F32), NEG_INF)
            m = jnp.maximum(jnp.max(s1, axis=-1, keepdims=True), jnp.max(s0, axis=-1, keepdims=True))
            p1 = jnp.exp2(s1 - m).astype(BF16)
            p0 = jnp.exp2(s0 - m).astype(BF16)
            acc = (jnp.dot(p1, vc_aug[0:n_cur, :], preferred_element_type=F32)
                   + jnp.dot(p0, vp_aug[p_lo:tq, :], preferred_element_type=F32))
            ow = acc[:, 0:HEAD_DIM] / acc[:, HEAD_DIM:2 * HEAD_DIM]
            g_cmp = gates[:, h:h + 1]
            g_sel = gates[:, HPG + h:HPG + h + 1]
            g_win = gates[:, 2 * HPG + h:2 * HPG + h + 1]
            o_ref[rows, sl] = (g_cmp * ocmp_ref[rows, sl] + g_sel * osel_ref[rows, sl]
                               + g_win * ow).astype(o_ref.dtype)


def _win_attn_combine(qkv, o_cmp, o_sel, gates, bsz, seq):
    tq = WINDOW
    assert seq % tq == 0
    nq = seq // tq
    kcol = ATT_WIDTH // HEAD_DIM + 2 * N_KV
    qspec = pl.BlockSpec((tq, HPG * HEAD_DIM), lambda b, g, i: (b * nq + i, g))
    cur = lambda c: pl.BlockSpec((tq, HEAD_DIM), lambda b, g, i: (b * nq + i, kcol + c * N_KV + g))
    prev = lambda c: pl.BlockSpec((tq, HEAD_DIM),
                                  lambda b, g, i: (b * nq + jnp.maximum(i - 1, 0), kcol + c * N_KV + g))
    return pl.pallas_call(
        functools.partial(_win_attn_body, tq=tq),
        out_shape=jax.ShapeDtypeStruct((bsz * seq, ATT_WIDTH), BF16),
        grid=(bsz, N_KV, nq),
        in_specs=[qspec, cur(0), prev(0), cur(1), prev(1), qspec, qspec,
                  pl.BlockSpec((None, None, tq, LANES), lambda b, g, i: (b, g, i, 0))],
        out_specs=qspec,
        compiler_params=_params(("parallel", "parallel", "parallel")),
        name="nsa_win_attn_gate",
    )(qkv, qkv, qkv, qkv, qkv, o_cmp, o_sel, gates)


def _merge_body(hn_ref, ya_ref, yb_ref, wga_ref, wgb_ref, wa_ref, wb_ref, o_ref):
    hn = hn_ref[...]
    ga = jnp.dot(hn, wga_ref[...], preferred_element_type=F32)
    gb = jnp.dot(hn, wgb_ref[...], preferred_element_type=F32)
    pa = jnp.dot(ya_ref[...], wa_ref[...], preferred_element_type=F32)
    pb = jnp.dot(yb_ref[...], wb_ref[...], preferred_element_type=F32)
    o_ref[...] = (jax.nn.sigmoid(ga) * pa + jax.nn.sigmoid(gb) * pb).astype(o_ref.dtype)


def _merge(hn, ya, yb, wga, wgb, wa, wb, tm=512, tn=512):
    m, d = hn.shape
    ka = ya.shape[1]
    tm = min(tm, m)
    row = lambda k: pl.BlockSpec((tm, k), lambda i, j: (i, 0))
    col = lambda k: pl.BlockSpec((k, tn), lambda i, j: (0, j))
    return pl.pallas_call(
        _merge_body,
        out_shape=jax.ShapeDtypeStruct((m, d), BF16),
        grid=(m // tm, d // tn),
        in_specs=[row(d), row(ka), row(ka), col(d), col(d), col(ka), col(ka)],
        out_specs=pl.BlockSpec((tm, tn), lambda i, j: (i, j)),
        compiler_params=_params(("parallel", "arbitrary")),
        name="mixer_merge",
    )(hn, ya, yb, wga, wgb, wa, wb)


def _rms(x, g):
    return x * lax.rsqrt(jnp.mean(x * x, axis=-1, keepdims=True) + RMS_EPS) * g


def _res_norm2_body(raw_ref, x_ref, gpost_ref, gpre_ref, h_ref, hn_ref):
    h = x_ref[...] + _rms(raw_ref[...], gpost_ref[...])
    h_ref[...] = h
    hn_ref[...] = _rms(h, gpre_ref[...]).astype(hn_ref.dtype)


def _res_norm2(raw, x, g_post, g_pre, tm=256):
    m, d = x.shape
    tm = min(tm, m)
    rows = pl.BlockSpec((tm, d), lambda i: (i, 0))
    vec = pl.BlockSpec((1, d), lambda i: (0, 0))
    return pl.pallas_call(
        _res_norm2_body,
        out_shape=(jax.ShapeDtypeStruct((m, d), F32), jax.ShapeDtypeStruct((m, d), BF16)),
        grid=(m // tm,),
        in_specs=[rows, rows, vec, vec],
        out_specs=(rows, rows),
        compiler_params=_params(("parallel",)),
        name="residual_norm_prenorm",
    )(raw, x, g_post.reshape(1, d), g_pre.reshape(1, d))


def _res_norm_body(raw_ref, x_ref, g_ref, o_ref):
    o_ref[...] = x_ref[...] + _rms(raw_ref[...], g_ref[...])


def _res_norm(raw, x, g, tm=256):
    m, d = x.shape
    tm = min(tm, m)
    rows = pl.BlockSpec((tm, d), lambda i: (i, 0))
    return pl.pallas_call(
        _res_norm_body,
        out_shape=jax.ShapeDtypeStruct((m, d), F32),
        grid=(m // tm,),
        in_specs=[rows, rows, pl.BlockSpec((1, d), lambda i: (0, 0))],
        out_specs=rows,
        compiler_params=_params(("parallel",)),
        name="residual_norm",
    )(raw, x, g.reshape(1, d))


def _swiglu_body(a_ref, wg_ref, wu_ref, o_ref):
    a = a_ref[...]
    g = jnp.dot(a, wg_ref[...], preferred_element_type=F32)
    u = jnp.dot(a, wu_ref[...], preferred_element_type=F32)
    o_ref[...] = (g * jax.nn.sigmoid(g) * u).astype(o_ref.dtype)


def _swiglu(a, wg, wu, tm=1024, tn=256):
    m, k = a.shape
    n = wg.shape[1]
    tm = min(tm, m)
    assert n % tn == 0
    return pl.pallas_call(
        _swiglu_body,
        out_shape=jax.ShapeDtypeStruct((m, n), BF16),
        grid=(m // tm, n // tn),
        in_specs=[pl.BlockSpec((tm, k), lambda i, j: (i, 0)),
                  pl.BlockSpec((k, tn), lambda i, j: (0, j)),
                  pl.BlockSpec((k, tn), lambda i, j: (0, j))],
        out_specs=pl.BlockSpec((tm, tn), lambda i, j: (i, j)),
        compiler_params=_params(("parallel", "arbitrary")),
        name="ffn_swiglu",
    )(a, wg, wu)


def kernel(x, norm_mix_pre, w_in, ssm_a_re, ssm_a_im, ssm_log_dt, ssm_b_re, ssm_b_im, ssm_c_re, ssm_c_im, ssm_d, ssm_w_glu, ssm_b_glu, cmp_pe_k, cmp_w1_k, cmp_w2_k, cmp_pe_v, cmp_w1_v, cmp_w2_v, w_proj_a, w_proj_b, w_out, norm_mix_post, norm_ffn_pre, w_ffn_gate, w_ffn_up, w_ffn_down, norm_ffn_post):
    bsz, seq, d = x.shape
    m = bsz * seq
    depth = w_in.shape[0]
    h = x.reshape(m, d)
    o_q, o_kvc, o_kv, o_gn = SSM_WIDTH, SSM_WIDTH + ATT_WIDTH, SSM_WIDTH + ATT_WIDTH + 2 * KV_WIDTH, 7168
    o_ga = o_gn + 3 * N_HEADS
    o_gb = o_ga + D_MODEL
    bf = lambda a: a.astype(BF16)
    q_scale = jnp.concatenate([jnp.full((ATT_WIDTH,), HEAD_DIM ** -0.5 * LOG2E, F32),
                               jnp.ones((4 * KV_WIDTH,), F32)])
    for l in range(depth):
        w = w_in[l]
        w_f32 = bf(jnp.concatenate([w[:, 0:o_q], w[:, o_kvc:o_kv]], axis=1))
        w_b16 = bf(jnp.concatenate([w[:, o_q:o_kvc], w[:, o_kv:o_gn]], axis=1))
        w_gn = bf(jnp.pad(w[:, o_gn:o_ga], ((0, 0), (0, LANES - 3 * N_HEADS))))
        w_ga, w_gb = bf(w[:, o_ga:o_gb]), bf(w[:, o_gb:o_gb + D_MODEL])

        hn = _rmsnorm(h, norm_mix_pre[l], BF16)
        proj_f32 = _matmul(hn, w_f32, F32, 1024, 512, name="in_proj_f32")
        qkv = _matmul(hn, w_b16, BF16, 1024, 512, col_scale=q_scale, name="in_proj_bf16")
        gn = _matmul(hn, w_gn, F32, 1024, LANES, name="in_proj_gates")

        y = _s5_mixer(proj_f32.reshape(bsz, seq, -1), ssm_a_re[l], ssm_a_im[l], ssm_log_dt[l],
                      ssm_b_re[l], ssm_b_im[l], ssm_c_re[l], ssm_c_im[l], ssm_d[l])
        y_a = _glu(y.reshape(m, SSM_WIDTH), bf(ssm_w_glu[l]), ssm_b_glu[l])

        pe = jnp.stack([cmp_pe_k[l], cmp_pe_v[l]])
        w1 = bf(jnp.stack([cmp_w1_k[l], cmp_w1_v[l]]))
        w2 = bf(jnp.stack([cmp_w2_k[l], cmp_w2_v[l]]))
        kcv = _compress(proj_f32, SSM_WIDTH, pe, w1, w2, bsz, seq)
        o_cmp, bias_t = _cmp_attn(qkv, kcv, bsz, seq)
        nblk = seq // L_SEL
        bias = bf(jnp.pad(jnp.swapaxes(bias_t, 2, 3), ((0, 0), (0, 0), (0, 0), (0, LANES - nblk))))
        o_sel = _sel_attn(qkv, bias, bsz, seq)
        gates = gn[:, :3 * N_HEADS].reshape(bsz, seq, 3, N_KV, HPG).transpose(0, 3, 1, 2, 4)
        gates = jnp.pad(gates.reshape(bsz, N_KV, seq, 3 * HPG), ((0, 0), (0, 0), (0, 0), (0, LANES - 3 * HPG)))
        y_b = _win_attn_combine(qkv, o_cmp, o_sel, gates, bsz, seq)

        merged = _merge(hn, y_a, y_b, w_ga, w_gb, bf(w_proj_a[l]), bf(w_proj_b[l]))
        mix = _matmul(merged, bf(w_out[l]), F32, 1024, 512, name="out_proj")
        h, hn2 = _res_norm2(mix, h, norm_mix_post[l], norm_ffn_pre[l])

        act = _swiglu(hn2, bf(w_ffn_gate[l]), bf(w_ffn_up[l]))
        f = _matmul(act, bf(w_ffn_down[l]), F32, 512, 512, name="ffn_down")
        h = _res_norm(f, h, norm_ffn_post[l])
    return h.reshape(bsz, seq, d)
```

```python
import functools
import math

import jax
import jax.numpy as jnp
from jax import lax
from jax.experimental import pallas as pl
from jax.experimental.pallas import tpu as pltpu

F32 = jnp.float32
BF16 = jnp.bfloat16

D_MODEL = 4096
SSM_WIDTH = 2048
SSM_GROUP = 16
SSM_GROUPS = 128
SSM_STATE = 64
N_HEADS = 16
HEAD_DIM = 128
N_KV = 4
HPG = 4
ATT_WIDTH = 2048
KV_WIDTH = 512
L_CMP = 32
STRIDE_CMP = 16
L_SEL = 64
N_SEL = 16
WINDOW = 512
D_FF = 11008
RMS_EPS = 1e-6
NEG_INF = -1e30
FORCE_SCORE = 1e9
TAKEN_SCORE = -3e38
LOG2E = math.log2(math.e)

V7X_VMEM_LIMIT_BYTES = 56 * 1024 * 1024
LANES = 128
SUBLANES = 8

S5_GT = 16
S5_NC = S5_GT * SSM_GROUP
S5_NP = S5_GT * SSM_STATE
S5_TILES = SSM_GROUPS // S5_GT
S5_SLABS = 2 * S5_NP // LANES

ATT_ROWS = 128
SEL_TQ, SEL_RB = 512, 64


def _params(sem, vmem=V7X_VMEM_LIMIT_BYTES):
    return pltpu.CompilerParams(dimension_semantics=sem, vmem_limit_bytes=vmem)


def _lane_tile(x, n):
    return jnp.concatenate([x] * n, axis=1)


def _rmsnorm_body(x_ref, g_ref, o_ref):
    x = x_ref[...]
    ms = jnp.mean(x * x, axis=-1, keepdims=True)
    o_ref[...] = (x * lax.rsqrt(ms + RMS_EPS) * g_ref[...]).astype(o_ref.dtype)


def _rmsnorm(x, gain, out_dtype, tm=256):
    m, d = x.shape
    tm = min(tm, m)
    return pl.pallas_call(
        _rmsnorm_body,
        out_shape=jax.ShapeDtypeStruct((m, d), out_dtype),
        grid=(m // tm,),
        in_specs=[pl.BlockSpec((tm, d), lambda i: (i, 0)), pl.BlockSpec((1, d), lambda i: (0, 0))],
        out_specs=pl.BlockSpec((tm, d), lambda i: (i, 0)),
        compiler_params=_params(("parallel",)),
        name="rmsnorm",
    )(x, gain.reshape(1, d))


def _mm_body(a_ref, w_ref, o_ref):
    o_ref[...] = jnp.dot(a_ref[...], w_ref[...], preferred_element_type=F32).astype(o_ref.dtype)


def _matmul(a, w, out_dtype, tm, tn, name="matmul"):
    m, k = a.shape
    n = w.shape[1]
    tm, tn = min(tm, m), min(tn, n)
    assert m % tm == 0 and n % tn == 0
    return pl.pallas_call(
        _mm_body,
        out_shape=jax.ShapeDtypeStruct((m, n), out_dtype),
        grid=(m // tm, n // tn),
        in_specs=[pl.BlockSpec((tm, k), lambda i, j: (i, 0)), pl.BlockSpec((k, tn), lambda i, j: (0, j))],
        out_specs=pl.BlockSpec((tm, tn), lambda i, j: (i, j)),
        compiler_params=_params(("parallel", "arbitrary")),
        name=name,
    )(a, w)


def _mm_f32w_body(a_ref, w_ref, o_ref, wbf_scr, *, scale_fn):
    @pl.when(pl.program_id(1) == 0)
    def _():
        wbf_scr[...] = w_ref[...].astype(BF16)

    acc = jnp.dot(a_ref[...], wbf_scr[...], preferred_element_type=F32)
    if scale_fn is not None:
        acc = acc * scale_fn(pl.program_id(0))
    o_ref[...] = acc.astype(o_ref.dtype)


def _matmul_f32w(a, w, n_out, out_dtype, tm, tn, col_block=None, scale_fn=None, name="matmul_f32w"):
    m, k = a.shape
    tm = min(tm, m)
    assert m % tm == 0 and n_out % tn == 0
    col_block = col_block or (lambda j: j)
    return pl.pallas_call(
        functools.partial(_mm_f32w_body, scale_fn=scale_fn),
        out_shape=jax.ShapeDtypeStruct((m, n_out), out_dtype),
        grid=(n_out // tn, m // tm),
        in_specs=[pl.BlockSpec((tm, k), lambda j, i: (i, 0)),
                  pl.BlockSpec((k, tn), lambda j, i: (0, col_block(j)))],
        out_specs=pl.BlockSpec((tm, tn), lambda j, i: (i, j)),
        scratch_shapes=[pltpu.VMEM((k, tn), BF16)],
        compiler_params=_params(("arbitrary", "arbitrary")),
        name=name,
    )(a, w)


def _s5_body(u_ref, are_ref, aim_ref, ldt_ref, bre_ref, bim_ref, cre_ref, cim_ref, d_ref, y_ref,
             abar_scr, bbar_scr, h_scr, st_scr, *, tc, bsz, tps):
    ti = pl.program_id(1)
    half = S5_SLABS // 2

    @pl.when(ti == 0)
    def _():
        row_tile = lax.broadcasted_iota(jnp.int32, (SUBLANES, S5_NP), 0) // bsz
        abr8 = jnp.zeros((SUBLANES, S5_NP), F32)
        abi8 = jnp.zeros((SUBLANES, S5_NP), F32)
        for k in range(tps):
            cols = slice(k * S5_NP, (k + 1) * S5_NP)
            ar, ai = are_ref[:, cols], aim_ref[:, cols]
            dt = jnp.exp(ldt_ref[:, cols])
            decay = jnp.exp(dt * ar)
            abr, abi = decay * jnp.cos(dt * ai), decay * jnp.sin(dt * ai)
            den = ar * ar + ai * ai
            zr = ((abr - 1.0) * ar + abi * ai) / den
            zi = (abi * ar - (abr - 1.0) * ai) / den
            abr8 = jnp.where(row_tile == k, abr, abr8)
            abi8 = jnp.where(row_tile == k, abi, abi8)
            br, bi = bre_ref[k], bim_ref[k]
            bbar_scr[k, :, 0:S5_NP] = (zr * br - zi * bi).astype(BF16)
            bbar_scr[k, :, S5_NP:2 * S5_NP] = (zr * bi + zi * br).astype(BF16)
        abar_scr[0:SUBLANES, :] = abr8
        abar_scr[SUBLANES:2 * SUBLANES, :] = abi8
        st_scr[...] = jnp.zeros_like(st_scr)

    for k in range(tps):
        cols = slice(k * S5_NC, (k + 1) * S5_NC)
        ub = jnp.concatenate([u_ref[b, :, cols] for b in range(bsz)], axis=0).astype(BF16)
        bu = jnp.dot(ub, bbar_scr[k], preferred_element_type=F32)
        for b in range(bsz):
            for s in range(S5_SLABS):
                h_scr[s, pl.ds(k * bsz + b, tc, stride=SUBLANES), :] = bu[b * tc:(b + 1) * tc,
                                                                        s * LANES:(s + 1) * LANES]

    ar = abar_scr[0:SUBLANES, :]
    ai = abar_scr[SUBLANES:2 * SUBLANES, :]

    def step(t, carry):
        hr, hi = carry
        r0 = pl.multiple_of(t * SUBLANES, SUBLANES)
        bur = jnp.concatenate([h_scr[s, pl.ds(r0, SUBLANES), :] for s in range(half)], axis=1)
        bui = jnp.concatenate([h_scr[half + s, pl.ds(r0, SUBLANES), :] for s in range(half)], axis=1)
        nhr = ar * hr - ai * hi + bur
        nhi = ar * hi + ai * hr + bui
        for s in range(half):
            h_scr[s, pl.ds(r0, SUBLANES), :] = nhr[:, s * LANES:(s + 1) * LANES]
            h_scr[half + s, pl.ds(r0, SUBLANES), :] = nhi[:, s * LANES:(s + 1) * LANES]
        return nhr, nhi

    hr, hi = lax.fori_loop(0, tc, step, (st_scr[0:SUBLANES, :], st_scr[SUBLANES:2 * SUBLANES, :]), unroll=2)
    st_scr[0:SUBLANES, :] = hr
    st_scr[SUBLANES:2 * SUBLANES, :] = hi

    for k in range(tps):
        cols = slice(k * S5_NC, (k + 1) * S5_NC)

        def gather(s0):
            return jnp.concatenate(
                [jnp.concatenate([h_scr[s0 + s, pl.ds(k * bsz + b, tc, stride=SUBLANES), :] for s in range(half)],
                                 axis=1) for b in range(bsz)], axis=0).astype(BF16)

        ch = (jnp.dot(gather(0), cre_ref[k].astype(BF16), preferred_element_type=F32)
              - jnp.dot(gather(half), cim_ref[k].astype(BF16), preferred_element_type=F32))
        for b in range(bsz):
            y = ch[b * tc:(b + 1) * tc, :] + d_ref[:, cols] * u_ref[b, :, cols]
            y_ref[b, :, cols] = jax.nn.gelu(y)


def _s5_mixer(u3, a_re, a_im, log_dt, b_re, b_im, c_re, c_im, d_skip, tc=128):
    bsz, seq, _ = u3.shape
    assert SUBLANES % bsz == 0
    tps = min(SUBLANES // bsz, S5_TILES)
    assert tps * bsz == SUBLANES
    tc = min(tc, seq)
    eye = jnp.eye(S5_GT, dtype=F32)

    def bdiag_b(b):
        bt = b.reshape(S5_TILES, S5_GT, SSM_STATE, SSM_GROUP).transpose(0, 1, 3, 2)
        return jnp.einsum("tgcp,gh->tgchp", bt, eye).reshape(S5_TILES, S5_NC, S5_NP)

    def bdiag_c(c):
        ct = c.reshape(S5_TILES, S5_GT, SSM_GROUP, SSM_STATE)
        return jnp.einsum("tgcp,gh->thpgc", ct, eye).reshape(S5_TILES, S5_NP, S5_NC)

    flat = lambda a: a.reshape(1, SSM_GROUPS * SSM_STATE)
    ldt = jnp.repeat(log_dt, SSM_STATE).reshape(1, SSM_GROUPS * SSM_STATE)
    vec_spec = pl.BlockSpec((1, tps * S5_NP), lambda g, t: (0, g))
    b_spec = pl.BlockSpec((tps, S5_NC, S5_NP), lambda g, t: (g, 0, 0))
    c_spec = pl.BlockSpec((tps, S5_NP, S5_NC), lambda g, t: (g, 0, 0))
    return pl.pallas_call(
        functools.partial(_s5_body, tc=tc, bsz=bsz, tps=tps),
        out_shape=jax.ShapeDtypeStruct((bsz, seq, SSM_WIDTH), F32),
        grid=(S5_TILES // tps, seq // tc),
        in_specs=[
            pl.BlockSpec((bsz, tc, tps * S5_NC), lambda g, t: (0, t, g)),
            vec_spec, vec_spec, vec_spec, b_spec, b_spec, c_spec, c_spec,
            pl.BlockSpec((1, tps * S5_NC), lambda g, t: (0, g)),
        ],
        out_specs=pl.BlockSpec((bsz, tc, tps * S5_NC), lambda g, t: (0, t, g)),
        scratch_shapes=[
            pltpu.VMEM((2 * SUBLANES, S5_NP), F32),
            pltpu.VMEM((tps, S5_NC, 2 * S5_NP), BF16),
            pltpu.VMEM((S5_SLABS, tc * SUBLANES, LANES), F32),
            pltpu.VMEM((2 * SUBLANES, S5_NP), F32),
        ],
        compiler_params=_params(("arbitrary", "arbitrary")),
        name="s5_scan",
    )(u3, flat(a_re), flat(a_im), ldt, bdiag_b(b_re), bdiag_b(b_im), bdiag_c(c_re), bdiag_c(c_im),
      d_skip.reshape(1, SSM_WIDTH))


def _glu_body(y_ref, w_ref, b_ref, o_ref, ybf_scr, *, tn):
    j = pl.program_id(1)

    @pl.when(j == 0)
    def _():
        ybf_scr[...] = y_ref[...].astype(BF16)

    z = jnp.dot(ybf_scr[...], w_ref[...], preferred_element_type=F32) + b_ref[...]
    yt = y_ref[:, pl.ds(pl.multiple_of(j * tn, tn), tn)]
    o_ref[...] = (yt * jax.nn.sigmoid(z)).astype(o_ref.dtype)


def _glu(y, w_bf, bias, tm=1024, tn=1024):
    m, k = y.shape
    tm = min(tm, m)
    return pl.pallas_call(
        functools.partial(_glu_body, tn=tn),
        out_shape=jax.ShapeDtypeStruct((m, k), BF16),
        grid=(m // tm, k // tn),
        in_specs=[pl.BlockSpec((tm, k), lambda i, j: (i, 0)),
                  pl.BlockSpec((k, tn), lambda i, j: (0, j)),
                  pl.BlockSpec((1, tn), lambda i, j: (0, j))],
        out_specs=pl.BlockSpec((tm, tn), lambda i, j: (i, j)),
        scratch_shapes=[pltpu.VMEM((tm, k), BF16)],
        compiler_params=_params(("parallel", "arbitrary")),
        name="s5_glu",
    )(y, w_bf, bias.reshape(1, k))


def _compress_body(x_ref, pe_ref, w1_ref, w2_ref, o_ref, *, ncmp):
    half = L_CMP // 2
    acc_a = jnp.zeros((ncmp, HEAD_DIM), F32)
    acc_b = jnp.zeros((ncmp, HEAD_DIM), F32)
    for r in range(half):
        xr = x_ref[pl.ds(r, ncmp, stride=STRIDE_CMP), :]
        xa = (xr + pe_ref[r:r + 1, :]).astype(BF16)
        xb = (xr + pe_ref[half + r:half + r + 1, :]).astype(BF16)
        acc_a += jnp.dot(xa, w1_ref[r * HEAD_DIM:(r + 1) * HEAD_DIM, :], preferred_element_type=F32)
        acc_b += jnp.dot(xb, w1_ref[(half + r) * HEAD_DIM:(half + r + 1) * HEAD_DIM, :],
                         preferred_element_type=F32)
    pre = acc_a + jnp.concatenate([acc_b[1:], acc_b[:1]], axis=0)
    o_ref[...] = jnp.dot(jax.nn.gelu(pre).astype(BF16), w2_ref[...], preferred_element_type=F32).astype(o_ref.dtype)


def _compress(proj_f32, col0, pe, w1_bf, w2_bf, bsz, seq):
    ncmp = seq // STRIDE_CMP
    cb = col0 // HEAD_DIM
    return pl.pallas_call(
        functools.partial(_compress_body, ncmp=ncmp),
        out_shape=jax.ShapeDtypeStruct((bsz, 2, N_KV, ncmp, HEAD_DIM), BF16),
        grid=(bsz, 2, N_KV),
        in_specs=[pl.BlockSpec((seq, HEAD_DIM), lambda b, w, g: (b, cb + w * N_KV + g)),
                  pl.BlockSpec((None, L_CMP, HEAD_DIM), lambda b, w, g: (w, 0, 0)),
                  pl.BlockSpec((None, L_CMP * HEAD_DIM, HEAD_DIM), lambda b, w, g: (w, 0, 0)),
                  pl.BlockSpec((None, HEAD_DIM, HEAD_DIM), lambda b, w, g: (w, 0, 0))],
        out_specs=pl.BlockSpec((None, None, None, ncmp, HEAD_DIM), lambda b, w, g: (b, w, g, 0, 0)),
        compiler_params=_params(("parallel", "parallel", "parallel")),
        name="nsa_compress",
    )(proj_f32, pe, w1_bf, w2_bf)


def _cmp_attn_body(q_ref, kc_ref, vc_ref, o_ref, bias_ref, *, tq, ncmp, nblk, ntop):
    qi = pl.program_id(2)
    t0 = qi * tq
    tpos = t0 + lax.broadcasted_iota(jnp.int32, (tq, ncmp), 0)
    blk_end = lax.broadcasted_iota(jnp.int32, (tq, ncmp), 1) * STRIDE_CMP + (L_CMP - 1)
    ok = blk_end <= tpos
    okf = ok.astype(F32)
    kc = kc_ref[...]
    vc = vc_ref[...]
    psum = jnp.zeros((tq, ncmp), F32)
    for h in range(HPG):
        qh = q_ref[:, h * HEAD_DIM:(h + 1) * HEAD_DIM]
        s = lax.dot_general(qh, kc, (((1,), (1,)), ((), ())), preferred_element_type=F32)
        s = jnp.where(ok, s, NEG_INF)
        e = jnp.exp2(s - jnp.max(s, axis=-1, keepdims=True))
        p = e / jnp.sum(e, axis=-1, keepdims=True) * okf
        o_ref[:, h * HEAD_DIM:(h + 1) * HEAD_DIM] = jnp.dot(p.astype(BF16), vc, preferred_element_type=F32)
        psum = psum + p

    jj = lax.broadcasted_iota(jnp.int32, (nblk, ncmp), 0)
    nn = lax.broadcasted_iota(jnp.int32, (nblk, ncmp), 1)
    ov = ((nn * STRIDE_CMP < (jj + 1) * L_SEL) & (nn * STRIDE_CMP + L_CMP > jj * L_SEL)).astype(BF16)
    p_hi = psum.astype(BF16)
    p_lo = (psum - p_hi.astype(F32)).astype(BF16)
    nt = (((1,), (1,)), ((), ()))
    imp = (lax.dot_general(ov, p_hi, nt, preferred_element_type=F32)
           + lax.dot_general(ov, p_lo, nt, preferred_element_type=F32))

    jb = lax.broadcasted_iota(jnp.int32, (nblk, tq), 0)
    tt = t0 + lax.broadcasted_iota(jnp.int32, (nblk, tq), 1)
    cur = tt // L_SEL
    allowed = jb * L_SEL <= tt
    forced = (jb == 0) | (jb == cur) | (jb == cur - 1)
    score = jnp.where(forced, FORCE_SCORE, jnp.where(allowed, imp, NEG_INF))
    taken = jnp.zeros((nblk, tq), F32)
    for _ in range(ntop):
        best = jnp.max(score, axis=0, keepdims=True)
        first = jnp.min(jnp.where(score == best, jb, nblk), axis=0, keepdims=True)
        pick = jb == first
        taken = jnp.where(pick, 1.0, taken)
        score = jnp.where(pick, TAKEN_SCORE, score)
    bias_ref[...] = jnp.where(taken > 0.5, 0.0, NEG_INF).astype(bias_ref.dtype)


def _cmp_attn(qkv, kcv, bsz, seq, tq=512):
    tq = min(tq, seq)
    ncmp = seq // STRIDE_CMP
    nblk = seq // L_SEL
    ntop = min(N_SEL, nblk)
    nq = seq // tq
    return pl.pallas_call(
        functools.partial(_cmp_attn_body, tq=tq, ncmp=ncmp, nblk=nblk, ntop=ntop),
        out_shape=(jax.ShapeDtypeStruct((bsz * seq, ATT_WIDTH), F32),
                   jax.ShapeDtypeStruct((bsz, N_KV, nblk, seq), F32)),
        grid=(bsz, N_KV, nq),
        in_specs=[pl.BlockSpec((tq, HPG * HEAD_DIM), lambda b, g, i: (b * nq + i, g)),
                  pl.BlockSpec((None, None, None, ncmp, HEAD_DIM), lambda b, g, i: (b, 0, g, 0, 0)),
                  pl.BlockSpec((None, None, None, ncmp, HEAD_DIM), lambda b, g, i: (b, 1, g, 0, 0))],
        out_specs=(pl.BlockSpec((tq, HPG * HEAD_DIM), lambda b, g, i: (b * nq + i, g)),
                   pl.BlockSpec((None, None, nblk, tq), lambda b, g, i: (b, g, 0, i))),
        compiler_params=_params(("parallel", "parallel", "parallel")),
        name="nsa_cmp_attn_topk",
    )(qkv, kcv, kcv)


def _sel_attn_body(q_ref, k_ref, v_ref, bias_ref, o_ref, qa_scr, m_scr, acc_scr, s_scr, p_scr, alpha_scr,
                   *, tq, rb):
    qi = pl.program_id(2)
    tk = tq
    m_scr[...] = jnp.full_like(m_scr, NEG_INF)
    acc_scr[...] = jnp.zeros_like(acc_scr)
    for h in range(HPG):
        qa_scr[h, :, 0:HEAD_DIM] = q_ref[:, h * HEAD_DIM:(h + 1) * HEAD_DIM]
        qa_scr[h, :, HEAD_DIM:2 * HEAD_DIM] = bias_ref[...]
    nt = (((1,), (1,)), ((), ()))

    def key_tile(kt):
        k0 = pl.multiple_of(kt * tk, tk)
        key_blk = kt * (tk // L_SEL) + lax.broadcasted_iota(jnp.int32, (tk, LANES), 0) // L_SEL
        onehot = jnp.where(lax.broadcasted_iota(jnp.int32, (tk, LANES), 1) == key_blk, 1.0, 0.0).astype(BF16)
        return jnp.concatenate([k_ref[pl.ds(k0, tk), :], onehot], axis=1)

    def scores(h, k_aug, s_buf):
        s_buf[h] = lax.dot_general(qa_scr[h], k_aug, nt, preferred_element_type=F32)

    def softmax(h, s_buf, diagonal):
        for r in range(tq // rb):
            rows = slice(r * rb, (r + 1) * rb)
            s = s_buf[h, rows, :]
            if diagonal:
                col = lax.broadcasted_iota(jnp.int32, (rb, tk), 1)
                row = lax.broadcasted_iota(jnp.int32, (rb, tk), 0) + r * rb
                s = jnp.where(col <= row, s, NEG_INF)
            m_prev = m_scr[h, rows, :]
            m_next = jnp.maximum(m_prev, jnp.max(s, axis=-1, keepdims=True))
            p_scr[h, rows, :] = jnp.exp2(s - _lane_tile(m_next, tk // LANES)).astype(BF16)
            alpha_scr[h, rows, :] = jnp.exp2(m_prev - m_next)
            m_scr[h, rows, :] = m_next

    def values(h, v_aug):
        acc_scr[h] = (_lane_tile(alpha_scr[h], 2) * acc_scr[h]
                      + jnp.dot(p_scr[h], v_aug, preferred_element_type=F32))

    def tile(kt, diagonal):
        k0 = pl.multiple_of(kt * tk, tk)
        k_aug = key_tile(kt)
        v_aug = jnp.concatenate([v_ref[pl.ds(k0, tk), :], jnp.ones((tk, LANES), BF16)], axis=1)
        for h in range(HPG):
            scores(h, k_aug, s_scr)
        for h in range(HPG):
            softmax(h, s_scr, diagonal)
        for h in range(HPG):
            values(h, v_aug)

    def full_tile(kt, carry):
        tile(kt, False)
        return carry

    lax.fori_loop(0, qi, full_tile, 0)
    tile(qi, True)

    for h in range(HPG):
        o_ref[:, h * HEAD_DIM:(h + 1) * HEAD_DIM] = (acc_scr[h, :, 0:HEAD_DIM]
                                                     / acc_scr[h, :, HEAD_DIM:2 * HEAD_DIM])


def _sel_attn(qkv, bias, bsz, seq, tq=SEL_TQ, rb=SEL_RB):
    tq = min(tq, seq)
    rb = min(rb, tq)
    nq = seq // tq
    kcol = ATT_WIDTH // HEAD_DIM
    return pl.pallas_call(
        functools.partial(_sel_attn_body, tq=tq, rb=rb),
        out_shape=jax.ShapeDtypeStruct((bsz * seq, ATT_WIDTH), F32),
        grid=(bsz, N_KV, nq),
        in_specs=[pl.BlockSpec((tq, HPG * HEAD_DIM), lambda b, g, i: (b * nq + i, g)),
                  pl.BlockSpec((seq, HEAD_DIM), lambda b, g, i: (b, kcol + g)),
                  pl.BlockSpec((seq, HEAD_DIM), lambda b, g, i: (b, kcol + N_KV + g)),
                  pl.BlockSpec((None, None, tq, LANES), lambda b, g, i: (b, g, i, 0))],
        out_specs=pl.BlockSpec((tq, HPG * HEAD_DIM), lambda b, g, i: (b * nq + i, g)),
        scratch_shapes=[pltpu.VMEM((HPG, tq, 2 * HEAD_DIM), BF16),
                        pltpu.VMEM((HPG, tq, LANES), F32),
                        pltpu.VMEM((HPG, tq, 2 * HEAD_DIM), F32),
                        pltpu.VMEM((HPG, tq, tq), F32),
                        pltpu.VMEM((HPG, tq, tq), BF16),
                        pltpu.VMEM((HPG, tq, LANES), F32)],
        compiler_params=_params(("parallel", "parallel", "arbitrary")),
        name="nsa_sel_attn",
    )(qkv, qkv, qkv, bias)


def _win_attn_body(q_ref, kc_ref, kp_ref, vc_ref, vp_ref, ocmp_ref, osel_ref, gate_ref, o_ref, *, tq):
    qi = pl.program_id(2)
    nt = (((1,), (1,)), ((), ()))
    ones = jnp.ones((tq, LANES), BF16)
    vc_aug = jnp.concatenate([vc_ref[...], ones], axis=1)
    vp_aug = jnp.concatenate([vp_ref[...], ones], axis=1)
    for r in range(tq // ATT_ROWS):
        rows = slice(r * ATT_ROWS, (r + 1) * ATT_ROWS)
        n_cur, p_lo = (r + 1) * ATT_ROWS, r * ATT_ROWS
        row_c = lax.broadcasted_iota(jnp.int32, (ATT_ROWS, n_cur), 0) + r * ATT_ROWS
        ok_cur = lax.broadcasted_iota(jnp.int32, (ATT_ROWS, n_cur), 1) <= row_c
        row_p = lax.broadcasted_iota(jnp.int32, (ATT_ROWS, tq - p_lo), 0) + r * ATT_ROWS
        ok_prev = (lax.broadcasted_iota(jnp.int32, (ATT_ROWS, tq - p_lo), 1) + p_lo > row_p) & (qi > 0)
        gates = jax.nn.sigmoid(gate_ref[rows, :])
        for h in range(HPG):
            sl = slice(h * HEAD_DIM, (h + 1) * HEAD_DIM)
            qh = q_ref[rows, sl]
            s1 = jnp.where(ok_cur, lax.dot_general(qh, kc_ref[0:n_cur, :], nt, preferred_element_type=F32), NEG_INF)
            s0 = jnp.where(ok_prev, lax.dot_general(qh, kp_ref[p_lo:tq, :], nt, preferred_element_type=F32), NEG_INF)
            m = jnp.maximum(jnp.max(s1, axis=-1, keepdims=True), jnp.max(s0, axis=-1, keepdims=True))
            p1 = jnp.exp2(s1 - m).astype(BF16)
            p0 = jnp.exp2(s0 - m).astype(BF16)
            acc = (jnp.dot(p1, vc_aug[0:n_cur, :], preferred_element_type=F32)
                   + jnp.dot(p0, vp_aug[p_lo:tq, :], preferred_element_type=F32))
            ow = acc[:, 0:HEAD_DIM] / acc[:, HEAD_DIM:2 * HEAD_DIM]
            g_cmp = gates[:, h:h + 1]
            g_sel = gates[:, HPG + h:HPG + h + 1]
            g_win = gates[:, 2 * HPG + h:2 * HPG + h + 1]
            o_ref[rows, sl] = (g_cmp * ocmp_ref[rows, sl] + g_sel * osel_ref[rows, sl]
                               + g_win * ow).astype(o_ref.dtype)


def _win_attn_combine(qkv, o_cmp, o_sel, gates, bsz, seq):
    tq = WINDOW
    assert seq % tq == 0
    nq = seq // tq
    kcol = ATT_WIDTH // HEAD_DIM + 2 * N_KV
    qspec = pl.BlockSpec((tq, HPG * HEAD_DIM), lambda b, g, i: (b * nq + i, g))
    cur = lambda c: pl.BlockSpec((tq, HEAD_DIM), lambda b, g, i: (b * nq + i, kcol + c * N_KV + g))
    prev = lambda c: pl.BlockSpec((tq, HEAD_DIM),
                                  lambda b, g, i: (b * nq + jnp.maximum(i - 1, 0), kcol + c * N_KV + g))
    return pl.pallas_call(
        functools.partial(_win_attn_body, tq=tq),
        out_shape=jax.ShapeDtypeStruct((bsz * seq, ATT_WIDTH), BF16),
        grid=(bsz, N_KV, nq),
        in_specs=[qspec, cur(0), prev(0), cur(1), prev(1), qspec, qspec,
                  pl.BlockSpec((None, None, tq, LANES), lambda b, g, i: (b, g, i, 0))],
        out_specs=qspec,
        compiler_params=_params(("parallel", "parallel", "parallel")),
        name="nsa_win_attn_gate",
    )(qkv, qkv, qkv, qkv, qkv, o_cmp, o_sel, gates)


def _merge_body(hn_ref, ya_ref, yb_ref, wga_ref, wgb_ref, wa_ref, wb_ref, o_ref):
    hn = hn_ref[...]
    ga = jnp.dot(hn, wga_ref[...], preferred_element_type=F32)
    gb = jnp.dot(hn, wgb_ref[...], preferred_element_type=F32)
    pa = jnp.dot(ya_ref[...], wa_ref[...], preferred_element_type=F32)
    pb = jnp.dot(yb_ref[...], wb_ref[...], preferred_element_type=F32)
    o_ref[...] = (jax.nn.sigmoid(ga) * pa + jax.nn.sigmoid(gb) * pb).astype(o_ref.dtype)


def _merge(hn, ya, yb, wga, wgb, wa, wb, tm=512, tn=512):
    m, d = hn.shape
    ka = ya.shape[1]
    tm = min(tm, m)
    row = lambda k: pl.BlockSpec((tm, k), lambda i, j: (i, 0))
    col = lambda k: pl.BlockSpec((k, tn), lambda i, j: (0, j))
    return pl.pallas_call(
        _merge_body,
        out_shape=jax.ShapeDtypeStruct((m, d), BF16),
        grid=(m // tm, d // tn),
        in_specs=[row(d), row(ka), row(ka), col(d), col(d), col(ka), col(ka)],
        out_specs=pl.BlockSpec((tm, tn), lambda i, j: (i, j)),
        compiler_params=_params(("parallel", "arbitrary")),
        name="mixer_merge",
    )(hn, ya, yb, wga, wgb, wa, wb)


def _rms(x, g):
    return x * lax.rsqrt(jnp.mean(x * x, axis=-1, keepdims=True) + RMS_EPS) * g


def _res_norm2_body(raw_ref, x_ref, gpost_ref, gpre_ref, h_ref, hn_ref):
    h = x_ref[...] + _rms(raw_ref[...], gpost_ref[...])
    h_ref[...] = h
    hn_ref[...] = _rms(h, gpre_ref[...]).astype(hn_ref.dtype)


def _res_norm2(raw, x, g_post, g_pre, tm=256):
    m, d = x.shape
    tm = min(tm, m)
    rows = pl.BlockSpec((tm, d), lambda i: (i, 0))
    vec = pl.BlockSpec((1, d), lambda i: (0, 0))
    return pl.pallas_call(
        _res_norm2_body,
        out_shape=(jax.ShapeDtypeStruct((m, d), F32), jax.ShapeDtypeStruct((m, d), BF16)),
        grid=(m // tm,),
        in_specs=[rows, rows, vec, vec],
        out_specs=(rows, rows),
        compiler_params=_params(("parallel",)),
        name="residual_norm_prenorm",
    )(raw, x, g_post.reshape(1, d), g_pre.reshape(1, d))


def _res_norm_body(raw_ref, x_ref, g_ref, o_ref):
    o_ref[...] = x_ref[...] + _rms(raw_ref[...], g_ref[...])


def _res_norm(raw, x, g, tm=256):
    m, d = x.shape
    tm = min(tm, m)
    rows = pl.BlockSpec((tm, d), lambda i: (i, 0))
    return pl.pallas_call(
        _res_norm_body,
        out_shape=jax.ShapeDtypeStruct((m, d), F32),
        grid=(m // tm,),
        in_specs=[rows, rows, pl.BlockSpec((1, d), lambda i: (0, 0))],
        out_specs=rows,
        compiler_params=_params(("parallel",)),
        name="residual_norm",
    )(raw, x, g.reshape(1, d))


def _swiglu_body(a_ref, wg_ref, wu_ref, o_ref, wg_scr, wu_scr):
    @pl.when(pl.program_id(1) == 0)
    def _():
        wg_scr[...] = wg_ref[...].astype(BF16)
        wu_scr[...] = wu_ref[...].astype(BF16)

    a = a_ref[...]
    g = jnp.dot(a, wg_scr[...], preferred_element_type=F32)
    u = jnp.dot(a, wu_scr[...], preferred_element_type=F32)
    o_ref[...] = (g * jax.nn.sigmoid(g) * u).astype(o_ref.dtype)


def _swiglu(a, wg, wu, tm=1024, tn=256):
    m, k = a.shape
    n = wg.shape[1]
    tm = min(tm, m)
    assert n % tn == 0
    return pl.pallas_call(
        _swiglu_body,
        out_shape=jax.ShapeDtypeStruct((m, n), BF16),
        grid=(n // tn, m // tm),
        in_specs=[pl.BlockSpec((tm, k), lambda j, i: (i, 0)),
                  pl.BlockSpec((k, tn), lambda j, i: (0, j)),
                  pl.BlockSpec((k, tn), lambda j, i: (0, j))],
        out_specs=pl.BlockSpec((tm, tn), lambda j, i: (i, j)),
        scratch_shapes=[pltpu.VMEM((k, tn), BF16), pltpu.VMEM((k, tn), BF16)],
        compiler_params=_params(("arbitrary", "arbitrary")),
        name="ffn_swiglu",
    )(a, wg, wu)


def kernel(x, norm_mix_pre, w_in, ssm_a_re, ssm_a_im, ssm_log_dt, ssm_b_re, ssm_b_im, ssm_c_re, ssm_c_im, ssm_d, ssm_w_glu, ssm_b_glu, cmp_pe_k, cmp_w1_k, cmp_w2_k, cmp_pe_v, cmp_w1_v, cmp_w2_v, w_proj_a, w_proj_b, w_out, norm_mix_post, norm_ffn_pre, w_ffn_gate, w_ffn_up, w_ffn_down, norm_ffn_post):
    bsz, seq, d = x.shape
    m = bsz * seq
    depth = w_in.shape[0]
    h = x.reshape(m, d)
    o_q, o_kvc, o_kv, o_gn = SSM_WIDTH, SSM_WIDTH + ATT_WIDTH, SSM_WIDTH + ATT_WIDTH + 2 * KV_WIDTH, 7168
    o_ga = o_gn + 3 * N_HEADS
    o_gb = o_ga + D_MODEL
    bf = lambda a: a.astype(BF16)
    tn_in = 512
    nb_u, nb_q = SSM_WIDTH // tn_in, ATT_WIDTH // tn_in
    qk_scale = HEAD_DIM ** -0.5 * LOG2E
    for l in range(depth):
        w = w_in[l]
        w_gn = bf(jnp.pad(w[:, o_gn:o_ga], ((0, 0), (0, LANES - 3 * N_HEADS))))
        w_ga, w_gb = bf(w[:, o_ga:o_gb]), bf(w[:, o_gb:o_gb + D_MODEL])

        hn = _rmsnorm(h, norm_mix_pre[l], BF16)
        proj_f32 = _matmul_f32w(hn, w, SSM_WIDTH + 2 * KV_WIDTH, F32, 1024, tn_in,
                                col_block=lambda j: jnp.where(j < nb_u, j, j - nb_u + o_kvc // tn_in),
                                name="in_proj_f32")
        qkv = _matmul_f32w(hn, w, ATT_WIDTH + 4 * KV_WIDTH, BF16, 1024, tn_in,
                           col_block=lambda j: jnp.where(j < nb_q, j + o_q // tn_in, j - nb_q + o_kv // tn_in),
                           scale_fn=lambda j: jnp.where(j < nb_q, qk_scale, 1.0),
                           name="in_proj_bf16")
        gn = _matmul(hn, w_gn, F32, 1024, LANES, name="in_proj_gates")

        y = _s5_mixer(proj_f32.reshape(bsz, seq, -1), ssm_a_re[l], ssm_a_im[l], ssm_log_dt[l],
                      ssm_b_re[l], ssm_b_im[l], ssm_c_re[l], ssm_c_im[l], ssm_d[l])
        y_a = _glu(y.reshape(m, SSM_WIDTH), bf(ssm_w_glu[l]), ssm_b_glu[l])

        pe = jnp.stack([cmp_pe_k[l], cmp_pe_v[l]])
        w1 = bf(jnp.stack([cmp_w1_k[l], cmp_w1_v[l]]))
        w2 = bf(jnp.stack([cmp_w2_k[l], cmp_w2_v[l]]))
        kcv = _compress(proj_f32, SSM_WIDTH, pe, w1, w2, bsz, seq)
        o_cmp, bias_t = _cmp_attn(qkv, kcv, bsz, seq)
        nblk = seq // L_SEL
        bias = bf(jnp.pad(jnp.swapaxes(bias_t, 2, 3), ((0, 0), (0, 0), (0, 0), (0, LANES - nblk))))
        o_sel = _sel_attn(qkv, bias, bsz, seq)
        gates = gn[:, :3 * N_HEADS].reshape(bsz, seq, 3, N_KV, HPG).transpose(0, 3, 1, 2, 4)
        gates = jnp.pad(gates.reshape(bsz, N_KV, seq, 3 * HPG), ((0, 0), (0, 0), (0, 0), (0, LANES - 3 * HPG)))
        y_b = _win_attn_combine(qkv, o_cmp, o_sel, gates, bsz, seq)

        merged = _merge(hn, y_a, y_b, w_ga, w_gb, bf(w_proj_a[l]), bf(w_proj_b[l]))
        mix = _matmul_f32w(merged, w_out[l], D_MODEL, F32, 1024, 512, name="out_proj")
        h, hn2 = _res_norm2(mix, h, norm_mix_post[l], norm_ffn_pre[l])

        act = _swiglu(hn2, w_ffn_gate[l], w_ffn_up[l])
        f = _matmul(act, bf(w_ffn_down[l]), F32, 512, 512, name="ffn_down")
        h = _res_norm(f, h, norm_ffn_post[l])
    return h.reshape(bsz, seq, d)
```

```python
import functools
import math

import jax
import jax.numpy as jnp
from jax import lax
from jax.experimental import pallas as pl
from jax.experimental.pallas import tpu as pltpu

F32 = jnp.float32
BF16 = jnp.bfloat16

D_MODEL = 4096
SSM_WIDTH = 2048
SSM_GROUP = 16
SSM_GROUPS = 128
SSM_STATE = 64
N_HEADS = 16
HEAD_DIM = 128
N_KV = 4
HPG = 4
ATT_WIDTH = 2048
KV_WIDTH = 512
L_CMP = 32
STRIDE_CMP = 16
L_SEL = 64
N_SEL = 16
WINDOW = 512
D_FF = 11008
RMS_EPS = 1e-6
NEG_INF = -1e30
FORCE_SCORE = 1e9
TAKEN_SCORE = -3e38
LOG2E = math.log2(math.e)

V7X_VMEM_LIMIT_BYTES = 56 * 1024 * 1024
LANES = 128
SUBLANES = 8

S5_GT = 16
S5_NC = S5_GT * SSM_GROUP
S5_NP = S5_GT * SSM_STATE
S5_TILES = SSM_GROUPS // S5_GT
S5_SLABS = 2 * S5_NP // LANES

ATT_ROWS = 128
SEL_TQ, SEL_RB = 512, 64


def _params(sem, vmem=V7X_VMEM_LIMIT_BYTES):
    return pltpu.CompilerParams(dimension_semantics=sem, vmem_limit_bytes=vmem)


def _lane_tile(x, n):
    return jnp.concatenate([x] * n, axis=1)


def _rmsnorm_body(x_ref, g_ref, o_ref):
    x = x_ref[...]
    ms = jnp.mean(x * x, axis=-1, keepdims=True)
    o_ref[...] = (x * lax.rsqrt(ms + RMS_EPS) * g_ref[...]).astype(o_ref.dtype)


def _rmsnorm(x, gain, out_dtype, tm=256):
    m, d = x.shape
    tm = min(tm, m)
    return pl.pallas_call(
        _rmsnorm_body,
        out_shape=jax.ShapeDtypeStruct((m, d), out_dtype),
        grid=(m // tm,),
        in_specs=[pl.BlockSpec((tm, d), lambda i: (i, 0)), pl.BlockSpec((1, d), lambda i: (0, 0))],
        out_specs=pl.BlockSpec((tm, d), lambda i: (i, 0)),
        compiler_params=_params(("parallel",)),
        name="rmsnorm",
    )(x, gain.reshape(1, d))


def _mm_body(a_ref, w_ref, o_ref):
    o_ref[...] = jnp.dot(a_ref[...], w_ref[...], preferred_element_type=F32).astype(o_ref.dtype)


def _matmul(a, w, out_dtype, tm, tn, name="matmul"):
    m, k = a.shape
    n = w.shape[1]
    tm, tn = min(tm, m), min(tn, n)
    assert m % tm == 0 and n % tn == 0
    return pl.pallas_call(
        _mm_body,
        out_shape=jax.ShapeDtypeStruct((m, n), out_dtype),
        grid=(m // tm, n // tn),
        in_specs=[pl.BlockSpec((tm, k), lambda i, j: (i, 0)), pl.BlockSpec((k, tn), lambda i, j: (0, j))],
        out_specs=pl.BlockSpec((tm, tn), lambda i, j: (i, j)),
        compiler_params=_params(("parallel", "arbitrary")),
        name=name,
    )(a, w)


def _mm_f32w_body(a_ref, w_ref, o_ref, wbf_scr, *, scale_fn):
    @pl.when(pl.program_id(1) == 0)
    def _():
        wbf_scr[...] = w_ref[...].astype(BF16)

    acc = jnp.dot(a_ref[...], wbf_scr[...], preferred_element_type=F32)
    if scale_fn is not None:
        acc = acc * scale_fn(pl.program_id(0))
    o_ref[...] = acc.astype(o_ref.dtype)


def _matmul_f32w(a, w, n_out, out_dtype, tm, tn, col_block=None, scale_fn=None, name="matmul_f32w"):
    m, k = a.shape
    tm = min(tm, m)
    assert m % tm == 0 and n_out % tn == 0
    col_block = col_block or (lambda j: j)
    return pl.pallas_call(
        functools.partial(_mm_f32w_body, scale_fn=scale_fn),
        out_shape=jax.ShapeDtypeStruct((m, n_out), out_dtype),
        grid=(n_out // tn, m // tm),
        in_specs=[pl.BlockSpec((tm, k), lambda j, i: (i, 0)),
                  pl.BlockSpec((k, tn), lambda j, i: (0, col_block(j)))],
        out_specs=pl.BlockSpec((tm, tn), lambda j, i: (i, j)),
        scratch_shapes=[pltpu.VMEM((k, tn), BF16)],
        compiler_params=_params(("arbitrary", "arbitrary")),
        name=name,
    )(a, w)


def _shifted_cast_body(a_ref, b_ref, o_ref, *, shift):
    x = jnp.concatenate([a_ref[...], b_ref[...]], axis=1)
    o_ref[...] = x[:, shift:shift + o_ref.shape[1]].astype(o_ref.dtype)


def _shifted_cast(w, col0, n_out, out_dtype, tr=512, tc=1024):
    rows = w.shape[0]
    base, shift = (col0 // tc) * tc, col0 % tc
    assert shift < LANES and base % tc == 0 and n_out % tc == 0 and rows % tr == 0
    return pl.pallas_call(
        functools.partial(_shifted_cast_body, shift=shift),
        out_shape=jax.ShapeDtypeStruct((rows, n_out), out_dtype),
        grid=(rows // tr, n_out // tc),
        in_specs=[pl.BlockSpec((tr, tc), lambda i, c: (i, base // tc + c)),
                  pl.BlockSpec((tr, LANES), lambda i, c: (i, (base + (c + 1) * tc) // LANES))],
        out_specs=pl.BlockSpec((tr, tc), lambda i, c: (i, c)),
        compiler_params=_params(("parallel", "parallel")),
        name="gate_weight_cast",
    )(w, w)


def _s5_body(u_ref, are_ref, aim_ref, ldt_ref, bre_ref, bim_ref, cre_ref, cim_ref, d_ref, y_ref,
             abar_scr, bbar_scr, h_scr, st_scr, *, tc, bsz, tps):
    ti = pl.program_id(1)
    half = S5_SLABS // 2

    @pl.when(ti == 0)
    def _():
        row_tile = lax.broadcasted_iota(jnp.int32, (SUBLANES, S5_NP), 0) // bsz
        abr8 = jnp.zeros((SUBLANES, S5_NP), F32)
        abi8 = jnp.zeros((SUBLANES, S5_NP), F32)
        for k in range(tps):
            cols = slice(k * S5_NP, (k + 1) * S5_NP)
            ar, ai = are_ref[:, cols], aim_ref[:, cols]
            dt = jnp.exp(ldt_ref[:, cols])
            decay = jnp.exp(dt * ar)
            abr, abi = decay * jnp.cos(dt * ai), decay * jnp.sin(dt * ai)
            den = ar * ar + ai * ai
            zr = ((abr - 1.0) * ar + abi * ai) / den
            zi = (abi * ar - (abr - 1.0) * ai) / den
            abr8 = jnp.where(row_tile == k, abr, abr8)
            abi8 = jnp.where(row_tile == k, abi, abi8)
            br, bi = bre_ref[k], bim_ref[k]
            bbar_scr[k, :, 0:S5_NP] = (zr * br - zi * bi).astype(BF16)
            bbar_scr[k, :, S5_NP:2 * S5_NP] = (zr * bi + zi * br).astype(BF16)
        abar_scr[0:SUBLANES, :] = abr8
        abar_scr[SUBLANES:2 * SUBLANES, :] = abi8
        st_scr[...] = jnp.zeros_like(st_scr)

    for k in range(tps):
        cols = slice(k * S5_NC, (k + 1) * S5_NC)
        ub = jnp.concatenate([u_ref[b, :, cols] for b in range(bsz)], axis=0).astype(BF16)
        bu = jnp.dot(ub, bbar_scr[k], preferred_element_type=F32)
        for b in range(bsz):
            for s in range(S5_SLABS):
                h_scr[s, pl.ds(k * bsz + b, tc, stride=SUBLANES), :] = bu[b * tc:(b + 1) * tc,
                                                                        s * LANES:(s + 1) * LANES]

    ar = abar_scr[0:SUBLANES, :]
    ai = abar_scr[SUBLANES:2 * SUBLANES, :]

    def step(t, carry):
        hr, hi = carry
        r0 = pl.multiple_of(t * SUBLANES, SUBLANES)
        bur = jnp.concatenate([h_scr[s, pl.ds(r0, SUBLANES), :] for s in range(half)], axis=1)
        bui = jnp.concatenate([h_scr[half + s, pl.ds(r0, SUBLANES), :] for s in range(half)], axis=1)
        nhr = ar * hr - ai * hi + bur
        nhi = ar * hi + ai * hr + bui
        for s in range(half):
            h_scr[s, pl.ds(r0, SUBLANES), :] = nhr[:, s * LANES:(s + 1) * LANES]
            h_scr[half + s, pl.ds(r0, SUBLANES), :] = nhi[:, s * LANES:(s + 1) * LANES]
        return nhr, nhi

    hr, hi = lax.fori_loop(0, tc, step, (st_scr[0:SUBLANES, :], st_scr[SUBLANES:2 * SUBLANES, :]), unroll=2)
    st_scr[0:SUBLANES, :] = hr
    st_scr[SUBLANES:2 * SUBLANES, :] = hi

    for k in range(tps):
        cols = slice(k * S5_NC, (k + 1) * S5_NC)

        def gather(s0):
            return jnp.concatenate(
                [jnp.concatenate([h_scr[s0 + s, pl.ds(k * bsz + b, tc, stride=SUBLANES), :] for s in range(half)],
                                 axis=1) for b in range(bsz)], axis=0).astype(BF16)

        ch = (jnp.dot(gather(0), cre_ref[k].astype(BF16), preferred_element_type=F32)
              - jnp.dot(gather(half), cim_ref[k].astype(BF16), preferred_element_type=F32))
        for b in range(bsz):
            y = ch[b * tc:(b + 1) * tc, :] + d_ref[:, cols] * u_ref[b, :, cols]
            y_ref[b, :, cols] = jax.nn.gelu(y)


def _s5_mixer(u3, a_re, a_im, log_dt, b_re, b_im, c_re, c_im, d_skip, tc=128):
    bsz, seq, _ = u3.shape
    assert SUBLANES % bsz == 0
    tps = min(SUBLANES // bsz, S5_TILES)
    assert tps * bsz == SUBLANES
    tc = min(tc, seq)
    eye = jnp.eye(S5_GT, dtype=F32)

    def bdiag_b(b):
        bt = b.reshape(S5_TILES, S5_GT, SSM_STATE, SSM_GROUP).transpose(0, 1, 3, 2)
        return jnp.einsum("tgcp,gh->tgchp", bt, eye).reshape(S5_TILES, S5_NC, S5_NP)

    def bdiag_c(c):
        ct = c.reshape(S5_TILES, S5_GT, SSM_GROUP, SSM_STATE)
        return jnp.einsum("tgcp,gh->thpgc", ct, eye).reshape(S5_TILES, S5_NP, S5_NC)

    flat = lambda a: a.reshape(1, SSM_GROUPS * SSM_STATE)
    ldt = jnp.repeat(log_dt, SSM_STATE).reshape(1, SSM_GROUPS * SSM_STATE)
    vec_spec = pl.BlockSpec((1, tps * S5_NP), lambda g, t: (0, g))
    b_spec = pl.BlockSpec((tps, S5_NC, S5_NP), lambda g, t: (g, 0, 0))
    c_spec = pl.BlockSpec((tps, S5_NP, S5_NC), lambda g, t: (g, 0, 0))
    return pl.pallas_call(
        functools.partial(_s5_body, tc=tc, bsz=bsz, tps=tps),
        out_shape=jax.ShapeDtypeStruct((bsz, seq, SSM_WIDTH), F32),
        grid=(S5_TILES // tps, seq // tc),
        in_specs=[
            pl.BlockSpec((bsz, tc, tps * S5_NC), lambda g, t: (0, t, g)),
            vec_spec, vec_spec, vec_spec, b_spec, b_spec, c_spec, c_spec,
            pl.BlockSpec((1, tps * S5_NC), lambda g, t: (0, g)),
        ],
        out_specs=pl.BlockSpec((bsz, tc, tps * S5_NC), lambda g, t: (0, t, g)),
        scratch_shapes=[
            pltpu.VMEM((2 * SUBLANES, S5_NP), F32),
            pltpu.VMEM((tps, S5_NC, 2 * S5_NP), BF16),
            pltpu.VMEM((S5_SLABS, tc * SUBLANES, LANES), F32),
            pltpu.VMEM((2 * SUBLANES, S5_NP), F32),
        ],
        compiler_params=_params(("arbitrary", "arbitrary")),
        name="s5_scan",
    )(u3, flat(a_re), flat(a_im), ldt, bdiag_b(b_re), bdiag_b(b_im), bdiag_c(c_re), bdiag_c(c_im),
      d_skip.reshape(1, SSM_WIDTH))


def _glu_body(y_ref, w_ref, b_ref, o_ref, ybf_scr, *, tn):
    j = pl.program_id(1)

    @pl.when(j == 0)
    def _():
        ybf_scr[...] = y_ref[...].astype(BF16)

    z = jnp.dot(ybf_scr[...], w_ref[...], preferred_element_type=F32) + b_ref[...]
    yt = y_ref[:, pl.ds(pl.multiple_of(j * tn, tn), tn)]
    o_ref[...] = (yt * jax.nn.sigmoid(z)).astype(o_ref.dtype)


def _glu(y, w_bf, bias, tm=1024, tn=1024):
    m, k = y.shape
    tm = min(tm, m)
    return pl.pallas_call(
        functools.partial(_glu_body, tn=tn),
        out_shape=jax.ShapeDtypeStruct((m, k), BF16),
        grid=(m // tm, k // tn),
        in_specs=[pl.BlockSpec((tm, k), lambda i, j: (i, 0)),
                  pl.BlockSpec((k, tn), lambda i, j: (0, j)),
                  pl.BlockSpec((1, tn), lambda i, j: (0, j))],
        out_specs=pl.BlockSpec((tm, tn), lambda i, j: (i, j)),
        scratch_shapes=[pltpu.VMEM((tm, k), BF16)],
        compiler_params=_params(("parallel", "arbitrary")),
        name="s5_glu",
    )(y, w_bf, bias.reshape(1, k))


def _compress_body(x_ref, pe_ref, w1_ref, w2_ref, o_ref, *, ncmp):
    half = L_CMP // 2
    acc_a = jnp.zeros((ncmp, HEAD_DIM), F32)
    acc_b = jnp.zeros((ncmp, HEAD_DIM), F32)
    for r in range(half):
        xr = x_ref[pl.ds(r, ncmp, stride=STRIDE_CMP), :]
        xa = (xr + pe_ref[r:r + 1, :]).astype(BF16)
        xb = (xr + pe_ref[half + r:half + r + 1, :]).astype(BF16)
        acc_a += jnp.dot(xa, w1_ref[r * HEAD_DIM:(r + 1) * HEAD_DIM, :], preferred_element_type=F32)
        acc_b += jnp.dot(xb, w1_ref[(half + r) * HEAD_DIM:(half + r + 1) * HEAD_DIM, :],
                         preferred_element_type=F32)
    pre = acc_a + jnp.concatenate([acc_b[1:], acc_b[:1]], axis=0)
    o_ref[...] = jnp.dot(jax.nn.gelu(pre).astype(BF16), w2_ref[...], preferred_element_type=F32).astype(o_ref.dtype)


def _compress(proj_f32, col0, pe, w1_bf, w2_bf, bsz, seq):
    ncmp = seq // STRIDE_CMP
    cb = col0 // HEAD_DIM
    return pl.pallas_call(
        functools.partial(_compress_body, ncmp=ncmp),
        out_shape=jax.ShapeDtypeStruct((bsz, 2, N_KV, ncmp, HEAD_DIM), BF16),
        grid=(bsz, 2, N_KV),
        in_specs=[pl.BlockSpec((seq, HEAD_DIM), lambda b, w, g: (b, cb + w * N_KV + g)),
                  pl.BlockSpec((None, L_CMP, HEAD_DIM), lambda b, w, g: (w, 0, 0)),
                  pl.BlockSpec((None, L_CMP * HEAD_DIM, HEAD_DIM), lambda b, w, g: (w, 0, 0)),
                  pl.BlockSpec((None, HEAD_DIM, HEAD_DIM), lambda b, w, g: (w, 0, 0))],
        out_specs=pl.BlockSpec((None, None, None, ncmp, HEAD_DIM), lambda b, w, g: (b, w, g, 0, 0)),
        compiler_params=_params(("parallel", "parallel", "parallel")),
        name="nsa_compress",
    )(proj_f32, pe, w1_bf, w2_bf)


def _cmp_attn_body(q_ref, kc_ref, vc_ref, o_ref, bias_ref, *, tq, ncmp, nblk, ntop):
    qi = pl.program_id(2)
    t0 = qi * tq
    tpos = t0 + lax.broadcasted_iota(jnp.int32, (tq, ncmp), 0)
    blk_end = lax.broadcasted_iota(jnp.int32, (tq, ncmp), 1) * STRIDE_CMP + (L_CMP - 1)
    ok = blk_end <= tpos
    okf = ok.astype(F32)
    kc = kc_ref[...]
    vc = vc_ref[...]
    psum = jnp.zeros((tq, ncmp), F32)
    for h in range(HPG):
        qh = q_ref[:, h * HEAD_DIM:(h + 1) * HEAD_DIM]
        s = lax.dot_general(qh, kc, (((1,), (1,)), ((), ())), preferred_element_type=F32)
        s = jnp.where(ok, s, NEG_INF)
        e = jnp.exp2(s - jnp.max(s, axis=-1, keepdims=True))
        p = e / jnp.sum(e, axis=-1, keepdims=True) * okf
        o_ref[:, h * HEAD_DIM:(h + 1) * HEAD_DIM] = jnp.dot(p.astype(BF16), vc, preferred_element_type=F32)
        psum = psum + p

    jj = lax.broadcasted_iota(jnp.int32, (nblk, ncmp), 0)
    nn = lax.broadcasted_iota(jnp.int32, (nblk, ncmp), 1)
    ov = ((nn * STRIDE_CMP < (jj + 1) * L_SEL) & (nn * STRIDE_CMP + L_CMP > jj * L_SEL)).astype(BF16)
    p_hi = psum.astype(BF16)
    p_lo = (psum - p_hi.astype(F32)).astype(BF16)
    nt = (((1,), (1,)), ((), ()))
    imp = (lax.dot_general(ov, p_hi, nt, preferred_element_type=F32)
           + lax.dot_general(ov, p_lo, nt, preferred_element_type=F32))

    jb = lax.broadcasted_iota(jnp.int32, (nblk, tq), 0)
    tt = t0 + lax.broadcasted_iota(jnp.int32, (nblk, tq), 1)
    cur = tt // L_SEL
    allowed = jb * L_SEL <= tt
    forced = (jb == 0) | (jb == cur) | (jb == cur - 1)
    score = jnp.where(forced, FORCE_SCORE, jnp.where(allowed, imp, NEG_INF))
    taken = jnp.zeros((nblk, tq), F32)
    for _ in range(ntop):
        best = jnp.max(score, axis=0, keepdims=True)
        first = jnp.min(jnp.where(score == best, jb, nblk), axis=0, keepdims=True)
        pick = jb == first
        taken = jnp.where(pick, 1.0, taken)
        score = jnp.where(pick, TAKEN_SCORE, score)
    bias_ref[...] = jnp.where(taken > 0.5, 0.0, NEG_INF).astype(bias_ref.dtype)


def _cmp_attn(qkv, kcv, bsz, seq, tq=512):
    tq = min(tq, seq)
    ncmp = seq // STRIDE_CMP
    nblk = seq // L_SEL
    ntop = min(N_SEL, nblk)
    nq = seq // tq
    return pl.pallas_call(
        functools.partial(_cmp_attn_body, tq=tq, ncmp=ncmp, nblk=nblk, ntop=ntop),
        out_shape=(jax.ShapeDtypeStruct((bsz * seq, ATT_WIDTH), F32),
                   jax.ShapeDtypeStruct((bsz, N_KV, nblk, seq), F32)),
        grid=(bsz, N_KV, nq),
        in_specs=[pl.BlockSpec((tq, HPG * HEAD_DIM), lambda b, g, i: (b * nq + i, g)),
                  pl.BlockSpec((None, None, None, ncmp, HEAD_DIM), lambda b, g, i: (b, 0, g, 0, 0)),
                  pl.BlockSpec((None, None, None, ncmp, HEAD_DIM), lambda b, g, i: (b, 1, g, 0, 0))],
        out_specs=(pl.BlockSpec((tq, HPG * HEAD_DIM), lambda b, g, i: (b * nq + i, g)),
                   pl.BlockSpec((None, None, nblk, tq), lambda b, g, i: (b, g, 0, i))),
        compiler_params=_params(("parallel", "parallel", "parallel")),
        name="nsa_cmp_attn_topk",
    )(qkv, kcv, kcv)


def _sel_attn_body(q_ref, k_ref, v_ref, bias_ref, o_ref, qa_scr, m_scr, acc_scr, s_scr, p_scr, alpha_scr,
                   *, tq, rb):
    qi = pl.program_id(2)
    tk = tq
    m_scr[...] = jnp.full_like(m_scr, NEG_INF)
    acc_scr[...] = jnp.zeros_like(acc_scr)
    for h in range(HPG):
        qa_scr[h, :, 0:HEAD_DIM] = q_ref[:, h * HEAD_DIM:(h + 1) * HEAD_DIM]
        qa_scr[h, :, HEAD_DIM:2 * HEAD_DIM] = bias_ref[...]
    nt = (((1,), (1,)), ((), ()))

    def key_tile(kt):
        k0 = pl.multiple_of(kt * tk, tk)
        key_blk = kt * (tk // L_SEL) + lax.broadcasted_iota(jnp.int32, (tk, LANES), 0) // L_SEL
        onehot = jnp.where(lax.broadcasted_iota(jnp.int32, (tk, LANES), 1) == key_blk, 1.0, 0.0).astype(BF16)
        return jnp.concatenate([k_ref[pl.ds(k0, tk), :], onehot], axis=1)

    def scores(h, k_aug, s_buf):
        s_buf[h] = lax.dot_general(qa_scr[h], k_aug, nt, preferred_element_type=F32)

    def softmax(h, s_buf, diagonal):
        for r in range(tq // rb):
            rows = slice(r * rb, (r + 1) * rb)
            s = s_buf[h, rows, :]
            if diagonal:
                col = lax.broadcasted_iota(jnp.int32, (rb, tk), 1)
                row = lax.broadcasted_iota(jnp.int32, (rb, tk), 0) + r * rb
                s = jnp.where(col <= row, s, NEG_INF)
            m_prev = m_scr[h, rows, :]
            m_next = jnp.maximum(m_prev, jnp.max(s, axis=-1, keepdims=True))
            p_scr[h, rows, :] = jnp.exp2(s - _lane_tile(m_next, tk // LANES)).astype(BF16)
            alpha_scr[h, rows, :] = jnp.exp2(m_prev - m_next)
            m_scr[h, rows, :] = m_next

    def values(h, v_aug):
        acc_scr[h] = (_lane_tile(alpha_scr[h], 2) * acc_scr[h]
                      + jnp.dot(p_scr[h], v_aug, preferred_element_type=F32))

    def tile(kt, diagonal):
        k0 = pl.multiple_of(kt * tk, tk)
        k_aug = key_tile(kt)
        v_aug = jnp.concatenate([v_ref[pl.ds(k0, tk), :], jnp.ones((tk, LANES), BF16)], axis=1)
        for h in range(HPG):
            scores(h, k_aug, s_scr)
        for h in range(HPG):
            softmax(h, s_scr, diagonal)
        for h in range(HPG):
            values(h, v_aug)

    def full_tile(kt, carry):
        tile(kt, False)
        return carry

    lax.fori_loop(0, qi, full_tile, 0)
    tile(qi, True)

    for h in range(HPG):
        o_ref[:, h * HEAD_DIM:(h + 1) * HEAD_DIM] = (acc_scr[h, :, 0:HEAD_DIM]
                                                     / acc_scr[h, :, HEAD_DIM:2 * HEAD_DIM])


def _sel_attn(qkv, bias, bsz, seq, tq=SEL_TQ, rb=SEL_RB):
    tq = min(tq, seq)
    rb = min(rb, tq)
    nq = seq // tq
    kcol = ATT_WIDTH // HEAD_DIM
    return pl.pallas_call(
        functools.partial(_sel_attn_body, tq=tq, rb=rb),
        out_shape=jax.ShapeDtypeStruct((bsz * seq, ATT_WIDTH), F32),
        grid=(bsz, N_KV, nq),
        in_specs=[pl.BlockSpec((tq, HPG * HEAD_DIM), lambda b, g, i: (b * nq + i, g)),
                  pl.BlockSpec((seq, HEAD_DIM), lambda b, g, i: (b, kcol + g)),
                  pl.BlockSpec((seq, HEAD_DIM), lambda b, g, i: (b, kcol + N_KV + g)),
                  pl.BlockSpec((None, None, tq, LANES), lambda b, g, i: (b, g, i, 0))],
        out_specs=pl.BlockSpec((tq, HPG * HEAD_DIM), lambda b, g, i: (b * nq + i, g)),
        scratch_shapes=[pltpu.VMEM((HPG, tq, 2 * HEAD_DIM), BF16),
                        pltpu.VMEM((HPG, tq, LANES), F32),
                        pltpu.VMEM((HPG, tq, 2 * HEAD_DIM), F32),
                        pltpu.VMEM((HPG, tq, tq), F32),
                        pltpu.VMEM((HPG, tq, tq), BF16),
                        pltpu.VMEM((HPG, tq, LANES), F32)],
        compiler_params=_params(("parallel", "parallel", "arbitrary")),
        name="nsa_sel_attn",
    )(qkv, qkv, qkv, bias)


def _win_attn_body(q_ref, kc_ref, kp_ref, vc_ref, vp_ref, ocmp_ref, osel_ref, gate_ref, o_ref, *, tq):
    qi = pl.program_id(2)
    nt = (((1,), (1,)), ((), ()))
    ones = jnp.ones((tq, LANES), BF16)
    vc_aug = jnp.concatenate([vc_ref[...], ones], axis=1)
    vp_aug = jnp.concatenate([vp_ref[...], ones], axis=1)
    for r in range(tq // ATT_ROWS):
        rows = slice(r * ATT_ROWS, (r + 1) * ATT_ROWS)
        n_cur, p_lo = (r + 1) * ATT_ROWS, r * ATT_ROWS
        row_c = lax.broadcasted_iota(jnp.int32, (ATT_ROWS, n_cur), 0) + r * ATT_ROWS
        ok_cur = lax.broadcasted_iota(jnp.int32, (ATT_ROWS, n_cur), 1) <= row_c
        row_p = lax.broadcasted_iota(jnp.int32, (ATT_ROWS, tq - p_lo), 0) + r * ATT_ROWS
        ok_prev = (lax.broadcasted_iota(jnp.int32, (ATT_ROWS, tq - p_lo), 1) + p_lo > row_p) & (qi > 0)
        gates = jax.nn.sigmoid(gate_ref[rows, :])
        for h in range(HPG):
            sl = slice(h * HEAD_DIM, (h + 1) * HEAD_DIM)
            qh = q_ref[rows, sl]
            s1 = jnp.where(ok_cur, lax.dot_general(qh, kc_ref[0:n_cur, :], nt, preferred_element_type=F32), NEG_INF)
            s0 = jnp.where(ok_prev, lax.dot_general(qh, kp_ref[p_lo:tq, :], nt, preferred_element_type=F32), NEG_INF)
            m = jnp.maximum(jnp.max(s1, axis=-1, keepdims=True), jnp.max(s0, axis=-1, keepdims=True))
            p1 = jnp.exp2(s1 - m).astype(BF16)
            p0 = jnp.exp2(s0 - m).astype(BF16)
            acc = (jnp.dot(p1, vc_aug[0:n_cur, :], preferred_element_type=F32)
                   + jnp.dot(p0, vp_aug[p_lo:tq, :], preferred_element_type=F32))
            ow = acc[:, 0:HEAD_DIM] / acc[:, HEAD_DIM:2 * HEAD_DIM]
            g_cmp = gates[:, h:h + 1]
            g_sel = gates[:, HPG + h:HPG + h + 1]
            g_win = gates[:, 2 * HPG + h:2 * HPG + h + 1]
            o_ref[rows, sl] = (g_cmp * ocmp_ref[rows, sl] + g_sel * osel_ref[rows, sl]
                               + g_win * ow).astype(o_ref.dtype)


def _win_attn_combine(qkv, o_cmp, o_sel, gates, bsz, seq):
    tq = WINDOW
    assert seq % tq == 0
    nq = seq // tq
    kcol = ATT_WIDTH // HEAD_DIM + 2 * N_KV
    qspec = pl.BlockSpec((tq, HPG * HEAD_DIM), lambda b, g, i: (b * nq + i, g))
    cur = lambda c: pl.BlockSpec((tq, HEAD_DIM), lambda b, g, i: (b * nq + i, kcol + c * N_KV + g))
    prev = lambda c: pl.BlockSpec((tq, HEAD_DIM),
                                  lambda b, g, i: (b * nq + jnp.maximum(i - 1, 0), kcol + c * N_KV + g))
    return pl.pallas_call(
        functools.partial(_win_attn_body, tq=tq),
        out_shape=jax.ShapeDtypeStruct((bsz * seq, ATT_WIDTH), BF16),
        grid=(bsz, N_KV, nq),
        in_specs=[qspec, cur(0), prev(0), cur(1), prev(1), qspec, qspec,
                  pl.BlockSpec((None, None, tq, LANES), lambda b, g, i: (b, g, i, 0))],
        out_specs=qspec,
        compiler_params=_params(("parallel", "parallel", "parallel")),
        name="nsa_win_attn_gate",
    )(qkv, qkv, qkv, qkv, qkv, o_cmp, o_sel, gates)


def _merge_body(hn_ref, ya_ref, yb_ref, wga_ref, wgb_ref, wa_ref, wb_ref, o_ref):
    hn = hn_ref[...]
    ga = jnp.dot(hn, wga_ref[...], preferred_element_type=F32)
    gb = jnp.dot(hn, wgb_ref[...], preferred_element_type=F32)
    pa = jnp.dot(ya_ref[...], wa_ref[...], preferred_element_type=F32)
    pb = jnp.dot(yb_ref[...], wb_ref[...], preferred_element_type=F32)
    o_ref[...] = (jax.nn.sigmoid(ga) * pa + jax.nn.sigmoid(gb) * pb).astype(o_ref.dtype)


def _merge(hn, ya, yb, wg, wa, wb, tm=512, tn=512):
    m, d = hn.shape
    ka = ya.shape[1]
    tm = min(tm, m)
    row = lambda k: pl.BlockSpec((tm, k), lambda i, j: (i, 0))
    col = lambda k: pl.BlockSpec((k, tn), lambda i, j: (0, j))
    col_b = pl.BlockSpec((d, tn), lambda i, j: (0, d // tn + j))
    return pl.pallas_call(
        _merge_body,
        out_shape=jax.ShapeDtypeStruct((m, d), BF16),
        grid=(m // tm, d // tn),
        in_specs=[row(d), row(ka), row(ka), col(d), col_b, col(ka), col(ka)],
        out_specs=pl.BlockSpec((tm, tn), lambda i, j: (i, j)),
        compiler_params=_params(("parallel", "arbitrary")),
        name="mixer_merge",
    )(hn, ya, yb, wg, wg, wa, wb)


def _rms(x, g):
    return x * lax.rsqrt(jnp.mean(x * x, axis=-1, keepdims=True) + RMS_EPS) * g


def _res_norm2_body(raw_ref, x_ref, gpost_ref, gpre_ref, h_ref, hn_ref):
    h = x_ref[...] + _rms(raw_ref[...], gpost_ref[...])
    h_ref[...] = h
    hn_ref[...] = _rms(h, gpre_ref[...]).astype(hn_ref.dtype)


def _res_norm2(raw, x, g_post, g_pre, tm=256):
    m, d = x.shape
    tm = min(tm, m)
    rows = pl.BlockSpec((tm, d), lambda i: (i, 0))
    vec = pl.BlockSpec((1, d), lambda i: (0, 0))
    return pl.pallas_call(
        _res_norm2_body,
        out_shape=(jax.ShapeDtypeStruct((m, d), F32), jax.ShapeDtypeStruct((m, d), BF16)),
        grid=(m // tm,),
        in_specs=[rows, rows, vec, vec],
        out_specs=(rows, rows),
        compiler_params=_params(("parallel",)),
        name="residual_norm_prenorm",
    )(raw, x, g_post.reshape(1, d), g_pre.reshape(1, d))


def _res_norm_body(raw_ref, x_ref, g_ref, o_ref):
    o_ref[...] = x_ref[...] + _rms(raw_ref[...], g_ref[...])


def _res_norm(raw, x, g, tm=256):
    m, d = x.shape
    tm = min(tm, m)
    rows = pl.BlockSpec((tm, d), lambda i: (i, 0))
    return pl.pallas_call(
        _res_norm_body,
        out_shape=jax.ShapeDtypeStruct((m, d), F32),
        grid=(m // tm,),
        in_specs=[rows, rows, pl.BlockSpec((1, d), lambda i: (0, 0))],
        out_specs=rows,
        compiler_params=_params(("parallel",)),
        name="residual_norm",
    )(raw, x, g.reshape(1, d))


def _swiglu_body(a_ref, wg_ref, wu_ref, o_ref, wg_scr, wu_scr):
    @pl.when(pl.program_id(1) == 0)
    def _():
        wg_scr[...] = wg_ref[...].astype(BF16)
        wu_scr[...] = wu_ref[...].astype(BF16)

    a = a_ref[...]
    g = jnp.dot(a, wg_scr[...], preferred_element_type=F32)
    u = jnp.dot(a, wu_scr[...], preferred_element_type=F32)
    o_ref[...] = (g * jax.nn.sigmoid(g) * u).astype(o_ref.dtype)


def _swiglu(a, wg, wu, tm=1024, tn=256):
    m, k = a.shape
    n = wg.shape[1]
    tm = min(tm, m)
    assert n % tn == 0
    return pl.pallas_call(
        _swiglu_body,
        out_shape=jax.ShapeDtypeStruct((m, n), BF16),
        grid=(n // tn, m // tm),
        in_specs=[pl.BlockSpec((tm, k), lambda j, i: (i, 0)),
                  pl.BlockSpec((k, tn), lambda j, i: (0, j)),
                  pl.BlockSpec((k, tn), lambda j, i: (0, j))],
        out_specs=pl.BlockSpec((tm, tn), lambda j, i: (i, j)),
        scratch_shapes=[pltpu.VMEM((k, tn), BF16), pltpu.VMEM((k, tn), BF16)],
        compiler_params=_params(("arbitrary", "arbitrary")),
        name="ffn_swiglu",
    )(a, wg, wu)


def kernel(x, norm_mix_pre, w_in, ssm_a_re, ssm_a_im, ssm_log_dt, ssm_b_re, ssm_b_im, ssm_c_re, ssm_c_im, ssm_d, ssm_w_glu, ssm_b_glu, cmp_pe_k, cmp_w1_k, cmp_w2_k, cmp_pe_v, cmp_w1_v, cmp_w2_v, w_proj_a, w_proj_b, w_out, norm_mix_post, norm_ffn_pre, w_ffn_gate, w_ffn_up, w_ffn_down, norm_ffn_post):
    bsz, seq, d = x.shape
    m = bsz * seq
    depth = w_in.shape[0]
    h = x.reshape(m, d)
    o_q, o_kvc, o_kv, o_gn = SSM_WIDTH, SSM_WIDTH + ATT_WIDTH, SSM_WIDTH + ATT_WIDTH + 2 * KV_WIDTH, 7168
    o_ga = o_gn + 3 * N_HEADS
    o_gb = o_ga + D_MODEL
    bf = lambda a: a.astype(BF16)
    tn_in = 512
    nb_u, nb_q = SSM_WIDTH // tn_in, ATT_WIDTH // tn_in
    qk_scale = HEAD_DIM ** -0.5 * LOG2E
    for l in range(depth):
        w = w_in[l]
        w_g = _shifted_cast(w, o_ga, 2 * D_MODEL, BF16)

        hn = _rmsnorm(h, norm_mix_pre[l], BF16)
        proj_f32 = _matmul_f32w(hn, w, SSM_WIDTH + 2 * KV_WIDTH, F32, 1024, tn_in,
                                col_block=lambda j: jnp.where(j < nb_u, j, j - nb_u + o_kvc // tn_in),
                                name="in_proj_f32")
        qkv = _matmul_f32w(hn, w, ATT_WIDTH + 4 * KV_WIDTH, BF16, 1024, tn_in,
                           col_block=lambda j: jnp.where(j < nb_q, j + o_q // tn_in, j - nb_q + o_kv // tn_in),
                           scale_fn=lambda j: jnp.where(j < nb_q, qk_scale, 1.0),
                           name="in_proj_bf16")
        gn = _matmul_f32w(hn, w, LANES, F32, 1024, LANES, col_block=lambda j: o_gn // LANES,
                          name="in_proj_gates")

        y = _s5_mixer(proj_f32.reshape(bsz, seq, -1), ssm_a_re[l], ssm_a_im[l], ssm_log_dt[l],
                      ssm_b_re[l], ssm_b_im[l], ssm_c_re[l], ssm_c_im[l], ssm_d[l])
        y_a = _glu(y.reshape(m, SSM_WIDTH), bf(ssm_w_glu[l]), ssm_b_glu[l])

        pe = jnp.stack([cmp_pe_k[l], cmp_pe_v[l]])
        w1 = bf(jnp.stack([cmp_w1_k[l], cmp_w1_v[l]]))
        w2 = bf(jnp.stack([cmp_w2_k[l], cmp_w2_v[l]]))
        kcv = _compress(proj_f32, SSM_WIDTH, pe, w1, w2, bsz, seq)
        o_cmp, bias_t = _cmp_attn(qkv, kcv, bsz, seq)
        nblk = seq // L_SEL
        bias = bf(jnp.pad(jnp.swapaxes(bias_t, 2, 3), ((0, 0), (0, 0), (0, 0), (0, LANES - nblk))))
        o_sel = _sel_attn(qkv, bias, bsz, seq)
        gates = gn[:, :3 * N_HEADS].reshape(bsz, seq, 3, N_KV, HPG).transpose(0, 3, 1, 2, 4)
        gates = jnp.pad(gates.reshape(bsz, N_KV, seq, 3 * HPG), ((0, 0), (0, 0), (0, 0), (0, LANES - 3 * HPG)))
        y_b = _win_attn_combine(qkv, o_cmp, o_sel, gates, bsz, seq)

        merged = _merge(hn, y_a, y_b, w_g, bf(w_proj_a[l]), bf(w_proj_b[l]))
        mix = _matmul_f32w(merged, w_out[l], D_MODEL, F32, 1024, 512, name="out_proj")
        h, hn2 = _res_norm2(mix, h, norm_mix_post[l], norm_ffn_pre[l])

        act = _swiglu(hn2, w_ffn_gate[l], w_ffn_up[l])
        f = _matmul(act, bf(w_ffn_down[l]), F32, 512, 512, name="ffn_down")
        h = _res_norm(f, h, norm_ffn_post[l])
    return h.reshape(bsz, seq, d)
```

```python
import functools
import math

import jax
import jax.numpy as jnp
from jax import lax
from jax.experimental import pallas as pl
from jax.experimental.pallas import tpu as pltpu

F32 = jnp.float32
BF16 = jnp.bfloat16

D_MODEL = 4096
SSM_WIDTH = 2048
SSM_GROUP = 16
SSM_GROUPS = 128
SSM_STATE = 64
N_HEADS = 16
HEAD_DIM = 128
N_KV = 4
HPG = 4
ATT_WIDTH = 2048
KV_WIDTH = 512
L_CMP = 32
STRIDE_CMP = 16
L_SEL = 64
N_SEL = 16
WINDOW = 512
D_FF = 11008
RMS_EPS = 1e-6
NEG_INF = -1e30
FORCE_SCORE = 1e9
TAKEN_SCORE = -3e38
LOG2E = math.log2(math.e)

V7X_VMEM_LIMIT_BYTES = 56 * 1024 * 1024
LANES = 128
SUBLANES = 8

S5_GT = 16
S5_NC = S5_GT * SSM_GROUP
S5_NP = S5_GT * SSM_STATE
S5_TILES = SSM_GROUPS // S5_GT
S5_SLABS = 2 * S5_NP // LANES

ATT_ROWS = 128
SEL_TQ, SEL_RB = 512, 64


def _params(sem, vmem=V7X_VMEM_LIMIT_BYTES):
    return pltpu.CompilerParams(dimension_semantics=sem, vmem_limit_bytes=vmem)


def _lane_tile(x, n):
    return jnp.concatenate([x] * n, axis=1)


def _rmsnorm_body(x_ref, g_ref, o_ref):
    x = x_ref[...]
    ms = jnp.mean(x * x, axis=-1, keepdims=True)
    o_ref[...] = (x * lax.rsqrt(ms + RMS_EPS) * g_ref[...]).astype(o_ref.dtype)


def _rmsnorm(x, gain, out_dtype, tm=256):
    m, d = x.shape
    tm = min(tm, m)
    return pl.pallas_call(
        _rmsnorm_body,
        out_shape=jax.ShapeDtypeStruct((m, d), out_dtype),
        grid=(m // tm,),
        in_specs=[pl.BlockSpec((tm, d), lambda i: (i, 0)), pl.BlockSpec((1, d), lambda i: (0, 0))],
        out_specs=pl.BlockSpec((tm, d), lambda i: (i, 0)),
        compiler_params=_params(("parallel",)),
        name="rmsnorm",
    )(x, gain.reshape(1, d))


def _mm_body(a_ref, w_ref, o_ref):
    o_ref[...] = jnp.dot(a_ref[...], w_ref[...], preferred_element_type=F32).astype(o_ref.dtype)


def _matmul(a, w, out_dtype, tm, tn, name="matmul"):
    m, k = a.shape
    n = w.shape[1]
    tm, tn = min(tm, m), min(tn, n)
    assert m % tm == 0 and n % tn == 0
    return pl.pallas_call(
        _mm_body,
        out_shape=jax.ShapeDtypeStruct((m, n), out_dtype),
        grid=(m // tm, n // tn),
        in_specs=[pl.BlockSpec((tm, k), lambda i, j: (i, 0)), pl.BlockSpec((k, tn), lambda i, j: (0, j))],
        out_specs=pl.BlockSpec((tm, tn), lambda i, j: (i, j)),
        compiler_params=_params(("parallel", "arbitrary")),
        name=name,
    )(a, w)


_NT = (((1,), (1,)), ((), ()))


def _mm_f32w_body(a_ref, w_ref, o_ref, wbf_scr, *, scale_fn, trans_w):
    @pl.when(pl.program_id(1) == 0)
    def _():
        wbf_scr[...] = w_ref[...].astype(BF16)

    if trans_w:
        acc = lax.dot_general(a_ref[...], wbf_scr[...], _NT, preferred_element_type=F32)
    else:
        acc = jnp.dot(a_ref[...], wbf_scr[...], preferred_element_type=F32)
    if scale_fn is not None:
        acc = acc * scale_fn(pl.program_id(0))
    o_ref[...] = acc.astype(o_ref.dtype)


def _matmul_f32w(a, w, n_out, out_dtype, tm, tn, col_block=None, scale_fn=None, trans_w=False,
                 name="matmul_f32w"):
    m, k = a.shape
    tm = min(tm, m)
    assert m % tm == 0 and n_out % tn == 0
    col_block = col_block or (lambda j: j)
    if trans_w:
        w_spec = pl.BlockSpec((tn, k), lambda j, i: (col_block(j), 0))
        w_scratch = pltpu.VMEM((tn, k), BF16)
    else:
        w_spec = pl.BlockSpec((k, tn), lambda j, i: (0, col_block(j)))
        w_scratch = pltpu.VMEM((k, tn), BF16)
    return pl.pallas_call(
        functools.partial(_mm_f32w_body, scale_fn=scale_fn, trans_w=trans_w),
        out_shape=jax.ShapeDtypeStruct((m, n_out), out_dtype),
        grid=(n_out // tn, m // tm),
        in_specs=[pl.BlockSpec((tm, k), lambda j, i: (i, 0)), w_spec],
        out_specs=pl.BlockSpec((tm, tn), lambda j, i: (i, j)),
        scratch_shapes=[w_scratch],
        compiler_params=_params(("arbitrary", "arbitrary")),
        name=name,
    )(a, w)


def _rows_transposed_cast_body(w_ref, o_ref):
    o_ref[...] = w_ref[...].T.astype(o_ref.dtype)


def _rows_transposed_cast(wt, row0, n_rows, out_dtype, tr=512):
    k = wt.shape[1]
    assert row0 % (2 * SUBLANES) == 0 and n_rows % tr == 0
    return pl.pallas_call(
        _rows_transposed_cast_body,
        out_shape=jax.ShapeDtypeStruct((k, n_rows), out_dtype),
        grid=(n_rows // tr,),
        in_specs=[pl.BlockSpec((pl.Element(tr), pl.Element(k)),
                               lambda c: (pl.multiple_of(row0 + c * tr, 2 * SUBLANES), 0))],
        out_specs=pl.BlockSpec((k, tr), lambda c: (0, c)),
        compiler_params=_params(("parallel",)),
        name="gate_weight_cast",
    )(wt)


def _s5_body(u_ref, are_ref, aim_ref, ldt_ref, bre_ref, bim_ref, cre_ref, cim_ref, d_ref, y_ref,
             abar_scr, bbar_scr, h_scr, st_scr, *, tc, bsz, tps):
    ti = pl.program_id(1)
    half = S5_SLABS // 2

    @pl.when(ti == 0)
    def _():
        row_tile = lax.broadcasted_iota(jnp.int32, (SUBLANES, S5_NP), 0) // bsz
        abr8 = jnp.zeros((SUBLANES, S5_NP), F32)
        abi8 = jnp.zeros((SUBLANES, S5_NP), F32)
        for k in range(tps):
            cols = slice(k * S5_NP, (k + 1) * S5_NP)
            ar, ai = are_ref[:, cols], aim_ref[:, cols]
            dt = jnp.exp(ldt_ref[:, cols])
            decay = jnp.exp(dt * ar)
            abr, abi = decay * jnp.cos(dt * ai), decay * jnp.sin(dt * ai)
            den = ar * ar + ai * ai
            zr = ((abr - 1.0) * ar + abi * ai) / den
            zi = (abi * ar - (abr - 1.0) * ai) / den
            abr8 = jnp.where(row_tile == k, abr, abr8)
            abi8 = jnp.where(row_tile == k, abi, abi8)
            br, bi = bre_ref[k], bim_ref[k]
            bbar_scr[k, :, 0:S5_NP] = (zr * br - zi * bi).astype(BF16)
            bbar_scr[k, :, S5_NP:2 * S5_NP] = (zr * bi + zi * br).astype(BF16)
        abar_scr[0:SUBLANES, :] = abr8
        abar_scr[SUBLANES:2 * SUBLANES, :] = abi8
        st_scr[...] = jnp.zeros_like(st_scr)

    for k in range(tps):
        cols = slice(k * S5_NC, (k + 1) * S5_NC)
        ub = jnp.concatenate([u_ref[b, :, cols] for b in range(bsz)], axis=0).astype(BF16)
        bu = jnp.dot(ub, bbar_scr[k], preferred_element_type=F32)
        for b in range(bsz):
            for s in range(S5_SLABS):
                h_scr[s, pl.ds(k * bsz + b, tc, stride=SUBLANES), :] = bu[b * tc:(b + 1) * tc,
                                                                        s * LANES:(s + 1) * LANES]

    ar = abar_scr[0:SUBLANES, :]
    ai = abar_scr[SUBLANES:2 * SUBLANES, :]

    def step(t, carry):
        hr, hi = carry
        r0 = pl.multiple_of(t * SUBLANES, SUBLANES)
        bur = jnp.concatenate([h_scr[s, pl.ds(r0, SUBLANES), :] for s in range(half)], axis=1)
        bui = jnp.concatenate([h_scr[half + s, pl.ds(r0, SUBLANES), :] for s in range(half)], axis=1)
        nhr = ar * hr - ai * hi + bur
        nhi = ar * hi + ai * hr + bui
        for s in range(half):
            h_scr[s, pl.ds(r0, SUBLANES), :] = nhr[:, s * LANES:(s + 1) * LANES]
            h_scr[half + s, pl.ds(r0, SUBLANES), :] = nhi[:, s * LANES:(s + 1) * LANES]
        return nhr, nhi

    hr, hi = lax.fori_loop(0, tc, step, (st_scr[0:SUBLANES, :], st_scr[SUBLANES:2 * SUBLANES, :]), unroll=2)
    st_scr[0:SUBLANES, :] = hr
    st_scr[SUBLANES:2 * SUBLANES, :] = hi

    for k in range(tps):
        cols = slice(k * S5_NC, (k + 1) * S5_NC)

        def gather(s0):
            return jnp.concatenate(
                [jnp.concatenate([h_scr[s0 + s, pl.ds(k * bsz + b, tc, stride=SUBLANES), :] for s in range(half)],
                                 axis=1) for b in range(bsz)], axis=0).astype(BF16)

        ch = (jnp.dot(gather(0), cre_ref[k].astype(BF16), preferred_element_type=F32)
              - jnp.dot(gather(half), cim_ref[k].astype(BF16), preferred_element_type=F32))
        for b in range(bsz):
            y = ch[b * tc:(b + 1) * tc, :] + d_ref[:, cols] * u_ref[b, :, cols]
            y_ref[b, :, cols] = jax.nn.gelu(y)


def _s5_mixer(u3, a_re, a_im, log_dt, b_re, b_im, c_re, c_im, d_skip, tc=128):
    bsz, seq, _ = u3.shape
    assert SUBLANES % bsz == 0
    tps = min(SUBLANES // bsz, S5_TILES)
    assert tps * bsz == SUBLANES
    tc = min(tc, seq)
    eye = jnp.eye(S5_GT, dtype=F32)

    def bdiag_b(b):
        bt = b.reshape(S5_TILES, S5_GT, SSM_STATE, SSM_GROUP).transpose(0, 1, 3, 2)
        return jnp.einsum("tgcp,gh->tgchp", bt, eye).reshape(S5_TILES, S5_NC, S5_NP)

    def bdiag_c(c):
        ct = c.reshape(S5_TILES, S5_GT, SSM_GROUP, SSM_STATE)
        return jnp.einsum("tgcp,gh->thpgc", ct, eye).reshape(S5_TILES, S5_NP, S5_NC)

    flat = lambda a: a.reshape(1, SSM_GROUPS * SSM_STATE)
    ldt = jnp.repeat(log_dt, SSM_STATE).reshape(1, SSM_GROUPS * SSM_STATE)
    vec_spec = pl.BlockSpec((1, tps * S5_NP), lambda g, t: (0, g))
    b_spec = pl.BlockSpec((tps, S5_NC, S5_NP), lambda g, t: (g, 0, 0))
    c_spec = pl.BlockSpec((tps, S5_NP, S5_NC), lambda g, t: (g, 0, 0))
    return pl.pallas_call(
        functools.partial(_s5_body, tc=tc, bsz=bsz, tps=tps),
        out_shape=jax.ShapeDtypeStruct((bsz, seq, SSM_WIDTH), F32),
        grid=(S5_TILES // tps, seq // tc),
        in_specs=[
            pl.BlockSpec((bsz, tc, tps * S5_NC), lambda g, t: (0, t, g)),
            vec_spec, vec_spec, vec_spec, b_spec, b_spec, c_spec, c_spec,
            pl.BlockSpec((1, tps * S5_NC), lambda g, t: (0, g)),
        ],
        out_specs=pl.BlockSpec((bsz, tc, tps * S5_NC), lambda g, t: (0, t, g)),
        scratch_shapes=[
            pltpu.VMEM((2 * SUBLANES, S5_NP), F32),
            pltpu.VMEM((tps, S5_NC, 2 * S5_NP), BF16),
            pltpu.VMEM((S5_SLABS, tc * SUBLANES, LANES), F32),
            pltpu.VMEM((2 * SUBLANES, S5_NP), F32),
        ],
        compiler_params=_params(("arbitrary", "arbitrary")),
        name="s5_scan",
    )(u3, flat(a_re), flat(a_im), ldt, bdiag_b(b_re), bdiag_b(b_im), bdiag_c(c_re), bdiag_c(c_im),
      d_skip.reshape(1, SSM_WIDTH))


def _glu_body(y_ref, w_ref, b_ref, o_ref, ybf_scr, *, tn):
    j = pl.program_id(1)

    @pl.when(j == 0)
    def _():
        ybf_scr[...] = y_ref[...].astype(BF16)

    z = jnp.dot(ybf_scr[...], w_ref[...], preferred_element_type=F32) + b_ref[...]
    yt = y_ref[:, pl.ds(pl.multiple_of(j * tn, tn), tn)]
    o_ref[...] = (yt * jax.nn.sigmoid(z)).astype(o_ref.dtype)


def _glu(y, w_bf, bias, tm=1024, tn=1024):
    m, k = y.shape
    tm = min(tm, m)
    return pl.pallas_call(
        functools.partial(_glu_body, tn=tn),
        out_shape=jax.ShapeDtypeStruct((m, k), BF16),
        grid=(m // tm, k // tn),
        in_specs=[pl.BlockSpec((tm, k), lambda i, j: (i, 0)),
                  pl.BlockSpec((k, tn), lambda i, j: (0, j)),
                  pl.BlockSpec((1, tn), lambda i, j: (0, j))],
        out_specs=pl.BlockSpec((tm, tn), lambda i, j: (i, j)),
        scratch_shapes=[pltpu.VMEM((tm, k), BF16)],
        compiler_params=_params(("parallel", "arbitrary")),
        name="s5_glu",
    )(y, w_bf, bias.reshape(1, k))


def _compress_body(x_ref, pe_ref, w1_ref, w2_ref, o_ref, *, ncmp):
    half = L_CMP // 2
    acc_a = jnp.zeros((ncmp, HEAD_DIM), F32)
    acc_b = jnp.zeros((ncmp, HEAD_DIM), F32)
    for r in range(half):
        xr = x_ref[pl.ds(r, ncmp, stride=STRIDE_CMP), :]
        xa = (xr + pe_ref[r:r + 1, :]).astype(BF16)
        xb = (xr + pe_ref[half + r:half + r + 1, :]).astype(BF16)
        acc_a += jnp.dot(xa, w1_ref[r * HEAD_DIM:(r + 1) * HEAD_DIM, :], preferred_element_type=F32)
        acc_b += jnp.dot(xb, w1_ref[(half + r) * HEAD_DIM:(half + r + 1) * HEAD_DIM, :],
                         preferred_element_type=F32)
    pre = acc_a + jnp.concatenate([acc_b[1:], acc_b[:1]], axis=0)
    o_ref[...] = jnp.dot(jax.nn.gelu(pre).astype(BF16), w2_ref[...], preferred_element_type=F32).astype(o_ref.dtype)


def _compress(proj_f32, col0, pe, w1_bf, w2_bf, bsz, seq):
    ncmp = seq // STRIDE_CMP
    cb = col0 // HEAD_DIM
    return pl.pallas_call(
        functools.partial(_compress_body, ncmp=ncmp),
        out_shape=jax.ShapeDtypeStruct((bsz, 2, N_KV, ncmp, HEAD_DIM), BF16),
        grid=(bsz, 2, N_KV),
        in_specs=[pl.BlockSpec((seq, HEAD_DIM), lambda b, w, g: (b, cb + w * N_KV + g)),
                  pl.BlockSpec((None, L_CMP, HEAD_DIM), lambda b, w, g: (w, 0, 0)),
                  pl.BlockSpec((None, L_CMP * HEAD_DIM, HEAD_DIM), lambda b, w, g: (w, 0, 0)),
                  pl.BlockSpec((None, HEAD_DIM, HEAD_DIM), lambda b, w, g: (w, 0, 0))],
        out_specs=pl.BlockSpec((None, None, None, ncmp, HEAD_DIM), lambda b, w, g: (b, w, g, 0, 0)),
        compiler_params=_params(("parallel", "parallel", "parallel")),
        name="nsa_compress",
    )(proj_f32, pe, w1_bf, w2_bf)


def _cmp_attn_body(q_ref, kc_ref, vc_ref, o_ref, bias_ref, *, tq, ncmp, nblk, ntop):
    qi = pl.program_id(2)
    t0 = qi * tq
    tpos = t0 + lax.broadcasted_iota(jnp.int32, (tq, ncmp), 0)
    blk_end = lax.broadcasted_iota(jnp.int32, (tq, ncmp), 1) * STRIDE_CMP + (L_CMP - 1)
    ok = blk_end <= tpos
    okf = ok.astype(F32)
    kc = kc_ref[...]
    vc = vc_ref[...]
    psum = jnp.zeros((tq, ncmp), F32)
    for h in range(HPG):
        qh = q_ref[:, h * HEAD_DIM:(h + 1) * HEAD_DIM]
        s = lax.dot_general(qh, kc, (((1,), (1,)), ((), ())), preferred_element_type=F32)
        s = jnp.where(ok, s, NEG_INF)
        e = jnp.exp2(s - jnp.max(s, axis=-1, keepdims=True))
        p = e / jnp.sum(e, axis=-1, keepdims=True) * okf
        o_ref[:, h * HEAD_DIM:(h + 1) * HEAD_DIM] = jnp.dot(p.astype(BF16), vc, preferred_element_type=F32)
        psum = psum + p

    jj = lax.broadcasted_iota(jnp.int32, (nblk, ncmp), 0)
    nn = lax.broadcasted_iota(jnp.int32, (nblk, ncmp), 1)
    ov = ((nn * STRIDE_CMP < (jj + 1) * L_SEL) & (nn * STRIDE_CMP + L_CMP > jj * L_SEL)).astype(BF16)
    p_hi = psum.astype(BF16)
    p_lo = (psum - p_hi.astype(F32)).astype(BF16)
    nt = (((1,), (1,)), ((), ()))
    imp = (lax.dot_general(ov, p_hi, nt, preferred_element_type=F32)
           + lax.dot_general(ov, p_lo, nt, preferred_element_type=F32))

    jb = lax.broadcasted_iota(jnp.int32, (nblk, tq), 0)
    tt = t0 + lax.broadcasted_iota(jnp.int32, (nblk, tq), 1)
    cur = tt // L_SEL
    allowed = jb * L_SEL <= tt
    forced = (jb == 0) | (jb == cur) | (jb == cur - 1)
    score = jnp.where(forced, FORCE_SCORE, jnp.where(allowed, imp, NEG_INF))
    taken = jnp.zeros((nblk, tq), F32)
    for _ in range(ntop):
        best = jnp.max(score, axis=0, keepdims=True)
        first = jnp.min(jnp.where(score == best, jb, nblk), axis=0, keepdims=True)
        pick = jb == first
        taken = jnp.where(pick, 1.0, taken)
        score = jnp.where(pick, TAKEN_SCORE, score)
    bias_ref[...] = jnp.where(taken > 0.5, 0.0, NEG_INF).astype(bias_ref.dtype)


def _cmp_attn(qkv, kcv, bsz, seq, tq=512):
    tq = min(tq, seq)
    ncmp = seq // STRIDE_CMP
    nblk = seq // L_SEL
    ntop = min(N_SEL, nblk)
    nq = seq // tq
    return pl.pallas_call(
        functools.partial(_cmp_attn_body, tq=tq, ncmp=ncmp, nblk=nblk, ntop=ntop),
        out_shape=(jax.ShapeDtypeStruct((bsz * seq, ATT_WIDTH), F32),
                   jax.ShapeDtypeStruct((bsz, N_KV, nblk, seq), F32)),
        grid=(bsz, N_KV, nq),
        in_specs=[pl.BlockSpec((tq, HPG * HEAD_DIM), lambda b, g, i: (b * nq + i, g)),
                  pl.BlockSpec((None, None, None, ncmp, HEAD_DIM), lambda b, g, i: (b, 0, g, 0, 0)),
                  pl.BlockSpec((None, None, None, ncmp, HEAD_DIM), lambda b, g, i: (b, 1, g, 0, 0))],
        out_specs=(pl.BlockSpec((tq, HPG * HEAD_DIM), lambda b, g, i: (b * nq + i, g)),
                   pl.BlockSpec((None, None, nblk, tq), lambda b, g, i: (b, g, 0, i))),
        compiler_params=_params(("parallel", "parallel", "parallel")),
        name="nsa_cmp_attn_topk",
    )(qkv, kcv, kcv)


def _sel_attn_body(q_ref, k_ref, v_ref, bias_ref, o_ref, qa_scr, m_scr, acc_scr, s_scr, p_scr, alpha_scr,
                   *, tq, rb):
    qi = pl.program_id(2)
    tk = tq
    m_scr[...] = jnp.full_like(m_scr, NEG_INF)
    acc_scr[...] = jnp.zeros_like(acc_scr)
    for h in range(HPG):
        qa_scr[h, :, 0:HEAD_DIM] = q_ref[:, h * HEAD_DIM:(h + 1) * HEAD_DIM]
        qa_scr[h, :, HEAD_DIM:2 * HEAD_DIM] = bias_ref[...]
    nt = (((1,), (1,)), ((), ()))

    def key_tile(kt):
        k0 = pl.multiple_of(kt * tk, tk)
        key_blk = kt * (tk // L_SEL) + lax.broadcasted_iota(jnp.int32, (tk, LANES), 0) // L_SEL
        onehot = jnp.where(lax.broadcasted_iota(jnp.int32, (tk, LANES), 1) == key_blk, 1.0, 0.0).astype(BF16)
        return jnp.concatenate([k_ref[pl.ds(k0, tk), :], onehot], axis=1)

    def scores(h, k_aug, s_buf):
        s_buf[h] = lax.dot_general(qa_scr[h], k_aug, nt, preferred_element_type=F32)

    def softmax(h, s_buf, diagonal):
        for r in range(tq // rb):
            rows = slice(r * rb, (r + 1) * rb)
            s = s_buf[h, rows, :]
            if diagonal:
                col = lax.broadcasted_iota(jnp.int32, (rb, tk), 1)
                row = lax.broadcasted_iota(jnp.int32, (rb, tk), 0) + r * rb
                s = jnp.where(col <= row, s, NEG_INF)
            m_prev = m_scr[h, rows, :]
            m_next = jnp.maximum(m_prev, jnp.max(s, axis=-1, keepdims=True))
            p_scr[h, rows, :] = jnp.exp2(s - _lane_tile(m_next, tk // LANES)).astype(BF16)
            alpha_scr[h, rows, :] = jnp.exp2(m_prev - m_next)
            m_scr[h, rows, :] = m_next

    def values(h, v_aug):
        acc_scr[h] = (_lane_tile(alpha_scr[h], 2) * acc_scr[h]
                      + jnp.dot(p_scr[h], v_aug, preferred_element_type=F32))

    def tile(kt, diagonal):
        k0 = pl.multiple_of(kt * tk, tk)
        k_aug = key_tile(kt)
        v_aug = jnp.concatenate([v_ref[pl.ds(k0, tk), :], jnp.ones((tk, LANES), BF16)], axis=1)
        for h in range(HPG):
            scores(h, k_aug, s_scr)
        for h in range(HPG):
            softmax(h, s_scr, diagonal)
        for h in range(HPG):
            values(h, v_aug)

    def full_tile(kt, carry):
        tile(kt, False)
        return carry

    lax.fori_loop(0, qi, full_tile, 0)
    tile(qi, True)

    for h in range(HPG):
        o_ref[:, h * HEAD_DIM:(h + 1) * HEAD_DIM] = (acc_scr[h, :, 0:HEAD_DIM]
                                                     / acc_scr[h, :, HEAD_DIM:2 * HEAD_DIM])


def _sel_attn(qkv, bias, bsz, seq, tq=SEL_TQ, rb=SEL_RB):
    tq = min(tq, seq)
    rb = min(rb, tq)
    nq = seq // tq
    kcol = ATT_WIDTH // HEAD_DIM
    return pl.pallas_call(
        functools.partial(_sel_attn_body, tq=tq, rb=rb),
        out_shape=jax.ShapeDtypeStruct((bsz * seq, ATT_WIDTH), F32),
        grid=(bsz, N_KV, nq),
        in_specs=[pl.BlockSpec((tq, HPG * HEAD_DIM), lambda b, g, i: (b * nq + i, g)),
                  pl.BlockSpec((seq, HEAD_DIM), lambda b, g, i: (b, kcol + g)),
                  pl.BlockSpec((seq, HEAD_DIM), lambda b, g, i: (b, kcol + N_KV + g)),
                  pl.BlockSpec((None, None, tq, LANES), lambda b, g, i: (b, g, i, 0))],
        out_specs=pl.BlockSpec((tq, HPG * HEAD_DIM), lambda b, g, i: (b * nq + i, g)),
        scratch_shapes=[pltpu.VMEM((HPG, tq, 2 * HEAD_DIM), BF16),
                        pltpu.VMEM((HPG, tq, LANES), F32),
                        pltpu.VMEM((HPG, tq, 2 * HEAD_DIM), F32),
                        pltpu.VMEM((HPG, tq, tq), F32),
                        pltpu.VMEM((HPG, tq, tq), BF16),
                        pltpu.VMEM((HPG, tq, LANES), F32)],
        compiler_params=_params(("parallel", "parallel", "arbitrary")),
        name="nsa_sel_attn",
    )(qkv, qkv, qkv, bias)


def _win_attn_body(q_ref, kc_ref, kp_ref, vc_ref, vp_ref, ocmp_ref, osel_ref, gate_ref, o_ref, *, tq):
    qi = pl.program_id(2)
    nt = (((1,), (1,)), ((), ()))
    ones = jnp.ones((tq, LANES), BF16)
    vc_aug = jnp.concatenate([vc_ref[...], ones], axis=1)
    vp_aug = jnp.concatenate([vp_ref[...], ones], axis=1)
    for r in range(tq // ATT_ROWS):
        rows = slice(r * ATT_ROWS, (r + 1) * ATT_ROWS)
        n_cur, p_lo = (r + 1) * ATT_ROWS, r * ATT_ROWS
        row_c = lax.broadcasted_iota(jnp.int32, (ATT_ROWS, n_cur), 0) + r * ATT_ROWS
        ok_cur = lax.broadcasted_iota(jnp.int32, (ATT_ROWS, n_cur), 1) <= row_c
        row_p = lax.broadcasted_iota(jnp.int32, (ATT_ROWS, tq - p_lo), 0) + r * ATT_ROWS
        ok_prev = (lax.broadcasted_iota(jnp.int32, (ATT_ROWS, tq - p_lo), 1) + p_lo > row_p) & (qi > 0)
        gates = jax.nn.sigmoid(gate_ref[rows, :])
        for h in range(HPG):
            sl = slice(h * HEAD_DIM, (h + 1) * HEAD_DIM)
            qh = q_ref[rows, sl]
            s1 = jnp.where(ok_cur, lax.dot_general(qh, kc_ref[0:n_cur, :], nt, preferred_element_type=F32), NEG_INF)
            s0 = jnp.where(ok_prev, lax.dot_general(qh, kp_ref[p_lo:tq, :], nt, preferred_element_type=F32), NEG_INF)
            m = jnp.maximum(jnp.max(s1, axis=-1, keepdims=True), jnp.max(s0, axis=-1, keepdims=True))
            p1 = jnp.exp2(s1 - m).astype(BF16)
            p0 = jnp.exp2(s0 - m).astype(BF16)
            acc = (jnp.dot(p1, vc_aug[0:n_cur, :], preferred_element_type=F32)
                   + jnp.dot(p0, vp_aug[p_lo:tq, :], preferred_element_type=F32))
            ow = acc[:, 0:HEAD_DIM] / acc[:, HEAD_DIM:2 * HEAD_DIM]
            g_cmp = gates[:, h:h + 1]
            g_sel = gates[:, HPG + h:HPG + h + 1]
            g_win = gates[:, 2 * HPG + h:2 * HPG + h + 1]
            o_ref[rows, sl] = (g_cmp * ocmp_ref[rows, sl] + g_sel * osel_ref[rows, sl]
                               + g_win * ow).astype(o_ref.dtype)


def _win_attn_combine(qkv, o_cmp, o_sel, gates, bsz, seq):
    tq = WINDOW
    assert seq % tq == 0
    nq = seq // tq
    kcol = ATT_WIDTH // HEAD_DIM + 2 * N_KV
    qspec = pl.BlockSpec((tq, HPG * HEAD_DIM), lambda b, g, i: (b * nq + i, g))
    cur = lambda c: pl.BlockSpec((tq, HEAD_DIM), lambda b, g, i: (b * nq + i, kcol + c * N_KV + g))
    prev = lambda c: pl.BlockSpec((tq, HEAD_DIM),
                                  lambda b, g, i: (b * nq + jnp.maximum(i - 1, 0), kcol + c * N_KV + g))
    return pl.pallas_call(
        functools.partial(_win_attn_body, tq=tq),
        out_shape=jax.ShapeDtypeStruct((bsz * seq, ATT_WIDTH), BF16),
        grid=(bsz, N_KV, nq),
        in_specs=[qspec, cur(0), prev(0), cur(1), prev(1), qspec, qspec,
                  pl.BlockSpec((None, None, tq, LANES), lambda b, g, i: (b, g, i, 0))],
        out_specs=qspec,
        compiler_params=_params(("parallel", "parallel", "parallel")),
        name="nsa_win_attn_gate",
    )(qkv, qkv, qkv, qkv, qkv, o_cmp, o_sel, gates)


def _merge_body(hn_ref, ya_ref, yb_ref, wga_ref, wgb_ref, wa_ref, wb_ref, o_ref):
    hn = hn_ref[...]
    ga = jnp.dot(hn, wga_ref[...], preferred_element_type=F32)
    gb = jnp.dot(hn, wgb_ref[...], preferred_element_type=F32)
    pa = jnp.dot(ya_ref[...], wa_ref[...], preferred_element_type=F32)
    pb = jnp.dot(yb_ref[...], wb_ref[...], preferred_element_type=F32)
    o_ref[...] = (jax.nn.sigmoid(ga) * pa + jax.nn.sigmoid(gb) * pb).astype(o_ref.dtype)


def _merge(hn, ya, yb, wg, wa, wb, tm=512, tn=512):
    m, d = hn.shape
    ka = ya.shape[1]
    tm = min(tm, m)
    row = lambda k: pl.BlockSpec((tm, k), lambda i, j: (i, 0))
    col = lambda k: pl.BlockSpec((k, tn), lambda i, j: (0, j))
    col_b = pl.BlockSpec((d, tn), lambda i, j: (0, d // tn + j))
    return pl.pallas_call(
        _merge_body,
        out_shape=jax.ShapeDtypeStruct((m, d), BF16),
        grid=(m // tm, d // tn),
        in_specs=[row(d), row(ka), row(ka), col(d), col_b, col(ka), col(ka)],
        out_specs=pl.BlockSpec((tm, tn), lambda i, j: (i, j)),
        compiler_params=_params(("parallel", "arbitrary")),
        name="mixer_merge",
    )(hn, ya, yb, wg, wg, wa, wb)


def _rms(x, g):
    return x * lax.rsqrt(jnp.mean(x * x, axis=-1, keepdims=True) + RMS_EPS) * g


def _res_norm2_body(raw_ref, x_ref, gpost_ref, gpre_ref, h_ref, hn_ref):
    h = x_ref[...] + _rms(raw_ref[...], gpost_ref[...])
    h_ref[...] = h
    hn_ref[...] = _rms(h, gpre_ref[...]).astype(hn_ref.dtype)


def _res_norm2(raw, x, g_post, g_pre, tm=256):
    m, d = x.shape
    tm = min(tm, m)
    rows = pl.BlockSpec((tm, d), lambda i: (i, 0))
    vec = pl.BlockSpec((1, d), lambda i: (0, 0))
    return pl.pallas_call(
        _res_norm2_body,
        out_shape=(jax.ShapeDtypeStruct((m, d), F32), jax.ShapeDtypeStruct((m, d), BF16)),
        grid=(m // tm,),
        in_specs=[rows, rows, vec, vec],
        out_specs=(rows, rows),
        compiler_params=_params(("parallel",)),
        name="residual_norm_prenorm",
    )(raw, x, g_post.reshape(1, d), g_pre.reshape(1, d))


def _res_norm_body(raw_ref, x_ref, g_ref, o_ref):
    o_ref[...] = x_ref[...] + _rms(raw_ref[...], g_ref[...])


def _res_norm(raw, x, g, tm=256):
    m, d = x.shape
    tm = min(tm, m)
    rows = pl.BlockSpec((tm, d), lambda i: (i, 0))
    return pl.pallas_call(
        _res_norm_body,
        out_shape=jax.ShapeDtypeStruct((m, d), F32),
        grid=(m // tm,),
        in_specs=[rows, rows, pl.BlockSpec((1, d), lambda i: (0, 0))],
        out_specs=rows,
        compiler_params=_params(("parallel",)),
        name="residual_norm",
    )(raw, x, g.reshape(1, d))


def _swiglu_body(a_ref, wg_ref, wu_ref, o_ref, wg_scr, wu_scr):
    @pl.when(pl.program_id(1) == 0)
    def _():
        wg_scr[...] = wg_ref[...].astype(BF16)
        wu_scr[...] = wu_ref[...].astype(BF16)

    a = a_ref[...]
    g = jnp.dot(a, wg_scr[...], preferred_element_type=F32)
    u = jnp.dot(a, wu_scr[...], preferred_element_type=F32)
    o_ref[...] = (g * jax.nn.sigmoid(g) * u).astype(o_ref.dtype)


def _swiglu(a, wg, wu, tm=1024, tn=256):
    m, k = a.shape
    n = wg.shape[1]
    tm = min(tm, m)
    assert n % tn == 0
    return pl.pallas_call(
        _swiglu_body,
        out_shape=jax.ShapeDtypeStruct((m, n), BF16),
        grid=(n // tn, m // tm),
        in_specs=[pl.BlockSpec((tm, k), lambda j, i: (i, 0)),
                  pl.BlockSpec((k, tn), lambda j, i: (0, j)),
                  pl.BlockSpec((k, tn), lambda j, i: (0, j))],
        out_specs=pl.BlockSpec((tm, tn), lambda j, i: (i, j)),
        scratch_shapes=[pltpu.VMEM((k, tn), BF16), pltpu.VMEM((k, tn), BF16)],
        compiler_params=_params(("arbitrary", "arbitrary")),
        name="ffn_swiglu",
    )(a, wg, wu)


def kernel(x, norm_mix_pre, w_in, ssm_a_re, ssm_a_im, ssm_log_dt, ssm_b_re, ssm_b_im, ssm_c_re, ssm_c_im, ssm_d, ssm_w_glu, ssm_b_glu, cmp_pe_k, cmp_w1_k, cmp_w2_k, cmp_pe_v, cmp_w1_v, cmp_w2_v, w_proj_a, w_proj_b, w_out, norm_mix_post, norm_ffn_pre, w_ffn_gate, w_ffn_up, w_ffn_down, norm_ffn_post):
    bsz, seq, d = x.shape
    m = bsz * seq
    depth = w_in.shape[0]
    h = x.reshape(m, d)
    o_q, o_kvc, o_kv, o_gn = SSM_WIDTH, SSM_WIDTH + ATT_WIDTH, SSM_WIDTH + ATT_WIDTH + 2 * KV_WIDTH, 7168
    o_ga = o_gn + 3 * N_HEADS
    o_gb = o_ga + D_MODEL
    bf = lambda a: a.astype(BF16)
    tn_in = 512
    nb_u, nb_q = SSM_WIDTH // tn_in, ATT_WIDTH // tn_in
    qk_scale = HEAD_DIM ** -0.5 * LOG2E
    for l in range(depth):
        wt = jnp.swapaxes(w_in[l], 0, 1)
        w_g = _rows_transposed_cast(wt, o_ga, 2 * D_MODEL, BF16)

        hn = _rmsnorm(h, norm_mix_pre[l], BF16)
        proj_f32 = _matmul_f32w(hn, wt, SSM_WIDTH + 2 * KV_WIDTH, F32, 1024, tn_in, trans_w=True,
                                col_block=lambda j: jnp.where(j < nb_u, j, j - nb_u + o_kvc // tn_in),
                                name="in_proj_f32")
        qkv = _matmul_f32w(hn, wt, ATT_WIDTH + 4 * KV_WIDTH, BF16, 1024, tn_in, trans_w=True,
                           col_block=lambda j: jnp.where(j < nb_q, j + o_q // tn_in, j - nb_q + o_kv // tn_in),
                           scale_fn=lambda j: jnp.where(j < nb_q, qk_scale, 1.0),
                           name="in_proj_bf16")
        gn = _matmul_f32w(hn, wt, LANES, F32, 1024, LANES, trans_w=True, col_block=lambda j: o_gn // LANES,
                          name="in_proj_gates")

        y = _s5_mixer(proj_f32.reshape(bsz, seq, -1), ssm_a_re[l], ssm_a_im[l], ssm_log_dt[l],
                      ssm_b_re[l], ssm_b_im[l], ssm_c_re[l], ssm_c_im[l], ssm_d[l])
        y_a = _glu(y.reshape(m, SSM_WIDTH), bf(ssm_w_glu[l]), ssm_b_glu[l])

        pe = jnp.stack([cmp_pe_k[l], cmp_pe_v[l]])
        w1 = bf(jnp.stack([cmp_w1_k[l], cmp_w1_v[l]]))
        w2 = bf(jnp.stack([cmp_w2_k[l], cmp_w2_v[l]]))
        kcv = _compress(proj_f32, SSM_WIDTH, pe, w1, w2, bsz, seq)
        o_cmp, bias_t = _cmp_attn(qkv, kcv, bsz, seq)
        nblk = seq // L_SEL
        bias = bf(jnp.pad(jnp.swapaxes(bias_t, 2, 3), ((0, 0), (0, 0), (0, 0), (0, LANES - nblk))))
        o_sel = _sel_attn(qkv, bias, bsz, seq)
        gates = gn[:, :3 * N_HEADS].reshape(bsz, seq, 3, N_KV, HPG).transpose(0, 3, 1, 2, 4)
        gates = jnp.pad(gates.reshape(bsz, N_KV, seq, 3 * HPG), ((0, 0), (0, 0), (0, 0), (0, LANES - 3 * HPG)))
        y_b = _win_attn_combine(qkv, o_cmp, o_sel, gates, bsz, seq)

        merged = _merge(hn, y_a, y_b, w_g, bf(w_proj_a[l]), bf(w_proj_b[l]))
        mix = _matmul_f32w(merged, w_out[l], D_MODEL, F32, 1024, 512, name="out_proj")
        h, hn2 = _res_norm2(mix, h, norm_mix_post[l], norm_ffn_pre[l])

        act = _swiglu(hn2, w_ffn_gate[l], w_ffn_up[l])
        f = _matmul(act, bf(w_ffn_down[l]), F32, 512, 512, name="ffn_down")
        h = _res_norm(f, h, norm_ffn_post[l])
    return h.reshape(bsz, seq, d)
```

```python
import functools
import math

import jax
import jax.numpy as jnp
from jax import lax
from jax.experimental import pallas as pl
from jax.experimental.pallas import tpu as pltpu

F32 = jnp.float32
BF16 = jnp.bfloat16

D_MODEL = 4096
SSM_WIDTH = 2048
SSM_GROUP = 16
SSM_GROUPS = 128
SSM_STATE = 64
N_HEADS = 16
HEAD_DIM = 128
N_KV = 4
HPG = 4
ATT_WIDTH = 2048
KV_WIDTH = 512
L_CMP = 32
STRIDE_CMP = 16
L_SEL = 64
N_SEL = 16
WINDOW = 512
D_FF = 11008
RMS_EPS = 1e-6
NEG_INF = -1e30
FORCE_SCORE = 1e9
TAKEN_SCORE = -3e38
LOG2E = math.log2(math.e)

V7X_VMEM_LIMIT_BYTES = 56 * 1024 * 1024
LANES = 128
SUBLANES = 8

S5_GT = 16
S5_NC = S5_GT * SSM_GROUP
S5_NP = S5_GT * SSM_STATE
S5_TILES = SSM_GROUPS // S5_GT
S5_SLABS = 2 * S5_NP // LANES

ATT_ROWS = 128
SEL_TQ, SEL_RB = 512, 64


def _params(sem, vmem=V7X_VMEM_LIMIT_BYTES):
    return pltpu.CompilerParams(dimension_semantics=sem, vmem_limit_bytes=vmem)


def _lane_tile(x, n):
    return jnp.concatenate([x] * n, axis=1)


def _rmsnorm_body(x_ref, g_ref, o_ref):
    x = x_ref[...]
    ms = jnp.mean(x * x, axis=-1, keepdims=True)
    o_ref[...] = (x * lax.rsqrt(ms + RMS_EPS) * g_ref[...]).astype(o_ref.dtype)


def _rmsnorm(x, gain, out_dtype, tm=256):
    m, d = x.shape
    tm = min(tm, m)
    return pl.pallas_call(
        _rmsnorm_body,
        out_shape=jax.ShapeDtypeStruct((m, d), out_dtype),
        grid=(m // tm,),
        in_specs=[pl.BlockSpec((tm, d), lambda i: (i, 0)), pl.BlockSpec((1, d), lambda i: (0, 0))],
        out_specs=pl.BlockSpec((tm, d), lambda i: (i, 0)),
        compiler_params=_params(("parallel",)),
        name="rmsnorm",
    )(x, gain.reshape(1, d))


def _mm_body(a_ref, w_ref, o_ref):
    o_ref[...] = jnp.dot(a_ref[...], w_ref[...], preferred_element_type=F32).astype(o_ref.dtype)


def _matmul(a, w, out_dtype, tm, tn, name="matmul"):
    m, k = a.shape
    n = w.shape[1]
    tm, tn = min(tm, m), min(tn, n)
    assert m % tm == 0 and n % tn == 0
    return pl.pallas_call(
        _mm_body,
        out_shape=jax.ShapeDtypeStruct((m, n), out_dtype),
        grid=(m // tm, n // tn),
        in_specs=[pl.BlockSpec((tm, k), lambda i, j: (i, 0)), pl.BlockSpec((k, tn), lambda i, j: (0, j))],
        out_specs=pl.BlockSpec((tm, tn), lambda i, j: (i, j)),
        compiler_params=_params(("parallel", "arbitrary")),
        name=name,
    )(a, w)


_NT = (((1,), (1,)), ((), ()))


def _mm_f32w_body(a_ref, w_ref, o_ref, wbf_scr, *, scale_fn, trans_w):
    @pl.when(pl.program_id(1) == 0)
    def _():
        wbf_scr[...] = w_ref[...].astype(BF16)

    if trans_w:
        acc = lax.dot_general(a_ref[...], wbf_scr[...], _NT, preferred_element_type=F32)
    else:
        acc = jnp.dot(a_ref[...], wbf_scr[...], preferred_element_type=F32)
    if scale_fn is not None:
        acc = acc * scale_fn(pl.program_id(0))
    o_ref[...] = acc.astype(o_ref.dtype)


def _matmul_f32w(a, w, n_out, out_dtype, tm, tn, col_block=None, scale_fn=None, trans_w=False,
                 name="matmul_f32w"):
    m, k = a.shape
    tm = min(tm, m)
    assert m % tm == 0 and n_out % tn == 0
    col_block = col_block or (lambda j: j)
    if trans_w:
        w_spec = pl.BlockSpec((tn, k), lambda j, i: (col_block(j), 0))
        w_scratch = pltpu.VMEM((tn, k), BF16)
    else:
        w_spec = pl.BlockSpec((k, tn), lambda j, i: (0, col_block(j)))
        w_scratch = pltpu.VMEM((k, tn), BF16)
    return pl.pallas_call(
        functools.partial(_mm_f32w_body, scale_fn=scale_fn, trans_w=trans_w),
        out_shape=jax.ShapeDtypeStruct((m, n_out), out_dtype),
        grid=(n_out // tn, m // tm),
        in_specs=[pl.BlockSpec((tm, k), lambda j, i: (i, 0)), w_spec],
        out_specs=pl.BlockSpec((tm, tn), lambda j, i: (i, j)),
        scratch_shapes=[w_scratch],
        compiler_params=_params(("arbitrary", "arbitrary")),
        name=name,
    )(a, w)


def _rows_transposed_cast_body(w_ref, o_ref):
    o_ref[...] = w_ref[...].T.astype(o_ref.dtype)


def _rows_transposed_cast(wt, row0, n_rows, out_dtype, tr=512):
    k = wt.shape[1]
    assert row0 % (2 * SUBLANES) == 0 and n_rows % tr == 0
    return pl.pallas_call(
        _rows_transposed_cast_body,
        out_shape=jax.ShapeDtypeStruct((k, n_rows), out_dtype),
        grid=(n_rows // tr,),
        in_specs=[pl.BlockSpec((pl.Element(tr), pl.Element(k)),
                               lambda c: (pl.multiple_of(row0 + c * tr, 2 * SUBLANES), 0))],
        out_specs=pl.BlockSpec((k, tr), lambda c: (0, c)),
        compiler_params=_params(("parallel",)),
        name="gate_weight_cast",
    )(wt)


def _s5_body(u_ref, are_ref, aim_ref, ldt_ref, bre_ref, bim_ref, cre_ref, cim_ref, d_ref, y_ref,
             abar_scr, bbar_scr, cmat_scr, h_scr, st_scr, *, tc, bsz, tps):
    ti = pl.program_id(1)
    half = S5_SLABS // 2

    @pl.when(ti == 0)
    def _():
        row_tile = lax.broadcasted_iota(jnp.int32, (SUBLANES, S5_NP), 0) // bsz
        same_group = (lax.broadcasted_iota(jnp.int32, (S5_NC, S5_NP), 0) // SSM_GROUP
                      == lax.broadcasted_iota(jnp.int32, (S5_NC, S5_NP), 1) // SSM_STATE)
        bdiag = lambda x: jnp.where(same_group, jnp.concatenate([x] * S5_GT, axis=0), 0.0)
        abr8 = jnp.zeros((SUBLANES, S5_NP), F32)
        abi8 = jnp.zeros((SUBLANES, S5_NP), F32)
        for k in range(tps):
            cols = slice(k * S5_NP, (k + 1) * S5_NP)
            ar, ai = are_ref[:, cols], aim_ref[:, cols]
            dt = jnp.exp(ldt_ref[:, cols])
            decay = jnp.exp(dt * ar)
            abr, abi = decay * jnp.cos(dt * ai), decay * jnp.sin(dt * ai)
            den = ar * ar + ai * ai
            zr = ((abr - 1.0) * ar + abi * ai) / den
            zi = (abi * ar - (abr - 1.0) * ai) / den
            abr8 = jnp.where(row_tile == k, abr, abr8)
            abi8 = jnp.where(row_tile == k, abi, abi8)
            br, bi = bdiag(bre_ref[k]), bdiag(bim_ref[k])
            bbar_scr[k, :, 0:S5_NP] = (zr * br - zi * bi).astype(BF16)
            bbar_scr[k, :, S5_NP:2 * S5_NP] = (zr * bi + zi * br).astype(BF16)
            cmat_scr[k, :, 0:S5_NP] = bdiag(cre_ref[k]).astype(BF16)
            cmat_scr[k, :, S5_NP:2 * S5_NP] = (-bdiag(cim_ref[k])).astype(BF16)
        abar_scr[0:SUBLANES, :] = abr8
        abar_scr[SUBLANES:2 * SUBLANES, :] = abi8
        st_scr[...] = jnp.zeros_like(st_scr)

    for k in range(tps):
        cols = slice(k * S5_NC, (k + 1) * S5_NC)
        ub = jnp.concatenate([u_ref[b, :, cols] for b in range(bsz)], axis=0).astype(BF16)
        for s2 in range(S5_SLABS // 2):
            bu = jnp.dot(ub, bbar_scr[k, :, s2 * 2 * LANES:(s2 + 1) * 2 * LANES], preferred_element_type=F32)
            for b in range(bsz):
                for e in range(2):
                    h_scr[2 * s2 + e, pl.ds(k * bsz + b, tc, stride=SUBLANES), :] = bu[b * tc:(b + 1) * tc,
                                                                                      e * LANES:(e + 1) * LANES]

    ar = abar_scr[0:SUBLANES, :]
    ai = abar_scr[SUBLANES:2 * SUBLANES, :]

    def step(t, carry):
        hr, hi = carry
        r0 = pl.multiple_of(t * SUBLANES, SUBLANES)
        bur = jnp.concatenate([h_scr[s, pl.ds(r0, SUBLANES), :] for s in range(half)], axis=1)
        bui = jnp.concatenate([h_scr[half + s, pl.ds(r0, SUBLANES), :] for s in range(half)], axis=1)
        nhr = ar * hr - ai * hi + bur
        nhi = ar * hi + ai * hr + bui
        for s in range(half):
            h_scr[s, pl.ds(r0, SUBLANES), :] = nhr[:, s * LANES:(s + 1) * LANES]
            h_scr[half + s, pl.ds(r0, SUBLANES), :] = nhi[:, s * LANES:(s + 1) * LANES]
        return nhr, nhi

    hr, hi = lax.fori_loop(0, tc, step, (st_scr[0:SUBLANES, :], st_scr[SUBLANES:2 * SUBLANES, :]), unroll=2)
    st_scr[0:SUBLANES, :] = hr
    st_scr[SUBLANES:2 * SUBLANES, :] = hi

    for k in range(tps):
        cols = slice(k * S5_NC, (k + 1) * S5_NC)

        hb = jnp.concatenate(
            [jnp.concatenate([h_scr[s, pl.ds(k * bsz + b, tc, stride=SUBLANES), :] for s in range(S5_SLABS)], axis=1)
             for b in range(bsz)], axis=0).astype(BF16)
        ch = lax.dot_general(hb, cmat_scr[k], _NT, preferred_element_type=F32)
        for b in range(bsz):
            y = ch[b * tc:(b + 1) * tc, :] + d_ref[:, cols] * u_ref[b, :, cols]
            y_ref[b, :, cols] = jax.nn.gelu(y)


def _s5_mixer(u3, a_re, a_im, log_dt, b_re, b_im, c_re, c_im, d_skip, tc=128):
    bsz, seq, _ = u3.shape
    assert SUBLANES % bsz == 0
    tps = min(SUBLANES // bsz, S5_TILES)
    assert tps * bsz == SUBLANES
    tc = min(tc, seq)

    def per_tile_b(b):
        return (b.reshape(S5_TILES, S5_GT, SSM_STATE, SSM_GROUP).transpose(0, 3, 1, 2)
                .reshape(S5_TILES, SSM_GROUP, S5_NP))

    def per_tile_c(c):
        return (c.reshape(S5_TILES, S5_GT, SSM_GROUP, SSM_STATE).transpose(0, 2, 1, 3)
                .reshape(S5_TILES, SSM_GROUP, S5_NP))

    flat = lambda a: a.reshape(1, SSM_GROUPS * SSM_STATE)
    ldt = jnp.repeat(log_dt, SSM_STATE).reshape(1, SSM_GROUPS * SSM_STATE)
    vec_spec = pl.BlockSpec((1, tps * S5_NP), lambda g, t: (0, g))
    bc_spec = pl.BlockSpec((tps, SSM_GROUP, S5_NP), lambda g, t: (g, 0, 0))
    return pl.pallas_call(
        functools.partial(_s5_body, tc=tc, bsz=bsz, tps=tps),
        out_shape=jax.ShapeDtypeStruct((bsz, seq, SSM_WIDTH), F32),
        grid=(S5_TILES // tps, seq // tc),
        in_specs=[
            pl.BlockSpec((bsz, tc, tps * S5_NC), lambda g, t: (0, t, g)),
            vec_spec, vec_spec, vec_spec, bc_spec, bc_spec, bc_spec, bc_spec,
            pl.BlockSpec((1, tps * S5_NC), lambda g, t: (0, g)),
        ],
        out_specs=pl.BlockSpec((bsz, tc, tps * S5_NC), lambda g, t: (0, t, g)),
        scratch_shapes=[
            pltpu.VMEM((2 * SUBLANES, S5_NP), F32),
            pltpu.VMEM((tps, S5_NC, 2 * S5_NP), BF16),
            pltpu.VMEM((tps, S5_NC, 2 * S5_NP), BF16),
            pltpu.VMEM((S5_SLABS, tc * SUBLANES, LANES), F32),
            pltpu.VMEM((2 * SUBLANES, S5_NP), F32),
        ],
        compiler_params=_params(("arbitrary", "arbitrary")),
        name="s5_scan",
    )(u3, flat(a_re), flat(a_im), ldt, per_tile_b(b_re), per_tile_b(b_im), per_tile_c(c_re), per_tile_c(c_im),
      d_skip.reshape(1, SSM_WIDTH))


def _glu_body(y_ref, w_ref, b_ref, o_ref, ybf_scr, *, tn):
    j = pl.program_id(1)

    @pl.when(j == 0)
    def _():
        ybf_scr[...] = y_ref[...].astype(BF16)

    z = jnp.dot(ybf_scr[...], w_ref[...], preferred_element_type=F32) + b_ref[...]
    yt = y_ref[:, pl.ds(pl.multiple_of(j * tn, tn), tn)]
    o_ref[...] = (yt * jax.nn.sigmoid(z)).astype(o_ref.dtype)


def _glu(y, w_bf, bias, tm=1024, tn=1024):
    m, k = y.shape
    tm = min(tm, m)
    return pl.pallas_call(
        functools.partial(_glu_body, tn=tn),
        out_shape=jax.ShapeDtypeStruct((m, k), BF16),
        grid=(m // tm, k // tn),
        in_specs=[pl.BlockSpec((tm, k), lambda i, j: (i, 0)),
                  pl.BlockSpec((k, tn), lambda i, j: (0, j)),
                  pl.BlockSpec((1, tn), lambda i, j: (0, j))],
        out_specs=pl.BlockSpec((tm, tn), lambda i, j: (i, j)),
        scratch_shapes=[pltpu.VMEM((tm, k), BF16)],
        compiler_params=_params(("parallel", "arbitrary")),
        name="s5_glu",
    )(y, w_bf, bias.reshape(1, k))


def _compress_body(x_ref, pe_ref, w1_ref, w2_ref, o_ref, *, ncmp):
    half = L_CMP // 2
    acc_a = jnp.zeros((ncmp, HEAD_DIM), F32)
    acc_b = jnp.zeros((ncmp, HEAD_DIM), F32)
    for r in range(half):
        xr = x_ref[pl.ds(r, ncmp, stride=STRIDE_CMP), :]
        xa = (xr + pe_ref[r:r + 1, :]).astype(BF16)
        xb = (xr + pe_ref[half + r:half + r + 1, :]).astype(BF16)
        acc_a += jnp.dot(xa, w1_ref[r * HEAD_DIM:(r + 1) * HEAD_DIM, :], preferred_element_type=F32)
        acc_b += jnp.dot(xb, w1_ref[(half + r) * HEAD_DIM:(half + r + 1) * HEAD_DIM, :],
                         preferred_element_type=F32)
    pre = acc_a + jnp.concatenate([acc_b[1:], acc_b[:1]], axis=0)
    o_ref[...] = jnp.dot(jax.nn.gelu(pre).astype(BF16), w2_ref[...], preferred_element_type=F32).astype(o_ref.dtype)


def _compress(proj_f32, col0, pe, w1_bf, w2_bf, bsz, seq):
    ncmp = seq // STRIDE_CMP
    cb = col0 // HEAD_DIM
    return pl.pallas_call(
        functools.partial(_compress_body, ncmp=ncmp),
        out_shape=jax.ShapeDtypeStruct((bsz, 2, N_KV, ncmp, HEAD_DIM), BF16),
        grid=(bsz, 2, N_KV),
        in_specs=[pl.BlockSpec((seq, HEAD_DIM), lambda b, w, g: (b, cb + w * N_KV + g)),
                  pl.BlockSpec((None, L_CMP, HEAD_DIM), lambda b, w, g: (w, 0, 0)),
                  pl.BlockSpec((None, L_CMP * HEAD_DIM, HEAD_DIM), lambda b, w, g: (w, 0, 0)),
                  pl.BlockSpec((None, HEAD_DIM, HEAD_DIM), lambda b, w, g: (w, 0, 0))],
        out_specs=pl.BlockSpec((None, None, None, ncmp, HEAD_DIM), lambda b, w, g: (b, w, g, 0, 0)),
        compiler_params=_params(("parallel", "parallel", "parallel")),
        name="nsa_compress",
    )(proj_f32, pe, w1_bf, w2_bf)


def _cmp_attn_body(q_ref, kc_ref, vc_ref, o_ref, bias_ref, *, tq, ncmp, nblk, ntop):
    qi = pl.program_id(2)
    t0 = qi * tq
    tpos = t0 + lax.broadcasted_iota(jnp.int32, (tq, ncmp), 0)
    blk_end = lax.broadcasted_iota(jnp.int32, (tq, ncmp), 1) * STRIDE_CMP + (L_CMP - 1)
    ok = blk_end <= tpos
    okf = ok.astype(F32)
    kc = kc_ref[...]
    vc = vc_ref[...]
    psum = jnp.zeros((tq, ncmp), F32)
    for h in range(HPG):
        qh = q_ref[:, h * HEAD_DIM:(h + 1) * HEAD_DIM]
        s = lax.dot_general(qh, kc, (((1,), (1,)), ((), ())), preferred_element_type=F32)
        s = jnp.where(ok, s, NEG_INF)
        e = jnp.exp2(s - jnp.max(s, axis=-1, keepdims=True))
        p = e / jnp.sum(e, axis=-1, keepdims=True) * okf
        o_ref[:, h * HEAD_DIM:(h + 1) * HEAD_DIM] = jnp.dot(p.astype(BF16), vc, preferred_element_type=F32)
        psum = psum + p

    jj = lax.broadcasted_iota(jnp.int32, (nblk, ncmp), 0)
    nn = lax.broadcasted_iota(jnp.int32, (nblk, ncmp), 1)
    ov = ((nn * STRIDE_CMP < (jj + 1) * L_SEL) & (nn * STRIDE_CMP + L_CMP > jj * L_SEL)).astype(BF16)
    p_hi = psum.astype(BF16)
    p_lo = (psum - p_hi.astype(F32)).astype(BF16)
    nt = (((1,), (1,)), ((), ()))
    imp = (lax.dot_general(ov, p_hi, nt, preferred_element_type=F32)
           + lax.dot_general(ov, p_lo, nt, preferred_element_type=F32))

    jb = lax.broadcasted_iota(jnp.int32, (nblk, tq), 0)
    tt = t0 + lax.broadcasted_iota(jnp.int32, (nblk, tq), 1)
    cur = tt // L_SEL
    allowed = jb * L_SEL <= tt
    forced = (jb == 0) | (jb == cur) | (jb == cur - 1)
    score = jnp.where(forced, FORCE_SCORE, jnp.where(allowed, imp, NEG_INF))
    taken = jnp.zeros((nblk, tq), F32)
    for _ in range(ntop):
        best = jnp.max(score, axis=0, keepdims=True)
        first = jnp.min(jnp.where(score == best, jb, nblk), axis=0, keepdims=True)
        pick = jb == first
        taken = jnp.where(pick, 1.0, taken)
        score = jnp.where(pick, TAKEN_SCORE, score)
    bias_ref[...] = jnp.where(taken > 0.5, 0.0, NEG_INF).astype(bias_ref.dtype)


def _cmp_attn(qkv, kcv, bsz, seq, tq=512):
    tq = min(tq, seq)
    ncmp = seq // STRIDE_CMP
    nblk = seq // L_SEL
    ntop = min(N_SEL, nblk)
    nq = seq // tq
    return pl.pallas_call(
        functools.partial(_cmp_attn_body, tq=tq, ncmp=ncmp, nblk=nblk, ntop=ntop),
        out_shape=(jax.ShapeDtypeStruct((bsz * seq, ATT_WIDTH), F32),
                   jax.ShapeDtypeStruct((bsz, N_KV, nblk, seq), F32)),
        grid=(bsz, N_KV, nq),
        in_specs=[pl.BlockSpec((tq, HPG * HEAD_DIM), lambda b, g, i: (b * nq + i, g)),
                  pl.BlockSpec((None, None, None, ncmp, HEAD_DIM), lambda b, g, i: (b, 0, g, 0, 0)),
                  pl.BlockSpec((None, None, None, ncmp, HEAD_DIM), lambda b, g, i: (b, 1, g, 0, 0))],
        out_specs=(pl.BlockSpec((tq, HPG * HEAD_DIM), lambda b, g, i: (b * nq + i, g)),
                   pl.BlockSpec((None, None, nblk, tq), lambda b, g, i: (b, g, 0, i))),
        compiler_params=_params(("parallel", "parallel", "parallel")),
        name="nsa_cmp_attn_topk",
    )(qkv, kcv, kcv)


def _sel_attn_body(q_ref, k_ref, v_ref, bias_ref, o_ref, qa_scr, m_scr, acc_scr, s_scr, p_scr, alpha_scr,
                   *, tq, rb):
    qi = pl.program_id(2)
    tk = tq
    m_scr[...] = jnp.full_like(m_scr, NEG_INF)
    acc_scr[...] = jnp.zeros_like(acc_scr)
    for h in range(HPG):
        qa_scr[h, :, 0:HEAD_DIM] = q_ref[:, h * HEAD_DIM:(h + 1) * HEAD_DIM]
        qa_scr[h, :, HEAD_DIM:2 * HEAD_DIM] = bias_ref[...]
    nt = (((1,), (1,)), ((), ()))

    def key_tile(kt):
        k0 = pl.multiple_of(kt * tk, tk)
        key_blk = kt * (tk // L_SEL) + lax.broadcasted_iota(jnp.int32, (tk, LANES), 0) // L_SEL
        onehot = jnp.where(lax.broadcasted_iota(jnp.int32, (tk, LANES), 1) == key_blk, 1.0, 0.0).astype(BF16)
        return jnp.concatenate([k_ref[pl.ds(k0, tk), :], onehot], axis=1)

    def scores(h, k_aug, s_buf):
        s_buf[h] = lax.dot_general(qa_scr[h], k_aug, nt, preferred_element_type=F32)

    def softmax(h, s_buf, diagonal):
        for r in range(tq // rb):
            rows = slice(r * rb, (r + 1) * rb)
            s = s_buf[h, rows, :]
            if diagonal:
                col = lax.broadcasted_iota(jnp.int32, (rb, tk), 1)
                row = lax.broadcasted_iota(jnp.int32, (rb, tk), 0) + r * rb
                s = jnp.where(col <= row, s, NEG_INF)
            m_prev = m_scr[h, rows, :]
            m_next = jnp.maximum(m_prev, jnp.max(s, axis=-1, keepdims=True))
            p_scr[h, rows, :] = jnp.exp2(s - _lane_tile(m_next, tk // LANES)).astype(BF16)
            alpha_scr[h, rows, :] = jnp.exp2(m_prev - m_next)
            m_scr[h, rows, :] = m_next

    def values(h, v_aug):
        acc_scr[h] = (_lane_tile(alpha_scr[h], 2) * acc_scr[h]
                      + jnp.dot(p_scr[h], v_aug, preferred_element_type=F32))

    def tile(kt, diagonal):
        k0 = pl.multiple_of(kt * tk, tk)
        k_aug = key_tile(kt)
        v_aug = jnp.concatenate([v_ref[pl.ds(k0, tk), :], jnp.ones((tk, LANES), BF16)], axis=1)
        for h in range(HPG):
            scores(h, k_aug, s_scr)
        for h in range(HPG):
            softmax(h, s_scr, diagonal)
        for h in range(HPG):
            values(h, v_aug)

    def full_tile(kt, carry):
        tile(kt, False)
        return carry

    lax.fori_loop(0, qi, full_tile, 0)
    tile(qi, True)

    for h in range(HPG):
        o_ref[:, h * HEAD_DIM:(h + 1) * HEAD_DIM] = (acc_scr[h, :, 0:HEAD_DIM]
                                                     / acc_scr[h, :, HEAD_DIM:2 * HEAD_DIM])


def _sel_attn(qkv, bias, bsz, seq, tq=SEL_TQ, rb=SEL_RB):
    tq = min(tq, seq)
    rb = min(rb, tq)
    nq = seq // tq
    kcol = ATT_WIDTH // HEAD_DIM
    return pl.pallas_call(
        functools.partial(_sel_attn_body, tq=tq, rb=rb),
        out_shape=jax.ShapeDtypeStruct((bsz * seq, ATT_WIDTH), F32),
        grid=(bsz, N_KV, nq),
        in_specs=[pl.BlockSpec((tq, HPG * HEAD_DIM), lambda b, g, i: (b * nq + i, g)),
                  pl.BlockSpec((seq, HEAD_DIM), lambda b, g, i: (b, kcol + g)),
                  pl.BlockSpec((seq, HEAD_DIM), lambda b, g, i: (b, kcol + N_KV + g)),
                  pl.BlockSpec((None, None, tq, LANES), lambda b, g, i: (b, g, i, 0))],
        out_specs=pl.BlockSpec((tq, HPG * HEAD_DIM), lambda b, g, i: (b * nq + i, g)),
        scratch_shapes=[pltpu.VMEM((HPG, tq, 2 * HEAD_DIM), BF16),
                        pltpu.VMEM((HPG, tq, LANES), F32),
                        pltpu.VMEM((HPG, tq, 2 * HEAD_DIM), F32),
                        pltpu.VMEM((HPG, tq, tq), F32),
                        pltpu.VMEM((HPG, tq, tq), BF16),
                        pltpu.VMEM((HPG, tq, LANES), F32)],
        compiler_params=_params(("parallel", "parallel", "arbitrary")),
        name="nsa_sel_attn",
    )(qkv, qkv, qkv, bias)


def _win_attn_body(q_ref, kc_ref, kp_ref, vc_ref, vp_ref, ocmp_ref, osel_ref, gate_ref, o_ref, *, tq):
    qi = pl.program_id(2)
    nt = (((1,), (1,)), ((), ()))
    ones = jnp.ones((tq, LANES), BF16)
    vc_aug = jnp.concatenate([vc_ref[...], ones], axis=1)
    vp_aug = jnp.concatenate([vp_ref[...], ones], axis=1)
    for r in range(tq // ATT_ROWS):
        rows = slice(r * ATT_ROWS, (r + 1) * ATT_ROWS)
        n_cur, p_lo = (r + 1) * ATT_ROWS, r * ATT_ROWS
        row_c = lax.broadcasted_iota(jnp.int32, (ATT_ROWS, n_cur), 0) + r * ATT_ROWS
        ok_cur = lax.broadcasted_iota(jnp.int32, (ATT_ROWS, n_cur), 1) <= row_c
        row_p = lax.broadcasted_iota(jnp.int32, (ATT_ROWS, tq - p_lo), 0) + r * ATT_ROWS
        ok_prev = (lax.broadcasted_iota(jnp.int32, (ATT_ROWS, tq - p_lo), 1) + p_lo > row_p) & (qi > 0)
        gates = jax.nn.sigmoid(gate_ref[rows, :])
        for h in range(HPG):
            sl = slice(h * HEAD_DIM, (h + 1) * HEAD_DIM)
            qh = q_ref[rows, sl]
            s1 = jnp.where(ok_cur, lax.dot_general(qh, kc_ref[0:n_cur, :], nt, preferred_element_type=F32), NEG_INF)
            s0 = jnp.where(ok_prev, lax.dot_general(qh, kp_ref[p_lo:tq, :], nt, preferred_element_type=F32), NEG_INF)
            m = jnp.maximum(jnp.max(s1, axis=-1, keepdims=True), jnp.max(s0, axis=-1, keepdims=True))
            p1 = jnp.exp2(s1 - m).astype(BF16)
            p0 = jnp.exp2(s0 - m).astype(BF16)
            acc = (jnp.dot(p1, vc_aug[0:n_cur, :], preferred_element_type=F32)
                   + jnp.dot(p0, vp_aug[p_lo:tq, :], preferred_element_type=F32))
            ow = acc[:, 0:HEAD_DIM] / acc[:, HEAD_DIM:2 * HEAD_DIM]
            g_cmp = gates[:, h:h + 1]
            g_sel = gates[:, HPG + h:HPG + h + 1]
            g_win = gates[:, 2 * HPG + h:2 * HPG + h + 1]
            o_ref[rows, sl] = (g_cmp * ocmp_ref[rows, sl] + g_sel * osel_ref[rows, sl]
                               + g_win * ow).astype(o_ref.dtype)


def _win_attn_combine(qkv, o_cmp, o_sel, gates, bsz, seq):
    tq = WINDOW
    assert seq % tq == 0
    nq = seq // tq
    kcol = ATT_WIDTH // HEAD_DIM + 2 * N_KV
    qspec = pl.BlockSpec((tq, HPG * HEAD_DIM), lambda b, g, i: (b * nq + i, g))
    cur = lambda c: pl.BlockSpec((tq, HEAD_DIM), lambda b, g, i: (b * nq + i, kcol + c * N_KV + g))
    prev = lambda c: pl.BlockSpec((tq, HEAD_DIM),
                                  lambda b, g, i: (b * nq + jnp.maximum(i - 1, 0), kcol + c * N_KV + g))
    return pl.pallas_call(
        functools.partial(_win_attn_body, tq=tq),
        out_shape=jax.ShapeDtypeStruct((bsz * seq, ATT_WIDTH), BF16),
        grid=(bsz, N_KV, nq),
        in_specs=[qspec, cur(0), prev(0), cur(1), prev(1), qspec, qspec,
                  pl.BlockSpec((None, None, tq, LANES), lambda b, g, i: (b, g, i, 0))],
        out_specs=qspec,
        compiler_params=_params(("parallel", "parallel", "parallel")),
        name="nsa_win_attn_gate",
    )(qkv, qkv, qkv, qkv, qkv, o_cmp, o_sel, gates)


def _merge_body(hn_ref, ya_ref, yb_ref, wga_ref, wgb_ref, wa_ref, wb_ref, o_ref):
    hn = hn_ref[...]
    ga = jnp.dot(hn, wga_ref[...], preferred_element_type=F32)
    gb = jnp.dot(hn, wgb_ref[...], preferred_element_type=F32)
    pa = jnp.dot(ya_ref[...], wa_ref[...], preferred_element_type=F32)
    pb = jnp.dot(yb_ref[...], wb_ref[...], preferred_element_type=F32)
    o_ref[...] = (jax.nn.sigmoid(ga) * pa + jax.nn.sigmoid(gb) * pb).astype(o_ref.dtype)


def _merge(hn, ya, yb, wg, wa, wb, tm=512, tn=512):
    m, d = hn.shape
    ka = ya.shape[1]
    tm = min(tm, m)
    row = lambda k: pl.BlockSpec((tm, k), lambda i, j: (i, 0))
    col = lambda k: pl.BlockSpec((k, tn), lambda i, j: (0, j))
    col_b = pl.BlockSpec((d, tn), lambda i, j: (0, d // tn + j))
    return pl.pallas_call(
        _merge_body,
        out_shape=jax.ShapeDtypeStruct((m, d), BF16),
        grid=(m // tm, d // tn),
        in_specs=[row(d), row(ka), row(ka), col(d), col_b, col(ka), col(ka)],
        out_specs=pl.BlockSpec((tm, tn), lambda i, j: (i, j)),
        compiler_params=_params(("parallel", "arbitrary")),
        name="mixer_merge",
    )(hn, ya, yb, wg, wg, wa, wb)


def _rms(x, g):
    return x * lax.rsqrt(jnp.mean(x * x, axis=-1, keepdims=True) + RMS_EPS) * g


def _res_norm2_body(raw_ref, x_ref, gpost_ref, gpre_ref, h_ref, hn_ref):
    h = x_ref[...] + _rms(raw_ref[...], gpost_ref[...])
    h_ref[...] = h
    hn_ref[...] = _rms(h, gpre_ref[...]).astype(hn_ref.dtype)


def _res_norm2(raw, x, g_post, g_pre, tm=256):
    m, d = x.shape
    tm = min(tm, m)
    rows = pl.BlockSpec((tm, d), lambda i: (i, 0))
    vec = pl.BlockSpec((1, d), lambda i: (0, 0))
    return pl.pallas_call(
        _res_norm2_body,
        out_shape=(jax.ShapeDtypeStruct((m, d), F32), jax.ShapeDtypeStruct((m, d), BF16)),
        grid=(m // tm,),
        in_specs=[rows, rows, vec, vec],
        out_specs=(rows, rows),
        compiler_params=_params(("parallel",)),
        name="residual_norm_prenorm",
    )(raw, x, g_post.reshape(1, d), g_pre.reshape(1, d))


def _res_norm_body(raw_ref, x_ref, g_ref, o_ref):
    o_ref[...] = x_ref[...] + _rms(raw_ref[...], g_ref[...])


def _res_norm(raw, x, g, tm=256):
    m, d = x.shape
    tm = min(tm, m)
    rows = pl.BlockSpec((tm, d), lambda i: (i, 0))
    return pl.pallas_call(
        _res_norm_body,
        out_shape=jax.ShapeDtypeStruct((m, d), F32),
        grid=(m // tm,),
        in_specs=[rows, rows, pl.BlockSpec((1, d), lambda i: (0, 0))],
        out_specs=rows,
        compiler_params=_params(("parallel",)),
        name="residual_norm",
    )(raw, x, g.reshape(1, d))


def _swiglu_body(a_ref, wg_ref, wu_ref, o_ref, wg_scr, wu_scr):
    @pl.when(pl.program_id(1) == 0)
    def _():
        wg_scr[...] = wg_ref[...].astype(BF16)
        wu_scr[...] = wu_ref[...].astype(BF16)

    a = a_ref[...]
    g = jnp.dot(a, wg_scr[...], preferred_element_type=F32)
    u = jnp.dot(a, wu_scr[...], preferred_element_type=F32)
    o_ref[...] = (g * jax.nn.sigmoid(g) * u).astype(o_ref.dtype)


def _swiglu(a, wg, wu, tm=1024, tn=256):
    m, k = a.shape
    n = wg.shape[1]
    tm = min(tm, m)
    assert n % tn == 0
    return pl.pallas_call(
        _swiglu_body,
        out_shape=jax.ShapeDtypeStruct((m, n), BF16),
        grid=(n // tn, m // tm),
        in_specs=[pl.BlockSpec((tm, k), lambda j, i: (i, 0)),
                  pl.BlockSpec((k, tn), lambda j, i: (0, j)),
                  pl.BlockSpec((k, tn), lambda j, i: (0, j))],
        out_specs=pl.BlockSpec((tm, tn), lambda j, i: (i, j)),
        scratch_shapes=[pltpu.VMEM((k, tn), BF16), pltpu.VMEM((k, tn), BF16)],
        compiler_params=_params(("arbitrary", "arbitrary")),
        name="ffn_swiglu",
    )(a, wg, wu)


def kernel(x, norm_mix_pre, w_in, ssm_a_re, ssm_a_im, ssm_log_dt, ssm_b_re, ssm_b_im, ssm_c_re, ssm_c_im, ssm_d, ssm_w_glu, ssm_b_glu, cmp_pe_k, cmp_w1_k, cmp_w2_k, cmp_pe_v, cmp_w1_v, cmp_w2_v, w_proj_a, w_proj_b, w_out, norm_mix_post, norm_ffn_pre, w_ffn_gate, w_ffn_up, w_ffn_down, norm_ffn_post):
    bsz, seq, d = x.shape
    m = bsz * seq
    depth = w_in.shape[0]
    h = x.reshape(m, d)
    o_q, o_kvc, o_kv, o_gn = SSM_WIDTH, SSM_WIDTH + ATT_WIDTH, SSM_WIDTH + ATT_WIDTH + 2 * KV_WIDTH, 7168
    o_ga = o_gn + 3 * N_HEADS
    o_gb = o_ga + D_MODEL
    bf = lambda a: a.astype(BF16)
    tn_in = 512
    nb_u, nb_q = SSM_WIDTH // tn_in, ATT_WIDTH // tn_in
    qk_scale = HEAD_DIM ** -0.5 * LOG2E
    for l in range(depth):
        wt = jnp.swapaxes(w_in[l], 0, 1)
        w_g = _rows_transposed_cast(wt, o_ga, 2 * D_MODEL, BF16)

        hn = _rmsnorm(h, norm_mix_pre[l], BF16)
        proj_f32 = _matmul_f32w(hn, wt, SSM_WIDTH + 2 * KV_WIDTH, F32, 1024, tn_in, trans_w=True,
                                col_block=lambda j: jnp.where(j < nb_u, j, j - nb_u + o_kvc // tn_in),
                                name="in_proj_f32")
        qkv = _matmul_f32w(hn, wt, ATT_WIDTH + 4 * KV_WIDTH, BF16, 1024, tn_in, trans_w=True,
                           col_block=lambda j: jnp.where(j < nb_q, j + o_q // tn_in, j - nb_q + o_kv // tn_in),
                           scale_fn=lambda j: jnp.where(j < nb_q, qk_scale, 1.0),
                           name="in_proj_bf16")
        gn = _matmul_f32w(hn, wt, LANES, F32, 1024, LANES, trans_w=True, col_block=lambda j: o_gn // LANES,
                          name="in_proj_gates")

        y = _s5_mixer(proj_f32.reshape(bsz, seq, -1), ssm_a_re[l], ssm_a_im[l], ssm_log_dt[l],
                      ssm_b_re[l], ssm_b_im[l], ssm_c_re[l], ssm_c_im[l], ssm_d[l])
        y_a = _glu(y.reshape(m, SSM_WIDTH), bf(ssm_w_glu[l]), ssm_b_glu[l])

        pe = jnp.stack([cmp_pe_k[l], cmp_pe_v[l]])
        w1 = bf(jnp.stack([cmp_w1_k[l], cmp_w1_v[l]]))
        w2 = bf(jnp.stack([cmp_w2_k[l], cmp_w2_v[l]]))
        kcv = _compress(proj_f32, SSM_WIDTH, pe, w1, w2, bsz, seq)
        o_cmp, bias_t = _cmp_attn(qkv, kcv, bsz, seq)
        nblk = seq // L_SEL
        bias = bf(jnp.pad(jnp.swapaxes(bias_t, 2, 3), ((0, 0), (0, 0), (0, 0), (0, LANES - nblk))))
        o_sel = _sel_attn(qkv, bias, bsz, seq)
        gates = gn[:, :3 * N_HEADS].reshape(bsz, seq, 3, N_KV, HPG).transpose(0, 3, 1, 2, 4)
        gates = jnp.pad(gates.reshape(bsz, N_KV, seq, 3 * HPG), ((0, 0), (0, 0), (0, 0), (0, LANES - 3 * HPG)))
        y_b = _win_attn_combine(qkv, o_cmp, o_sel, gates, bsz, seq)

        merged = _merge(hn, y_a, y_b, w_g, bf(w_proj_a[l]), bf(w_proj_b[l]))
        mix = _matmul_f32w(merged, w_out[l], D_MODEL, F32, 1024, 512, name="out_proj")
        h, hn2 = _res_norm2(mix, h, norm_mix_post[l], norm_ffn_pre[l])

        act = _swiglu(hn2, w_ffn_gate[l], w_ffn_up[l])
        f = _matmul(act, bf(w_ffn_down[l]), F32, 512, 512, name="ffn_down")
        h = _res_norm(f, h, norm_ffn_post[l])
    return h.reshape(bsz, seq, d)
```

```python
import functools
import math

import jax
import jax.numpy as jnp
from jax import lax
from jax.experimental import pallas as pl
from jax.experimental.pallas import tpu as pltpu

F32 = jnp.float32
BF16 = jnp.bfloat16

D_MODEL = 4096
SSM_WIDTH = 2048
SSM_GROUP = 16
SSM_GROUPS = 128
SSM_STATE = 64
N_HEADS = 16
HEAD_DIM = 128
N_KV = 4
HPG = 4
ATT_WIDTH = 2048
KV_WIDTH = 512
L_CMP = 32
STRIDE_CMP = 16
L_SEL = 64
N_SEL = 16
WINDOW = 512
D_FF = 11008
RMS_EPS = 1e-6
NEG_INF = -1e30
FORCE_SCORE = 1e9
TAKEN_SCORE = -3e38
LOG2E = math.log2(math.e)

V7X_VMEM_LIMIT_BYTES = 56 * 1024 * 1024
LANES = 128
SUBLANES = 8

S5_GT = 16
S5_NC = S5_GT * SSM_GROUP
S5_NP = S5_GT * SSM_STATE
S5_TILES = SSM_GROUPS // S5_GT
S5_SLABS = 2 * S5_NP // LANES

ATT_ROWS = 128
SEL_TQ, SEL_RB = 512, 64


def _params(sem, vmem=V7X_VMEM_LIMIT_BYTES):
    return pltpu.CompilerParams(dimension_semantics=sem, vmem_limit_bytes=vmem)


def _lane_tile(x, n):
    return jnp.concatenate([x] * n, axis=1)


def _rmsnorm_gates_body(x_ref, g_ref, w_ref, hn_ref, gn_ref, wbf_scr):
    @pl.when(pl.program_id(0) == 0)
    def _():
        wbf_scr[...] = w_ref[...].astype(BF16)

    x = x_ref[...]
    ms = jnp.mean(x * x, axis=-1, keepdims=True)
    hn = (x * lax.rsqrt(ms + RMS_EPS) * g_ref[...]).astype(BF16)
    hn_ref[...] = hn
    gn_ref[...] = lax.dot_general(hn, wbf_scr[...], (((1,), (1,)), ((), ())), preferred_element_type=F32)


def _rmsnorm_gates(x, gain, wt, feature_block, tm=256):
    m, d = x.shape
    tm = min(tm, m)
    return pl.pallas_call(
        _rmsnorm_gates_body,
        out_shape=(jax.ShapeDtypeStruct((m, d), BF16), jax.ShapeDtypeStruct((m, LANES), F32)),
        grid=(m // tm,),
        in_specs=[pl.BlockSpec((tm, d), lambda i: (i, 0)),
                  pl.BlockSpec((1, d), lambda i: (0, 0)),
                  pl.BlockSpec((LANES, d), lambda i: (feature_block, 0))],
        out_specs=(pl.BlockSpec((tm, d), lambda i: (i, 0)), pl.BlockSpec((tm, LANES), lambda i: (i, 0))),
        scratch_shapes=[pltpu.VMEM((LANES, d), BF16)],
        compiler_params=_params(("arbitrary",)),
        name="rmsnorm_gates",
    )(x, gain.reshape(1, d), wt)


def _mm_body(a_ref, w_ref, o_ref):
    o_ref[...] = jnp.dot(a_ref[...], w_ref[...], preferred_element_type=F32).astype(o_ref.dtype)


def _matmul(a, w, out_dtype, tm, tn, name="matmul"):
    m, k = a.shape
    n = w.shape[1]
    tm, tn = min(tm, m), min(tn, n)
    assert m % tm == 0 and n % tn == 0
    return pl.pallas_call(
        _mm_body,
        out_shape=jax.ShapeDtypeStruct((m, n), out_dtype),
        grid=(m // tm, n // tn),
        in_specs=[pl.BlockSpec((tm, k), lambda i, j: (i, 0)), pl.BlockSpec((k, tn), lambda i, j: (0, j))],
        out_specs=pl.BlockSpec((tm, tn), lambda i, j: (i, j)),
        compiler_params=_params(("parallel", "arbitrary")),
        name=name,
    )(a, w)


_NT = (((1,), (1,)), ((), ()))


def _mm_f32w_body(a_ref, w_ref, o_ref, wbf_scr, *, scale_fn, trans_w):
    @pl.when(pl.program_id(1) == 0)
    def _():
        wbf_scr[...] = w_ref[...].astype(BF16)

    if trans_w:
        acc = lax.dot_general(a_ref[...], wbf_scr[...], _NT, preferred_element_type=F32)
    else:
        acc = jnp.dot(a_ref[...], wbf_scr[...], preferred_element_type=F32)
    if scale_fn is not None:
        acc = acc * scale_fn(pl.program_id(0))
    o_ref[...] = acc.astype(o_ref.dtype)


def _matmul_f32w(a, w, n_out, out_dtype, tm, tn, col_block=None, scale_fn=None, trans_w=False,
                 name="matmul_f32w"):
    m, k = a.shape
    tm = min(tm, m)
    assert m % tm == 0 and n_out % tn == 0
    col_block = col_block or (lambda j: j)
    if trans_w:
        w_spec = pl.BlockSpec((tn, k), lambda j, i: (col_block(j), 0))
        w_scratch = pltpu.VMEM((tn, k), BF16)
    else:
        w_spec = pl.BlockSpec((k, tn), lambda j, i: (0, col_block(j)))
        w_scratch = pltpu.VMEM((k, tn), BF16)
    return pl.pallas_call(
        functools.partial(_mm_f32w_body, scale_fn=scale_fn, trans_w=trans_w),
        out_shape=jax.ShapeDtypeStruct((m, n_out), out_dtype),
        grid=(n_out // tn, m // tm),
        in_specs=[pl.BlockSpec((tm, k), lambda j, i: (i, 0)), w_spec],
        out_specs=pl.BlockSpec((tm, tn), lambda j, i: (i, j)),
        scratch_shapes=[w_scratch],
        compiler_params=_params(("arbitrary", "arbitrary")),
        name=name,
    )(a, w)


def _rows_transposed_cast_body(w_ref, o_ref):
    o_ref[...] = w_ref[...].T.astype(o_ref.dtype)


def _rows_transposed_cast(wt, row0, n_rows, out_dtype, tr=512):
    k = wt.shape[1]
    assert row0 % (2 * SUBLANES) == 0 and n_rows % tr == 0
    return pl.pallas_call(
        _rows_transposed_cast_body,
        out_shape=jax.ShapeDtypeStruct((k, n_rows), out_dtype),
        grid=(n_rows // tr,),
        in_specs=[pl.BlockSpec((pl.Element(tr), pl.Element(k)),
                               lambda c: (pl.multiple_of(row0 + c * tr, 2 * SUBLANES), 0))],
        out_specs=pl.BlockSpec((k, tr), lambda c: (0, c)),
        compiler_params=_params(("parallel",)),
        name="gate_weight_cast",
    )(wt)


def _s5_body(u_ref, are_ref, aim_ref, ldt_ref, bre_ref, bim_ref, cre_ref, cim_ref, d_ref, y_ref,
             abar_scr, bbar_scr, cmat_scr, h_scr, st_scr, *, tc, bsz, tps):
    ti = pl.program_id(1)
    half = S5_SLABS // 2

    @pl.when(ti == 0)
    def _():
        row_tile = lax.broadcasted_iota(jnp.int32, (SUBLANES, S5_NP), 0) // bsz
        same_group = (lax.broadcasted_iota(jnp.int32, (S5_NC, S5_NP), 0) // SSM_GROUP
                      == lax.broadcasted_iota(jnp.int32, (S5_NC, S5_NP), 1) // SSM_STATE)
        bdiag = lambda x: jnp.where(same_group, jnp.concatenate([x] * S5_GT, axis=0), 0.0)
        abr8 = jnp.zeros((SUBLANES, S5_NP), F32)
        abi8 = jnp.zeros((SUBLANES, S5_NP), F32)
        for k in range(tps):
            cols = slice(k * S5_NP, (k + 1) * S5_NP)
            ar, ai = are_ref[:, cols], aim_ref[:, cols]
            dt = jnp.exp(ldt_ref[:, cols])
            decay = jnp.exp(dt * ar)
            abr, abi = decay * jnp.cos(dt * ai), decay * jnp.sin(dt * ai)
            den = ar * ar + ai * ai
            zr = ((abr - 1.0) * ar + abi * ai) / den
            zi = (abi * ar - (abr - 1.0) * ai) / den
            abr8 = jnp.where(row_tile == k, abr, abr8)
            abi8 = jnp.where(row_tile == k, abi, abi8)
            br, bi = bdiag(bre_ref[k]), bdiag(bim_ref[k])
            bbar_scr[k, :, 0:S5_NP] = (zr * br - zi * bi).astype(BF16)
            bbar_scr[k, :, S5_NP:2 * S5_NP] = (zr * bi + zi * br).astype(BF16)
            cmat_scr[k, :, 0:S5_NP] = bdiag(cre_ref[k]).astype(BF16)
            cmat_scr[k, :, S5_NP:2 * S5_NP] = (-bdiag(cim_ref[k])).astype(BF16)
        abar_scr[0:SUBLANES, :] = abr8
        abar_scr[SUBLANES:2 * SUBLANES, :] = abi8
        st_scr[...] = jnp.zeros_like(st_scr)

    for k in range(tps):
        cols = slice(k * S5_NC, (k + 1) * S5_NC)
        ub = jnp.concatenate([u_ref[b, :, cols] for b in range(bsz)], axis=0).astype(BF16)
        for s2 in range(S5_SLABS // 2):
            bu = jnp.dot(ub, bbar_scr[k, :, s2 * 2 * LANES:(s2 + 1) * 2 * LANES], preferred_element_type=F32)
            for b in range(bsz):
                for e in range(2):
                    h_scr[2 * s2 + e, pl.ds(k * bsz + b, tc, stride=SUBLANES), :] = bu[b * tc:(b + 1) * tc,
                                                                                      e * LANES:(e + 1) * LANES]

    ar = abar_scr[0:SUBLANES, :]
    ai = abar_scr[SUBLANES:2 * SUBLANES, :]

    def step(t, carry):
        hr, hi = carry
        r0 = pl.multiple_of(t * SUBLANES, SUBLANES)
        bur = jnp.concatenate([h_scr[s, pl.ds(r0, SUBLANES), :] for s in range(half)], axis=1)
        bui = jnp.concatenate([h_scr[half + s, pl.ds(r0, SUBLANES), :] for s in range(half)], axis=1)
        nhr = ar * hr - ai * hi + bur
        nhi = ar * hi + ai * hr + bui
        for s in range(half):
            h_scr[s, pl.ds(r0, SUBLANES), :] = nhr[:, s * LANES:(s + 1) * LANES]
            h_scr[half + s, pl.ds(r0, SUBLANES), :] = nhi[:, s * LANES:(s + 1) * LANES]
        return nhr, nhi

    hr, hi = lax.fori_loop(0, tc, step, (st_scr[0:SUBLANES, :], st_scr[SUBLANES:2 * SUBLANES, :]), unroll=2)
    st_scr[0:SUBLANES, :] = hr
    st_scr[SUBLANES:2 * SUBLANES, :] = hi

    for k in range(tps):
        cols = slice(k * S5_NC, (k + 1) * S5_NC)

        hb = jnp.concatenate(
            [jnp.concatenate([h_scr[s, pl.ds(k * bsz + b, tc, stride=SUBLANES), :] for s in range(S5_SLABS)], axis=1)
             for b in range(bsz)], axis=0).astype(BF16)
        ch = lax.dot_general(hb, cmat_scr[k], _NT, preferred_element_type=F32)
        for b in range(bsz):
            y = ch[b * tc:(b + 1) * tc, :] + d_ref[:, cols] * u_ref[b, :, cols]
            y_ref[b, :, cols] = jax.nn.gelu(y)


def _s5_mixer(u3, a_re, a_im, log_dt, b_re, b_im, c_re, c_im, d_skip, tc=256):
    bsz, seq, _ = u3.shape
    assert SUBLANES % bsz == 0
    tps = min(SUBLANES // bsz, S5_TILES)
    assert tps * bsz == SUBLANES
    tc = min(tc, seq)

    def per_tile_b(b):
        return (b.reshape(S5_TILES, S5_GT, SSM_STATE, SSM_GROUP).transpose(0, 3, 1, 2)
                .reshape(S5_TILES, SSM_GROUP, S5_NP))

    def per_tile_c(c):
        return (c.reshape(S5_TILES, S5_GT, SSM_GROUP, SSM_STATE).transpose(0, 2, 1, 3)
                .reshape(S5_TILES, SSM_GROUP, S5_NP))

    flat = lambda a: a.reshape(1, SSM_GROUPS * SSM_STATE)
    ldt = jnp.repeat(log_dt, SSM_STATE).reshape(1, SSM_GROUPS * SSM_STATE)
    vec_spec = pl.BlockSpec((1, tps * S5_NP), lambda g, t: (0, g))
    bc_spec = pl.BlockSpec((tps, SSM_GROUP, S5_NP), lambda g, t: (g, 0, 0))
    return pl.pallas_call(
        functools.partial(_s5_body, tc=tc, bsz=bsz, tps=tps),
        out_shape=jax.ShapeDtypeStruct((bsz, seq, SSM_WIDTH), F32),
        grid=(S5_TILES // tps, seq // tc),
        in_specs=[
            pl.BlockSpec((bsz, tc, tps * S5_NC), lambda g, t: (0, t, g)),
            vec_spec, vec_spec, vec_spec, bc_spec, bc_spec, bc_spec, bc_spec,
            pl.BlockSpec((1, tps * S5_NC), lambda g, t: (0, g)),
        ],
        out_specs=pl.BlockSpec((bsz, tc, tps * S5_NC), lambda g, t: (0, t, g)),
        scratch_shapes=[
            pltpu.VMEM((2 * SUBLANES, S5_NP), F32),
            pltpu.VMEM((tps, S5_NC, 2 * S5_NP), BF16),
            pltpu.VMEM((tps, S5_NC, 2 * S5_NP), BF16),
            pltpu.VMEM((S5_SLABS, tc * SUBLANES, LANES), F32),
            pltpu.VMEM((2 * SUBLANES, S5_NP), F32),
        ],
        compiler_params=_params(("arbitrary", "arbitrary")),
        name="s5_scan",
    )(u3, flat(a_re), flat(a_im), ldt, per_tile_b(b_re), per_tile_b(b_im), per_tile_c(c_re), per_tile_c(c_im),
      d_skip.reshape(1, SSM_WIDTH))


def _glu_body(y_ref, w_ref, b_ref, o_ref, ybf_scr, *, tn):
    j = pl.program_id(1)

    @pl.when(j == 0)
    def _():
        ybf_scr[...] = y_ref[...].astype(BF16)

    z = jnp.dot(ybf_scr[...], w_ref[...], preferred_element_type=F32) + b_ref[...]
    yt = y_ref[:, pl.ds(pl.multiple_of(j * tn, tn), tn)]
    o_ref[...] = (yt * jax.nn.sigmoid(z)).astype(o_ref.dtype)


def _glu(y, w_bf, bias, tm=1024, tn=1024):
    m, k = y.shape
    tm = min(tm, m)
    return pl.pallas_call(
        functools.partial(_glu_body, tn=tn),
        out_shape=jax.ShapeDtypeStruct((m, k), BF16),
        grid=(m // tm, k // tn),
        in_specs=[pl.BlockSpec((tm, k), lambda i, j: (i, 0)),
                  pl.BlockSpec((k, tn), lambda i, j: (0, j)),
                  pl.BlockSpec((1, tn), lambda i, j: (0, j))],
        out_specs=pl.BlockSpec((tm, tn), lambda i, j: (i, j)),
        scratch_shapes=[pltpu.VMEM((tm, k), BF16)],
        compiler_params=_params(("parallel", "arbitrary")),
        name="s5_glu",
    )(y, w_bf, bias.reshape(1, k))


def _compress_body(x_ref, pe_ref, w1_ref, w2_ref, o_ref, *, ncmp):
    half = L_CMP // 2
    acc_a = jnp.zeros((ncmp, HEAD_DIM), F32)
    acc_b = jnp.zeros((ncmp, HEAD_DIM), F32)
    for r in range(half):
        xr = x_ref[pl.ds(r, ncmp, stride=STRIDE_CMP), :]
        xa = (xr + pe_ref[r:r + 1, :]).astype(BF16)
        xb = (xr + pe_ref[half + r:half + r + 1, :]).astype(BF16)
        acc_a += jnp.dot(xa, w1_ref[r * HEAD_DIM:(r + 1) * HEAD_DIM, :], preferred_element_type=F32)
        acc_b += jnp.dot(xb, w1_ref[(half + r) * HEAD_DIM:(half + r + 1) * HEAD_DIM, :],
                         preferred_element_type=F32)
    pre = acc_a + jnp.concatenate([acc_b[1:], acc_b[:1]], axis=0)
    o_ref[...] = jnp.dot(jax.nn.gelu(pre).astype(BF16), w2_ref[...], preferred_element_type=F32).astype(o_ref.dtype)


def _compress(proj_f32, col0, pe, w1_bf, w2_bf, bsz, seq):
    ncmp = seq // STRIDE_CMP
    cb = col0 // HEAD_DIM
    return pl.pallas_call(
        functools.partial(_compress_body, ncmp=ncmp),
        out_shape=jax.ShapeDtypeStruct((bsz, 2, N_KV, ncmp, HEAD_DIM), BF16),
        grid=(bsz, 2, N_KV),
        in_specs=[pl.BlockSpec((seq, HEAD_DIM), lambda b, w, g: (b, cb + w * N_KV + g)),
                  pl.BlockSpec((None, L_CMP, HEAD_DIM), lambda b, w, g: (w, 0, 0)),
                  pl.BlockSpec((None, L_CMP * HEAD_DIM, HEAD_DIM), lambda b, w, g: (w, 0, 0)),
                  pl.BlockSpec((None, HEAD_DIM, HEAD_DIM), lambda b, w, g: (w, 0, 0))],
        out_specs=pl.BlockSpec((None, None, None, ncmp, HEAD_DIM), lambda b, w, g: (b, w, g, 0, 0)),
        compiler_params=_params(("parallel", "parallel", "parallel")),
        name="nsa_compress",
    )(proj_f32, pe, w1_bf, w2_bf)


def _cmp_attn_body(q_ref, kc_ref, vc_ref, o_ref, bias_ref, *, tq, ncmp, nblk, ntop):
    qi = pl.program_id(2)
    t0 = qi * tq
    tpos = t0 + lax.broadcasted_iota(jnp.int32, (tq, ncmp), 0)
    blk_end = lax.broadcasted_iota(jnp.int32, (tq, ncmp), 1) * STRIDE_CMP + (L_CMP - 1)
    ok = blk_end <= tpos
    okf = ok.astype(F32)
    kc = kc_ref[...]
    vc = vc_ref[...]
    psum = jnp.zeros((tq, ncmp), F32)
    for h in range(HPG):
        qh = q_ref[:, h * HEAD_DIM:(h + 1) * HEAD_DIM]
        s = lax.dot_general(qh, kc, (((1,), (1,)), ((), ())), preferred_element_type=F32)
        s = jnp.where(ok, s, NEG_INF)
        e = jnp.exp2(s - jnp.max(s, axis=-1, keepdims=True))
        p = e / jnp.sum(e, axis=-1, keepdims=True) * okf
        o_ref[:, h * HEAD_DIM:(h + 1) * HEAD_DIM] = jnp.dot(p.astype(BF16), vc, preferred_element_type=F32)
        psum = psum + p

    jj = lax.broadcasted_iota(jnp.int32, (nblk, ncmp), 0)
    nn = lax.broadcasted_iota(jnp.int32, (nblk, ncmp), 1)
    ov = ((nn * STRIDE_CMP < (jj + 1) * L_SEL) & (nn * STRIDE_CMP + L_CMP > jj * L_SEL)).astype(BF16)
    p_hi = psum.astype(BF16)
    p_lo = (psum - p_hi.astype(F32)).astype(BF16)
    nt = (((1,), (1,)), ((), ()))
    imp = (lax.dot_general(ov, p_hi, nt, preferred_element_type=F32)
           + lax.dot_general(ov, p_lo, nt, preferred_element_type=F32))

    jb = lax.broadcasted_iota(jnp.int32, (nblk, tq), 0)
    tt = t0 + lax.broadcasted_iota(jnp.int32, (nblk, tq), 1)
    cur = tt // L_SEL
    allowed = jb * L_SEL <= tt
    forced = (jb == 0) | (jb == cur) | (jb == cur - 1)
    score = jnp.where(forced, FORCE_SCORE, jnp.where(allowed, imp, NEG_INF))
    taken = jnp.zeros((nblk, tq), F32)
    for _ in range(ntop):
        best = jnp.max(score, axis=0, keepdims=True)
        first = jnp.min(jnp.where(score == best, jb, nblk), axis=0, keepdims=True)
        pick = jb == first
        taken = jnp.where(pick, 1.0, taken)
        score = jnp.where(pick, TAKEN_SCORE, score)
    bias_ref[...] = jnp.where(taken > 0.5, 0.0, NEG_INF).astype(bias_ref.dtype)


def _cmp_attn(qkv, kcv, bsz, seq, tq=512):
    tq = min(tq, seq)
    ncmp = seq // STRIDE_CMP
    nblk = seq // L_SEL
    ntop = min(N_SEL, nblk)
    nq = seq // tq
    return pl.pallas_call(
        functools.partial(_cmp_attn_body, tq=tq, ncmp=ncmp, nblk=nblk, ntop=ntop),
        out_shape=(jax.ShapeDtypeStruct((bsz * seq, ATT_WIDTH), F32),
                   jax.ShapeDtypeStruct((bsz, N_KV, nblk, seq), F32)),
        grid=(bsz, N_KV, nq),
        in_specs=[pl.BlockSpec((tq, HPG * HEAD_DIM), lambda b, g, i: (b * nq + i, g)),
                  pl.BlockSpec((None, None, None, ncmp, HEAD_DIM), lambda b, g, i: (b, 0, g, 0, 0)),
                  pl.BlockSpec((None, None, None, ncmp, HEAD_DIM), lambda b, g, i: (b, 1, g, 0, 0))],
        out_specs=(pl.BlockSpec((tq, HPG * HEAD_DIM), lambda b, g, i: (b * nq + i, g)),
                   pl.BlockSpec((None, None, nblk, tq), lambda b, g, i: (b, g, 0, i))),
        compiler_params=_params(("parallel", "parallel", "parallel")),
        name="nsa_cmp_attn_topk",
    )(qkv, kcv, kcv)


def _sel_attn_body(q_ref, k_ref, v_ref, bias_ref, o_ref, qa_scr, m_scr, acc_scr, s_scr, p_scr, alpha_scr,
                   *, tq, rb):
    qi = pl.program_id(2)
    tk = tq
    m_scr[...] = jnp.full_like(m_scr, NEG_INF)
    acc_scr[...] = jnp.zeros_like(acc_scr)
    for h in range(HPG):
        qa_scr[h, :, 0:HEAD_DIM] = q_ref[:, h * HEAD_DIM:(h + 1) * HEAD_DIM]
        qa_scr[h, :, HEAD_DIM:2 * HEAD_DIM] = bias_ref[...]
    nt = (((1,), (1,)), ((), ()))

    def key_tile(kt):
        k0 = pl.multiple_of(kt * tk, tk)
        key_blk = kt * (tk // L_SEL) + lax.broadcasted_iota(jnp.int32, (tk, LANES), 0) // L_SEL
        onehot = jnp.where(lax.broadcasted_iota(jnp.int32, (tk, LANES), 1) == key_blk, 1.0, 0.0).astype(BF16)
        return jnp.concatenate([k_ref[pl.ds(k0, tk), :], onehot], axis=1)

    def scores(h, k_aug, s_buf):
        s_buf[h] = lax.dot_general(qa_scr[h], k_aug, nt, preferred_element_type=F32)

    def softmax(h, s_buf, diagonal):
        for r in range(tq // rb):
            rows = slice(r * rb, (r + 1) * rb)
            s = s_buf[h, rows, :]
            if diagonal:
                col = lax.broadcasted_iota(jnp.int32, (rb, tk), 1)
                row = lax.broadcasted_iota(jnp.int32, (rb, tk), 0) + r * rb
                s = jnp.where(col <= row, s, NEG_INF)
            m_prev = m_scr[h, rows, :]
            m_next = jnp.maximum(m_prev, jnp.max(s, axis=-1, keepdims=True))
            p_scr[h, rows, :] = jnp.exp2(s - _lane_tile(m_next, tk // LANES)).astype(BF16)
            alpha_scr[h, rows, :] = jnp.exp2(m_prev - m_next)
            m_scr[h, rows, :] = m_next

    def values(h, v_aug):
        acc_scr[h] = (_lane_tile(alpha_scr[h], 2) * acc_scr[h]
                      + jnp.dot(p_scr[h], v_aug, preferred_element_type=F32))

    def tile(kt, diagonal):
        k0 = pl.multiple_of(kt * tk, tk)
        k_aug = key_tile(kt)
        v_aug = jnp.concatenate([v_ref[pl.ds(k0, tk), :], jnp.ones((tk, LANES), BF16)], axis=1)
        for h in range(HPG):
            scores(h, k_aug, s_scr)
        for h in range(HPG):
            softmax(h, s_scr, diagonal)
        for h in range(HPG):
            values(h, v_aug)

    def full_tile(kt, carry):
        tile(kt, False)
        return carry

    lax.fori_loop(0, qi, full_tile, 0)
    tile(qi, True)

    for h in range(HPG):
        o_ref[:, h * HEAD_DIM:(h + 1) * HEAD_DIM] = (acc_scr[h, :, 0:HEAD_DIM]
                                                     / acc_scr[h, :, HEAD_DIM:2 * HEAD_DIM])


def _sel_attn(qkv, bias, bsz, seq, tq=SEL_TQ, rb=SEL_RB):
    tq = min(tq, seq)
    rb = min(rb, tq)
    nq = seq // tq
    kcol = ATT_WIDTH // HEAD_DIM
    return pl.pallas_call(
        functools.partial(_sel_attn_body, tq=tq, rb=rb),
        out_shape=jax.ShapeDtypeStruct((bsz * seq, ATT_WIDTH), F32),
        grid=(bsz, N_KV, nq),
        in_specs=[pl.BlockSpec((tq, HPG * HEAD_DIM), lambda b, g, i: (b * nq + i, g)),
                  pl.BlockSpec((seq, HEAD_DIM), lambda b, g, i: (b, kcol + g)),
                  pl.BlockSpec((seq, HEAD_DIM), lambda b, g, i: (b, kcol + N_KV + g)),
                  pl.BlockSpec((None, None, tq, LANES), lambda b, g, i: (b, g, i, 0))],
        out_specs=pl.BlockSpec((tq, HPG * HEAD_DIM), lambda b, g, i: (b * nq + i, g)),
        scratch_shapes=[pltpu.VMEM((HPG, tq, 2 * HEAD_DIM), BF16),
                        pltpu.VMEM((HPG, tq, LANES), F32),
                        pltpu.VMEM((HPG, tq, 2 * HEAD_DIM), F32),
                        pltpu.VMEM((HPG, tq, tq), F32),
                        pltpu.VMEM((HPG, tq, tq), BF16),
                        pltpu.VMEM((HPG, tq, LANES), F32)],
        compiler_params=_params(("parallel", "parallel", "arbitrary")),
        name="nsa_sel_attn",
    )(qkv, qkv, qkv, bias)


def _win_attn_body(q_ref, kc_ref, kp_ref, vc_ref, vp_ref, ocmp_ref, osel_ref, gate_ref, o_ref, *, tq):
    qi = pl.program_id(2)
    nt = (((1,), (1,)), ((), ()))
    ones = jnp.ones((tq, LANES), BF16)
    vc_aug = jnp.concatenate([vc_ref[...], ones], axis=1)
    vp_aug = jnp.concatenate([vp_ref[...], ones], axis=1)
    for r in range(tq // ATT_ROWS):
        rows = slice(r * ATT_ROWS, (r + 1) * ATT_ROWS)
        n_cur, p_lo = (r + 1) * ATT_ROWS, r * ATT_ROWS
        row_c = lax.broadcasted_iota(jnp.int32, (ATT_ROWS, n_cur), 0) + r * ATT_ROWS
        ok_cur = lax.broadcasted_iota(jnp.int32, (ATT_ROWS, n_cur), 1) <= row_c
        row_p = lax.broadcasted_iota(jnp.int32, (ATT_ROWS, tq - p_lo), 0) + r * ATT_ROWS
        ok_prev = (lax.broadcasted_iota(jnp.int32, (ATT_ROWS, tq - p_lo), 1) + p_lo > row_p) & (qi > 0)
        gates = jax.nn.sigmoid(gate_ref[rows, :])
        for h in range(HPG):
            sl = slice(h * HEAD_DIM, (h + 1) * HEAD_DIM)
            qh = q_ref[rows, sl]
            s1 = jnp.where(ok_cur, lax.dot_general(qh, kc_ref[0:n_cur, :], nt, preferred_element_type=F32), NEG_INF)
            s0 = jnp.where(ok_prev, lax.dot_general(qh, kp_ref[p_lo:tq, :], nt, preferred_element_type=F32), NEG_INF)
            m = jnp.maximum(jnp.max(s1, axis=-1, keepdims=True), jnp.max(s0, axis=-1, keepdims=True))
            p1 = jnp.exp2(s1 - m).astype(BF16)
            p0 = jnp.exp2(s0 - m).astype(BF16)
            acc = (jnp.dot(p1, vc_aug[0:n_cur, :], preferred_element_type=F32)
                   + jnp.dot(p0, vp_aug[p_lo:tq, :], preferred_element_type=F32))
            ow = acc[:, 0:HEAD_DIM] / acc[:, HEAD_DIM:2 * HEAD_DIM]
            g_cmp = gates[:, h:h + 1]
            g_sel = gates[:, HPG + h:HPG + h + 1]
            g_win = gates[:, 2 * HPG + h:2 * HPG + h + 1]
            o_ref[rows, sl] = (g_cmp * ocmp_ref[rows, sl] + g_sel * osel_ref[rows, sl]
                               + g_win * ow).astype(o_ref.dtype)


def _win_attn_combine(qkv, o_cmp, o_sel, gates, bsz, seq):
    tq = WINDOW
    assert seq % tq == 0
    nq = seq // tq
    kcol = ATT_WIDTH // HEAD_DIM + 2 * N_KV
    qspec = pl.BlockSpec((tq, HPG * HEAD_DIM), lambda b, g, i: (b * nq + i, g))
    cur = lambda c: pl.BlockSpec((tq, HEAD_DIM), lambda b, g, i: (b * nq + i, kcol + c * N_KV + g))
    prev = lambda c: pl.BlockSpec((tq, HEAD_DIM),
                                  lambda b, g, i: (b * nq + jnp.maximum(i - 1, 0), kcol + c * N_KV + g))
    return pl.pallas_call(
        functools.partial(_win_attn_body, tq=tq),
        out_shape=jax.ShapeDtypeStruct((bsz * seq, ATT_WIDTH), BF16),
        grid=(bsz, N_KV, nq),
        in_specs=[qspec, cur(0), prev(0), cur(1), prev(1), qspec, qspec,
                  pl.BlockSpec((None, None, tq, LANES), lambda b, g, i: (b, g, i, 0))],
        out_specs=qspec,
        compiler_params=_params(("parallel", "parallel", "parallel")),
        name="nsa_win_attn_gate",
    )(qkv, qkv, qkv, qkv, qkv, o_cmp, o_sel, gates)


def _merge_body(hn_ref, ya_ref, yb_ref, wga_ref, wgb_ref, wa_ref, wb_ref, o_ref):
    hn = hn_ref[...]
    ga = jnp.dot(hn, wga_ref[...], preferred_element_type=F32)
    gb = jnp.dot(hn, wgb_ref[...], preferred_element_type=F32)
    pa = jnp.dot(ya_ref[...], wa_ref[...], preferred_element_type=F32)
    pb = jnp.dot(yb_ref[...], wb_ref[...], preferred_element_type=F32)
    o_ref[...] = (jax.nn.sigmoid(ga) * pa + jax.nn.sigmoid(gb) * pb).astype(o_ref.dtype)


def _merge(hn, ya, yb, wg, wa, wb, tm=512, tn=512):
    m, d = hn.shape
    ka = ya.shape[1]
    tm = min(tm, m)
    row = lambda k: pl.BlockSpec((tm, k), lambda i, j: (i, 0))
    col = lambda k: pl.BlockSpec((k, tn), lambda i, j: (0, j))
    col_b = pl.BlockSpec((d, tn), lambda i, j: (0, d // tn + j))
    return pl.pallas_call(
        _merge_body,
        out_shape=jax.ShapeDtypeStruct((m, d), BF16),
        grid=(m // tm, d // tn),
        in_specs=[row(d), row(ka), row(ka), col(d), col_b, col(ka), col(ka)],
        out_specs=pl.BlockSpec((tm, tn), lambda i, j: (i, j)),
        compiler_params=_params(("parallel", "arbitrary")),
        name="mixer_merge",
    )(hn, ya, yb, wg, wg, wa, wb)


def _rms(x, g):
    return x * lax.rsqrt(jnp.mean(x * x, axis=-1, keepdims=True) + RMS_EPS) * g


def _res_norm2_body(raw_ref, x_ref, gpost_ref, gpre_ref, h_ref, hn_ref):
    h = x_ref[...] + _rms(raw_ref[...], gpost_ref[...])
    h_ref[...] = h
    hn_ref[...] = _rms(h, gpre_ref[...]).astype(hn_ref.dtype)


def _res_norm2(raw, x, g_post, g_pre, tm=256):
    m, d = x.shape
    tm = min(tm, m)
    rows = pl.BlockSpec((tm, d), lambda i: (i, 0))
    vec = pl.BlockSpec((1, d), lambda i: (0, 0))
    return pl.pallas_call(
        _res_norm2_body,
        out_shape=(jax.ShapeDtypeStruct((m, d), F32), jax.ShapeDtypeStruct((m, d), BF16)),
        grid=(m // tm,),
        in_specs=[rows, rows, vec, vec],
        out_specs=(rows, rows),
        compiler_params=_params(("parallel",)),
        name="residual_norm_prenorm",
    )(raw, x, g_post.reshape(1, d), g_pre.reshape(1, d))


def _res_norm_body(raw_ref, x_ref, g_ref, o_ref):
    o_ref[...] = x_ref[...] + _rms(raw_ref[...], g_ref[...])


def _res_norm(raw, x, g, tm=256):
    m, d = x.shape
    tm = min(tm, m)
    rows = pl.BlockSpec((tm, d), lambda i: (i, 0))
    return pl.pallas_call(
        _res_norm_body,
        out_shape=jax.ShapeDtypeStruct((m, d), F32),
        grid=(m // tm,),
        in_specs=[rows, rows, pl.BlockSpec((1, d), lambda i: (0, 0))],
        out_specs=rows,
        compiler_params=_params(("parallel",)),
        name="residual_norm",
    )(raw, x, g.reshape(1, d))


def _swiglu_body(a_ref, wg_ref, wu_ref, o_ref, wg_scr, wu_scr):
    @pl.when(pl.program_id(1) == 0)
    def _():
        wg_scr[...] = wg_ref[...].astype(BF16)
        wu_scr[...] = wu_ref[...].astype(BF16)

    a = a_ref[...]
    g = jnp.dot(a, wg_scr[...], preferred_element_type=F32)
    u = jnp.dot(a, wu_scr[...], preferred_element_type=F32)
    o_ref[...] = (g * jax.nn.sigmoid(g) * u).astype(o_ref.dtype)


def _swiglu(a, wg, wu, tm=1024, tn=256):
    m, k = a.shape
    n = wg.shape[1]
    tm = min(tm, m)
    assert n % tn == 0
    return pl.pallas_call(
        _swiglu_body,
        out_shape=jax.ShapeDtypeStruct((m, n), BF16),
        grid=(n // tn, m // tm),
        in_specs=[pl.BlockSpec((tm, k), lambda j, i: (i, 0)),
                  pl.BlockSpec((k, tn), lambda j, i: (0, j)),
                  pl.BlockSpec((k, tn), lambda j, i: (0, j))],
        out_specs=pl.BlockSpec((tm, tn), lambda j, i: (i, j)),
        scratch_shapes=[pltpu.VMEM((k, tn), BF16), pltpu.VMEM((k, tn), BF16)],
        compiler_params=_params(("arbitrary", "arbitrary")),
        name="ffn_swiglu",
    )(a, wg, wu)


def kernel(x, norm_mix_pre, w_in, ssm_a_re, ssm_a_im, ssm_log_dt, ssm_b_re, ssm_b_im, ssm_c_re, ssm_c_im, ssm_d, ssm_w_glu, ssm_b_glu, cmp_pe_k, cmp_w1_k, cmp_w2_k, cmp_pe_v, cmp_w1_v, cmp_w2_v, w_proj_a, w_proj_b, w_out, norm_mix_post, norm_ffn_pre, w_ffn_gate, w_ffn_up, w_ffn_down, norm_ffn_post):
    bsz, seq, d = x.shape
    m = bsz * seq
    depth = w_in.shape[0]
    h = x.reshape(m, d)
    o_q, o_kvc, o_kv, o_gn = SSM_WIDTH, SSM_WIDTH + ATT_WIDTH, SSM_WIDTH + ATT_WIDTH + 2 * KV_WIDTH, 7168
    o_ga = o_gn + 3 * N_HEADS
    o_gb = o_ga + D_MODEL
    bf = lambda a: a.astype(BF16)
    tn_in = 512
    nb_u, nb_q = SSM_WIDTH // tn_in, ATT_WIDTH // tn_in
    qk_scale = HEAD_DIM ** -0.5 * LOG2E
    for l in range(depth):
        wt = jnp.swapaxes(w_in[l], 0, 1)
        w_g = _rows_transposed_cast(wt, o_ga, 2 * D_MODEL, BF16)

        hn, gn = _rmsnorm_gates(h, norm_mix_pre[l], wt, o_gn // LANES)
        proj_f32 = _matmul_f32w(hn, wt, SSM_WIDTH + 2 * KV_WIDTH, F32, 1024, tn_in, trans_w=True,
                                col_block=lambda j: jnp.where(j < nb_u, j, j - nb_u + o_kvc // tn_in),
                                name="in_proj_f32")
        qkv = _matmul_f32w(hn, wt, ATT_WIDTH + 4 * KV_WIDTH, BF16, 1024, tn_in, trans_w=True,
                           col_block=lambda j: jnp.where(j < nb_q, j + o_q // tn_in, j - nb_q + o_kv // tn_in),
                           scale_fn=lambda j: jnp.where(j < nb_q, qk_scale, 1.0),
                           name="in_proj_bf16")

        y = _s5_mixer(proj_f32.reshape(bsz, seq, -1), ssm_a_re[l], ssm_a_im[l], ssm_log_dt[l],
                      ssm_b_re[l], ssm_b_im[l], ssm_c_re[l], ssm_c_im[l], ssm_d[l])
        y_a = _glu(y.reshape(m, SSM_WIDTH), bf(ssm_w_glu[l]), ssm_b_glu[l])

        pe = jnp.stack([cmp_pe_k[l], cmp_pe_v[l]])
        w1 = bf(jnp.stack([cmp_w1_k[l], cmp_w1_v[l]]))
        w2 = bf(jnp.stack([cmp_w2_k[l], cmp_w2_v[l]]))
        kcv = _compress(proj_f32, SSM_WIDTH, pe, w1, w2, bsz, seq)
        o_cmp, bias_t = _cmp_attn(qkv, kcv, bsz, seq)
        nblk = seq // L_SEL
        bias = bf(jnp.pad(jnp.swapaxes(bias_t, 2, 3), ((0, 0), (0, 0), (0, 0), (0, LANES - nblk))))
        o_sel = _sel_attn(qkv, bias, bsz, seq)
        gates = gn[:, :3 * N_HEADS].reshape(bsz, seq, 3, N_KV, HPG).transpose(0, 3, 1, 2, 4)
        gates = jnp.pad(gates.reshape(bsz, N_KV, seq, 3 * HPG), ((0, 0), (0, 0), (0, 0), (0, LANES - 3 * HPG)))
        y_b = _win_attn_combine(qkv, o_cmp, o_sel, gates, bsz, seq)

        merged = _merge(hn, y_a, y_b, w_g, bf(w_proj_a[l]), bf(w_proj_b[l]))
        mix = _matmul_f32w(merged, w_out[l], D_MODEL, F32, 1024, 512, name="out_proj")
        h, hn2 = _res_norm2(mix, h, norm_mix_post[l], norm_ffn_pre[l])

        act = _swiglu(hn2, w_ffn_gate[l], w_ffn_up[l])
        f = _matmul(act, bf(w_ffn_down[l]), F32, 512, 512, name="ffn_down")
        h = _res_norm(f, h, norm_ffn_post[l])
    return h.reshape(bsz, seq, d)
```

```python
import functools
import math

import jax
import jax.numpy as jnp
from jax import lax
from jax.experimental import pallas as pl
from jax.experimental.pallas import tpu as pltpu

F32 = jnp.float32
BF16 = jnp.bfloat16

D_MODEL = 4096
SSM_WIDTH = 2048
SSM_GROUP = 16
SSM_GROUPS = 128
SSM_STATE = 64
N_HEADS = 16
HEAD_DIM = 128
N_KV = 4
HPG = 4
ATT_WIDTH = 2048
KV_WIDTH = 512
L_CMP = 32
STRIDE_CMP = 16
L_SEL = 64
N_SEL = 16
WINDOW = 512
D_FF = 11008
RMS_EPS = 1e-6
NEG_INF = -1e30
FORCE_SCORE = 1e9
TAKEN_SCORE = -3e38
LOG2E = math.log2(math.e)

V7X_VMEM_LIMIT_BYTES = 56 * 1024 * 1024
LANES = 128
SUBLANES = 8

S5_GT = 16
S5_NC = S5_GT * SSM_GROUP
S5_NP = S5_GT * SSM_STATE
S5_TILES = SSM_GROUPS // S5_GT
S5_SLABS = 2 * S5_NP // LANES

ATT_ROWS = 128
SEL_TQ, SEL_RB = 512, 64


def _params(sem, vmem=V7X_VMEM_LIMIT_BYTES):
    return pltpu.CompilerParams(dimension_semantics=sem, vmem_limit_bytes=vmem)


def _lane_tile(x, n):
    return jnp.concatenate([x] * n, axis=1)


def _rmsnorm_gates_body(x_ref, g_ref, w_ref, hn_ref, gn_ref, wbf_scr):
    @pl.when(pl.program_id(0) == 0)
    def _():
        wbf_scr[...] = w_ref[...].astype(BF16)

    x = x_ref[...]
    ms = jnp.mean(x * x, axis=-1, keepdims=True)
    hn = (x * lax.rsqrt(ms + RMS_EPS) * g_ref[...]).astype(BF16)
    hn_ref[...] = hn
    gn_ref[...] = lax.dot_general(hn, wbf_scr[...], (((1,), (1,)), ((), ())), preferred_element_type=F32)


def _rmsnorm_gates(x, gain, wt, feature_block, tm=256):
    m, d = x.shape
    tm = min(tm, m)
    return pl.pallas_call(
        _rmsnorm_gates_body,
        out_shape=(jax.ShapeDtypeStruct((m, d), BF16), jax.ShapeDtypeStruct((m, LANES), F32)),
        grid=(m // tm,),
        in_specs=[pl.BlockSpec((tm, d), lambda i: (i, 0)),
                  pl.BlockSpec((1, d), lambda i: (0, 0)),
                  pl.BlockSpec((LANES, d), lambda i: (feature_block, 0))],
        out_specs=(pl.BlockSpec((tm, d), lambda i: (i, 0)), pl.BlockSpec((tm, LANES), lambda i: (i, 0))),
        scratch_shapes=[pltpu.VMEM((LANES, d), BF16)],
        compiler_params=_params(("arbitrary",)),
        name="rmsnorm_gates",
    )(x, gain.reshape(1, d), wt)


def _mm_body(a_ref, w_ref, o_ref):
    o_ref[...] = jnp.dot(a_ref[...], w_ref[...], preferred_element_type=F32).astype(o_ref.dtype)


def _matmul(a, w, out_dtype, tm, tn, name="matmul"):
    m, k = a.shape
    n = w.shape[1]
    tm, tn = min(tm, m), min(tn, n)
    assert m % tm == 0 and n % tn == 0
    return pl.pallas_call(
        _mm_body,
        out_shape=jax.ShapeDtypeStruct((m, n), out_dtype),
        grid=(m // tm, n // tn),
        in_specs=[pl.BlockSpec((tm, k), lambda i, j: (i, 0)), pl.BlockSpec((k, tn), lambda i, j: (0, j))],
        out_specs=pl.BlockSpec((tm, tn), lambda i, j: (i, j)),
        compiler_params=_params(("parallel", "arbitrary")),
        name=name,
    )(a, w)


_NT = (((1,), (1,)), ((), ()))


def _mm_f32w_body(a_ref, w_ref, o_ref, wbf_scr, *, scale_fn, trans_w):
    @pl.when(pl.program_id(1) == 0)
    def _():
        wbf_scr[...] = w_ref[...].astype(BF16)

    if trans_w:
        acc = lax.dot_general(a_ref[...], wbf_scr[...], _NT, preferred_element_type=F32)
    else:
        acc = jnp.dot(a_ref[...], wbf_scr[...], preferred_element_type=F32)
    if scale_fn is not None:
        acc = acc * scale_fn(pl.program_id(0))
    o_ref[...] = acc.astype(o_ref.dtype)


def _matmul_f32w(a, w, n_out, out_dtype, tm, tn, col_block=None, scale_fn=None, trans_w=False,
                 name="matmul_f32w"):
    m, k = a.shape
    tm = min(tm, m)
    assert m % tm == 0 and n_out % tn == 0
    col_block = col_block or (lambda j: j)
    if trans_w:
        w_spec = pl.BlockSpec((tn, k), lambda j, i: (col_block(j), 0))
        w_scratch = pltpu.VMEM((tn, k), BF16)
    else:
        w_spec = pl.BlockSpec((k, tn), lambda j, i: (0, col_block(j)))
        w_scratch = pltpu.VMEM((k, tn), BF16)
    return pl.pallas_call(
        functools.partial(_mm_f32w_body, scale_fn=scale_fn, trans_w=trans_w),
        out_shape=jax.ShapeDtypeStruct((m, n_out), out_dtype),
        grid=(n_out // tn, m // tm),
        in_specs=[pl.BlockSpec((tm, k), lambda j, i: (i, 0)), w_spec],
        out_specs=pl.BlockSpec((tm, tn), lambda j, i: (i, j)),
        scratch_shapes=[w_scratch],
        compiler_params=_params(("arbitrary", "arbitrary")),
        name=name,
    )(a, w)


def _rows_transposed_cast_body(w_ref, o_ref):
    o_ref[...] = w_ref[...].T.astype(o_ref.dtype)


def _rows_transposed_cast(wt, row0, n_rows, out_dtype, tr=512):
    k = wt.shape[1]
    assert row0 % (2 * SUBLANES) == 0 and n_rows % tr == 0
    return pl.pallas_call(
        _rows_transposed_cast_body,
        out_shape=jax.ShapeDtypeStruct((k, n_rows), out_dtype),
        grid=(n_rows // tr,),
        in_specs=[pl.BlockSpec((pl.Element(tr), pl.Element(k)),
                               lambda c: (pl.multiple_of(row0 + c * tr, 2 * SUBLANES), 0))],
        out_specs=pl.BlockSpec((k, tr), lambda c: (0, c)),
        compiler_params=_params(("parallel",)),
        name="gate_weight_cast",
    )(wt)


def _s5_body(u_ref, are_ref, aim_ref, ldt_ref, bre_ref, bim_ref, cre_ref, cim_ref, d_ref, y_ref,
             abar_scr, bbar_scr, cmat_scr, h_scr, st_scr, *, tc, bsz, tps):
    ti = pl.program_id(1)
    half = S5_SLABS // 2

    @pl.when(ti == 0)
    def _():
        row_tile = lax.broadcasted_iota(jnp.int32, (SUBLANES, S5_NP), 0) // bsz
        same_group = (lax.broadcasted_iota(jnp.int32, (S5_NC, S5_NP), 0) // SSM_GROUP
                      == lax.broadcasted_iota(jnp.int32, (S5_NC, S5_NP), 1) // SSM_STATE)
        bdiag = lambda x: jnp.where(same_group, jnp.concatenate([x] * S5_GT, axis=0), 0.0)
        abr8 = jnp.zeros((SUBLANES, S5_NP), F32)
        abi8 = jnp.zeros((SUBLANES, S5_NP), F32)
        for k in range(tps):
            cols = slice(k * S5_NP, (k + 1) * S5_NP)
            ar, ai = are_ref[:, cols], aim_ref[:, cols]
            dt = jnp.exp(ldt_ref[:, cols])
            decay = jnp.exp(dt * ar)
            abr, abi = decay * jnp.cos(dt * ai), decay * jnp.sin(dt * ai)
            den = ar * ar + ai * ai
            zr = ((abr - 1.0) * ar + abi * ai) / den
            zi = (abi * ar - (abr - 1.0) * ai) / den
            abr8 = jnp.where(row_tile == k, abr, abr8)
            abi8 = jnp.where(row_tile == k, abi, abi8)
            br, bi = bdiag(bre_ref[k]), bdiag(bim_ref[k])
            bbar_scr[k, :, 0:S5_NP] = (zr * br - zi * bi).astype(BF16)
            bbar_scr[k, :, S5_NP:2 * S5_NP] = (zr * bi + zi * br).astype(BF16)
            cmat_scr[k, :, 0:S5_NP] = bdiag(cre_ref[k]).astype(BF16)
            cmat_scr[k, :, S5_NP:2 * S5_NP] = (-bdiag(cim_ref[k])).astype(BF16)
        abar_scr[0:SUBLANES, :] = abr8
        abar_scr[SUBLANES:2 * SUBLANES, :] = abi8
        st_scr[...] = jnp.zeros_like(st_scr)

    for k in range(tps):
        cols = slice(k * S5_NC, (k + 1) * S5_NC)
        ub = jnp.concatenate([u_ref[b, :, cols] for b in range(bsz)], axis=0).astype(BF16)
        for s2 in range(S5_SLABS // 2):
            bu = jnp.dot(ub, bbar_scr[k, :, s2 * 2 * LANES:(s2 + 1) * 2 * LANES], preferred_element_type=F32)
            for b in range(bsz):
                for e in range(2):
                    h_scr[2 * s2 + e, pl.ds(k * bsz + b, tc, stride=SUBLANES), :] = bu[b * tc:(b + 1) * tc,
                                                                                      e * LANES:(e + 1) * LANES]

    ar = abar_scr[0:SUBLANES, :]
    ai = abar_scr[SUBLANES:2 * SUBLANES, :]

    def step(t, carry):
        hr, hi = carry
        r0 = pl.multiple_of(t * SUBLANES, SUBLANES)
        bur = jnp.concatenate([h_scr[s, pl.ds(r0, SUBLANES), :] for s in range(half)], axis=1)
        bui = jnp.concatenate([h_scr[half + s, pl.ds(r0, SUBLANES), :] for s in range(half)], axis=1)
        nhr = ar * hr - ai * hi + bur
        nhi = ar * hi + ai * hr + bui
        for s in range(half):
            h_scr[s, pl.ds(r0, SUBLANES), :] = nhr[:, s * LANES:(s + 1) * LANES]
            h_scr[half + s, pl.ds(r0, SUBLANES), :] = nhi[:, s * LANES:(s + 1) * LANES]
        return nhr, nhi

    hr, hi = lax.fori_loop(0, tc, step, (st_scr[0:SUBLANES, :], st_scr[SUBLANES:2 * SUBLANES, :]), unroll=2)
    st_scr[0:SUBLANES, :] = hr
    st_scr[SUBLANES:2 * SUBLANES, :] = hi

    for k in range(tps):
        cols = slice(k * S5_NC, (k + 1) * S5_NC)

        hb = jnp.concatenate(
            [jnp.concatenate([h_scr[s, pl.ds(k * bsz + b, tc, stride=SUBLANES), :] for s in range(S5_SLABS)], axis=1)
             for b in range(bsz)], axis=0).astype(BF16)
        ch = lax.dot_general(hb, cmat_scr[k], _NT, preferred_element_type=F32)
        for b in range(bsz):
            y = ch[b * tc:(b + 1) * tc, :] + d_ref[:, cols] * u_ref[b, :, cols]
            y_ref[b, :, cols] = jax.nn.gelu(y)


def _s5_mixer(u3, a_re, a_im, log_dt, b_re, b_im, c_re, c_im, d_skip, tc=256):
    bsz, seq, _ = u3.shape
    assert SUBLANES % bsz == 0
    tps = min(SUBLANES // bsz, S5_TILES)
    assert tps * bsz == SUBLANES
    tc = min(tc, seq)

    def per_tile_b(b):
        return (b.reshape(S5_TILES, S5_GT, SSM_STATE, SSM_GROUP).transpose(0, 3, 1, 2)
                .reshape(S5_TILES, SSM_GROUP, S5_NP))

    def per_tile_c(c):
        return (c.reshape(S5_TILES, S5_GT, SSM_GROUP, SSM_STATE).transpose(0, 2, 1, 3)
                .reshape(S5_TILES, SSM_GROUP, S5_NP))

    flat = lambda a: a.reshape(1, SSM_GROUPS * SSM_STATE)
    ldt = jnp.repeat(log_dt, SSM_STATE).reshape(1, SSM_GROUPS * SSM_STATE)
    vec_spec = pl.BlockSpec((1, tps * S5_NP), lambda g, t: (0, g))
    bc_spec = pl.BlockSpec((tps, SSM_GROUP, S5_NP), lambda g, t: (g, 0, 0))
    return pl.pallas_call(
        functools.partial(_s5_body, tc=tc, bsz=bsz, tps=tps),
        out_shape=jax.ShapeDtypeStruct((bsz, seq, SSM_WIDTH), F32),
        grid=(S5_TILES // tps, seq // tc),
        in_specs=[
            pl.BlockSpec((bsz, tc, tps * S5_NC), lambda g, t: (0, t, g)),
            vec_spec, vec_spec, vec_spec, bc_spec, bc_spec, bc_spec, bc_spec,
            pl.BlockSpec((1, tps * S5_NC), lambda g, t: (0, g)),
        ],
        out_specs=pl.BlockSpec((bsz, tc, tps * S5_NC), lambda g, t: (0, t, g)),
        scratch_shapes=[
            pltpu.VMEM((2 * SUBLANES, S5_NP), F32),
            pltpu.VMEM((tps, S5_NC, 2 * S5_NP), BF16),
            pltpu.VMEM((tps, S5_NC, 2 * S5_NP), BF16),
            pltpu.VMEM((S5_SLABS, tc * SUBLANES, LANES), F32),
            pltpu.VMEM((2 * SUBLANES, S5_NP), F32),
        ],
        compiler_params=_params(("arbitrary", "arbitrary")),
        name="s5_scan",
    )(u3, flat(a_re), flat(a_im), ldt, per_tile_b(b_re), per_tile_b(b_im), per_tile_c(c_re), per_tile_c(c_im),
      d_skip.reshape(1, SSM_WIDTH))


def _glu_body(y_ref, w_ref, b_ref, o_ref, ybf_scr, *, tn):
    j = pl.program_id(1)

    @pl.when(j == 0)
    def _():
        ybf_scr[...] = y_ref[...].astype(BF16)

    z = jnp.dot(ybf_scr[...], w_ref[...], preferred_element_type=F32) + b_ref[...]
    yt = y_ref[:, pl.ds(pl.multiple_of(j * tn, tn), tn)]
    o_ref[...] = (yt * jax.nn.sigmoid(z)).astype(o_ref.dtype)


def _glu(y, w_bf, bias, tm=1024, tn=1024):
    m, k = y.shape
    tm = min(tm, m)
    return pl.pallas_call(
        functools.partial(_glu_body, tn=tn),
        out_shape=jax.ShapeDtypeStruct((m, k), BF16),
        grid=(m // tm, k // tn),
        in_specs=[pl.BlockSpec((tm, k), lambda i, j: (i, 0)),
                  pl.BlockSpec((k, tn), lambda i, j: (0, j)),
                  pl.BlockSpec((1, tn), lambda i, j: (0, j))],
        out_specs=pl.BlockSpec((tm, tn), lambda i, j: (i, j)),
        scratch_shapes=[pltpu.VMEM((tm, k), BF16)],
        compiler_params=_params(("parallel", "arbitrary")),
        name="s5_glu",
    )(y, w_bf, bias.reshape(1, k))


def _compress_body(x_ref, pe_ref, w1_ref, w2_ref, o_ref, *, ncmp):
    half = L_CMP // 2
    acc_a = jnp.zeros((ncmp, HEAD_DIM), F32)
    acc_b = jnp.zeros((ncmp, HEAD_DIM), F32)
    for r in range(half):
        xr = x_ref[pl.ds(r, ncmp, stride=STRIDE_CMP), :]
        xa = (xr + pe_ref[r:r + 1, :]).astype(BF16)
        xb = (xr + pe_ref[half + r:half + r + 1, :]).astype(BF16)
        acc_a += jnp.dot(xa, w1_ref[r * HEAD_DIM:(r + 1) * HEAD_DIM, :], preferred_element_type=F32)
        acc_b += jnp.dot(xb, w1_ref[(half + r) * HEAD_DIM:(half + r + 1) * HEAD_DIM, :],
                         preferred_element_type=F32)
    pre = acc_a + jnp.concatenate([acc_b[1:], acc_b[:1]], axis=0)
    o_ref[...] = jnp.dot(jax.nn.gelu(pre).astype(BF16), w2_ref[...], preferred_element_type=F32).astype(o_ref.dtype)


def _compress(proj_f32, col0, pe, w1_bf, w2_bf, bsz, seq):
    ncmp = seq // STRIDE_CMP
    cb = col0 // HEAD_DIM
    return pl.pallas_call(
        functools.partial(_compress_body, ncmp=ncmp),
        out_shape=jax.ShapeDtypeStruct((bsz, 2, N_KV, ncmp, HEAD_DIM), BF16),
        grid=(bsz, 2, N_KV),
        in_specs=[pl.BlockSpec((seq, HEAD_DIM), lambda b, w, g: (b, cb + w * N_KV + g)),
                  pl.BlockSpec((None, L_CMP, HEAD_DIM), lambda b, w, g: (w, 0, 0)),
                  pl.BlockSpec((None, L_CMP * HEAD_DIM, HEAD_DIM), lambda b, w, g: (w, 0, 0)),
                  pl.BlockSpec((None, HEAD_DIM, HEAD_DIM), lambda b, w, g: (w, 0, 0))],
        out_specs=pl.BlockSpec((None, None, None, ncmp, HEAD_DIM), lambda b, w, g: (b, w, g, 0, 0)),
        compiler_params=_params(("parallel", "parallel", "parallel")),
        name="nsa_compress",
    )(proj_f32, pe, w1_bf, w2_bf)


def _cmp_attn_body(q_ref, kc_ref, vc_ref, o_ref, bias_ref, *, tq, ncmp, nblk, ntop):
    qi = pl.program_id(2)
    t0 = qi * tq
    tpos = t0 + lax.broadcasted_iota(jnp.int32, (tq, ncmp), 0)
    blk_end = lax.broadcasted_iota(jnp.int32, (tq, ncmp), 1) * STRIDE_CMP + (L_CMP - 1)
    ok = blk_end <= tpos
    okf = ok.astype(F32)
    kc = kc_ref[...]
    vc = vc_ref[...]
    psum = jnp.zeros((tq, ncmp), F32)
    for h in range(HPG):
        qh = q_ref[:, h * HEAD_DIM:(h + 1) * HEAD_DIM]
        s = lax.dot_general(qh, kc, (((1,), (1,)), ((), ())), preferred_element_type=F32)
        s = jnp.where(ok, s, NEG_INF)
        e = jnp.exp2(s - jnp.max(s, axis=-1, keepdims=True))
        p = e / jnp.sum(e, axis=-1, keepdims=True) * okf
        o_ref[:, h * HEAD_DIM:(h + 1) * HEAD_DIM] = jnp.dot(p.astype(BF16), vc, preferred_element_type=F32)
        psum = psum + p

    jj = lax.broadcasted_iota(jnp.int32, (nblk, ncmp), 0)
    nn = lax.broadcasted_iota(jnp.int32, (nblk, ncmp), 1)
    ov = ((nn * STRIDE_CMP < (jj + 1) * L_SEL) & (nn * STRIDE_CMP + L_CMP > jj * L_SEL)).astype(BF16)
    p_hi = psum.astype(BF16)
    p_lo = (psum - p_hi.astype(F32)).astype(BF16)
    nt = (((1,), (1,)), ((), ()))
    imp = (lax.dot_general(ov, p_hi, nt, preferred_element_type=F32)
           + lax.dot_general(ov, p_lo, nt, preferred_element_type=F32))

    jb = lax.broadcasted_iota(jnp.int32, (nblk, tq), 0)
    tt = t0 + lax.broadcasted_iota(jnp.int32, (nblk, tq), 1)
    cur = tt // L_SEL
    allowed = jb * L_SEL <= tt
    forced = (jb == 0) | (jb == cur) | (jb == cur - 1)
    score = jnp.where(forced, FORCE_SCORE, jnp.where(allowed, imp, NEG_INF))
    taken = jnp.zeros((nblk, tq), F32)
    for _ in range(ntop):
        best = jnp.max(score, axis=0, keepdims=True)
        first = jnp.min(jnp.where(score == best, jb, nblk), axis=0, keepdims=True)
        pick = jb == first
        taken = jnp.where(pick, 1.0, taken)
        score = jnp.where(pick, TAKEN_SCORE, score)
    bias_ref[...] = jnp.where(taken > 0.5, 0.0, NEG_INF).astype(bias_ref.dtype)


def _cmp_attn(qkv, kcv, bsz, seq, tq=512):
    tq = min(tq, seq)
    ncmp = seq // STRIDE_CMP
    nblk = seq // L_SEL
    ntop = min(N_SEL, nblk)
    nq = seq // tq
    return pl.pallas_call(
        functools.partial(_cmp_attn_body, tq=tq, ncmp=ncmp, nblk=nblk, ntop=ntop),
        out_shape=(jax.ShapeDtypeStruct((bsz * seq, ATT_WIDTH), F32),
                   jax.ShapeDtypeStruct((bsz, N_KV, nblk, seq), F32)),
        grid=(bsz, N_KV, nq),
        in_specs=[pl.BlockSpec((tq, HPG * HEAD_DIM), lambda b, g, i: (b * nq + i, g)),
                  pl.BlockSpec((None, None, None, ncmp, HEAD_DIM), lambda b, g, i: (b, 0, g, 0, 0)),
                  pl.BlockSpec((None, None, None, ncmp, HEAD_DIM), lambda b, g, i: (b, 1, g, 0, 0))],
        out_specs=(pl.BlockSpec((tq, HPG * HEAD_DIM), lambda b, g, i: (b * nq + i, g)),
                   pl.BlockSpec((None, None, nblk, tq), lambda b, g, i: (b, g, 0, i))),
        compiler_params=_params(("parallel", "parallel", "parallel")),
        name="nsa_cmp_attn_topk",
    )(qkv, kcv, kcv)


def _sel_attn_body(q_ref, k_ref, v_ref, bias_ref, o_ref, qa_scr, m_scr, acc_scr, s_scr, p_scr, alpha_scr,
                   *, tq, rb):
    qi = pl.program_id(2)
    tk = tq
    m_scr[...] = jnp.full_like(m_scr, NEG_INF)
    acc_scr[...] = jnp.zeros_like(acc_scr)
    for h in range(HPG):
        qa_scr[h, :, 0:HEAD_DIM] = q_ref[:, h * HEAD_DIM:(h + 1) * HEAD_DIM]
        qa_scr[h, :, HEAD_DIM:2 * HEAD_DIM] = bias_ref[...]
    nt = (((1,), (1,)), ((), ()))

    def key_tile(kt):
        k0 = pl.multiple_of(kt * tk, tk)
        key_blk = kt * (tk // L_SEL) + lax.broadcasted_iota(jnp.int32, (tk, LANES), 0) // L_SEL
        onehot = jnp.where(lax.broadcasted_iota(jnp.int32, (tk, LANES), 1) == key_blk, 1.0, 0.0).astype(BF16)
        return jnp.concatenate([k_ref[pl.ds(k0, tk), :], onehot], axis=1)

    def scores(h, k_aug, s_buf):
        s_buf[h] = lax.dot_general(qa_scr[h], k_aug, nt, preferred_element_type=F32)

    def softmax(h, s_buf, diagonal):
        for r in range(tq // rb):
            rows = slice(r * rb, (r + 1) * rb)
            s = s_buf[h, rows, :]
            if diagonal:
                col = lax.broadcasted_iota(jnp.int32, (rb, tk), 1)
                row = lax.broadcasted_iota(jnp.int32, (rb, tk), 0) + r * rb
                s = jnp.where(col <= row, s, NEG_INF)
            m_prev = m_scr[h, rows, :]
            m_next = jnp.maximum(m_prev, jnp.max(s, axis=-1, keepdims=True))
            p_scr[h, rows, :] = jnp.exp2(s - _lane_tile(m_next, tk // LANES)).astype(BF16)
            alpha_scr[h, rows, :] = jnp.exp2(m_prev - m_next)
            m_scr[h, rows, :] = m_next

    def values(h, v_aug):
        acc_scr[h] = (_lane_tile(alpha_scr[h], 2) * acc_scr[h]
                      + jnp.dot(p_scr[h], v_aug, preferred_element_type=F32))

    def tile(kt, diagonal):
        k0 = pl.multiple_of(kt * tk, tk)
        k_aug = key_tile(kt)
        v_aug = jnp.concatenate([v_ref[pl.ds(k0, tk), :], jnp.ones((tk, LANES), BF16)], axis=1)
        for h in range(HPG):
            scores(h, k_aug, s_scr)
        for h in range(HPG):
            softmax(h, s_scr, diagonal)
        for h in range(HPG):
            values(h, v_aug)

    def full_tile(kt, carry):
        tile(kt, False)
        return carry

    lax.fori_loop(0, qi, full_tile, 0)
    tile(qi, True)

    for h in range(HPG):
        o_ref[:, h * HEAD_DIM:(h + 1) * HEAD_DIM] = (acc_scr[h, :, 0:HEAD_DIM]
                                                     / acc_scr[h, :, HEAD_DIM:2 * HEAD_DIM])


def _sel_attn(qkv, bias, bsz, seq, tq=SEL_TQ, rb=SEL_RB):
    tq = min(tq, seq)
    rb = min(rb, tq)
    nq = seq // tq
    kcol = ATT_WIDTH // HEAD_DIM
    return pl.pallas_call(
        functools.partial(_sel_attn_body, tq=tq, rb=rb),
        out_shape=jax.ShapeDtypeStruct((bsz * seq, ATT_WIDTH), F32),
        grid=(bsz, N_KV, nq),
        in_specs=[pl.BlockSpec((tq, HPG * HEAD_DIM), lambda b, g, i: (b * nq + i, g)),
                  pl.BlockSpec((seq, HEAD_DIM), lambda b, g, i: (b, kcol + g)),
                  pl.BlockSpec((seq, HEAD_DIM), lambda b, g, i: (b, kcol + N_KV + g)),
                  pl.BlockSpec((None, None, tq, LANES), lambda b, g, i: (b, g, i, 0))],
        out_specs=pl.BlockSpec((tq, HPG * HEAD_DIM), lambda b, g, i: (b * nq + i, g)),
        scratch_shapes=[pltpu.VMEM((HPG, tq, 2 * HEAD_DIM), BF16),
                        pltpu.VMEM((HPG, tq, LANES), F32),
                        pltpu.VMEM((HPG, tq, 2 * HEAD_DIM), F32),
                        pltpu.VMEM((HPG, tq, tq), F32),
                        pltpu.VMEM((HPG, tq, tq), BF16),
                        pltpu.VMEM((HPG, tq, LANES), F32)],
        compiler_params=_params(("parallel", "parallel", "arbitrary")),
        name="nsa_sel_attn",
    )(qkv, qkv, qkv, bias)


def _win_attn_body(q_ref, kc_ref, kp_ref, vc_ref, vp_ref, ocmp_ref, osel_ref, gate_ref, o_ref, *, tq):
    qi = pl.program_id(2)
    nt = (((1,), (1,)), ((), ()))
    ones = jnp.ones((tq, LANES), BF16)
    vc_aug = jnp.concatenate([vc_ref[...], ones], axis=1)
    vp_aug = jnp.concatenate([vp_ref[...], ones], axis=1)
    for r in range(tq // ATT_ROWS):
        rows = slice(r * ATT_ROWS, (r + 1) * ATT_ROWS)
        n_cur, p_lo = (r + 1) * ATT_ROWS, r * ATT_ROWS
        row_c = lax.broadcasted_iota(jnp.int32, (ATT_ROWS, n_cur), 0) + r * ATT_ROWS
        ok_cur = lax.broadcasted_iota(jnp.int32, (ATT_ROWS, n_cur), 1) <= row_c
        row_p = lax.broadcasted_iota(jnp.int32, (ATT_ROWS, tq - p_lo), 0) + r * ATT_ROWS
        ok_prev = (lax.broadcasted_iota(jnp.int32, (ATT_ROWS, tq - p_lo), 1) + p_lo > row_p) & (qi > 0)
        gates = jax.nn.sigmoid(gate_ref[rows, :])
        for h in range(HPG):
            sl = slice(h * HEAD_DIM, (h + 1) * HEAD_DIM)
            qh = q_ref[rows, sl]
            s1 = jnp.where(ok_cur, lax.dot_general(qh, kc_ref[0:n_cur, :], nt, preferred_element_type=F32), NEG_INF)
            s0 = jnp.where(ok_prev, lax.dot_general(qh, kp_ref[p_lo:tq, :], nt, preferred_element_type=F32), NEG_INF)
            m = jnp.maximum(jnp.max(s1, axis=-1, keepdims=True), jnp.max(s0, axis=-1, keepdims=True))
            p1 = jnp.exp2(s1 - m).astype(BF16)
            p0 = jnp.exp2(s0 - m).astype(BF16)
            acc = (jnp.dot(p1, vc_aug[0:n_cur, :], preferred_element_type=F32)
                   + jnp.dot(p0, vp_aug[p_lo:tq, :], preferred_element_type=F32))
            ow = acc[:, 0:HEAD_DIM] / acc[:, HEAD_DIM:2 * HEAD_DIM]
            g_cmp = gates[:, h:h + 1]
            g_sel = gates[:, HPG + h:HPG + h + 1]
            g_win = gates[:, 2 * HPG + h:2 * HPG + h + 1]
            o_ref[rows, sl] = (g_cmp * ocmp_ref[rows, sl] + g_sel * osel_ref[rows, sl]
                               + g_win * ow).astype(o_ref.dtype)


def _win_attn_combine(qkv, o_cmp, o_sel, gates, bsz, seq):
    tq = WINDOW
    assert seq % tq == 0
    nq = seq // tq
    kcol = ATT_WIDTH // HEAD_DIM + 2 * N_KV
    qspec = pl.BlockSpec((tq, HPG * HEAD_DIM), lambda b, g, i: (b * nq + i, g))
    cur = lambda c: pl.BlockSpec((tq, HEAD_DIM), lambda b, g, i: (b * nq + i, kcol + c * N_KV + g))
    prev = lambda c: pl.BlockSpec((tq, HEAD_DIM),
                                  lambda b, g, i: (b * nq + jnp.maximum(i - 1, 0), kcol + c * N_KV + g))
    return pl.pallas_call(
        functools.partial(_win_attn_body, tq=tq),
        out_shape=jax.ShapeDtypeStruct((bsz * seq, ATT_WIDTH), BF16),
        grid=(bsz, N_KV, nq),
        in_specs=[qspec, cur(0), prev(0), cur(1), prev(1), qspec, qspec,
                  pl.BlockSpec((None, None, tq, LANES), lambda b, g, i: (b, g, i, 0))],
        out_specs=qspec,
        compiler_params=_params(("parallel", "parallel", "parallel")),
        name="nsa_win_attn_gate",
    )(qkv, qkv, qkv, qkv, qkv, o_cmp, o_sel, gates)


def _merge_body(hn_ref, ya_ref, yb_ref, wga_ref, wgb_ref, wa_ref, wb_ref, o_ref):
    hn = hn_ref[...]
    ga = jnp.dot(hn, wga_ref[...], preferred_element_type=F32)
    gb = jnp.dot(hn, wgb_ref[...], preferred_element_type=F32)
    pa = jnp.dot(ya_ref[...], wa_ref[...], preferred_element_type=F32)
    pb = jnp.dot(yb_ref[...], wb_ref[...], preferred_element_type=F32)
    o_ref[...] = (jax.nn.sigmoid(ga) * pa + jax.nn.sigmoid(gb) * pb).astype(o_ref.dtype)


def _merge(hn, ya, yb, wg, wa, wb, tm=512, tn=512):
    m, d = hn.shape
    ka = ya.shape[1]
    tm = min(tm, m)
    row = lambda k: pl.BlockSpec((tm, k), lambda i, j: (i, 0))
    col = lambda k: pl.BlockSpec((k, tn), lambda i, j: (0, j))
    col_b = pl.BlockSpec((d, tn), lambda i, j: (0, d // tn + j))
    return pl.pallas_call(
        _merge_body,
        out_shape=jax.ShapeDtypeStruct((m, d), BF16),
        grid=(m // tm, d // tn),
        in_specs=[row(d), row(ka), row(ka), col(d), col_b, col(ka), col(ka)],
        out_specs=pl.BlockSpec((tm, tn), lambda i, j: (i, j)),
        compiler_params=_params(("parallel", "arbitrary")),
        name="mixer_merge",
    )(hn, ya, yb, wg, wg, wa, wb)


def _rms(x, g):
    return x * lax.rsqrt(jnp.mean(x * x, axis=-1, keepdims=True) + RMS_EPS) * g


def _res_norm2_body(raw_ref, x_ref, gpost_ref, gpre_ref, h_ref, hn_ref):
    h = x_ref[...] + _rms(raw_ref[...].astype(F32), gpost_ref[...])
    h_ref[...] = h
    hn_ref[...] = _rms(h, gpre_ref[...]).astype(hn_ref.dtype)


def _res_norm2(raw, x, g_post, g_pre, tm=256):
    m, d = x.shape
    tm = min(tm, m)
    rows = pl.BlockSpec((tm, d), lambda i: (i, 0))
    vec = pl.BlockSpec((1, d), lambda i: (0, 0))
    return pl.pallas_call(
        _res_norm2_body,
        out_shape=(jax.ShapeDtypeStruct((m, d), F32), jax.ShapeDtypeStruct((m, d), BF16)),
        grid=(m // tm,),
        in_specs=[rows, rows, vec, vec],
        out_specs=(rows, rows),
        compiler_params=_params(("parallel",)),
        name="residual_norm_prenorm",
    )(raw, x, g_post.reshape(1, d), g_pre.reshape(1, d))


def _res_norm_body(raw_ref, x_ref, g_ref, o_ref):
    o_ref[...] = x_ref[...] + _rms(raw_ref[...].astype(F32), g_ref[...])


def _res_norm(raw, x, g, tm=256):
    m, d = x.shape
    tm = min(tm, m)
    rows = pl.BlockSpec((tm, d), lambda i: (i, 0))
    return pl.pallas_call(
        _res_norm_body,
        out_shape=jax.ShapeDtypeStruct((m, d), F32),
        grid=(m // tm,),
        in_specs=[rows, rows, pl.BlockSpec((1, d), lambda i: (0, 0))],
        out_specs=rows,
        compiler_params=_params(("parallel",)),
        name="residual_norm",
    )(raw, x, g.reshape(1, d))


def _swiglu_body(a_ref, wg_ref, wu_ref, o_ref, wg_scr, wu_scr):
    @pl.when(pl.program_id(1) == 0)
    def _():
        wg_scr[...] = wg_ref[...].astype(BF16)
        wu_scr[...] = wu_ref[...].astype(BF16)

    a = a_ref[...]
    g = jnp.dot(a, wg_scr[...], preferred_element_type=F32)
    u = jnp.dot(a, wu_scr[...], preferred_element_type=F32)
    o_ref[...] = (g * jax.nn.sigmoid(g) * u).astype(o_ref.dtype)


def _swiglu(a, wg, wu, tm=1024, tn=256):
    m, k = a.shape
    n = wg.shape[1]
    tm = min(tm, m)
    assert n % tn == 0
    return pl.pallas_call(
        _swiglu_body,
        out_shape=jax.ShapeDtypeStruct((m, n), BF16),
        grid=(n // tn, m // tm),
        in_specs=[pl.BlockSpec((tm, k), lambda j, i: (i, 0)),
                  pl.BlockSpec((k, tn), lambda j, i: (0, j)),
                  pl.BlockSpec((k, tn), lambda j, i: (0, j))],
        out_specs=pl.BlockSpec((tm, tn), lambda j, i: (i, j)),
        scratch_shapes=[pltpu.VMEM((k, tn), BF16), pltpu.VMEM((k, tn), BF16)],
        compiler_params=_params(("arbitrary", "arbitrary")),
        name="ffn_swiglu",
    )(a, wg, wu)


def kernel(x, norm_mix_pre, w_in, ssm_a_re, ssm_a_im, ssm_log_dt, ssm_b_re, ssm_b_im, ssm_c_re, ssm_c_im, ssm_d, ssm_w_glu, ssm_b_glu, cmp_pe_k, cmp_w1_k, cmp_w2_k, cmp_pe_v, cmp_w1_v, cmp_w2_v, w_proj_a, w_proj_b, w_out, norm_mix_post, norm_ffn_pre, w_ffn_gate, w_ffn_up, w_ffn_down, norm_ffn_post):
    bsz, seq, d = x.shape
    m = bsz * seq
    depth = w_in.shape[0]
    h = x.reshape(m, d)
    o_q, o_kvc, o_kv, o_gn = SSM_WIDTH, SSM_WIDTH + ATT_WIDTH, SSM_WIDTH + ATT_WIDTH + 2 * KV_WIDTH, 7168
    o_ga = o_gn + 3 * N_HEADS
    o_gb = o_ga + D_MODEL
    bf = lambda a: a.astype(BF16)
    tn_in = 512
    nb_u, nb_q = SSM_WIDTH // tn_in, ATT_WIDTH // tn_in
    qk_scale = HEAD_DIM ** -0.5 * LOG2E
    for l in range(depth):
        wt = jnp.swapaxes(w_in[l], 0, 1)
        w_g = _rows_transposed_cast(wt, o_ga, 2 * D_MODEL, BF16)

        hn, gn = _rmsnorm_gates(h, norm_mix_pre[l], wt, o_gn // LANES)
        proj_f32 = _matmul_f32w(hn, wt, SSM_WIDTH + 2 * KV_WIDTH, F32, 1024, tn_in, trans_w=True,
                                col_block=lambda j: jnp.where(j < nb_u, j, j - nb_u + o_kvc // tn_in),
                                name="in_proj_f32")
        qkv = _matmul_f32w(hn, wt, ATT_WIDTH + 4 * KV_WIDTH, BF16, 1024, tn_in, trans_w=True,
                           col_block=lambda j: jnp.where(j < nb_q, j + o_q // tn_in, j - nb_q + o_kv // tn_in),
                           scale_fn=lambda j: jnp.where(j < nb_q, qk_scale, 1.0),
                           name="in_proj_bf16")

        y = _s5_mixer(proj_f32.reshape(bsz, seq, -1), ssm_a_re[l], ssm_a_im[l], ssm_log_dt[l],
                      ssm_b_re[l], ssm_b_im[l], ssm_c_re[l], ssm_c_im[l], ssm_d[l])
        y_a = _glu(y.reshape(m, SSM_WIDTH), bf(ssm_w_glu[l]), ssm_b_glu[l])

        pe = jnp.stack([cmp_pe_k[l], cmp_pe_v[l]])
        w1 = bf(jnp.stack([cmp_w1_k[l], cmp_w1_v[l]]))
        w2 = bf(jnp.stack([cmp_w2_k[l], cmp_w2_v[l]]))
        kcv = _compress(proj_f32, SSM_WIDTH, pe, w1, w2, bsz, seq)
        o_cmp, bias_t = _cmp_attn(qkv, kcv, bsz, seq)
        nblk = seq // L_SEL
        bias = bf(jnp.pad(jnp.swapaxes(bias_t, 2, 3), ((0, 0), (0, 0), (0, 0), (0, LANES - nblk))))
        o_sel = _sel_attn(qkv, bias, bsz, seq)
        gates = gn[:, :3 * N_HEADS].reshape(bsz, seq, 3, N_KV, HPG).transpose(0, 3, 1, 2, 4)
        gates = jnp.pad(gates.reshape(bsz, N_KV, seq, 3 * HPG), ((0, 0), (0, 0), (0, 0), (0, LANES - 3 * HPG)))
        y_b = _win_attn_combine(qkv, o_cmp, o_sel, gates, bsz, seq)

        merged = _merge(hn, y_a, y_b, w_g, bf(w_proj_a[l]), bf(w_proj_b[l]))
        mix = _matmul_f32w(merged, w_out[l], D_MODEL, BF16, 1024, 512, name="out_proj")
        h, hn2 = _res_norm2(mix, h, norm_mix_post[l], norm_ffn_pre[l])

        act = _swiglu(hn2, w_ffn_gate[l], w_ffn_up[l])
        f = _matmul(act, bf(w_ffn_down[l]), BF16, 512, 512, name="ffn_down")
        h = _res_norm(f, h, norm_ffn_post[l])
    return h.reshape(bsz, seq, d)
```

```python
import functools
import math

import jax
import jax.numpy as jnp
from jax import lax
from jax.experimental import pallas as pl
from jax.experimental.pallas import tpu as pltpu

F32 = jnp.float32
BF16 = jnp.bfloat16

D_MODEL = 4096
SSM_WIDTH = 2048
SSM_GROUP = 16
SSM_GROUPS = 128
SSM_STATE = 64
N_HEADS = 16
HEAD_DIM = 128
N_KV = 4
HPG = 4
ATT_WIDTH = 2048
KV_WIDTH = 512
L_CMP = 32
STRIDE_CMP = 16
L_SEL = 64
N_SEL = 16
WINDOW = 512
D_FF = 11008
RMS_EPS = 1e-6
NEG_INF = -1e30
FORCE_SCORE = 1e9
TAKEN_SCORE = -3e38
LOG2E = math.log2(math.e)

V7X_VMEM_BYTES = 64 * 1024 * 1024
V7X_VMEM_LIMIT_BYTES = 56 * 1024 * 1024
V7X_VMEM_SLACK_BYTES = 6 * 1024 * 1024
LANES = 128
SUBLANES = 8

S5_GT = 16
S5_NC = S5_GT * SSM_GROUP
S5_NP = S5_GT * SSM_STATE
S5_TILES = SSM_GROUPS // S5_GT
S5_SLABS = 2 * S5_NP // LANES

ATT_ROWS = 128
SEL_TQ, SEL_RB = 512, 64


def _params(sem, vmem=V7X_VMEM_LIMIT_BYTES):
    return pltpu.CompilerParams(dimension_semantics=sem, vmem_limit_bytes=vmem)


def _lane_tile(x, n):
    return jnp.concatenate([x] * n, axis=1)


def _rmsnorm_gates_body(x_ref, g_ref, w_ref, hn_ref, gn_ref, wbf_scr):
    @pl.when(pl.program_id(0) == 0)
    def _():
        wbf_scr[...] = w_ref[...].astype(BF16)

    x = x_ref[...]
    ms = jnp.mean(x * x, axis=-1, keepdims=True)
    hn = (x * lax.rsqrt(ms + RMS_EPS) * g_ref[...]).astype(BF16)
    hn_ref[...] = hn
    gn_ref[...] = lax.dot_general(hn, wbf_scr[...], (((1,), (1,)), ((), ())), preferred_element_type=F32)


def _rmsnorm_gates(x, gain, wt, feature_block, tm=256):
    m, d = x.shape
    tm = min(tm, m)
    return pl.pallas_call(
        _rmsnorm_gates_body,
        out_shape=(jax.ShapeDtypeStruct((m, d), BF16), jax.ShapeDtypeStruct((m, LANES), F32)),
        grid=(m // tm,),
        in_specs=[pl.BlockSpec((tm, d), lambda i: (i, 0)),
                  pl.BlockSpec((1, d), lambda i: (0, 0)),
                  pl.BlockSpec((LANES, d), lambda i: (feature_block, 0))],
        out_specs=(pl.BlockSpec((tm, d), lambda i: (i, 0)), pl.BlockSpec((tm, LANES), lambda i: (i, 0))),
        scratch_shapes=[pltpu.VMEM((LANES, d), BF16)],
        compiler_params=_params(("arbitrary",)),
        name="rmsnorm_gates",
    )(x, gain.reshape(1, d), wt)


def _mm_body(a_ref, w_ref, o_ref):
    o_ref[...] = jnp.dot(a_ref[...], w_ref[...], preferred_element_type=F32).astype(o_ref.dtype)


def _matmul(a, w, out_dtype, tm, tn, name="matmul"):
    m, k = a.shape
    n = w.shape[1]
    tm, tn = min(tm, m), min(tn, n)
    assert m % tm == 0 and n % tn == 0
    return pl.pallas_call(
        _mm_body,
        out_shape=jax.ShapeDtypeStruct((m, n), out_dtype),
        grid=(m // tm, n // tn),
        in_specs=[pl.BlockSpec((tm, k), lambda i, j: (i, 0)), pl.BlockSpec((k, tn), lambda i, j: (0, j))],
        out_specs=pl.BlockSpec((tm, tn), lambda i, j: (i, j)),
        compiler_params=_params(("parallel", "arbitrary")),
        name=name,
    )(a, w)


_NT = (((1,), (1,)), ((), ()))


def _mm_f32w_body(a_ref, w_ref, o_ref, wbf_scr, *, scale_fn, trans_w):
    @pl.when(pl.program_id(1) == 0)
    def _():
        wbf_scr[...] = w_ref[...].astype(BF16)

    if trans_w:
        acc = lax.dot_general(a_ref[...], wbf_scr[...], _NT, preferred_element_type=F32)
    else:
        acc = jnp.dot(a_ref[...], wbf_scr[...], preferred_element_type=F32)
    if scale_fn is not None:
        acc = acc * scale_fn(pl.program_id(0))
    o_ref[...] = acc.astype(o_ref.dtype)


def _matmul_f32w(a, w, n_out, out_dtype, tm, tn, col_block=None, scale_fn=None, trans_w=False,
                 name="matmul_f32w"):
    m, k = a.shape
    tm = min(tm, m)
    assert m % tm == 0 and n_out % tn == 0
    col_block = col_block or (lambda j: j)
    if trans_w:
        w_spec = pl.BlockSpec((tn, k), lambda j, i: (col_block(j), 0))
        w_scratch = pltpu.VMEM((tn, k), BF16)
    else:
        w_spec = pl.BlockSpec((k, tn), lambda j, i: (0, col_block(j)))
        w_scratch = pltpu.VMEM((k, tn), BF16)
    return pl.pallas_call(
        functools.partial(_mm_f32w_body, scale_fn=scale_fn, trans_w=trans_w),
        out_shape=jax.ShapeDtypeStruct((m, n_out), out_dtype),
        grid=(n_out // tn, m // tm),
        in_specs=[pl.BlockSpec((tm, k), lambda j, i: (i, 0)), w_spec],
        out_specs=pl.BlockSpec((tm, tn), lambda j, i: (i, j)),
        scratch_shapes=[w_scratch],
        compiler_params=_params(("arbitrary", "arbitrary")),
        name=name,
    )(a, w)


def _rows_transposed_cast_body(w_ref, o_ref):
    o_ref[...] = w_ref[...].T.astype(o_ref.dtype)


def _rows_transposed_cast(wt, row0, n_rows, out_dtype, tr=512):
    k = wt.shape[1]
    assert row0 % (2 * SUBLANES) == 0 and n_rows % tr == 0
    return pl.pallas_call(
        _rows_transposed_cast_body,
        out_shape=jax.ShapeDtypeStruct((k, n_rows), out_dtype),
        grid=(n_rows // tr,),
        in_specs=[pl.BlockSpec((pl.Element(tr), pl.Element(k)),
                               lambda c: (pl.multiple_of(row0 + c * tr, 2 * SUBLANES), 0))],
        out_specs=pl.BlockSpec((k, tr), lambda c: (0, c)),
        compiler_params=_params(("parallel",)),
        name="gate_weight_cast",
    )(wt)


def _s5_body(u_ref, are_ref, aim_ref, ldt_ref, bre_ref, bim_ref, cre_ref, cim_ref, d_ref, y_ref,
             abar_scr, bbar_scr, cmat_scr, h_scr, st_scr, *, tc, bsz, tps):
    ti = pl.program_id(1)
    half = S5_SLABS // 2

    @pl.when(ti == 0)
    def _():
        row_tile = lax.broadcasted_iota(jnp.int32, (SUBLANES, S5_NP), 0) // bsz
        same_group = (lax.broadcasted_iota(jnp.int32, (S5_NC, S5_NP), 0) // SSM_GROUP
                      == lax.broadcasted_iota(jnp.int32, (S5_NC, S5_NP), 1) // SSM_STATE)
        bdiag = lambda x: jnp.where(same_group, jnp.concatenate([x] * S5_GT, axis=0), 0.0)
        abr8 = jnp.zeros((SUBLANES, S5_NP), F32)
        abi8 = jnp.zeros((SUBLANES, S5_NP), F32)
        for k in range(tps):
            cols = slice(k * S5_NP, (k + 1) * S5_NP)
            ar, ai = are_ref[:, cols], aim_ref[:, cols]
            dt = jnp.exp(ldt_ref[:, cols])
            decay = jnp.exp(dt * ar)
            abr, abi = decay * jnp.cos(dt * ai), decay * jnp.sin(dt * ai)
            den = ar * ar + ai * ai
            zr = ((abr - 1.0) * ar + abi * ai) / den
            zi = (abi * ar - (abr - 1.0) * ai) / den
            abr8 = jnp.where(row_tile == k, abr, abr8)
            abi8 = jnp.where(row_tile == k, abi, abi8)
            br, bi = bdiag(bre_ref[k]), bdiag(bim_ref[k])
            bbar_scr[k, :, 0:S5_NP] = (zr * br - zi * bi).astype(BF16)
            bbar_scr[k, :, S5_NP:2 * S5_NP] = (zr * bi + zi * br).astype(BF16)
            cmat_scr[k, :, 0:S5_NP] = bdiag(cre_ref[k]).astype(BF16)
            cmat_scr[k, :, S5_NP:2 * S5_NP] = (-bdiag(cim_ref[k])).astype(BF16)
        abar_scr[0:SUBLANES, :] = abr8
        abar_scr[SUBLANES:2 * SUBLANES, :] = abi8
        st_scr[...] = jnp.zeros_like(st_scr)

    for k in range(tps):
        cols = slice(k * S5_NC, (k + 1) * S5_NC)
        ub = jnp.concatenate([u_ref[b, :, cols] for b in range(bsz)], axis=0).astype(BF16)
        for s2 in range(S5_SLABS // 2):
            bu = jnp.dot(ub, bbar_scr[k, :, s2 * 2 * LANES:(s2 + 1) * 2 * LANES], preferred_element_type=F32)
            for b in range(bsz):
                for e in range(2):
                    h_scr[2 * s2 + e, pl.ds(k * bsz + b, tc, stride=SUBLANES), :] = bu[b * tc:(b + 1) * tc,
                                                                                      e * LANES:(e + 1) * LANES]

    ar = abar_scr[0:SUBLANES, :]
    ai = abar_scr[SUBLANES:2 * SUBLANES, :]

    def step(t, carry):
        hr, hi = carry
        r0 = pl.multiple_of(t * SUBLANES, SUBLANES)
        bur = jnp.concatenate([h_scr[s, pl.ds(r0, SUBLANES), :] for s in range(half)], axis=1)
        bui = jnp.concatenate([h_scr[half + s, pl.ds(r0, SUBLANES), :] for s in range(half)], axis=1)
        nhr = ar * hr - ai * hi + bur
        nhi = ar * hi + ai * hr + bui
        for s in range(half):
            h_scr[s, pl.ds(r0, SUBLANES), :] = nhr[:, s * LANES:(s + 1) * LANES]
            h_scr[half + s, pl.ds(r0, SUBLANES), :] = nhi[:, s * LANES:(s + 1) * LANES]
        return nhr, nhi

    hr, hi = lax.fori_loop(0, tc, step, (st_scr[0:SUBLANES, :], st_scr[SUBLANES:2 * SUBLANES, :]), unroll=2)
    st_scr[0:SUBLANES, :] = hr
    st_scr[SUBLANES:2 * SUBLANES, :] = hi

    for k in range(tps):
        cols = slice(k * S5_NC, (k + 1) * S5_NC)

        hb = jnp.concatenate(
            [jnp.concatenate([h_scr[s, pl.ds(k * bsz + b, tc, stride=SUBLANES), :] for s in range(S5_SLABS)], axis=1)
             for b in range(bsz)], axis=0).astype(BF16)
        ch = lax.dot_general(hb, cmat_scr[k], _NT, preferred_element_type=F32)
        for b in range(bsz):
            y = ch[b * tc:(b + 1) * tc, :] + d_ref[:, cols] * u_ref[b, :, cols]
            y_ref[b, :, cols] = jax.nn.gelu(y)


def _s5_mixer(u3, a_re, a_im, log_dt, b_re, b_im, c_re, c_im, d_skip, tc=256):
    bsz, seq, _ = u3.shape
    assert SUBLANES % bsz == 0
    tps = min(SUBLANES // bsz, S5_TILES)
    assert tps * bsz == SUBLANES
    tc = min(tc, seq)

    def per_tile_b(b):
        return (b.reshape(S5_TILES, S5_GT, SSM_STATE, SSM_GROUP).transpose(0, 3, 1, 2)
                .reshape(S5_TILES, SSM_GROUP, S5_NP))

    def per_tile_c(c):
        return (c.reshape(S5_TILES, S5_GT, SSM_GROUP, SSM_STATE).transpose(0, 2, 1, 3)
                .reshape(S5_TILES, SSM_GROUP, S5_NP))

    flat = lambda a: a.reshape(1, SSM_GROUPS * SSM_STATE)
    ldt = jnp.repeat(log_dt, SSM_STATE).reshape(1, SSM_GROUPS * SSM_STATE)
    vec_spec = pl.BlockSpec((1, tps * S5_NP), lambda g, t: (0, g))
    bc_spec = pl.BlockSpec((tps, SSM_GROUP, S5_NP), lambda g, t: (g, 0, 0))
    return pl.pallas_call(
        functools.partial(_s5_body, tc=tc, bsz=bsz, tps=tps),
        out_shape=jax.ShapeDtypeStruct((bsz, seq, SSM_WIDTH), F32),
        grid=(S5_TILES // tps, seq // tc),
        in_specs=[
            pl.BlockSpec((bsz, tc, tps * S5_NC), lambda g, t: (0, t, g)),
            vec_spec, vec_spec, vec_spec, bc_spec, bc_spec, bc_spec, bc_spec,
            pl.BlockSpec((1, tps * S5_NC), lambda g, t: (0, g)),
        ],
        out_specs=pl.BlockSpec((bsz, tc, tps * S5_NC), lambda g, t: (0, t, g)),
        scratch_shapes=[
            pltpu.VMEM((2 * SUBLANES, S5_NP), F32),
            pltpu.VMEM((tps, S5_NC, 2 * S5_NP), BF16),
            pltpu.VMEM((tps, S5_NC, 2 * S5_NP), BF16),
            pltpu.VMEM((S5_SLABS, tc * SUBLANES, LANES), F32),
            pltpu.VMEM((2 * SUBLANES, S5_NP), F32),
        ],
        compiler_params=_params(("arbitrary", "arbitrary")),
        name="s5_scan",
    )(u3, flat(a_re), flat(a_im), ldt, per_tile_b(b_re), per_tile_b(b_im), per_tile_c(c_re), per_tile_c(c_im),
      d_skip.reshape(1, SSM_WIDTH))


def _glu_body(y_ref, w_ref, b_ref, o_ref, ybf_scr, *, tn):
    j = pl.program_id(1)

    @pl.when(j == 0)
    def _():
        ybf_scr[...] = y_ref[...].astype(BF16)

    z = jnp.dot(ybf_scr[...], w_ref[...], preferred_element_type=F32) + b_ref[...]
    yt = y_ref[:, pl.ds(pl.multiple_of(j * tn, tn), tn)]
    o_ref[...] = (yt * jax.nn.sigmoid(z)).astype(o_ref.dtype)


def _glu(y, w_bf, bias, tm=1024, tn=1024):
    m, k = y.shape
    tm = min(tm, m)
    return pl.pallas_call(
        functools.partial(_glu_body, tn=tn),
        out_shape=jax.ShapeDtypeStruct((m, k), BF16),
        grid=(m // tm, k // tn),
        in_specs=[pl.BlockSpec((tm, k), lambda i, j: (i, 0)),
                  pl.BlockSpec((k, tn), lambda i, j: (0, j)),
                  pl.BlockSpec((1, tn), lambda i, j: (0, j))],
        out_specs=pl.BlockSpec((tm, tn), lambda i, j: (i, j)),
        scratch_shapes=[pltpu.VMEM((tm, k), BF16)],
        compiler_params=_params(("parallel", "arbitrary")),
        name="s5_glu",
    )(y, w_bf, bias.reshape(1, k))


def _compress_body(x_ref, pe_ref, w1_ref, w2_ref, o_ref, *, ncmp):
    half = L_CMP // 2
    acc_a = jnp.zeros((ncmp, HEAD_DIM), F32)
    acc_b = jnp.zeros((ncmp, HEAD_DIM), F32)
    for r in range(half):
        xr = x_ref[pl.ds(r, ncmp, stride=STRIDE_CMP), :]
        xa = (xr + pe_ref[r:r + 1, :]).astype(BF16)
        xb = (xr + pe_ref[half + r:half + r + 1, :]).astype(BF16)
        acc_a += jnp.dot(xa, w1_ref[r * HEAD_DIM:(r + 1) * HEAD_DIM, :], preferred_element_type=F32)
        acc_b += jnp.dot(xb, w1_ref[(half + r) * HEAD_DIM:(half + r + 1) * HEAD_DIM, :],
                         preferred_element_type=F32)
    pre = acc_a + jnp.concatenate([acc_b[1:], acc_b[:1]], axis=0)
    o_ref[...] = jnp.dot(jax.nn.gelu(pre).astype(BF16), w2_ref[...], preferred_element_type=F32).astype(o_ref.dtype)


def _compress(proj_f32, col0, pe, w1_bf, w2_bf, bsz, seq):
    ncmp = seq // STRIDE_CMP
    cb = col0 // HEAD_DIM
    return pl.pallas_call(
        functools.partial(_compress_body, ncmp=ncmp),
        out_shape=jax.ShapeDtypeStruct((bsz, 2, N_KV, ncmp, HEAD_DIM), BF16),
        grid=(bsz, 2, N_KV),
        in_specs=[pl.BlockSpec((seq, HEAD_DIM), lambda b, w, g: (b, cb + w * N_KV + g)),
                  pl.BlockSpec((None, L_CMP, HEAD_DIM), lambda b, w, g: (w, 0, 0)),
                  pl.BlockSpec((None, L_CMP * HEAD_DIM, HEAD_DIM), lambda b, w, g: (w, 0, 0)),
                  pl.BlockSpec((None, HEAD_DIM, HEAD_DIM), lambda b, w, g: (w, 0, 0))],
        out_specs=pl.BlockSpec((None, None, None, ncmp, HEAD_DIM), lambda b, w, g: (b, w, g, 0, 0)),
        compiler_params=_params(("parallel", "parallel", "parallel")),
        name="nsa_compress",
    )(proj_f32, pe, w1_bf, w2_bf)


def _cmp_attn_body(q_ref, kc_ref, vc_ref, o_ref, bias_ref, *, tq, ncmp, nblk, ntop):
    qi = pl.program_id(2)
    t0 = qi * tq
    tpos = t0 + lax.broadcasted_iota(jnp.int32, (tq, ncmp), 0)
    blk_end = lax.broadcasted_iota(jnp.int32, (tq, ncmp), 1) * STRIDE_CMP + (L_CMP - 1)
    ok = blk_end <= tpos
    okf = ok.astype(F32)
    kc = kc_ref[...]
    vc = vc_ref[...]
    psum = jnp.zeros((tq, ncmp), F32)
    for h in range(HPG):
        qh = q_ref[:, h * HEAD_DIM:(h + 1) * HEAD_DIM]
        s = lax.dot_general(qh, kc, (((1,), (1,)), ((), ())), preferred_element_type=F32)
        s = jnp.where(ok, s, NEG_INF)
        e = jnp.exp2(s - jnp.max(s, axis=-1, keepdims=True))
        p = e / jnp.sum(e, axis=-1, keepdims=True) * okf
        o_ref[:, h * HEAD_DIM:(h + 1) * HEAD_DIM] = jnp.dot(p.astype(BF16), vc, preferred_element_type=F32)
        psum = psum + p

    jj = lax.broadcasted_iota(jnp.int32, (nblk, ncmp), 0)
    nn = lax.broadcasted_iota(jnp.int32, (nblk, ncmp), 1)
    ov = ((nn * STRIDE_CMP < (jj + 1) * L_SEL) & (nn * STRIDE_CMP + L_CMP > jj * L_SEL)).astype(BF16)
    p_hi = psum.astype(BF16)
    p_lo = (psum - p_hi.astype(F32)).astype(BF16)
    nt = (((1,), (1,)), ((), ()))
    imp = (lax.dot_general(ov, p_hi, nt, preferred_element_type=F32)
           + lax.dot_general(ov, p_lo, nt, preferred_element_type=F32))

    jb = lax.broadcasted_iota(jnp.int32, (nblk, tq), 0)
    tt = t0 + lax.broadcasted_iota(jnp.int32, (nblk, tq), 1)
    cur = tt // L_SEL
    allowed = jb * L_SEL <= tt
    forced = (jb == 0) | (jb == cur) | (jb == cur - 1)
    score = jnp.where(forced, FORCE_SCORE, jnp.where(allowed, imp, NEG_INF))
    taken = jnp.zeros((nblk, tq), F32)
    for _ in range(ntop):
        best = jnp.max(score, axis=0, keepdims=True)
        first = jnp.min(jnp.where(score == best, jb, nblk), axis=0, keepdims=True)
        pick = jb == first
        taken = jnp.where(pick, 1.0, taken)
        score = jnp.where(pick, TAKEN_SCORE, score)
    bias_ref[...] = jnp.where(taken > 0.5, 0.0, NEG_INF).astype(bias_ref.dtype)


def _cmp_attn(qkv, kcv, bsz, seq, tq=512):
    tq = min(tq, seq)
    ncmp = seq // STRIDE_CMP
    nblk = seq // L_SEL
    ntop = min(N_SEL, nblk)
    nq = seq // tq
    return pl.pallas_call(
        functools.partial(_cmp_attn_body, tq=tq, ncmp=ncmp, nblk=nblk, ntop=ntop),
        out_shape=(jax.ShapeDtypeStruct((bsz * seq, ATT_WIDTH), F32),
                   jax.ShapeDtypeStruct((bsz, N_KV, nblk, seq), F32)),
        grid=(bsz, N_KV, nq),
        in_specs=[pl.BlockSpec((tq, HPG * HEAD_DIM), lambda b, g, i: (b * nq + i, g)),
                  pl.BlockSpec((None, None, None, ncmp, HEAD_DIM), lambda b, g, i: (b, 0, g, 0, 0)),
                  pl.BlockSpec((None, None, None, ncmp, HEAD_DIM), lambda b, g, i: (b, 1, g, 0, 0))],
        out_specs=(pl.BlockSpec((tq, HPG * HEAD_DIM), lambda b, g, i: (b * nq + i, g)),
                   pl.BlockSpec((None, None, nblk, tq), lambda b, g, i: (b, g, 0, i))),
        compiler_params=_params(("parallel", "parallel", "parallel")),
        name="nsa_cmp_attn_topk",
    )(qkv, kcv, kcv)


def _sel_attn_body(q_ref, k_ref, v_ref, bias_ref, o_ref, qa_scr, m_scr, acc_scr, s_scr, p_scr, alpha_scr,
                   *, tq, rb):
    qi = pl.program_id(2)
    tk = tq
    m_scr[...] = jnp.full_like(m_scr, NEG_INF)
    acc_scr[...] = jnp.zeros_like(acc_scr)
    for h in range(HPG):
        qa_scr[h, :, 0:HEAD_DIM] = q_ref[:, h * HEAD_DIM:(h + 1) * HEAD_DIM]
        qa_scr[h, :, HEAD_DIM:2 * HEAD_DIM] = bias_ref[...]
    nt = (((1,), (1,)), ((), ()))

    def key_tile(kt):
        k0 = pl.multiple_of(kt * tk, tk)
        key_blk = kt * (tk // L_SEL) + lax.broadcasted_iota(jnp.int32, (tk, LANES), 0) // L_SEL
        onehot = jnp.where(lax.broadcasted_iota(jnp.int32, (tk, LANES), 1) == key_blk, 1.0, 0.0).astype(BF16)
        return jnp.concatenate([k_ref[pl.ds(k0, tk), :], onehot], axis=1)

    def scores(h, k_aug, s_buf):
        s_buf[h] = lax.dot_general(qa_scr[h], k_aug, nt, preferred_element_type=F32)

    def softmax(h, s_buf, diagonal):
        for r in range(tq // rb):
            rows = slice(r * rb, (r + 1) * rb)
            s = s_buf[h, rows, :]
            if diagonal:
                col = lax.broadcasted_iota(jnp.int32, (rb, tk), 1)
                row = lax.broadcasted_iota(jnp.int32, (rb, tk), 0) + r * rb
                s = jnp.where(col <= row, s, NEG_INF)
            m_prev = m_scr[h, rows, :]
            m_next = jnp.maximum(m_prev, jnp.max(s, axis=-1, keepdims=True))
            p_scr[h, rows, :] = jnp.exp2(s - _lane_tile(m_next, tk // LANES)).astype(BF16)
            alpha_scr[h, rows, :] = jnp.exp2(m_prev - m_next)
            m_scr[h, rows, :] = m_next

    def values(h, v_aug):
        acc_scr[h] = (_lane_tile(alpha_scr[h], 2) * acc_scr[h]
                      + jnp.dot(p_scr[h], v_aug, preferred_element_type=F32))

    def tile(kt, diagonal):
        k0 = pl.multiple_of(kt * tk, tk)
        k_aug = key_tile(kt)
        v_aug = jnp.concatenate([v_ref[pl.ds(k0, tk), :], jnp.ones((tk, LANES), BF16)], axis=1)
        for h in range(HPG):
            scores(h, k_aug, s_scr)
        for h in range(HPG):
            softmax(h, s_scr, diagonal)
        for h in range(HPG):
            values(h, v_aug)

    def full_tile(kt, carry):
        tile(kt, False)
        return carry

    lax.fori_loop(0, qi, full_tile, 0)
    tile(qi, True)

    for h in range(HPG):
        o_ref[:, h * HEAD_DIM:(h + 1) * HEAD_DIM] = (acc_scr[h, :, 0:HEAD_DIM]
                                                     / acc_scr[h, :, HEAD_DIM:2 * HEAD_DIM])


def _sel_attn(qkv, bias, bsz, seq, tq=SEL_TQ, rb=SEL_RB):
    tq = min(tq, seq)
    rb = min(rb, tq)
    nq = seq // tq
    kcol = ATT_WIDTH // HEAD_DIM
    return pl.pallas_call(
        functools.partial(_sel_attn_body, tq=tq, rb=rb),
        out_shape=jax.ShapeDtypeStruct((bsz * seq, ATT_WIDTH), F32),
        grid=(bsz, N_KV, nq),
        in_specs=[pl.BlockSpec((tq, HPG * HEAD_DIM), lambda b, g, i: (b * nq + i, g)),
                  pl.BlockSpec((seq, HEAD_DIM), lambda b, g, i: (b, kcol + g)),
                  pl.BlockSpec((seq, HEAD_DIM), lambda b, g, i: (b, kcol + N_KV + g)),
                  pl.BlockSpec((None, None, tq, LANES), lambda b, g, i: (b, g, i, 0))],
        out_specs=pl.BlockSpec((tq, HPG * HEAD_DIM), lambda b, g, i: (b * nq + i, g)),
        scratch_shapes=[pltpu.VMEM((HPG, tq, 2 * HEAD_DIM), BF16),
                        pltpu.VMEM((HPG, tq, LANES), F32),
                        pltpu.VMEM((HPG, tq, 2 * HEAD_DIM), F32),
                        pltpu.VMEM((HPG, tq, tq), F32),
                        pltpu.VMEM((HPG, tq, tq), BF16),
                        pltpu.VMEM((HPG, tq, LANES), F32)],
        compiler_params=_params(("parallel", "parallel", "arbitrary")),
        name="nsa_sel_attn",
    )(qkv, qkv, qkv, bias)


def _win_attn_body(q_ref, kc_ref, kp_ref, vc_ref, vp_ref, ocmp_ref, osel_ref, gate_ref, o_ref, *, tq):
    qi = pl.program_id(2)
    nt = (((1,), (1,)), ((), ()))
    ones = jnp.ones((tq, LANES), BF16)
    vc_aug = jnp.concatenate([vc_ref[...], ones], axis=1)
    vp_aug = jnp.concatenate([vp_ref[...], ones], axis=1)
    for r in range(tq // ATT_ROWS):
        rows = slice(r * ATT_ROWS, (r + 1) * ATT_ROWS)
        n_cur, p_lo = (r + 1) * ATT_ROWS, r * ATT_ROWS
        row_c = lax.broadcasted_iota(jnp.int32, (ATT_ROWS, n_cur), 0) + r * ATT_ROWS
        ok_cur = lax.broadcasted_iota(jnp.int32, (ATT_ROWS, n_cur), 1) <= row_c
        row_p = lax.broadcasted_iota(jnp.int32, (ATT_ROWS, tq - p_lo), 0) + r * ATT_ROWS
        ok_prev = (lax.broadcasted_iota(jnp.int32, (ATT_ROWS, tq - p_lo), 1) + p_lo > row_p) & (qi > 0)
        gates = jax.nn.sigmoid(gate_ref[rows, :])
        for h in range(HPG):
            sl = slice(h * HEAD_DIM, (h + 1) * HEAD_DIM)
            qh = q_ref[rows, sl]
            s1 = jnp.where(ok_cur, lax.dot_general(qh, kc_ref[0:n_cur, :], nt, preferred_element_type=F32), NEG_INF)
            s0 = jnp.where(ok_prev, lax.dot_general(qh, kp_ref[p_lo:tq, :], nt, preferred_element_type=F32), NEG_INF)
            m = jnp.maximum(jnp.max(s1, axis=-1, keepdims=True), jnp.max(s0, axis=-1, keepdims=True))
            p1 = jnp.exp2(s1 - m).astype(BF16)
            p0 = jnp.exp2(s0 - m).astype(BF16)
            acc = (jnp.dot(p1, vc_aug[0:n_cur, :], preferred_element_type=F32)
                   + jnp.dot(p0, vp_aug[p_lo:tq, :], preferred_element_type=F32))
            ow = acc[:, 0:HEAD_DIM] / acc[:, HEAD_DIM:2 * HEAD_DIM]
            g_cmp = gates[:, h:h + 1]
            g_sel = gates[:, HPG + h:HPG + h + 1]
            g_win = gates[:, 2 * HPG + h:2 * HPG + h + 1]
            o_ref[rows, sl] = (g_cmp * ocmp_ref[rows, sl] + g_sel * osel_ref[rows, sl]
                               + g_win * ow).astype(o_ref.dtype)


def _win_attn_combine(qkv, o_cmp, o_sel, gates, bsz, seq):
    tq = WINDOW
    assert seq % tq == 0
    nq = seq // tq
    kcol = ATT_WIDTH // HEAD_DIM + 2 * N_KV
    qspec = pl.BlockSpec((tq, HPG * HEAD_DIM), lambda b, g, i: (b * nq + i, g))
    cur = lambda c: pl.BlockSpec((tq, HEAD_DIM), lambda b, g, i: (b * nq + i, kcol + c * N_KV + g))
    prev = lambda c: pl.BlockSpec((tq, HEAD_DIM),
                                  lambda b, g, i: (b * nq + jnp.maximum(i - 1, 0), kcol + c * N_KV + g))
    return pl.pallas_call(
        functools.partial(_win_attn_body, tq=tq),
        out_shape=jax.ShapeDtypeStruct((bsz * seq, ATT_WIDTH), BF16),
        grid=(bsz, N_KV, nq),
        in_specs=[qspec, cur(0), prev(0), cur(1), prev(1), qspec, qspec,
                  pl.BlockSpec((None, None, tq, LANES), lambda b, g, i: (b, g, i, 0))],
        out_specs=qspec,
        compiler_params=_params(("parallel", "parallel", "parallel")),
        name="nsa_win_attn_gate",
    )(qkv, qkv, qkv, qkv, qkv, o_cmp, o_sel, gates)


def _merge_body(hn_ref, ya_ref, yb_ref, wga_ref, wgb_ref, wa_ref, wb_ref, o_ref):
    hn = hn_ref[...]
    ga = jnp.dot(hn, wga_ref[...], preferred_element_type=F32)
    gb = jnp.dot(hn, wgb_ref[...], preferred_element_type=F32)
    pa = jnp.dot(ya_ref[...], wa_ref[...], preferred_element_type=F32)
    pb = jnp.dot(yb_ref[...], wb_ref[...], preferred_element_type=F32)
    o_ref[...] = (jax.nn.sigmoid(ga) * pa + jax.nn.sigmoid(gb) * pb).astype(o_ref.dtype)


def _merge(hn, ya, yb, wg, wa, wb, tm=512, tn=512):
    m, d = hn.shape
    ka = ya.shape[1]
    tm = min(tm, m)
    row = lambda k: pl.BlockSpec((tm, k), lambda i, j: (i, 0))
    col = lambda k: pl.BlockSpec((k, tn), lambda i, j: (0, j))
    col_b = pl.BlockSpec((d, tn), lambda i, j: (0, d // tn + j))
    return pl.pallas_call(
        _merge_body,
        out_shape=jax.ShapeDtypeStruct((m, d), BF16),
        grid=(m // tm, d // tn),
        in_specs=[row(d), row(ka), row(ka), col(d), col_b, col(ka), col(ka)],
        out_specs=pl.BlockSpec((tm, tn), lambda i, j: (i, j)),
        compiler_params=_params(("parallel", "arbitrary")),
        name="mixer_merge",
    )(hn, ya, yb, wg, wg, wa, wb)


def _rms(x, g):
    return x * lax.rsqrt(jnp.mean(x * x, axis=-1, keepdims=True) + RMS_EPS) * g


def _res_norm2_body(raw_ref, x_ref, gpost_ref, gpre_ref, h_ref, hn_ref):
    h = x_ref[...] + _rms(raw_ref[...].astype(F32), gpost_ref[...])
    h_ref[...] = h
    hn_ref[...] = _rms(h, gpre_ref[...]).astype(hn_ref.dtype)


def _res_norm2(raw, x, g_post, g_pre, tm=256):
    m, d = x.shape
    tm = min(tm, m)
    rows = pl.BlockSpec((tm, d), lambda i: (i, 0))
    vec = pl.BlockSpec((1, d), lambda i: (0, 0))
    return pl.pallas_call(
        _res_norm2_body,
        out_shape=(jax.ShapeDtypeStruct((m, d), F32), jax.ShapeDtypeStruct((m, d), BF16)),
        grid=(m // tm,),
        in_specs=[rows, rows, vec, vec],
        out_specs=(rows, rows),
        compiler_params=_params(("parallel",)),
        name="residual_norm_prenorm",
    )(raw, x, g_post.reshape(1, d), g_pre.reshape(1, d))


def _res_norm_body(raw_ref, x_ref, g_ref, o_ref):
    o_ref[...] = x_ref[...] + _rms(raw_ref[...].astype(F32), g_ref[...])


def _res_norm(raw, x, g, tm=256):
    m, d = x.shape
    tm = min(tm, m)
    rows = pl.BlockSpec((tm, d), lambda i: (i, 0))
    return pl.pallas_call(
        _res_norm_body,
        out_shape=jax.ShapeDtypeStruct((m, d), F32),
        grid=(m // tm,),
        in_specs=[rows, rows, pl.BlockSpec((1, d), lambda i: (0, 0))],
        out_specs=rows,
        compiler_params=_params(("parallel",)),
        name="residual_norm",
    )(raw, x, g.reshape(1, d))


def _swiglu_body(a_ref, wg_ref, wu_ref, o_ref, wg_scr, wu_scr):
    @pl.when(pl.program_id(1) == 0)
    def _():
        wg_scr[...] = wg_ref[...].astype(BF16)
        wu_scr[...] = wu_ref[...].astype(BF16)

    a = a_ref[...]
    g = jnp.dot(a, wg_scr[...], preferred_element_type=F32)
    u = jnp.dot(a, wu_scr[...], preferred_element_type=F32)
    o_ref[...] = (g * jax.nn.sigmoid(g) * u).astype(o_ref.dtype)


def _swiglu(a, wg, wu, tm=2048, tn=256):
    m, k = a.shape
    n = wg.shape[1]
    tm = min(tm, m)
    assert n % tn == 0
    vmem = 2 * tm * k * 2 + 2 * 2 * k * tn * 4 + 2 * tm * tn * 2 + 2 * k * tn * 2
    assert vmem + V7X_VMEM_SLACK_BYTES <= V7X_VMEM_BYTES
    return pl.pallas_call(
        _swiglu_body,
        out_shape=jax.ShapeDtypeStruct((m, n), BF16),
        grid=(n // tn, m // tm),
        in_specs=[pl.BlockSpec((tm, k), lambda j, i: (i, 0)),
                  pl.BlockSpec((k, tn), lambda j, i: (0, j)),
                  pl.BlockSpec((k, tn), lambda j, i: (0, j))],
        out_specs=pl.BlockSpec((tm, tn), lambda j, i: (i, j)),
        scratch_shapes=[pltpu.VMEM((k, tn), BF16), pltpu.VMEM((k, tn), BF16)],
        compiler_params=_params(("arbitrary", "arbitrary"), vmem=vmem + V7X_VMEM_SLACK_BYTES),
        name="ffn_swiglu",
    )(a, wg, wu)


def kernel(x, norm_mix_pre, w_in, ssm_a_re, ssm_a_im, ssm_log_dt, ssm_b_re, ssm_b_im, ssm_c_re, ssm_c_im, ssm_d, ssm_w_glu, ssm_b_glu, cmp_pe_k, cmp_w1_k, cmp_w2_k, cmp_pe_v, cmp_w1_v, cmp_w2_v, w_proj_a, w_proj_b, w_out, norm_mix_post, norm_ffn_pre, w_ffn_gate, w_ffn_up, w_ffn_down, norm_ffn_post):
    bsz, seq, d = x.shape
    m = bsz * seq
    depth = w_in.shape[0]
    h = x.reshape(m, d)
    o_q, o_kvc, o_kv, o_gn = SSM_WIDTH, SSM_WIDTH + ATT_WIDTH, SSM_WIDTH + ATT_WIDTH + 2 * KV_WIDTH, 7168
    o_ga = o_gn + 3 * N_HEADS
    o_gb = o_ga + D_MODEL
    bf = lambda a: a.astype(BF16)
    tn_in = 512
    nb_u, nb_q = SSM_WIDTH // tn_in, ATT_WIDTH // tn_in
    qk_scale = HEAD_DIM ** -0.5 * LOG2E
    for l in range(depth):
        wt = jnp.swapaxes(w_in[l], 0, 1)
        w_g = _rows_transposed_cast(wt, o_ga, 2 * D_MODEL, BF16)

        hn, gn = _rmsnorm_gates(h, norm_mix_pre[l], wt, o_gn // LANES)
        proj_f32 = _matmul_f32w(hn, wt, SSM_WIDTH + 2 * KV_WIDTH, F32, 1024, tn_in, trans_w=True,
                                col_block=lambda j: jnp.where(j < nb_u, j, j - nb_u + o_kvc // tn_in),
                                name="in_proj_f32")
        qkv = _matmul_f32w(hn, wt, ATT_WIDTH + 4 * KV_WIDTH, BF16, 1024, tn_in, trans_w=True,
                           col_block=lambda j: jnp.where(j < nb_q, j + o_q // tn_in, j - nb_q + o_kv // tn_in),
                           scale_fn=lambda j: jnp.where(j < nb_q, qk_scale, 1.0),
                           name="in_proj_bf16")

        y = _s5_mixer(proj_f32.reshape(bsz, seq, -1), ssm_a_re[l], ssm_a_im[l], ssm_log_dt[l],
                      ssm_b_re[l], ssm_b_im[l], ssm_c_re[l], ssm_c_im[l], ssm_d[l])
        y_a = _glu(y.reshape(m, SSM_WIDTH), bf(ssm_w_glu[l]), ssm_b_glu[l])

        pe = jnp.stack([cmp_pe_k[l], cmp_pe_v[l]])
        w1 = bf(jnp.stack([cmp_w1_k[l], cmp_w1_v[l]]))
        w2 = bf(jnp.stack([cmp_w2_k[l], cmp_w2_v[l]]))
        kcv = _compress(proj_f32, SSM_WIDTH, pe, w1, w2, bsz, seq)
        o_cmp, bias_t = _cmp_attn(qkv, kcv, bsz, seq)
        nblk = seq // L_SEL
        bias = bf(jnp.pad(jnp.swapaxes(bias_t, 2, 3), ((0, 0), (0, 0), (0, 0), (0, LANES - nblk))))
        o_sel = _sel_attn(qkv, bias, bsz, seq)
        gates = gn[:, :3 * N_HEADS].reshape(bsz, seq, 3, N_KV, HPG).transpose(0, 3, 1, 2, 4)
        gates = jnp.pad(gates.reshape(bsz, N_KV, seq, 3 * HPG), ((0, 0), (0, 0), (0, 0), (0, LANES - 3 * HPG)))
        y_b = _win_attn_combine(qkv, o_cmp, o_sel, gates, bsz, seq)

        merged = _merge(hn, y_a, y_b, w_g, bf(w_proj_a[l]), bf(w_proj_b[l]))
        mix = _matmul_f32w(merged, w_out[l], D_MODEL, BF16, 1024, 512, name="out_proj")
        h, hn2 = _res_norm2(mix, h, norm_mix_post[l], norm_ffn_pre[l])

        act = _swiglu(hn2, w_ffn_gate[l], w_ffn_up[l])
        f = _matmul(act, bf(w_ffn_down[l]), BF16, 512, 512, name="ffn_down")
        h = _res_norm(f, h, norm_ffn_post[l])
    return h.reshape(bsz, seq, d)
```

```python
import functools
import math

import jax
import jax.numpy as jnp
from jax import lax
from jax.experimental import pallas as pl
from jax.experimental.pallas import tpu as pltpu

F32 = jnp.float32
BF16 = jnp.bfloat16

D_MODEL = 4096
SSM_WIDTH = 2048
SSM_GROUP = 16
SSM_GROUPS = 128
SSM_STATE = 64
N_HEADS = 16
HEAD_DIM = 128
N_KV = 4
HPG = 4
ATT_WIDTH = 2048
KV_WIDTH = 512
L_CMP = 32
STRIDE_CMP = 16
L_SEL = 64
N_SEL = 16
WINDOW = 512
D_FF = 11008
RMS_EPS = 1e-6
NEG_INF = -1e30
FORCE_SCORE = 1e9
TAKEN_SCORE = -3e38
LOG2E = math.log2(math.e)

V7X_VMEM_BYTES = 64 * 1024 * 1024
V7X_VMEM_LIMIT_BYTES = 56 * 1024 * 1024
V7X_VMEM_SLACK_BYTES = 4 * 1024 * 1024
LANES = 128
SUBLANES = 8

S5_GT = 16
S5_NC = S5_GT * SSM_GROUP
S5_NP = S5_GT * SSM_STATE
S5_TILES = SSM_GROUPS // S5_GT
S5_SLABS = 2 * S5_NP // LANES

ATT_ROWS = 128
SEL_TQ, SEL_RB = 512, 64


def _params(sem, vmem=V7X_VMEM_LIMIT_BYTES):
    return pltpu.CompilerParams(dimension_semantics=sem, vmem_limit_bytes=vmem)


def _lane_tile(x, n):
    return jnp.concatenate([x] * n, axis=1)


def _rmsnorm_gates_body(x_ref, g_ref, w_ref, hn_ref, gn_ref, wbf_scr):
    @pl.when(pl.program_id(0) == 0)
    def _():
        wbf_scr[...] = w_ref[...].astype(BF16)

    x = x_ref[...]
    ms = jnp.mean(x * x, axis=-1, keepdims=True)
    hn = (x * lax.rsqrt(ms + RMS_EPS) * g_ref[...]).astype(BF16)
    hn_ref[...] = hn
    gn_ref[...] = lax.dot_general(hn, wbf_scr[...], (((1,), (1,)), ((), ())), preferred_element_type=F32)


def _rmsnorm_gates(x, gain, wt, feature_block, tm=256):
    m, d = x.shape
    tm = min(tm, m)
    return pl.pallas_call(
        _rmsnorm_gates_body,
        out_shape=(jax.ShapeDtypeStruct((m, d), BF16), jax.ShapeDtypeStruct((m, LANES), F32)),
        grid=(m // tm,),
        in_specs=[pl.BlockSpec((tm, d), lambda i: (i, 0)),
                  pl.BlockSpec((1, d), lambda i: (0, 0)),
                  pl.BlockSpec((LANES, d), lambda i: (feature_block, 0))],
        out_specs=(pl.BlockSpec((tm, d), lambda i: (i, 0)), pl.BlockSpec((tm, LANES), lambda i: (i, 0))),
        scratch_shapes=[pltpu.VMEM((LANES, d), BF16)],
        compiler_params=_params(("arbitrary",)),
        name="rmsnorm_gates",
    )(x, gain.reshape(1, d), wt)


def _mm_body(a_ref, w_ref, o_ref):
    o_ref[...] = jnp.dot(a_ref[...], w_ref[...], preferred_element_type=F32).astype(o_ref.dtype)


def _matmul(a, w, out_dtype, tm, tn, name="matmul"):
    m, k = a.shape
    n = w.shape[1]
    tm, tn = min(tm, m), min(tn, n)
    assert m % tm == 0 and n % tn == 0
    return pl.pallas_call(
        _mm_body,
        out_shape=jax.ShapeDtypeStruct((m, n), out_dtype),
        grid=(m // tm, n // tn),
        in_specs=[pl.BlockSpec((tm, k), lambda i, j: (i, 0)), pl.BlockSpec((k, tn), lambda i, j: (0, j))],
        out_specs=pl.BlockSpec((tm, tn), lambda i, j: (i, j)),
        compiler_params=_params(("parallel", "arbitrary")),
        name=name,
    )(a, w)


_NT = (((1,), (1,)), ((), ()))


def _mm_f32w_body(a_ref, w_ref, o_ref, wbf_scr, *, scale_fn, trans_w):
    @pl.when(pl.program_id(1) == 0)
    def _():
        wbf_scr[...] = w_ref[...].astype(BF16)

    if trans_w:
        acc = lax.dot_general(a_ref[...], wbf_scr[...], _NT, preferred_element_type=F32)
    else:
        acc = jnp.dot(a_ref[...], wbf_scr[...], preferred_element_type=F32)
    if scale_fn is not None:
        acc = acc * scale_fn(pl.program_id(0))
    o_ref[...] = acc.astype(o_ref.dtype)


def _matmul_f32w(a, w, n_out, out_dtype, tm, tn, col_block=None, scale_fn=None, trans_w=False,
                 name="matmul_f32w"):
    m, k = a.shape
    tm = min(tm, m)
    assert m % tm == 0 and n_out % tn == 0
    col_block = col_block or (lambda j: j)
    if trans_w:
        w_spec = pl.BlockSpec((tn, k), lambda j, i: (col_block(j), 0))
        w_scratch = pltpu.VMEM((tn, k), BF16)
    else:
        w_spec = pl.BlockSpec((k, tn), lambda j, i: (0, col_block(j)))
        w_scratch = pltpu.VMEM((k, tn), BF16)
    vmem = 2 * tm * k * 2 + 2 * k * tn * 4 + 2 * tm * tn * jnp.dtype(out_dtype).itemsize + k * tn * 2
    assert vmem + V7X_VMEM_SLACK_BYTES <= V7X_VMEM_BYTES
    return pl.pallas_call(
        functools.partial(_mm_f32w_body, scale_fn=scale_fn, trans_w=trans_w),
        out_shape=jax.ShapeDtypeStruct((m, n_out), out_dtype),
        grid=(n_out // tn, m // tm),
        in_specs=[pl.BlockSpec((tm, k), lambda j, i: (i, 0)), w_spec],
        out_specs=pl.BlockSpec((tm, tn), lambda j, i: (i, j)),
        scratch_shapes=[w_scratch],
        compiler_params=_params(("arbitrary", "arbitrary"), vmem=vmem + V7X_VMEM_SLACK_BYTES),
        name=name,
    )(a, w)


def _rows_transposed_cast_body(w_ref, o_ref):
    o_ref[...] = w_ref[...].T.astype(o_ref.dtype)


def _rows_transposed_cast(wt, row0, n_rows, out_dtype, tr=512):
    k = wt.shape[1]
    assert row0 % (2 * SUBLANES) == 0 and n_rows % tr == 0
    return pl.pallas_call(
        _rows_transposed_cast_body,
        out_shape=jax.ShapeDtypeStruct((k, n_rows), out_dtype),
        grid=(n_rows // tr,),
        in_specs=[pl.BlockSpec((pl.Element(tr), pl.Element(k)),
                               lambda c: (pl.multiple_of(row0 + c * tr, 2 * SUBLANES), 0))],
        out_specs=pl.BlockSpec((k, tr), lambda c: (0, c)),
        compiler_params=_params(("parallel",)),
        name="gate_weight_cast",
    )(wt)


def _s5_body(u_ref, are_ref, aim_ref, ldt_ref, bre_ref, bim_ref, cre_ref, cim_ref, d_ref, y_ref,
             abar_scr, bbar_scr, cmat_scr, h_scr, st_scr, *, tc, bsz, tps):
    ti = pl.program_id(1)
    half = S5_SLABS // 2

    @pl.when(ti == 0)
    def _():
        row_tile = lax.broadcasted_iota(jnp.int32, (SUBLANES, S5_NP), 0) // bsz
        same_group = (lax.broadcasted_iota(jnp.int32, (S5_NC, S5_NP), 0) // SSM_GROUP
                      == lax.broadcasted_iota(jnp.int32, (S5_NC, S5_NP), 1) // SSM_STATE)
        bdiag = lambda x: jnp.where(same_group, jnp.concatenate([x] * S5_GT, axis=0), 0.0)
        abr8 = jnp.zeros((SUBLANES, S5_NP), F32)
        abi8 = jnp.zeros((SUBLANES, S5_NP), F32)
        for k in range(tps):
            cols = slice(k * S5_NP, (k + 1) * S5_NP)
            ar, ai = are_ref[:, cols], aim_ref[:, cols]
            dt = jnp.exp(ldt_ref[:, cols])
            decay = jnp.exp(dt * ar)
            abr, abi = decay * jnp.cos(dt * ai), decay * jnp.sin(dt * ai)
            den = ar * ar + ai * ai
            zr = ((abr - 1.0) * ar + abi * ai) / den
            zi = (abi * ar - (abr - 1.0) * ai) / den
            abr8 = jnp.where(row_tile == k, abr, abr8)
            abi8 = jnp.where(row_tile == k, abi, abi8)
            br, bi = bdiag(bre_ref[k]), bdiag(bim_ref[k])
            bbar_scr[k, :, 0:S5_NP] = (zr * br - zi * bi).astype(BF16)
            bbar_scr[k, :, S5_NP:2 * S5_NP] = (zr * bi + zi * br).astype(BF16)
            cmat_scr[k, :, 0:S5_NP] = bdiag(cre_ref[k]).astype(BF16)
            cmat_scr[k, :, S5_NP:2 * S5_NP] = (-bdiag(cim_ref[k])).astype(BF16)
        abar_scr[0:SUBLANES, :] = abr8
        abar_scr[SUBLANES:2 * SUBLANES, :] = abi8
        st_scr[...] = jnp.zeros_like(st_scr)

    for k in range(tps):
        cols = slice(k * S5_NC, (k + 1) * S5_NC)
        ub = jnp.concatenate([u_ref[b, :, cols] for b in range(bsz)], axis=0).astype(BF16)
        for s2 in range(S5_SLABS // 2):
            bu = jnp.dot(ub, bbar_scr[k, :, s2 * 2 * LANES:(s2 + 1) * 2 * LANES], preferred_element_type=F32)
            for b in range(bsz):
                for e in range(2):
                    h_scr[2 * s2 + e, pl.ds(k * bsz + b, tc, stride=SUBLANES), :] = bu[b * tc:(b + 1) * tc,
                                                                                      e * LANES:(e + 1) * LANES]

    ar = abar_scr[0:SUBLANES, :]
    ai = abar_scr[SUBLANES:2 * SUBLANES, :]

    def step(t, carry):
        hr, hi = carry
        r0 = pl.multiple_of(t * SUBLANES, SUBLANES)
        bur = jnp.concatenate([h_scr[s, pl.ds(r0, SUBLANES), :] for s in range(half)], axis=1)
        bui = jnp.concatenate([h_scr[half + s, pl.ds(r0, SUBLANES), :] for s in range(half)], axis=1)
        nhr = ar * hr - ai * hi + bur
        nhi = ar * hi + ai * hr + bui
        for s in range(half):
            h_scr[s, pl.ds(r0, SUBLANES), :] = nhr[:, s * LANES:(s + 1) * LANES]
            h_scr[half + s, pl.ds(r0, SUBLANES), :] = nhi[:, s * LANES:(s + 1) * LANES]
        return nhr, nhi

    hr, hi = lax.fori_loop(0, tc, step, (st_scr[0:SUBLANES, :], st_scr[SUBLANES:2 * SUBLANES, :]), unroll=2)
    st_scr[0:SUBLANES, :] = hr
    st_scr[SUBLANES:2 * SUBLANES, :] = hi

    for k in range(tps):
        cols = slice(k * S5_NC, (k + 1) * S5_NC)

        hb = jnp.concatenate(
            [jnp.concatenate([h_scr[s, pl.ds(k * bsz + b, tc, stride=SUBLANES), :] for s in range(S5_SLABS)], axis=1)
             for b in range(bsz)], axis=0).astype(BF16)
        ch = lax.dot_general(hb, cmat_scr[k], _NT, preferred_element_type=F32)
        for b in range(bsz):
            y = ch[b * tc:(b + 1) * tc, :] + d_ref[:, cols] * u_ref[b, :, cols]
            y_ref[b, :, cols] = jax.nn.gelu(y)


def _s5_mixer(u3, a_re, a_im, log_dt, b_re, b_im, c_re, c_im, d_skip, tc=256):
    bsz, seq, _ = u3.shape
    assert SUBLANES % bsz == 0
    tps = min(SUBLANES // bsz, S5_TILES)
    assert tps * bsz == SUBLANES
    tc = min(tc, seq)

    def per_tile_b(b):
        return (b.reshape(S5_TILES, S5_GT, SSM_STATE, SSM_GROUP).transpose(0, 3, 1, 2)
                .reshape(S5_TILES, SSM_GROUP, S5_NP))

    def per_tile_c(c):
        return (c.reshape(S5_TILES, S5_GT, SSM_GROUP, SSM_STATE).transpose(0, 2, 1, 3)
                .reshape(S5_TILES, SSM_GROUP, S5_NP))

    flat = lambda a: a.reshape(1, SSM_GROUPS * SSM_STATE)
    ldt = jnp.repeat(log_dt, SSM_STATE).reshape(1, SSM_GROUPS * SSM_STATE)
    vec_spec = pl.BlockSpec((1, tps * S5_NP), lambda g, t: (0, g))
    bc_spec = pl.BlockSpec((tps, SSM_GROUP, S5_NP), lambda g, t: (g, 0, 0))
    return pl.pallas_call(
        functools.partial(_s5_body, tc=tc, bsz=bsz, tps=tps),
        out_shape=jax.ShapeDtypeStruct((bsz, seq, SSM_WIDTH), F32),
        grid=(S5_TILES // tps, seq // tc),
        in_specs=[
            pl.BlockSpec((bsz, tc, tps * S5_NC), lambda g, t: (0, t, g)),
            vec_spec, vec_spec, vec_spec, bc_spec, bc_spec, bc_spec, bc_spec,
            pl.BlockSpec((1, tps * S5_NC), lambda g, t: (0, g)),
        ],
        out_specs=pl.BlockSpec((bsz, tc, tps * S5_NC), lambda g, t: (0, t, g)),
        scratch_shapes=[
            pltpu.VMEM((2 * SUBLANES, S5_NP), F32),
            pltpu.VMEM((tps, S5_NC, 2 * S5_NP), BF16),
            pltpu.VMEM((tps, S5_NC, 2 * S5_NP), BF16),
            pltpu.VMEM((S5_SLABS, tc * SUBLANES, LANES), F32),
            pltpu.VMEM((2 * SUBLANES, S5_NP), F32),
        ],
        compiler_params=_params(("arbitrary", "arbitrary")),
        name="s5_scan",
    )(u3, flat(a_re), flat(a_im), ldt, per_tile_b(b_re), per_tile_b(b_im), per_tile_c(c_re), per_tile_c(c_im),
      d_skip.reshape(1, SSM_WIDTH))


def _glu_body(y_ref, w_ref, b_ref, o_ref, ybf_scr, *, tn):
    j = pl.program_id(1)

    @pl.when(j == 0)
    def _():
        ybf_scr[...] = y_ref[...].astype(BF16)

    z = jnp.dot(ybf_scr[...], w_ref[...], preferred_element_type=F32) + b_ref[...]
    yt = y_ref[:, pl.ds(pl.multiple_of(j * tn, tn), tn)]
    o_ref[...] = (yt * jax.nn.sigmoid(z)).astype(o_ref.dtype)


def _glu(y, w_bf, bias, tm=1024, tn=1024):
    m, k = y.shape
    tm = min(tm, m)
    return pl.pallas_call(
        functools.partial(_glu_body, tn=tn),
        out_shape=jax.ShapeDtypeStruct((m, k), BF16),
        grid=(m // tm, k // tn),
        in_specs=[pl.BlockSpec((tm, k), lambda i, j: (i, 0)),
                  pl.BlockSpec((k, tn), lambda i, j: (0, j)),
                  pl.BlockSpec((1, tn), lambda i, j: (0, j))],
        out_specs=pl.BlockSpec((tm, tn), lambda i, j: (i, j)),
        scratch_shapes=[pltpu.VMEM((tm, k), BF16)],
        compiler_params=_params(("parallel", "arbitrary")),
        name="s5_glu",
    )(y, w_bf, bias.reshape(1, k))


def _compress_body(x_ref, pe_ref, w1_ref, w2_ref, o_ref, *, ncmp):
    half = L_CMP // 2
    acc_a = jnp.zeros((ncmp, HEAD_DIM), F32)
    acc_b = jnp.zeros((ncmp, HEAD_DIM), F32)
    for r in range(half):
        xr = x_ref[pl.ds(r, ncmp, stride=STRIDE_CMP), :]
        xa = (xr + pe_ref[r:r + 1, :]).astype(BF16)
        xb = (xr + pe_ref[half + r:half + r + 1, :]).astype(BF16)
        acc_a += jnp.dot(xa, w1_ref[r * HEAD_DIM:(r + 1) * HEAD_DIM, :], preferred_element_type=F32)
        acc_b += jnp.dot(xb, w1_ref[(half + r) * HEAD_DIM:(half + r + 1) * HEAD_DIM, :],
                         preferred_element_type=F32)
    pre = acc_a + jnp.concatenate([acc_b[1:], acc_b[:1]], axis=0)
    o_ref[...] = jnp.dot(jax.nn.gelu(pre).astype(BF16), w2_ref[...], preferred_element_type=F32).astype(o_ref.dtype)


def _compress(proj_f32, col0, pe, w1_bf, w2_bf, bsz, seq):
    ncmp = seq // STRIDE_CMP
    cb = col0 // HEAD_DIM
    return pl.pallas_call(
        functools.partial(_compress_body, ncmp=ncmp),
        out_shape=jax.ShapeDtypeStruct((bsz, 2, N_KV, ncmp, HEAD_DIM), BF16),
        grid=(bsz, 2, N_KV),
        in_specs=[pl.BlockSpec((seq, HEAD_DIM), lambda b, w, g: (b, cb + w * N_KV + g)),
                  pl.BlockSpec((None, L_CMP, HEAD_DIM), lambda b, w, g: (w, 0, 0)),
                  pl.BlockSpec((None, L_CMP * HEAD_DIM, HEAD_DIM), lambda b, w, g: (w, 0, 0)),
                  pl.BlockSpec((None, HEAD_DIM, HEAD_DIM), lambda b, w, g: (w, 0, 0))],
        out_specs=pl.BlockSpec((None, None, None, ncmp, HEAD_DIM), lambda b, w, g: (b, w, g, 0, 0)),
        compiler_params=_params(("parallel", "parallel", "parallel")),
        name="nsa_compress",
    )(proj_f32, pe, w1_bf, w2_bf)


def _cmp_attn_body(q_ref, kc_ref, vc_ref, o_ref, bias_ref, *, tq, ncmp, nblk, ntop):
    qi = pl.program_id(2)
    t0 = qi * tq
    tpos = t0 + lax.broadcasted_iota(jnp.int32, (tq, ncmp), 0)
    blk_end = lax.broadcasted_iota(jnp.int32, (tq, ncmp), 1) * STRIDE_CMP + (L_CMP - 1)
    ok = blk_end <= tpos
    okf = ok.astype(F32)
    kc = kc_ref[...]
    vc = vc_ref[...]
    psum = jnp.zeros((tq, ncmp), F32)
    for h in range(HPG):
        qh = q_ref[:, h * HEAD_DIM:(h + 1) * HEAD_DIM]
        s = lax.dot_general(qh, kc, (((1,), (1,)), ((), ())), preferred_element_type=F32)
        s = jnp.where(ok, s, NEG_INF)
        e = jnp.exp2(s - jnp.max(s, axis=-1, keepdims=True))
        p = e / jnp.sum(e, axis=-1, keepdims=True) * okf
        o_ref[:, h * HEAD_DIM:(h + 1) * HEAD_DIM] = jnp.dot(p.astype(BF16), vc, preferred_element_type=F32)
        psum = psum + p

    jj = lax.broadcasted_iota(jnp.int32, (nblk, ncmp), 0)
    nn = lax.broadcasted_iota(jnp.int32, (nblk, ncmp), 1)
    ov = ((nn * STRIDE_CMP < (jj + 1) * L_SEL) & (nn * STRIDE_CMP + L_CMP > jj * L_SEL)).astype(BF16)
    p_hi = psum.astype(BF16)
    p_lo = (psum - p_hi.astype(F32)).astype(BF16)
    nt = (((1,), (1,)), ((), ()))
    imp = (lax.dot_general(ov, p_hi, nt, preferred_element_type=F32)
           + lax.dot_general(ov, p_lo, nt, preferred_element_type=F32))

    jb = lax.broadcasted_iota(jnp.int32, (nblk, tq), 0)
    tt = t0 + lax.broadcasted_iota(jnp.int32, (nblk, tq), 1)
    cur = tt // L_SEL
    allowed = jb * L_SEL <= tt
    forced = (jb == 0) | (jb == cur) | (jb == cur - 1)
    score = jnp.where(forced, FORCE_SCORE, jnp.where(allowed, imp, NEG_INF))
    taken = jnp.zeros((nblk, tq), F32)
    for _ in range(ntop):
        best = jnp.max(score, axis=0, keepdims=True)
        first = jnp.min(jnp.where(score == best, jb, nblk), axis=0, keepdims=True)
        pick = jb == first
        taken = jnp.where(pick, 1.0, taken)
        score = jnp.where(pick, TAKEN_SCORE, score)
    bias_t = jnp.where(taken > 0.5, 0.0, NEG_INF)
    bias_t = jnp.concatenate([bias_t, jnp.zeros((LANES - nblk, tq), F32)], axis=0)
    bias_ref[...] = bias_t.T.astype(bias_ref.dtype)


def _cmp_attn(qkv, kcv, bsz, seq, tq=512):
    tq = min(tq, seq)
    ncmp = seq // STRIDE_CMP
    nblk = seq // L_SEL
    ntop = min(N_SEL, nblk)
    nq = seq // tq
    return pl.pallas_call(
        functools.partial(_cmp_attn_body, tq=tq, ncmp=ncmp, nblk=nblk, ntop=ntop),
        out_shape=(jax.ShapeDtypeStruct((bsz * seq, ATT_WIDTH), F32),
                   jax.ShapeDtypeStruct((bsz, N_KV, seq, LANES), BF16)),
        grid=(bsz, N_KV, nq),
        in_specs=[pl.BlockSpec((tq, HPG * HEAD_DIM), lambda b, g, i: (b * nq + i, g)),
                  pl.BlockSpec((None, None, None, ncmp, HEAD_DIM), lambda b, g, i: (b, 0, g, 0, 0)),
                  pl.BlockSpec((None, None, None, ncmp, HEAD_DIM), lambda b, g, i: (b, 1, g, 0, 0))],
        out_specs=(pl.BlockSpec((tq, HPG * HEAD_DIM), lambda b, g, i: (b * nq + i, g)),
                   pl.BlockSpec((None, None, tq, LANES), lambda b, g, i: (b, g, i, 0))),
        compiler_params=_params(("parallel", "parallel", "parallel")),
        name="nsa_cmp_attn_topk",
    )(qkv, kcv, kcv)


def _sel_attn_body(q_ref, k_ref, v_ref, bias_ref, o_ref, qa_scr, m_scr, acc_scr, s_scr, p_scr, alpha_scr,
                   *, tq, rb):
    qi = pl.program_id(2)
    tk = tq
    m_scr[...] = jnp.full_like(m_scr, NEG_INF)
    acc_scr[...] = jnp.zeros_like(acc_scr)
    for h in range(HPG):
        qa_scr[h, :, 0:HEAD_DIM] = q_ref[:, h * HEAD_DIM:(h + 1) * HEAD_DIM]
        qa_scr[h, :, HEAD_DIM:2 * HEAD_DIM] = bias_ref[...]
    nt = (((1,), (1,)), ((), ()))

    def key_tile(kt):
        k0 = pl.multiple_of(kt * tk, tk)
        key_blk = kt * (tk // L_SEL) + lax.broadcasted_iota(jnp.int32, (tk, LANES), 0) // L_SEL
        onehot = jnp.where(lax.broadcasted_iota(jnp.int32, (tk, LANES), 1) == key_blk, 1.0, 0.0).astype(BF16)
        return jnp.concatenate([k_ref[pl.ds(k0, tk), :], onehot], axis=1)

    def scores(h, k_aug, r0, r1, nk):
        s_scr[h, r0:r1, 0:nk] = lax.dot_general(qa_scr[h, r0:r1, :], k_aug[0:nk, :], nt,
                                                preferred_element_type=F32)

    def softmax(h, r0, r1, nk, diagonal):
        for r in range(r0 // rb, r1 // rb):
            rows = slice(r * rb, (r + 1) * rb)
            s = s_scr[h, rows, 0:nk]
            if diagonal:
                col = lax.broadcasted_iota(jnp.int32, (rb, nk), 1)
                row = lax.broadcasted_iota(jnp.int32, (rb, nk), 0) + r * rb
                s = jnp.where(col <= row, s, NEG_INF)
            m_prev = m_scr[h, rows, :]
            m_next = jnp.maximum(m_prev, jnp.max(s, axis=-1, keepdims=True))
            p_scr[h, rows, 0:nk] = jnp.exp2(s - _lane_tile(m_next, nk // LANES)).astype(BF16)
            alpha_scr[h, rows, :] = jnp.exp2(m_prev - m_next)
            m_scr[h, rows, :] = m_next

    def values(h, v_aug, r0, r1, nk):
        acc_scr[h, r0:r1, :] = (_lane_tile(alpha_scr[h, r0:r1, :], 2) * acc_scr[h, r0:r1, :]
                                + jnp.dot(p_scr[h, r0:r1, 0:nk], v_aug[0:nk, :], preferred_element_type=F32))

    def tile(kt, diagonal):
        k0 = pl.multiple_of(kt * tk, tk)
        k_aug = key_tile(kt)
        v_aug = jnp.concatenate([v_ref[pl.ds(k0, tk), :], jnp.ones((tk, LANES), BF16)], axis=1)
        parts = [(0, tq // 2, tk // 2), (tq // 2, tq, tk)] if diagonal else [(0, tq, tk)]
        for h in range(HPG):
            for r0, r1, nk in parts:
                scores(h, k_aug, r0, r1, nk)
        for h in range(HPG):
            for r0, r1, nk in parts:
                softmax(h, r0, r1, nk, diagonal)
        for h in range(HPG):
            for r0, r1, nk in parts:
                values(h, v_aug, r0, r1, nk)

    def full_tile(kt, carry):
        tile(kt, False)
        return carry

    lax.fori_loop(0, qi, full_tile, 0)
    tile(qi, True)

    for h in range(HPG):
        o_ref[:, h * HEAD_DIM:(h + 1) * HEAD_DIM] = (acc_scr[h, :, 0:HEAD_DIM]
                                                     / acc_scr[h, :, HEAD_DIM:2 * HEAD_DIM])


def _sel_attn(qkv, bias, bsz, seq, tq=SEL_TQ, rb=SEL_RB):
    tq = min(tq, seq)
    rb = min(rb, tq)
    nq = seq // tq
    kcol = ATT_WIDTH // HEAD_DIM
    return pl.pallas_call(
        functools.partial(_sel_attn_body, tq=tq, rb=rb),
        out_shape=jax.ShapeDtypeStruct((bsz * seq, ATT_WIDTH), F32),
        grid=(bsz, N_KV, nq),
        in_specs=[pl.BlockSpec((tq, HPG * HEAD_DIM), lambda b, g, i: (b * nq + i, g)),
                  pl.BlockSpec((seq, HEAD_DIM), lambda b, g, i: (b, kcol + g)),
                  pl.BlockSpec((seq, HEAD_DIM), lambda b, g, i: (b, kcol + N_KV + g)),
                  pl.BlockSpec((None, None, tq, LANES), lambda b, g, i: (b, g, i, 0))],
        out_specs=pl.BlockSpec((tq, HPG * HEAD_DIM), lambda b, g, i: (b * nq + i, g)),
        scratch_shapes=[pltpu.VMEM((HPG, tq, 2 * HEAD_DIM), BF16),
                        pltpu.VMEM((HPG, tq, LANES), F32),
                        pltpu.VMEM((HPG, tq, 2 * HEAD_DIM), F32),
                        pltpu.VMEM((HPG, tq, tq), F32),
                        pltpu.VMEM((HPG, tq, tq), BF16),
                        pltpu.VMEM((HPG, tq, LANES), F32)],
        compiler_params=_params(("parallel", "parallel", "arbitrary")),
        name="nsa_sel_attn",
    )(qkv, qkv, qkv, bias)


def _win_attn_body(q_ref, kc_ref, kp_ref, vc_ref, vp_ref, ocmp_ref, osel_ref, gate_ref, o_ref, *, tq):
    qi = pl.program_id(2)
    nt = (((1,), (1,)), ((), ()))
    ones = jnp.ones((tq, LANES), BF16)
    vc_aug = jnp.concatenate([vc_ref[...], ones], axis=1)
    vp_aug = jnp.concatenate([vp_ref[...], ones], axis=1)
    for r in range(tq // ATT_ROWS):
        rows = slice(r * ATT_ROWS, (r + 1) * ATT_ROWS)
        n_cur, p_lo = (r + 1) * ATT_ROWS, r * ATT_ROWS
        row_c = lax.broadcasted_iota(jnp.int32, (ATT_ROWS, n_cur), 0) + r * ATT_ROWS
        ok_cur = lax.broadcasted_iota(jnp.int32, (ATT_ROWS, n_cur), 1) <= row_c
        row_p = lax.broadcasted_iota(jnp.int32, (ATT_ROWS, tq - p_lo), 0) + r * ATT_ROWS
        ok_prev = (lax.broadcasted_iota(jnp.int32, (ATT_ROWS, tq - p_lo), 1) + p_lo > row_p) & (qi > 0)
        gates = jax.nn.sigmoid(gate_ref[rows, :])
        for h in range(HPG):
            sl = slice(h * HEAD_DIM, (h + 1) * HEAD_DIM)
            qh = q_ref[rows, sl]
            s1 = jnp.where(ok_cur, lax.dot_general(qh, kc_ref[0:n_cur, :], nt, preferred_element_type=F32), NEG_INF)
            s0 = jnp.where(ok_prev, lax.dot_general(qh, kp_ref[p_lo:tq, :], nt, preferred_element_type=F32), NEG_INF)
            m = jnp.maximum(jnp.max(s1, axis=-1, keepdims=True), jnp.max(s0, axis=-1, keepdims=True))
            p1 = jnp.exp2(s1 - m).astype(BF16)
            p0 = jnp.exp2(s0 - m).astype(BF16)
            acc = (jnp.dot(p1, vc_aug[0:n_cur, :], preferred_element_type=F32)
                   + jnp.dot(p0, vp_aug[p_lo:tq, :], preferred_element_type=F32))
            ow = acc[:, 0:HEAD_DIM] / acc[:, HEAD_DIM:2 * HEAD_DIM]
            g_cmp = gates[:, h:h + 1]
            g_sel = gates[:, HPG + h:HPG + h + 1]
            g_win = gates[:, 2 * HPG + h:2 * HPG + h + 1]
            o_ref[rows, sl] = (g_cmp * ocmp_ref[rows, sl] + g_sel * osel_ref[rows, sl]
                               + g_win * ow).astype(o_ref.dtype)


def _win_attn_combine(qkv, o_cmp, o_sel, gates, bsz, seq):
    tq = WINDOW
    assert seq % tq == 0
    nq = seq // tq
    kcol = ATT_WIDTH // HEAD_DIM + 2 * N_KV
    qspec = pl.BlockSpec((tq, HPG * HEAD_DIM), lambda b, g, i: (b * nq + i, g))
    cur = lambda c: pl.BlockSpec((tq, HEAD_DIM), lambda b, g, i: (b * nq + i, kcol + c * N_KV + g))
    prev = lambda c: pl.BlockSpec((tq, HEAD_DIM),
                                  lambda b, g, i: (b * nq + jnp.maximum(i - 1, 0), kcol + c * N_KV + g))
    return pl.pallas_call(
        functools.partial(_win_attn_body, tq=tq),
        out_shape=jax.ShapeDtypeStruct((bsz * seq, ATT_WIDTH), BF16),
        grid=(bsz, N_KV, nq),
        in_specs=[qspec, cur(0), prev(0), cur(1), prev(1), qspec, qspec,
                  pl.BlockSpec((None, None, tq, LANES), lambda b, g, i: (b, g, i, 0))],
        out_specs=qspec,
        compiler_params=_params(("parallel", "parallel", "parallel")),
        name="nsa_win_attn_gate",
    )(qkv, qkv, qkv, qkv, qkv, o_cmp, o_sel, gates)


def _merge_body(hn_ref, ya_ref, yb_ref, wga_ref, wgb_ref, wa_ref, wb_ref, o_ref):
    hn = hn_ref[...]
    ga = jnp.dot(hn, wga_ref[...], preferred_element_type=F32)
    gb = jnp.dot(hn, wgb_ref[...], preferred_element_type=F32)
    pa = jnp.dot(ya_ref[...], wa_ref[...], preferred_element_type=F32)
    pb = jnp.dot(yb_ref[...], wb_ref[...], preferred_element_type=F32)
    o_ref[...] = (jax.nn.sigmoid(ga) * pa + jax.nn.sigmoid(gb) * pb).astype(o_ref.dtype)


def _merge(hn, ya, yb, wg, wa, wb, tm=512, tn=512):
    m, d = hn.shape
    ka = ya.shape[1]
    tm = min(tm, m)
    row = lambda k: pl.BlockSpec((tm, k), lambda i, j: (i, 0))
    col = lambda k: pl.BlockSpec((k, tn), lambda i, j: (0, j))
    col_b = pl.BlockSpec((d, tn), lambda i, j: (0, d // tn + j))
    return pl.pallas_call(
        _merge_body,
        out_shape=jax.ShapeDtypeStruct((m, d), BF16),
        grid=(m // tm, d // tn),
        in_specs=[row(d), row(ka), row(ka), col(d), col_b, col(ka), col(ka)],
        out_specs=pl.BlockSpec((tm, tn), lambda i, j: (i, j)),
        compiler_params=_params(("parallel", "arbitrary")),
        name="mixer_merge",
    )(hn, ya, yb, wg, wg, wa, wb)


def _rms(x, g):
    return x * lax.rsqrt(jnp.mean(x * x, axis=-1, keepdims=True) + RMS_EPS) * g


def _res_norm2_body(raw_ref, x_ref, gpost_ref, gpre_ref, h_ref, hn_ref):
    h = x_ref[...] + _rms(raw_ref[...].astype(F32), gpost_ref[...])
    h_ref[...] = h
    hn_ref[...] = _rms(h, gpre_ref[...]).astype(hn_ref.dtype)


def _res_norm2(raw, x, g_post, g_pre, tm=256):
    m, d = x.shape
    tm = min(tm, m)
    rows = pl.BlockSpec((tm, d), lambda i: (i, 0))
    vec = pl.BlockSpec((1, d), lambda i: (0, 0))
    return pl.pallas_call(
        _res_norm2_body,
        out_shape=(jax.ShapeDtypeStruct((m, d), F32), jax.ShapeDtypeStruct((m, d), BF16)),
        grid=(m // tm,),
        in_specs=[rows, rows, vec, vec],
        out_specs=(rows, rows),
        compiler_params=_params(("parallel",)),
        name="residual_norm_prenorm",
    )(raw, x, g_post.reshape(1, d), g_pre.reshape(1, d))


def _res_norm_body(raw_ref, x_ref, g_ref, o_ref):
    o_ref[...] = x_ref[...] + _rms(raw_ref[...].astype(F32), g_ref[...])


def _res_norm(raw, x, g, tm=256):
    m, d = x.shape
    tm = min(tm, m)
    rows = pl.BlockSpec((tm, d), lambda i: (i, 0))
    return pl.pallas_call(
        _res_norm_body,
        out_shape=jax.ShapeDtypeStruct((m, d), F32),
        grid=(m // tm,),
        in_specs=[rows, rows, pl.BlockSpec((1, d), lambda i: (0, 0))],
        out_specs=rows,
        compiler_params=_params(("parallel",)),
        name="residual_norm",
    )(raw, x, g.reshape(1, d))


def _swiglu_body(a_ref, wg_ref, wu_ref, o_ref, wg_scr, wu_scr):
    @pl.when(pl.program_id(1) == 0)
    def _():
        wg_scr[...] = wg_ref[...].astype(BF16)
        wu_scr[...] = wu_ref[...].astype(BF16)

    a = a_ref[...]
    g = jnp.dot(a, wg_scr[...], preferred_element_type=F32)
    u = jnp.dot(a, wu_scr[...], preferred_element_type=F32)
    o_ref[...] = (g * jax.nn.sigmoid(g) * u).astype(o_ref.dtype)


def _swiglu(a, wg, wu, tm=2048, tn=256):
    m, k = a.shape
    n = wg.shape[1]
    tm = min(tm, m)
    assert n % tn == 0
    vmem = 2 * tm * k * 2 + 2 * 2 * k * tn * 4 + 2 * tm * tn * 2 + 2 * k * tn * 2
    assert vmem + V7X_VMEM_SLACK_BYTES <= V7X_VMEM_BYTES
    return pl.pallas_call(
        _swiglu_body,
        out_shape=jax.ShapeDtypeStruct((m, n), BF16),
        grid=(n // tn, m // tm),
        in_specs=[pl.BlockSpec((tm, k), lambda j, i: (i, 0)),
                  pl.BlockSpec((k, tn), lambda j, i: (0, j)),
                  pl.BlockSpec((k, tn), lambda j, i: (0, j))],
        out_specs=pl.BlockSpec((tm, tn), lambda j, i: (i, j)),
        scratch_shapes=[pltpu.VMEM((k, tn), BF16), pltpu.VMEM((k, tn), BF16)],
        compiler_params=_params(("arbitrary", "arbitrary"), vmem=vmem + V7X_VMEM_SLACK_BYTES),
        name="ffn_swiglu",
    )(a, wg, wu)


def kernel(x, norm_mix_pre, w_in, ssm_a_re, ssm_a_im, ssm_log_dt, ssm_b_re, ssm_b_im, ssm_c_re, ssm_c_im, ssm_d, ssm_w_glu, ssm_b_glu, cmp_pe_k, cmp_w1_k, cmp_w2_k, cmp_pe_v, cmp_w1_v, cmp_w2_v, w_proj_a, w_proj_b, w_out, norm_mix_post, norm_ffn_pre, w_ffn_gate, w_ffn_up, w_ffn_down, norm_ffn_post):
    bsz, seq, d = x.shape
    m = bsz * seq
    depth = w_in.shape[0]
    h = x.reshape(m, d)
    o_q, o_kvc, o_kv, o_gn = SSM_WIDTH, SSM_WIDTH + ATT_WIDTH, SSM_WIDTH + ATT_WIDTH + 2 * KV_WIDTH, 7168
    o_ga = o_gn + 3 * N_HEADS
    o_gb = o_ga + D_MODEL
    bf = lambda a: a.astype(BF16)
    tn_in = 512
    nb_u, nb_q = SSM_WIDTH // tn_in, ATT_WIDTH // tn_in
    qk_scale = HEAD_DIM ** -0.5 * LOG2E
    for l in range(depth):
        wt = jnp.swapaxes(w_in[l], 0, 1)
        w_g = _rows_transposed_cast(wt, o_ga, 2 * D_MODEL, BF16)

        hn, gn = _rmsnorm_gates(h, norm_mix_pre[l], wt, o_gn // LANES)
        proj_f32 = _matmul_f32w(hn, wt, SSM_WIDTH + 2 * KV_WIDTH, F32, 1024, tn_in, trans_w=True,
                                col_block=lambda j: jnp.where(j < nb_u, j, j - nb_u + o_kvc // tn_in),
                                name="in_proj_f32")
        qkv = _matmul_f32w(hn, wt, ATT_WIDTH + 4 * KV_WIDTH, BF16, 2048, tn_in, trans_w=True,
                           col_block=lambda j: jnp.where(j < nb_q, j + o_q // tn_in, j - nb_q + o_kv // tn_in),
                           scale_fn=lambda j: jnp.where(j < nb_q, qk_scale, 1.0),
                           name="in_proj_bf16")

        y = _s5_mixer(proj_f32.reshape(bsz, seq, -1), ssm_a_re[l], ssm_a_im[l], ssm_log_dt[l],
                      ssm_b_re[l], ssm_b_im[l], ssm_c_re[l], ssm_c_im[l], ssm_d[l])
        y_a = _glu(y.reshape(m, SSM_WIDTH), bf(ssm_w_glu[l]), ssm_b_glu[l])

        pe = jnp.stack([cmp_pe_k[l], cmp_pe_v[l]])
        w1 = bf(jnp.stack([cmp_w1_k[l], cmp_w1_v[l]]))
        w2 = bf(jnp.stack([cmp_w2_k[l], cmp_w2_v[l]]))
        kcv = _compress(proj_f32, SSM_WIDTH, pe, w1, w2, bsz, seq)
        o_cmp, bias = _cmp_attn(qkv, kcv, bsz, seq)
        o_sel = _sel_attn(qkv, bias, bsz, seq)
        gates = gn[:, :3 * N_HEADS].reshape(bsz, seq, 3, N_KV, HPG).transpose(0, 3, 1, 2, 4)
        gates = jnp.pad(gates.reshape(bsz, N_KV, seq, 3 * HPG), ((0, 0), (0, 0), (0, 0), (0, LANES - 3 * HPG)))
        y_b = _win_attn_combine(qkv, o_cmp, o_sel, gates, bsz, seq)

        merged = _merge(hn, y_a, y_b, w_g, bf(w_proj_a[l]), bf(w_proj_b[l]))
        mix = _matmul_f32w(merged, w_out[l], D_MODEL, BF16, 2048, 512, name="out_proj")
        h, hn2 = _res_norm2(mix, h, norm_mix_post[l], norm_ffn_pre[l])

        act = _swiglu(hn2, w_ffn_gate[l], w_ffn_up[l])
        f = _matmul(act, bf(w_ffn_down[l]), BF16, 512, 512, name="ffn_down")
        h = _res_norm(f, h, norm_ffn_post[l])
    return h.reshape(bsz, seq, d)
```

```python
import functools
import math

import jax
import jax.numpy as jnp
from jax import lax
from jax.experimental import pallas as pl
from jax.experimental.pallas import tpu as pltpu

F32 = jnp.float32
BF16 = jnp.bfloat16

D_MODEL = 4096
SSM_WIDTH = 2048
SSM_GROUP = 16
SSM_GROUPS = 128
SSM_STATE = 64
N_HEADS = 16
HEAD_DIM = 128
N_KV = 4
HPG = 4
ATT_WIDTH = 2048
KV_WIDTH = 512
L_CMP = 32
STRIDE_CMP = 16
L_SEL = 64
N_SEL = 16
WINDOW = 512
D_FF = 11008
RMS_EPS = 1e-6
NEG_INF = -1e30
FORCE_SCORE = 1e9
TAKEN_SCORE = -3e38
LOG2E = math.log2(math.e)

V7X_VMEM_BYTES = 64 * 1024 * 1024
V7X_VMEM_LIMIT_BYTES = 56 * 1024 * 1024
V7X_VMEM_SLACK_BYTES = 4 * 1024 * 1024
LANES = 128
SUBLANES = 8

S5_GT = 16
S5_NC = S5_GT * SSM_GROUP
S5_NP = S5_GT * SSM_STATE
S5_TILES = SSM_GROUPS // S5_GT
S5_SLABS = 2 * S5_NP // LANES

ATT_ROWS = 256
SEL_TQ, SEL_RB = 512, 64
SEL_GROUP = 4


def _params(sem, vmem=V7X_VMEM_LIMIT_BYTES):
    return pltpu.CompilerParams(dimension_semantics=sem, vmem_limit_bytes=vmem)


def _lane_tile(x, n):
    return jnp.concatenate([x] * n, axis=1)


def _rmsnorm_gates_body(x_ref, g_ref, w_ref, hn_ref, gn_ref, wbf_scr):
    @pl.when(pl.program_id(0) == 0)
    def _():
        wbf_scr[...] = w_ref[...].astype(BF16)

    x = x_ref[...]
    ms = jnp.mean(x * x, axis=-1, keepdims=True)
    hn = (x * lax.rsqrt(ms + RMS_EPS) * g_ref[...]).astype(BF16)
    hn_ref[...] = hn
    gn_ref[...] = lax.dot_general(hn, wbf_scr[...], (((1,), (1,)), ((), ())), preferred_element_type=F32)


def _rmsnorm_gates(x, gain, wt, feature_block, tm=256):
    m, d = x.shape
    tm = min(tm, m)
    return pl.pallas_call(
        _rmsnorm_gates_body,
        out_shape=(jax.ShapeDtypeStruct((m, d), BF16), jax.ShapeDtypeStruct((m, LANES), F32)),
        grid=(m // tm,),
        in_specs=[pl.BlockSpec((tm, d), lambda i: (i, 0)),
                  pl.BlockSpec((1, d), lambda i: (0, 0)),
                  pl.BlockSpec((LANES, d), lambda i: (feature_block, 0))],
        out_specs=(pl.BlockSpec((tm, d), lambda i: (i, 0)), pl.BlockSpec((tm, LANES), lambda i: (i, 0))),
        scratch_shapes=[pltpu.VMEM((LANES, d), BF16)],
        compiler_params=_params(("arbitrary",)),
        name="rmsnorm_gates",
    )(x, gain.reshape(1, d), wt)


def _mm_body(a_ref, w_ref, o_ref):
    o_ref[...] = jnp.dot(a_ref[...], w_ref[...], preferred_element_type=F32).astype(o_ref.dtype)


def _matmul(a, w, out_dtype, tm, tn, name="matmul"):
    m, k = a.shape
    n = w.shape[1]
    tm, tn = min(tm, m), min(tn, n)
    assert m % tm == 0 and n % tn == 0
    return pl.pallas_call(
        _mm_body,
        out_shape=jax.ShapeDtypeStruct((m, n), out_dtype),
        grid=(m // tm, n // tn),
        in_specs=[pl.BlockSpec((tm, k), lambda i, j: (i, 0)), pl.BlockSpec((k, tn), lambda i, j: (0, j))],
        out_specs=pl.BlockSpec((tm, tn), lambda i, j: (i, j)),
        compiler_params=_params(("parallel", "arbitrary")),
        name=name,
    )(a, w)


_NT = (((1,), (1,)), ((), ()))


def _mm_f32w_body(a_ref, w_ref, o_ref, wbf_scr, *, scale_fn, trans_w):
    @pl.when(pl.program_id(1) == 0)
    def _():
        wbf_scr[...] = w_ref[...].astype(BF16)

    if trans_w:
        acc = lax.dot_general(a_ref[...], wbf_scr[...], _NT, preferred_element_type=F32)
    else:
        acc = jnp.dot(a_ref[...], wbf_scr[...], preferred_element_type=F32)
    if scale_fn is not None:
        acc = acc * scale_fn(pl.program_id(0))
    o_ref[...] = acc.astype(o_ref.dtype)


def _matmul_f32w(a, w, n_out, out_dtype, tm, tn, col_block=None, scale_fn=None, trans_w=False,
                 name="matmul_f32w"):
    m, k = a.shape
    tm = min(tm, m)
    assert m % tm == 0 and n_out % tn == 0
    col_block = col_block or (lambda j: j)
    if trans_w:
        w_spec = pl.BlockSpec((tn, k), lambda j, i: (col_block(j), 0))
        w_scratch = pltpu.VMEM((tn, k), BF16)
    else:
        w_spec = pl.BlockSpec((k, tn), lambda j, i: (0, col_block(j)))
        w_scratch = pltpu.VMEM((k, tn), BF16)
    vmem = 2 * tm * k * 2 + 2 * k * tn * 4 + 2 * tm * tn * jnp.dtype(out_dtype).itemsize + k * tn * 2
    assert vmem + V7X_VMEM_SLACK_BYTES <= V7X_VMEM_BYTES
    return pl.pallas_call(
        functools.partial(_mm_f32w_body, scale_fn=scale_fn, trans_w=trans_w),
        out_shape=jax.ShapeDtypeStruct((m, n_out), out_dtype),
        grid=(n_out // tn, m // tm),
        in_specs=[pl.BlockSpec((tm, k), lambda j, i: (i, 0)), w_spec],
        out_specs=pl.BlockSpec((tm, tn), lambda j, i: (i, j)),
        scratch_shapes=[w_scratch],
        compiler_params=_params(("arbitrary", "arbitrary"), vmem=vmem + V7X_VMEM_SLACK_BYTES),
        name=name,
    )(a, w)


def _rows_transposed_cast_body(w_ref, o_ref):
    o_ref[...] = w_ref[...].T.astype(o_ref.dtype)


def _rows_transposed_cast(wt, row0, n_rows, out_dtype, tr=512):
    k = wt.shape[1]
    assert row0 % (2 * SUBLANES) == 0 and n_rows % tr == 0
    return pl.pallas_call(
        _rows_transposed_cast_body,
        out_shape=jax.ShapeDtypeStruct((k, n_rows), out_dtype),
        grid=(n_rows // tr,),
        in_specs=[pl.BlockSpec((pl.Element(tr), pl.Element(k)),
                               lambda c: (pl.multiple_of(row0 + c * tr, 2 * SUBLANES), 0))],
        out_specs=pl.BlockSpec((k, tr), lambda c: (0, c)),
        compiler_params=_params(("parallel",)),
        name="gate_weight_cast",
    )(wt)


def _s5_body(u_ref, are_ref, aim_ref, ldt_ref, bre_ref, bim_ref, cre_ref, cim_ref, d_ref, y_ref,
             abar_scr, bbar_scr, cmat_scr, h_scr, st_scr, *, tc, bsz, tps):
    ti = pl.program_id(1)
    half = S5_SLABS // 2

    @pl.when(ti == 0)
    def _():
        row_tile = lax.broadcasted_iota(jnp.int32, (SUBLANES, S5_NP), 0) // bsz
        same_group = (lax.broadcasted_iota(jnp.int32, (S5_NC, S5_NP), 0) // SSM_GROUP
                      == lax.broadcasted_iota(jnp.int32, (S5_NC, S5_NP), 1) // SSM_STATE)
        bdiag = lambda x: jnp.where(same_group, jnp.concatenate([x] * S5_GT, axis=0), 0.0)
        abr8 = jnp.zeros((SUBLANES, S5_NP), F32)
        abi8 = jnp.zeros((SUBLANES, S5_NP), F32)
        for k in range(tps):
            cols = slice(k * S5_NP, (k + 1) * S5_NP)
            ar, ai = are_ref[:, cols], aim_ref[:, cols]
            dt = jnp.exp(ldt_ref[:, cols])
            decay = jnp.exp(dt * ar)
            abr, abi = decay * jnp.cos(dt * ai), decay * jnp.sin(dt * ai)
            den = ar * ar + ai * ai
            zr = ((abr - 1.0) * ar + abi * ai) / den
            zi = (abi * ar - (abr - 1.0) * ai) / den
            abr8 = jnp.where(row_tile == k, abr, abr8)
            abi8 = jnp.where(row_tile == k, abi, abi8)
            br, bi = bdiag(bre_ref[k]), bdiag(bim_ref[k])
            bbar_scr[k, :, 0:S5_NP] = (zr * br - zi * bi).astype(BF16)
            bbar_scr[k, :, S5_NP:2 * S5_NP] = (zr * bi + zi * br).astype(BF16)
            cmat_scr[k, :, 0:S5_NP] = bdiag(cre_ref[k]).astype(BF16)
            cmat_scr[k, :, S5_NP:2 * S5_NP] = (-bdiag(cim_ref[k])).astype(BF16)
        abar_scr[0:SUBLANES, :] = abr8
        abar_scr[SUBLANES:2 * SUBLANES, :] = abi8
        st_scr[...] = jnp.zeros_like(st_scr)

    for k in range(tps):
        cols = slice(k * S5_NC, (k + 1) * S5_NC)
        ub = jnp.concatenate([u_ref[b, :, cols] for b in range(bsz)], axis=0).astype(BF16)
        for s2 in range(S5_SLABS // 2):
            bu = jnp.dot(ub, bbar_scr[k, :, s2 * 2 * LANES:(s2 + 1) * 2 * LANES], preferred_element_type=F32)
            for b in range(bsz):
                for e in range(2):
                    h_scr[2 * s2 + e, pl.ds(k * bsz + b, tc, stride=SUBLANES), :] = bu[b * tc:(b + 1) * tc,
                                                                                      e * LANES:(e + 1) * LANES]

    ar = abar_scr[0:SUBLANES, :]
    ai = abar_scr[SUBLANES:2 * SUBLANES, :]

    def step(t, carry):
        hr, hi = carry
        r0 = pl.multiple_of(t * SUBLANES, SUBLANES)
        bur = jnp.concatenate([h_scr[s, pl.ds(r0, SUBLANES), :] for s in range(half)], axis=1)
        bui = jnp.concatenate([h_scr[half + s, pl.ds(r0, SUBLANES), :] for s in range(half)], axis=1)
        nhr = ar * hr - ai * hi + bur
        nhi = ar * hi + ai * hr + bui
        for s in range(half):
            h_scr[s, pl.ds(r0, SUBLANES), :] = nhr[:, s * LANES:(s + 1) * LANES]
            h_scr[half + s, pl.ds(r0, SUBLANES), :] = nhi[:, s * LANES:(s + 1) * LANES]
        return nhr, nhi

    hr, hi = lax.fori_loop(0, tc, step, (st_scr[0:SUBLANES, :], st_scr[SUBLANES:2 * SUBLANES, :]), unroll=2)
    st_scr[0:SUBLANES, :] = hr
    st_scr[SUBLANES:2 * SUBLANES, :] = hi

    for k in range(tps):
        cols = slice(k * S5_NC, (k + 1) * S5_NC)

        hb = jnp.concatenate(
            [jnp.concatenate([h_scr[s, pl.ds(k * bsz + b, tc, stride=SUBLANES), :] for s in range(S5_SLABS)], axis=1)
             for b in range(bsz)], axis=0).astype(BF16)
        ch = lax.dot_general(hb, cmat_scr[k], _NT, preferred_element_type=F32)
        for b in range(bsz):
            y = ch[b * tc:(b + 1) * tc, :] + d_ref[:, cols] * u_ref[b, :, cols]
            y_ref[b, :, cols] = jax.nn.gelu(y)


def _s5_mixer(u3, a_re, a_im, log_dt, b_re, b_im, c_re, c_im, d_skip, tc=256):
    bsz, seq, _ = u3.shape
    assert SUBLANES % bsz == 0
    tps = min(SUBLANES // bsz, S5_TILES)
    assert tps * bsz == SUBLANES
    tc = min(tc, seq)

    def per_tile_b(b):
        return (b.reshape(S5_TILES, S5_GT, SSM_STATE, SSM_GROUP).transpose(0, 3, 1, 2)
                .reshape(S5_TILES, SSM_GROUP, S5_NP))

    def per_tile_c(c):
        return (c.reshape(S5_TILES, S5_GT, SSM_GROUP, SSM_STATE).transpose(0, 2, 1, 3)
                .reshape(S5_TILES, SSM_GROUP, S5_NP))

    flat = lambda a: a.reshape(1, SSM_GROUPS * SSM_STATE)
    ldt = jnp.repeat(log_dt, SSM_STATE).reshape(1, SSM_GROUPS * SSM_STATE)
    vec_spec = pl.BlockSpec((1, tps * S5_NP), lambda g, t: (0, g))
    bc_spec = pl.BlockSpec((tps, SSM_GROUP, S5_NP), lambda g, t: (g, 0, 0))
    return pl.pallas_call(
        functools.partial(_s5_body, tc=tc, bsz=bsz, tps=tps),
        out_shape=jax.ShapeDtypeStruct((bsz, seq, SSM_WIDTH), F32),
        grid=(S5_TILES // tps, seq // tc),
        in_specs=[
            pl.BlockSpec((bsz, tc, tps * S5_NC), lambda g, t: (0, t, g)),
            vec_spec, vec_spec, vec_spec, bc_spec, bc_spec, bc_spec, bc_spec,
            pl.BlockSpec((1, tps * S5_NC), lambda g, t: (0, g)),
        ],
        out_specs=pl.BlockSpec((bsz, tc, tps * S5_NC), lambda g, t: (0, t, g)),
        scratch_shapes=[
            pltpu.VMEM((2 * SUBLANES, S5_NP), F32),
            pltpu.VMEM((tps, S5_NC, 2 * S5_NP), BF16),
            pltpu.VMEM((tps, S5_NC, 2 * S5_NP), BF16),
            pltpu.VMEM((S5_SLABS, tc * SUBLANES, LANES), F32),
            pltpu.VMEM((2 * SUBLANES, S5_NP), F32),
        ],
        compiler_params=_params(("arbitrary", "arbitrary")),
        name="s5_scan",
    )(u3, flat(a_re), flat(a_im), ldt, per_tile_b(b_re), per_tile_b(b_im), per_tile_c(c_re), per_tile_c(c_im),
      d_skip.reshape(1, SSM_WIDTH))


def _glu_body(y_ref, w_ref, b_ref, o_ref, ybf_scr, *, tn):
    j = pl.program_id(1)

    @pl.when(j == 0)
    def _():
        ybf_scr[...] = y_ref[...].astype(BF16)

    z = jnp.dot(ybf_scr[...], w_ref[...], preferred_element_type=F32) + b_ref[...]
    yt = y_ref[:, pl.ds(pl.multiple_of(j * tn, tn), tn)]
    o_ref[...] = (yt * jax.nn.sigmoid(z)).astype(o_ref.dtype)


def _glu(y, w_bf, bias, tm=1024, tn=1024):
    m, k = y.shape
    tm = min(tm, m)
    return pl.pallas_call(
        functools.partial(_glu_body, tn=tn),
        out_shape=jax.ShapeDtypeStruct((m, k), BF16),
        grid=(m // tm, k // tn),
        in_specs=[pl.BlockSpec((tm, k), lambda i, j: (i, 0)),
                  pl.BlockSpec((k, tn), lambda i, j: (0, j)),
                  pl.BlockSpec((1, tn), lambda i, j: (0, j))],
        out_specs=pl.BlockSpec((tm, tn), lambda i, j: (i, j)),
        scratch_shapes=[pltpu.VMEM((tm, k), BF16)],
        compiler_params=_params(("parallel", "arbitrary")),
        name="s5_glu",
    )(y, w_bf, bias.reshape(1, k))


def _compress_body(x_ref, pe_ref, w1_ref, w2_ref, o_ref, *, ncmp):
    half = L_CMP // 2
    acc_a = jnp.zeros((ncmp, HEAD_DIM), F32)
    acc_b = jnp.zeros((ncmp, HEAD_DIM), F32)
    for r in range(half):
        xr = x_ref[pl.ds(r, ncmp, stride=STRIDE_CMP), :]
        xa = (xr + pe_ref[r:r + 1, :]).astype(BF16)
        xb = (xr + pe_ref[half + r:half + r + 1, :]).astype(BF16)
        acc_a += jnp.dot(xa, w1_ref[r * HEAD_DIM:(r + 1) * HEAD_DIM, :], preferred_element_type=F32)
        acc_b += jnp.dot(xb, w1_ref[(half + r) * HEAD_DIM:(half + r + 1) * HEAD_DIM, :],
                         preferred_element_type=F32)
    pre = acc_a + jnp.concatenate([acc_b[1:], acc_b[:1]], axis=0)
    o_ref[...] = jnp.dot(jax.nn.gelu(pre).astype(BF16), w2_ref[...], preferred_element_type=F32).astype(o_ref.dtype)


def _compress(proj_f32, col0, pe, w1_bf, w2_bf, bsz, seq):
    ncmp = seq // STRIDE_CMP
    cb = col0 // HEAD_DIM
    return pl.pallas_call(
        functools.partial(_compress_body, ncmp=ncmp),
        out_shape=jax.ShapeDtypeStruct((bsz, 2, N_KV, ncmp, HEAD_DIM), BF16),
        grid=(bsz, 2, N_KV),
        in_specs=[pl.BlockSpec((seq, HEAD_DIM), lambda b, w, g: (b, cb + w * N_KV + g)),
                  pl.BlockSpec((None, L_CMP, HEAD_DIM), lambda b, w, g: (w, 0, 0)),
                  pl.BlockSpec((None, L_CMP * HEAD_DIM, HEAD_DIM), lambda b, w, g: (w, 0, 0)),
                  pl.BlockSpec((None, HEAD_DIM, HEAD_DIM), lambda b, w, g: (w, 0, 0))],
        out_specs=pl.BlockSpec((None, None, None, ncmp, HEAD_DIM), lambda b, w, g: (b, w, g, 0, 0)),
        compiler_params=_params(("parallel", "parallel", "parallel")),
        name="nsa_compress",
    )(proj_f32, pe, w1_bf, w2_bf)


def _cmp_attn_body(q_ref, kc_ref, vc_ref, o_ref, bias_ref, *, tq, ncmp, nblk, ntop):
    qi = pl.program_id(2)
    t0 = qi * tq
    tpos = t0 + lax.broadcasted_iota(jnp.int32, (tq, ncmp), 0)
    blk_end = lax.broadcasted_iota(jnp.int32, (tq, ncmp), 1) * STRIDE_CMP + (L_CMP - 1)
    ok = blk_end <= tpos
    okf = ok.astype(F32)
    kc = kc_ref[...]
    vc = vc_ref[...]
    psum = jnp.zeros((tq, ncmp), F32)
    for h in range(HPG):
        qh = q_ref[:, h * HEAD_DIM:(h + 1) * HEAD_DIM]
        s = lax.dot_general(qh, kc, (((1,), (1,)), ((), ())), preferred_element_type=F32)
        s = jnp.where(ok, s, NEG_INF)
        e = jnp.exp2(s - jnp.max(s, axis=-1, keepdims=True))
        p = e / jnp.sum(e, axis=-1, keepdims=True) * okf
        o_ref[:, h * HEAD_DIM:(h + 1) * HEAD_DIM] = jnp.dot(p.astype(BF16), vc, preferred_element_type=F32)
        psum = psum + p

    jj = lax.broadcasted_iota(jnp.int32, (nblk, ncmp), 0)
    nn = lax.broadcasted_iota(jnp.int32, (nblk, ncmp), 1)
    ov = ((nn * STRIDE_CMP < (jj + 1) * L_SEL) & (nn * STRIDE_CMP + L_CMP > jj * L_SEL)).astype(BF16)
    p_hi = psum.astype(BF16)
    p_lo = (psum - p_hi.astype(F32)).astype(BF16)
    nt = (((1,), (1,)), ((), ()))
    imp = (lax.dot_general(ov, p_hi, nt, preferred_element_type=F32)
           + lax.dot_general(ov, p_lo, nt, preferred_element_type=F32))

    jb = lax.broadcasted_iota(jnp.int32, (nblk, tq), 0)
    tt = t0 + lax.broadcasted_iota(jnp.int32, (nblk, tq), 1)
    cur = tt // L_SEL
    allowed = jb * L_SEL <= tt
    forced = (jb == 0) | (jb == cur) | (jb == cur - 1)
    score = jnp.where(forced, FORCE_SCORE, jnp.where(allowed, imp, NEG_INF))
    taken = jnp.zeros((nblk, tq), F32)
    for _ in range(ntop):
        best = jnp.max(score, axis=0, keepdims=True)
        first = jnp.min(jnp.where(score == best, jb, nblk), axis=0, keepdims=True)
        pick = jb == first
        taken = jnp.where(pick, 1.0, taken)
        score = jnp.where(pick, TAKEN_SCORE, score)
    bias_t = jnp.where(taken > 0.5, 0.0, NEG_INF)
    bias_t = jnp.concatenate([bias_t, jnp.zeros((LANES - nblk, tq), F32)], axis=0)
    bias_ref[...] = bias_t.T.astype(bias_ref.dtype)


def _cmp_attn(qkv, kcv, bsz, seq, tq=512):
    tq = min(tq, seq)
    ncmp = seq // STRIDE_CMP
    nblk = seq // L_SEL
    ntop = min(N_SEL, nblk)
    nq = seq // tq
    return pl.pallas_call(
        functools.partial(_cmp_attn_body, tq=tq, ncmp=ncmp, nblk=nblk, ntop=ntop),
        out_shape=(jax.ShapeDtypeStruct((bsz * seq, ATT_WIDTH), F32),
                   jax.ShapeDtypeStruct((bsz, N_KV, seq, LANES), BF16)),
        grid=(bsz, N_KV, nq),
        in_specs=[pl.BlockSpec((tq, HPG * HEAD_DIM), lambda b, g, i: (b * nq + i, g)),
                  pl.BlockSpec((None, None, None, ncmp, HEAD_DIM), lambda b, g, i: (b, 0, g, 0, 0)),
                  pl.BlockSpec((None, None, None, ncmp, HEAD_DIM), lambda b, g, i: (b, 1, g, 0, 0))],
        out_specs=(pl.BlockSpec((tq, HPG * HEAD_DIM), lambda b, g, i: (b * nq + i, g)),
                   pl.BlockSpec((None, None, tq, LANES), lambda b, g, i: (b, g, i, 0))),
        compiler_params=_params(("parallel", "parallel", "parallel")),
        name="nsa_cmp_attn_topk",
    )(qkv, kcv, kcv)


def _sel_attn_body(q_ref, k_ref, v_ref, bias_ref, o_ref, qa_scr, m_scr, acc_scr, s_scr, p_scr, alpha_scr,
                   *, tq, rb):
    qi = pl.program_id(2)
    tk = tq
    m_scr[...] = jnp.full_like(m_scr, NEG_INF)
    acc_scr[...] = jnp.zeros_like(acc_scr)
    for h in range(HPG):
        qa_scr[h, :, 0:HEAD_DIM] = q_ref[:, h * HEAD_DIM:(h + 1) * HEAD_DIM]
        qa_scr[h, :, HEAD_DIM:2 * HEAD_DIM] = bias_ref[...]
    nt = (((1,), (1,)), ((), ()))

    def key_tile(k0, nkeys):
        key_blk = k0 // L_SEL + lax.broadcasted_iota(jnp.int32, (nkeys, LANES), 0) // L_SEL
        onehot = jnp.where(lax.broadcasted_iota(jnp.int32, (nkeys, LANES), 1) == key_blk, 1.0, 0.0).astype(BF16)
        return jnp.concatenate([k_ref[pl.ds(k0, nkeys), :], onehot], axis=1)

    def scores(h, k_aug, r0, r1, nk):
        s_scr[h, r0:r1, 0:nk] = lax.dot_general(qa_scr[h, r0:r1, :], k_aug[0:nk, :], nt,
                                                preferred_element_type=F32)

    def softmax(h, r0, r1, nk, diagonal):
        for r in range(r0 // rb, r1 // rb):
            rows = slice(r * rb, (r + 1) * rb)
            s = s_scr[h, rows, 0:nk]
            if diagonal:
                col = lax.broadcasted_iota(jnp.int32, (rb, nk), 1)
                row = lax.broadcasted_iota(jnp.int32, (rb, nk), 0) + r * rb
                s = jnp.where(col <= row, s, NEG_INF)
            m_prev = m_scr[h, rows, :]
            m_next = jnp.maximum(m_prev, jnp.max(s, axis=-1, keepdims=True))
            p_scr[h, rows, 0:nk] = jnp.exp2(s - _lane_tile(m_next, nk // LANES)).astype(BF16)
            alpha_scr[h, rows, :] = jnp.exp2(m_prev - m_next)
            m_scr[h, rows, :] = m_next

    def values(h, v_aug, r0, r1, nk):
        acc_scr[h, r0:r1, :] = (_lane_tile(alpha_scr[h, r0:r1, :], 2) * acc_scr[h, r0:r1, :]
                                + jnp.dot(p_scr[h, r0:r1, 0:nk], v_aug[0:nk, :], preferred_element_type=F32))

    def tile(kt, nkeys, diagonal):
        k0 = pl.multiple_of(kt * tk, tk)
        k_aug = key_tile(k0, nkeys)
        v_aug = jnp.concatenate([v_ref[pl.ds(k0, nkeys), :], jnp.ones((nkeys, LANES), BF16)], axis=1)
        parts = [(0, tq // 2, nkeys // 2), (tq // 2, tq, nkeys)] if diagonal else [(0, tq, nkeys)]
        for h in range(HPG):
            for r0, r1, nk in parts:
                scores(h, k_aug, r0, r1, nk)
        for h in range(HPG):
            for r0, r1, nk in parts:
                softmax(h, r0, r1, nk, diagonal)
        for h in range(HPG):
            for r0, r1, nk in parts:
                values(h, v_aug, r0, r1, nk)

    def tile_group(j, carry):
        tile(SEL_GROUP * j, SEL_GROUP * tk, False)
        return carry

    lax.fori_loop(0, qi // SEL_GROUP, tile_group, 0)
    done = (qi // SEL_GROUP) * SEL_GROUP
    run = SEL_GROUP // 2
    while run >= 1:
        @pl.when((qi - done) % (2 * run) >= run)
        def _(done=done, run=run):
            tile(done, run * tk, False)

        done = done + jnp.where((qi - done) % (2 * run) >= run, run, 0)
        run //= 2

    tile(qi, tk, True)

    for h in range(HPG):
        o_ref[:, h * HEAD_DIM:(h + 1) * HEAD_DIM] = (acc_scr[h, :, 0:HEAD_DIM]
                                                     / acc_scr[h, :, HEAD_DIM:2 * HEAD_DIM])


def _sel_attn(qkv, bias, bsz, seq, tq=SEL_TQ, rb=SEL_RB):
    tq = min(tq, seq)
    rb = min(rb, tq)
    nq = seq // tq
    kcol = ATT_WIDTH // HEAD_DIM
    return pl.pallas_call(
        functools.partial(_sel_attn_body, tq=tq, rb=rb),
        out_shape=jax.ShapeDtypeStruct((bsz * seq, ATT_WIDTH), F32),
        grid=(bsz, N_KV, nq),
        in_specs=[pl.BlockSpec((tq, HPG * HEAD_DIM), lambda b, g, i: (b * nq + i, g)),
                  pl.BlockSpec((seq, HEAD_DIM), lambda b, g, i: (b, kcol + g)),
                  pl.BlockSpec((seq, HEAD_DIM), lambda b, g, i: (b, kcol + N_KV + g)),
                  pl.BlockSpec((None, None, tq, LANES), lambda b, g, i: (b, g, i, 0))],
        out_specs=pl.BlockSpec((tq, HPG * HEAD_DIM), lambda b, g, i: (b * nq + i, g)),
        scratch_shapes=[pltpu.VMEM((HPG, tq, 2 * HEAD_DIM), BF16),
                        pltpu.VMEM((HPG, tq, LANES), F32),
                        pltpu.VMEM((HPG, tq, 2 * HEAD_DIM), F32),
                        pltpu.VMEM((HPG, tq, SEL_GROUP * tq), F32),
                        pltpu.VMEM((HPG, tq, SEL_GROUP * tq), BF16),
                        pltpu.VMEM((HPG, tq, LANES), F32)],
        compiler_params=_params(("parallel", "parallel", "arbitrary")),
        name="nsa_sel_attn",
    )(qkv, qkv, qkv, bias)


def _win_attn_body(q_ref, kc_ref, kp_ref, vc_ref, vp_ref, ocmp_ref, osel_ref, gate_ref, o_ref, *, tq):
    qi = pl.program_id(2)
    nt = (((1,), (1,)), ((), ()))
    ones = jnp.ones((tq, LANES), BF16)
    vc_aug = jnp.concatenate([vc_ref[...], ones], axis=1)
    vp_aug = jnp.concatenate([vp_ref[...], ones], axis=1)
    for r in range(tq // ATT_ROWS):
        rows = slice(r * ATT_ROWS, (r + 1) * ATT_ROWS)
        n_cur, p_lo = (r + 1) * ATT_ROWS, r * ATT_ROWS
        row_c = lax.broadcasted_iota(jnp.int32, (ATT_ROWS, n_cur), 0) + r * ATT_ROWS
        ok_cur = lax.broadcasted_iota(jnp.int32, (ATT_ROWS, n_cur), 1) <= row_c
        row_p = lax.broadcasted_iota(jnp.int32, (ATT_ROWS, tq - p_lo), 0) + r * ATT_ROWS
        ok_prev = (lax.broadcasted_iota(jnp.int32, (ATT_ROWS, tq - p_lo), 1) + p_lo > row_p) & (qi > 0)
        gates = jax.nn.sigmoid(gate_ref[rows, :])
        for h in range(HPG):
            sl = slice(h * HEAD_DIM, (h + 1) * HEAD_DIM)
            qh = q_ref[rows, sl]
            s1 = jnp.where(ok_cur, lax.dot_general(qh, kc_ref[0:n_cur, :], nt, preferred_element_type=F32), NEG_INF)
            s0 = jnp.where(ok_prev, lax.dot_general(qh, kp_ref[p_lo:tq, :], nt, preferred_element_type=F32), NEG_INF)
            m = jnp.maximum(jnp.max(s1, axis=-1, keepdims=True), jnp.max(s0, axis=-1, keepdims=True))
            p1 = jnp.exp2(s1 - m).astype(BF16)
            p0 = jnp.exp2(s0 - m).astype(BF16)
            acc = (jnp.dot(p1, vc_aug[0:n_cur, :], preferred_element_type=F32)
                   + jnp.dot(p0, vp_aug[p_lo:tq, :], preferred_element_type=F32))
            ow = acc[:, 0:HEAD_DIM] / acc[:, HEAD_DIM:2 * HEAD_DIM]
            g_cmp = gates[:, h:h + 1]
            g_sel = gates[:, HPG + h:HPG + h + 1]
            g_win = gates[:, 2 * HPG + h:2 * HPG + h + 1]
            o_ref[rows, sl] = (g_cmp * ocmp_ref[rows, sl] + g_sel * osel_ref[rows, sl]
                               + g_win * ow).astype(o_ref.dtype)


def _win_attn_combine(qkv, o_cmp, o_sel, gates, bsz, seq):
    tq = WINDOW
    assert seq % tq == 0
    nq = seq // tq
    kcol = ATT_WIDTH // HEAD_DIM + 2 * N_KV
    qspec = pl.BlockSpec((tq, HPG * HEAD_DIM), lambda b, g, i: (b * nq + i, g))
    cur = lambda c: pl.BlockSpec((tq, HEAD_DIM), lambda b, g, i: (b * nq + i, kcol + c * N_KV + g))
    prev = lambda c: pl.BlockSpec((tq, HEAD_DIM),
                                  lambda b, g, i: (b * nq + jnp.maximum(i - 1, 0), kcol + c * N_KV + g))
    return pl.pallas_call(
        functools.partial(_win_attn_body, tq=tq),
        out_shape=jax.ShapeDtypeStruct((bsz * seq, ATT_WIDTH), BF16),
        grid=(bsz, N_KV, nq),
        in_specs=[qspec, cur(0), prev(0), cur(1), prev(1), qspec, qspec,
                  pl.BlockSpec((None, None, tq, LANES), lambda b, g, i: (b, g, i, 0))],
        out_specs=qspec,
        compiler_params=_params(("parallel", "parallel", "parallel")),
        name="nsa_win_attn_gate",
    )(qkv, qkv, qkv, qkv, qkv, o_cmp, o_sel, gates)


def _merge_body(hn_ref, ya_ref, yb_ref, wga_ref, wgb_ref, wa_ref, wb_ref, o_ref):
    hn = hn_ref[...]
    ga = jnp.dot(hn, wga_ref[...], preferred_element_type=F32)
    gb = jnp.dot(hn, wgb_ref[...], preferred_element_type=F32)
    pa = jnp.dot(ya_ref[...], wa_ref[...], preferred_element_type=F32)
    pb = jnp.dot(yb_ref[...], wb_ref[...], preferred_element_type=F32)
    o_ref[...] = (jax.nn.sigmoid(ga) * pa + jax.nn.sigmoid(gb) * pb).astype(o_ref.dtype)


def _merge(hn, ya, yb, wg, wa, wb, tm=512, tn=512):
    m, d = hn.shape
    ka = ya.shape[1]
    tm = min(tm, m)
    row = lambda k: pl.BlockSpec((tm, k), lambda i, j: (i, 0))
    col = lambda k: pl.BlockSpec((k, tn), lambda i, j: (0, j))
    col_b = pl.BlockSpec((d, tn), lambda i, j: (0, d // tn + j))
    return pl.pallas_call(
        _merge_body,
        out_shape=jax.ShapeDtypeStruct((m, d), BF16),
        grid=(m // tm, d // tn),
        in_specs=[row(d), row(ka), row(ka), col(d), col_b, col(ka), col(ka)],
        out_specs=pl.BlockSpec((tm, tn), lambda i, j: (i, j)),
        compiler_params=_params(("parallel", "arbitrary")),
        name="mixer_merge",
    )(hn, ya, yb, wg, wg, wa, wb)


def _rms(x, g):
    return x * lax.rsqrt(jnp.mean(x * x, axis=-1, keepdims=True) + RMS_EPS) * g


def _res_norm2_body(raw_ref, x_ref, gpost_ref, gpre_ref, h_ref, hn_ref):
    h = x_ref[...] + _rms(raw_ref[...].astype(F32), gpost_ref[...])
    h_ref[...] = h
    hn_ref[...] = _rms(h, gpre_ref[...]).astype(hn_ref.dtype)


def _res_norm2(raw, x, g_post, g_pre, tm=256):
    m, d = x.shape
    tm = min(tm, m)
    rows = pl.BlockSpec((tm, d), lambda i: (i, 0))
    vec = pl.BlockSpec((1, d), lambda i: (0, 0))
    return pl.pallas_call(
        _res_norm2_body,
        out_shape=(jax.ShapeDtypeStruct((m, d), F32), jax.ShapeDtypeStruct((m, d), BF16)),
        grid=(m // tm,),
        in_specs=[rows, rows, vec, vec],
        out_specs=(rows, rows),
        compiler_params=_params(("parallel",)),
        name="residual_norm_prenorm",
    )(raw, x, g_post.reshape(1, d), g_pre.reshape(1, d))


def _res_norm_body(raw_ref, x_ref, g_ref, o_ref):
    o_ref[...] = x_ref[...] + _rms(raw_ref[...].astype(F32), g_ref[...])


def _res_norm(raw, x, g, tm=256):
    m, d = x.shape
    tm = min(tm, m)
    rows = pl.BlockSpec((tm, d), lambda i: (i, 0))
    return pl.pallas_call(
        _res_norm_body,
        out_shape=jax.ShapeDtypeStruct((m, d), F32),
        grid=(m // tm,),
        in_specs=[rows, rows, pl.BlockSpec((1, d), lambda i: (0, 0))],
        out_specs=rows,
        compiler_params=_params(("parallel",)),
        name="residual_norm",
    )(raw, x, g.reshape(1, d))


def _swiglu_body(a_ref, wg_ref, wu_ref, o_ref, wg_scr, wu_scr):
    @pl.when(pl.program_id(1) == 0)
    def _():
        wg_scr[...] = wg_ref[...].astype(BF16)
        wu_scr[...] = wu_ref[...].astype(BF16)

    a = a_ref[...]
    g = jnp.dot(a, wg_scr[...], preferred_element_type=F32)
    u = jnp.dot(a, wu_scr[...], preferred_element_type=F32)
    o_ref[...] = (g * jax.nn.sigmoid(g) * u).astype(o_ref.dtype)


def _swiglu(a, wg, wu, tm=2048, tn=256):
    m, k = a.shape
    n = wg.shape[1]
    tm = min(tm, m)
    assert n % tn == 0
    vmem = 2 * tm * k * 2 + 2 * 2 * k * tn * 4 + 2 * tm * tn * 2 + 2 * k * tn * 2
    assert vmem + V7X_VMEM_SLACK_BYTES <= V7X_VMEM_BYTES
    return pl.pallas_call(
        _swiglu_body,
        out_shape=jax.ShapeDtypeStruct((m, n), BF16),
        grid=(n // tn, m // tm),
        in_specs=[pl.BlockSpec((tm, k), lambda j, i: (i, 0)),
                  pl.BlockSpec((k, tn), lambda j, i: (0, j)),
                  pl.BlockSpec((k, tn), lambda j, i: (0, j))],
        out_specs=pl.BlockSpec((tm, tn), lambda j, i: (i, j)),
        scratch_shapes=[pltpu.VMEM((k, tn), BF16), pltpu.VMEM((k, tn), BF16)],
        compiler_params=_params(("arbitrary", "arbitrary"), vmem=vmem + V7X_VMEM_SLACK_BYTES),
        name="ffn_swiglu",
    )(a, wg, wu)


def kernel(x, norm_mix_pre, w_in, ssm_a_re, ssm_a_im, ssm_log_dt, ssm_b_re, ssm_b_im, ssm_c_re, ssm_c_im, ssm_d, ssm_w_glu, ssm_b_glu, cmp_pe_k, cmp_w1_k, cmp_w2_k, cmp_pe_v, cmp_w1_v, cmp_w2_v, w_proj_a, w_proj_b, w_out, norm_mix_post, norm_ffn_pre, w_ffn_gate, w_ffn_up, w_ffn_down, norm_ffn_post):
    bsz, seq, d = x.shape
    m = bsz * seq
    depth = w_in.shape[0]
    h = x.reshape(m, d)
    o_q, o_kvc, o_kv, o_gn = SSM_WIDTH, SSM_WIDTH + ATT_WIDTH, SSM_WIDTH + ATT_WIDTH + 2 * KV_WIDTH, 7168
    o_ga = o_gn + 3 * N_HEADS
    o_gb = o_ga + D_MODEL
    bf = lambda a: a.astype(BF16)
    tn_in = 512
    nb_u, nb_q = SSM_WIDTH // tn_in, ATT_WIDTH // tn_in
    qk_scale = HEAD_DIM ** -0.5 * LOG2E
    for l in range(depth):
        wt = jnp.swapaxes(w_in[l], 0, 1)
        w_g = _rows_transposed_cast(wt, o_ga, 2 * D_MODEL, BF16)

        hn, gn = _rmsnorm_gates(h, norm_mix_pre[l], wt, o_gn // LANES)
        proj_f32 = _matmul_f32w(hn, wt, SSM_WIDTH + 2 * KV_WIDTH, F32, 1024, tn_in, trans_w=True,
                                col_block=lambda j: jnp.where(j < nb_u, j, j - nb_u + o_kvc // tn_in),
                                name="in_proj_f32")
        qkv = _matmul_f32w(hn, wt, ATT_WIDTH + 4 * KV_WIDTH, BF16, 2048, tn_in, trans_w=True,
                           col_block=lambda j: jnp.where(j < nb_q, j + o_q // tn_in, j - nb_q + o_kv // tn_in),
                           scale_fn=lambda j: jnp.where(j < nb_q, qk_scale, 1.0),
                           name="in_proj_bf16")

        y = _s5_mixer(proj_f32.reshape(bsz, seq, -1), ssm_a_re[l], ssm_a_im[l], ssm_log_dt[l],
                      ssm_b_re[l], ssm_b_im[l], ssm_c_re[l], ssm_c_im[l], ssm_d[l])
        y_a = _glu(y.reshape(m, SSM_WIDTH), bf(ssm_w_glu[l]), ssm_b_glu[l])

        pe = jnp.stack([cmp_pe_k[l], cmp_pe_v[l]])
        w1 = bf(jnp.stack([cmp_w1_k[l], cmp_w1_v[l]]))
        w2 = bf(jnp.stack([cmp_w2_k[l], cmp_w2_v[l]]))
        kcv = _compress(proj_f32, SSM_WIDTH, pe, w1, w2, bsz, seq)
        o_cmp, bias = _cmp_attn(qkv, kcv, bsz, seq)
        o_sel = _sel_attn(qkv, bias, bsz, seq)
        gates = gn[:, :3 * N_HEADS].reshape(bsz, seq, 3, N_KV, HPG).transpose(0, 3, 1, 2, 4)
        gates = jnp.pad(gates.reshape(bsz, N_KV, seq, 3 * HPG), ((0, 0), (0, 0), (0, 0), (0, LANES - 3 * HPG)))
        y_b = _win_attn_combine(qkv, o_cmp, o_sel, gates, bsz, seq)

        merged = _merge(hn, y_a, y_b, w_g, bf(w_proj_a[l]), bf(w_proj_b[l]))
        mix = _matmul_f32w(merged, w_out[l], D_MODEL, BF16, 2048, 512, name="out_proj")
        h, hn2 = _res_norm2(mix, h, norm_mix_post[l], norm_ffn_pre[l])

        act = _swiglu(hn2, w_ffn_gate[l], w_ffn_up[l])
        f = _matmul(act, bf(w_ffn_down[l]), BF16, 512, 512, name="ffn_down")
        h = _res_norm(f, h, norm_ffn_post[l])
    return h.reshape(bsz, seq, d)
```

```python
import functools
import math

import jax
import jax.numpy as jnp
from jax import lax
from jax.experimental import pallas as pl
from jax.experimental.pallas import tpu as pltpu

F32 = jnp.float32
BF16 = jnp.bfloat16

D_MODEL = 4096
SSM_WIDTH = 2048
SSM_GROUP = 16
SSM_GROUPS = 128
SSM_STATE = 64
N_HEADS = 16
HEAD_DIM = 128
N_KV = 4
HPG = 4
ATT_WIDTH = 2048
KV_WIDTH = 512
L_CMP = 32
STRIDE_CMP = 16
L_SEL = 64
N_SEL = 16
WINDOW = 512
D_FF = 11008
RMS_EPS = 1e-6
NEG_INF = -1e30
FORCE_SCORE = 1e9
TAKEN_SCORE = -3e38
LOG2E = math.log2(math.e)

V7X_VMEM_BYTES = 64 * 1024 * 1024
V7X_VMEM_LIMIT_BYTES = 56 * 1024 * 1024
V7X_VMEM_SLACK_BYTES = 4 * 1024 * 1024
LANES = 128
SUBLANES = 8

S5_GT = 16
S5_NC = S5_GT * SSM_GROUP
S5_NP = S5_GT * SSM_STATE
S5_TILES = SSM_GROUPS // S5_GT
S5_SLABS = 2 * S5_NP // LANES

ATT_ROWS = 256
SEL_TQ, SEL_RB = 512, 64
SEL_GROUP = 4


def _params(sem, vmem=V7X_VMEM_LIMIT_BYTES):
    return pltpu.CompilerParams(dimension_semantics=sem, vmem_limit_bytes=vmem)


def _lane_tile(x, n):
    return jnp.concatenate([x] * n, axis=1)


def _rmsnorm_gates_body(x_ref, g_ref, w_ref, hn_ref, gn_ref, wbf_scr):
    @pl.when(pl.program_id(0) == 0)
    def _():
        wbf_scr[...] = w_ref[...].astype(BF16)

    x = x_ref[...]
    ms = jnp.mean(x * x, axis=-1, keepdims=True)
    hn = (x * lax.rsqrt(ms + RMS_EPS) * g_ref[...]).astype(BF16)
    hn_ref[...] = hn
    gn_ref[...] = lax.dot_general(hn, wbf_scr[...], (((1,), (1,)), ((), ())), preferred_element_type=F32)


def _rmsnorm_gates(x, gain, wt, feature_block, tm=256):
    m, d = x.shape
    tm = min(tm, m)
    return pl.pallas_call(
        _rmsnorm_gates_body,
        out_shape=(jax.ShapeDtypeStruct((m, d), BF16), jax.ShapeDtypeStruct((m, LANES), F32)),
        grid=(m // tm,),
        in_specs=[pl.BlockSpec((tm, d), lambda i: (i, 0)),
                  pl.BlockSpec((1, d), lambda i: (0, 0)),
                  pl.BlockSpec((LANES, d), lambda i: (feature_block, 0))],
        out_specs=(pl.BlockSpec((tm, d), lambda i: (i, 0)), pl.BlockSpec((tm, LANES), lambda i: (i, 0))),
        scratch_shapes=[pltpu.VMEM((LANES, d), BF16)],
        compiler_params=_params(("arbitrary",)),
        name="rmsnorm_gates",
    )(x, gain.reshape(1, d), wt)


def _mm_body(a_ref, w_ref, o_ref):
    o_ref[...] = jnp.dot(a_ref[...], w_ref[...], preferred_element_type=F32).astype(o_ref.dtype)


def _matmul(a, w, out_dtype, tm, tn, name="matmul"):
    m, k = a.shape
    n = w.shape[1]
    tm, tn = min(tm, m), min(tn, n)
    assert m % tm == 0 and n % tn == 0
    return pl.pallas_call(
        _mm_body,
        out_shape=jax.ShapeDtypeStruct((m, n), out_dtype),
        grid=(m // tm, n // tn),
        in_specs=[pl.BlockSpec((tm, k), lambda i, j: (i, 0)), pl.BlockSpec((k, tn), lambda i, j: (0, j))],
        out_specs=pl.BlockSpec((tm, tn), lambda i, j: (i, j)),
        compiler_params=_params(("parallel", "arbitrary")),
        name=name,
    )(a, w)


_NT = (((1,), (1,)), ((), ()))


def _mm_f32w_body(a_ref, w_ref, o_ref, wbf_scr, *, scale_fn, trans_w):
    @pl.when(pl.program_id(1) == 0)
    def _():
        wbf_scr[...] = w_ref[...].astype(BF16)

    if trans_w:
        acc = lax.dot_general(a_ref[...], wbf_scr[...], _NT, preferred_element_type=F32)
    else:
        acc = jnp.dot(a_ref[...], wbf_scr[...], preferred_element_type=F32)
    if scale_fn is not None:
        acc = acc * scale_fn(pl.program_id(0))
    o_ref[...] = acc.astype(o_ref.dtype)


def _matmul_f32w(a, w, n_out, out_dtype, tm, tn, col_block=None, scale_fn=None, trans_w=False,
                 name="matmul_f32w"):
    m, k = a.shape
    tm = min(tm, m)
    assert m % tm == 0 and n_out % tn == 0
    col_block = col_block or (lambda j: j)
    if trans_w:
        w_spec = pl.BlockSpec((tn, k), lambda j, i: (col_block(j), 0))
        w_scratch = pltpu.VMEM((tn, k), BF16)
    else:
        w_spec = pl.BlockSpec((k, tn), lambda j, i: (0, col_block(j)))
        w_scratch = pltpu.VMEM((k, tn), BF16)
    vmem = 2 * tm * k * 2 + 2 * k * tn * 4 + 2 * tm * tn * jnp.dtype(out_dtype).itemsize + k * tn * 2
    assert vmem + V7X_VMEM_SLACK_BYTES <= V7X_VMEM_BYTES
    return pl.pallas_call(
        functools.partial(_mm_f32w_body, scale_fn=scale_fn, trans_w=trans_w),
        out_shape=jax.ShapeDtypeStruct((m, n_out), out_dtype),
        grid=(n_out // tn, m // tm),
        in_specs=[pl.BlockSpec((tm, k), lambda j, i: (i, 0)), w_spec],
        out_specs=pl.BlockSpec((tm, tn), lambda j, i: (i, j)),
        scratch_shapes=[w_scratch],
        compiler_params=_params(("arbitrary", "arbitrary"), vmem=vmem + V7X_VMEM_SLACK_BYTES),
        name=name,
    )(a, w)


def _rows_transposed_cast_body(w_ref, o_ref):
    o_ref[...] = w_ref[...].T.astype(o_ref.dtype)


def _rows_transposed_cast(wt, row0, n_rows, out_dtype, tr=512):
    k = wt.shape[1]
    assert row0 % (2 * SUBLANES) == 0 and n_rows % tr == 0
    return pl.pallas_call(
        _rows_transposed_cast_body,
        out_shape=jax.ShapeDtypeStruct((k, n_rows), out_dtype),
        grid=(n_rows // tr,),
        in_specs=[pl.BlockSpec((pl.Element(tr), pl.Element(k)),
                               lambda c: (pl.multiple_of(row0 + c * tr, 2 * SUBLANES), 0))],
        out_specs=pl.BlockSpec((k, tr), lambda c: (0, c)),
        compiler_params=_params(("parallel",)),
        name="gate_weight_cast",
    )(wt)


def _s5_body(u_ref, are_ref, aim_ref, ldt_ref, bre_ref, bim_ref, cre_ref, cim_ref, d_ref, y_ref,
             abar_scr, bbar_scr, cmat_scr, h_scr, st_scr, *, tc, bsz, tps):
    ti = pl.program_id(1)
    half = S5_SLABS // 2

    @pl.when(ti == 0)
    def _():
        row_tile = lax.broadcasted_iota(jnp.int32, (SUBLANES, S5_NP), 0) // bsz
        same_group = (lax.broadcasted_iota(jnp.int32, (S5_NC, S5_NP), 0) // SSM_GROUP
                      == lax.broadcasted_iota(jnp.int32, (S5_NC, S5_NP), 1) // SSM_STATE)
        bdiag = lambda x: jnp.where(same_group, jnp.concatenate([x] * S5_GT, axis=0), 0.0)
        abr8 = jnp.zeros((SUBLANES, S5_NP), F32)
        abi8 = jnp.zeros((SUBLANES, S5_NP), F32)
        for k in range(tps):
            cols = slice(k * S5_NP, (k + 1) * S5_NP)
            ar, ai = are_ref[:, cols], aim_ref[:, cols]
            dt = jnp.exp(ldt_ref[:, cols])
            decay = jnp.exp(dt * ar)
            abr, abi = decay * jnp.cos(dt * ai), decay * jnp.sin(dt * ai)
            den = ar * ar + ai * ai
            zr = ((abr - 1.0) * ar + abi * ai) / den
            zi = (abi * ar - (abr - 1.0) * ai) / den
            abr8 = jnp.where(row_tile == k, abr, abr8)
            abi8 = jnp.where(row_tile == k, abi, abi8)
            br, bi = bdiag(bre_ref[k]), bdiag(bim_ref[k])
            bbar_scr[k, :, 0:S5_NP] = (zr * br - zi * bi).astype(BF16)
            bbar_scr[k, :, S5_NP:2 * S5_NP] = (zr * bi + zi * br).astype(BF16)
            cmat_scr[k, :, 0:S5_NP] = bdiag(cre_ref[k]).astype(BF16)
            cmat_scr[k, :, S5_NP:2 * S5_NP] = (-bdiag(cim_ref[k])).astype(BF16)
        abar_scr[0:SUBLANES, :] = abr8
        abar_scr[SUBLANES:2 * SUBLANES, :] = abi8
        st_scr[...] = jnp.zeros_like(st_scr)

    for k in range(tps):
        cols = slice(k * S5_NC, (k + 1) * S5_NC)
        ub = jnp.concatenate([u_ref[b, :, cols] for b in range(bsz)], axis=0).astype(BF16)
        for s2 in range(S5_SLABS // 2):
            bu = jnp.dot(ub, bbar_scr[k, :, s2 * 2 * LANES:(s2 + 1) * 2 * LANES], preferred_element_type=F32)
            for b in range(bsz):
                for e in range(2):
                    h_scr[2 * s2 + e, pl.ds(k * bsz + b, tc, stride=SUBLANES), :] = bu[b * tc:(b + 1) * tc,
                                                                                      e * LANES:(e + 1) * LANES]

    ar = abar_scr[0:SUBLANES, :]
    ai = abar_scr[SUBLANES:2 * SUBLANES, :]

    def step(t, carry):
        hr, hi = carry
        r0 = pl.multiple_of(t * SUBLANES, SUBLANES)
        bur = jnp.concatenate([h_scr[s, pl.ds(r0, SUBLANES), :] for s in range(half)], axis=1)
        bui = jnp.concatenate([h_scr[half + s, pl.ds(r0, SUBLANES), :] for s in range(half)], axis=1)
        nhr = ar * hr - ai * hi + bur
        nhi = ar * hi + ai * hr + bui
        for s in range(half):
            h_scr[s, pl.ds(r0, SUBLANES), :] = nhr[:, s * LANES:(s + 1) * LANES]
            h_scr[half + s, pl.ds(r0, SUBLANES), :] = nhi[:, s * LANES:(s + 1) * LANES]
        return nhr, nhi

    hr, hi = lax.fori_loop(0, tc, step, (st_scr[0:SUBLANES, :], st_scr[SUBLANES:2 * SUBLANES, :]), unroll=2)
    st_scr[0:SUBLANES, :] = hr
    st_scr[SUBLANES:2 * SUBLANES, :] = hi

    for k in range(tps):
        cols = slice(k * S5_NC, (k + 1) * S5_NC)

        hb = jnp.concatenate(
            [jnp.concatenate([h_scr[s, pl.ds(k * bsz + b, tc, stride=SUBLANES), :] for s in range(S5_SLABS)], axis=1)
             for b in range(bsz)], axis=0).astype(BF16)
        ch = lax.dot_general(hb, cmat_scr[k], _NT, preferred_element_type=F32)
        for b in range(bsz):
            y = ch[b * tc:(b + 1) * tc, :] + d_ref[:, cols] * u_ref[b, :, cols]
            y_ref[b, :, cols] = jax.nn.gelu(y)


def _s5_mixer(u3, a_re, a_im, log_dt, b_re, b_im, c_re, c_im, d_skip, tc=256):
    bsz, seq, _ = u3.shape
    assert SUBLANES % bsz == 0
    tps = min(SUBLANES // bsz, S5_TILES)
    assert tps * bsz == SUBLANES
    tc = min(tc, seq)

    def per_tile_b(b):
        return (b.reshape(S5_TILES, S5_GT, SSM_STATE, SSM_GROUP).transpose(0, 3, 1, 2)
                .reshape(S5_TILES, SSM_GROUP, S5_NP))

    def per_tile_c(c):
        return (c.reshape(S5_TILES, S5_GT, SSM_GROUP, SSM_STATE).transpose(0, 2, 1, 3)
                .reshape(S5_TILES, SSM_GROUP, S5_NP))

    flat = lambda a: a.reshape(1, SSM_GROUPS * SSM_STATE)
    ldt = jnp.repeat(log_dt, SSM_STATE).reshape(1, SSM_GROUPS * SSM_STATE)
    vec_spec = pl.BlockSpec((1, tps * S5_NP), lambda g, t: (0, g))
    bc_spec = pl.BlockSpec((tps, SSM_GROUP, S5_NP), lambda g, t: (g, 0, 0))
    return pl.pallas_call(
        functools.partial(_s5_body, tc=tc, bsz=bsz, tps=tps),
        out_shape=jax.ShapeDtypeStruct((bsz, seq, SSM_WIDTH), F32),
        grid=(S5_TILES // tps, seq // tc),
        in_specs=[
            pl.BlockSpec((bsz, tc, tps * S5_NC), lambda g, t: (0, t, g)),
            vec_spec, vec_spec, vec_spec, bc_spec, bc_spec, bc_spec, bc_spec,
            pl.BlockSpec((1, tps * S5_NC), lambda g, t: (0, g)),
        ],
        out_specs=pl.BlockSpec((bsz, tc, tps * S5_NC), lambda g, t: (0, t, g)),
        scratch_shapes=[
            pltpu.VMEM((2 * SUBLANES, S5_NP), F32),
            pltpu.VMEM((tps, S5_NC, 2 * S5_NP), BF16),
            pltpu.VMEM((tps, S5_NC, 2 * S5_NP), BF16),
            pltpu.VMEM((S5_SLABS, tc * SUBLANES, LANES), F32),
            pltpu.VMEM((2 * SUBLANES, S5_NP), F32),
        ],
        compiler_params=_params(("arbitrary", "arbitrary")),
        name="s5_scan",
    )(u3, flat(a_re), flat(a_im), ldt, per_tile_b(b_re), per_tile_b(b_im), per_tile_c(c_re), per_tile_c(c_im),
      d_skip.reshape(1, SSM_WIDTH))


def _glu_body(y_ref, w_ref, b_ref, o_ref, ybf_scr, *, tn):
    j = pl.program_id(1)

    @pl.when(j == 0)
    def _():
        ybf_scr[...] = y_ref[...].astype(BF16)

    z = jnp.dot(ybf_scr[...], w_ref[...], preferred_element_type=F32) + b_ref[...]
    yt = y_ref[:, pl.ds(pl.multiple_of(j * tn, tn), tn)]
    o_ref[...] = (yt * jax.nn.sigmoid(z)).astype(o_ref.dtype)


def _glu(y, w_bf, bias, tm=1024, tn=1024):
    m, k = y.shape
    tm = min(tm, m)
    return pl.pallas_call(
        functools.partial(_glu_body, tn=tn),
        out_shape=jax.ShapeDtypeStruct((m, k), BF16),
        grid=(m // tm, k // tn),
        in_specs=[pl.BlockSpec((tm, k), lambda i, j: (i, 0)),
                  pl.BlockSpec((k, tn), lambda i, j: (0, j)),
                  pl.BlockSpec((1, tn), lambda i, j: (0, j))],
        out_specs=pl.BlockSpec((tm, tn), lambda i, j: (i, j)),
        scratch_shapes=[pltpu.VMEM((tm, k), BF16)],
        compiler_params=_params(("parallel", "arbitrary")),
        name="s5_glu",
    )(y, w_bf, bias.reshape(1, k))


def _compress_body(x_ref, pe_ref, w1_ref, w2_ref, o_ref, *, ncmp):
    half = L_CMP // 2
    acc_a = jnp.zeros((ncmp, HEAD_DIM), F32)
    acc_b = jnp.zeros((ncmp, HEAD_DIM), F32)
    for r in range(half):
        xr = x_ref[pl.ds(r, ncmp, stride=STRIDE_CMP), :]
        xa = (xr + pe_ref[r:r + 1, :]).astype(BF16)
        xb = (xr + pe_ref[half + r:half + r + 1, :]).astype(BF16)
        acc_a += jnp.dot(xa, w1_ref[r * HEAD_DIM:(r + 1) * HEAD_DIM, :], preferred_element_type=F32)
        acc_b += jnp.dot(xb, w1_ref[(half + r) * HEAD_DIM:(half + r + 1) * HEAD_DIM, :],
                         preferred_element_type=F32)
    pre = acc_a + jnp.concatenate([acc_b[1:], acc_b[:1]], axis=0)
    o_ref[...] = jnp.dot(jax.nn.gelu(pre).astype(BF16), w2_ref[...], preferred_element_type=F32).astype(o_ref.dtype)


def _compress(proj_f32, col0, pe, w1_bf, w2_bf, bsz, seq):
    ncmp = seq // STRIDE_CMP
    cb = col0 // HEAD_DIM
    return pl.pallas_call(
        functools.partial(_compress_body, ncmp=ncmp),
        out_shape=jax.ShapeDtypeStruct((bsz, 2, N_KV, ncmp, HEAD_DIM), BF16),
        grid=(bsz, 2, N_KV),
        in_specs=[pl.BlockSpec((seq, HEAD_DIM), lambda b, w, g: (b, cb + w * N_KV + g)),
                  pl.BlockSpec((None, L_CMP, HEAD_DIM), lambda b, w, g: (w, 0, 0)),
                  pl.BlockSpec((None, L_CMP * HEAD_DIM, HEAD_DIM), lambda b, w, g: (w, 0, 0)),
                  pl.BlockSpec((None, HEAD_DIM, HEAD_DIM), lambda b, w, g: (w, 0, 0))],
        out_specs=pl.BlockSpec((None, None, None, ncmp, HEAD_DIM), lambda b, w, g: (b, w, g, 0, 0)),
        compiler_params=_params(("parallel", "parallel", "parallel")),
        name="nsa_compress",
    )(proj_f32, pe, w1_bf, w2_bf)


def _cmp_attn_body(q_ref, kc_ref, vc_ref, o_ref, bias_ref, *, tq, ncmp, nblk, ntop):
    qi = pl.program_id(2)
    t0 = qi * tq
    tpos = t0 + lax.broadcasted_iota(jnp.int32, (tq, ncmp), 0)
    blk_end = lax.broadcasted_iota(jnp.int32, (tq, ncmp), 1) * STRIDE_CMP + (L_CMP - 1)
    ok = blk_end <= tpos
    okf = ok.astype(F32)
    kc = kc_ref[...]
    vc = vc_ref[...]
    psum = jnp.zeros((tq, ncmp), F32)
    for h in range(HPG):
        qh = q_ref[:, h * HEAD_DIM:(h + 1) * HEAD_DIM]
        s = lax.dot_general(qh, kc, (((1,), (1,)), ((), ())), preferred_element_type=F32)
        s = jnp.where(ok, s, NEG_INF)
        e = jnp.exp2(s - jnp.max(s, axis=-1, keepdims=True))
        p = e / jnp.sum(e, axis=-1, keepdims=True) * okf
        o_ref[:, h * HEAD_DIM:(h + 1) * HEAD_DIM] = jnp.dot(p.astype(BF16), vc, preferred_element_type=F32)
        psum = psum + p

    jj = lax.broadcasted_iota(jnp.int32, (nblk, ncmp), 0)
    nn = lax.broadcasted_iota(jnp.int32, (nblk, ncmp), 1)
    ov = ((nn * STRIDE_CMP < (jj + 1) * L_SEL) & (nn * STRIDE_CMP + L_CMP > jj * L_SEL)).astype(BF16)
    p_hi = psum.astype(BF16)
    p_lo = (psum - p_hi.astype(F32)).astype(BF16)
    nt = (((1,), (1,)), ((), ()))
    imp = (lax.dot_general(ov, p_hi, nt, preferred_element_type=F32)
           + lax.dot_general(ov, p_lo, nt, preferred_element_type=F32))

    jb = lax.broadcasted_iota(jnp.int32, (nblk, tq), 0)
    tt = t0 + lax.broadcasted_iota(jnp.int32, (nblk, tq), 1)
    cur = tt // L_SEL
    allowed = jb * L_SEL <= tt
    forced = (jb == 0) | (jb == cur) | (jb == cur - 1)
    score = jnp.where(forced, FORCE_SCORE, jnp.where(allowed, imp, NEG_INF))
    taken = jnp.zeros((nblk, tq), F32)
    for _ in range(ntop):
        best = jnp.max(score, axis=0, keepdims=True)
        first = jnp.min(jnp.where(score == best, jb, nblk), axis=0, keepdims=True)
        pick = jb == first
        taken = jnp.where(pick, 1.0, taken)
        score = jnp.where(pick, TAKEN_SCORE, score)
    bias_t = jnp.where(taken > 0.5, 0.0, NEG_INF)
    bias_t = jnp.concatenate([bias_t, jnp.zeros((LANES - nblk, tq), F32)], axis=0)
    bias_ref[...] = bias_t.T.astype(bias_ref.dtype)


def _cmp_attn(qkv, kcv, bsz, seq, tq=1024):
    tq = min(tq, seq)
    ncmp = seq // STRIDE_CMP
    nblk = seq // L_SEL
    ntop = min(N_SEL, nblk)
    nq = seq // tq
    return pl.pallas_call(
        functools.partial(_cmp_attn_body, tq=tq, ncmp=ncmp, nblk=nblk, ntop=ntop),
        out_shape=(jax.ShapeDtypeStruct((bsz * seq, ATT_WIDTH), F32),
                   jax.ShapeDtypeStruct((bsz, N_KV, seq, LANES), BF16)),
        grid=(bsz, N_KV, nq),
        in_specs=[pl.BlockSpec((tq, HPG * HEAD_DIM), lambda b, g, i: (b * nq + i, g)),
                  pl.BlockSpec((None, None, None, ncmp, HEAD_DIM), lambda b, g, i: (b, 0, g, 0, 0)),
                  pl.BlockSpec((None, None, None, ncmp, HEAD_DIM), lambda b, g, i: (b, 1, g, 0, 0))],
        out_specs=(pl.BlockSpec((tq, HPG * HEAD_DIM), lambda b, g, i: (b * nq + i, g)),
                   pl.BlockSpec((None, None, tq, LANES), lambda b, g, i: (b, g, i, 0))),
        compiler_params=_params(("parallel", "parallel", "parallel")),
        name="nsa_cmp_attn_topk",
    )(qkv, kcv, kcv)


def _sel_attn_body(q_ref, k_ref, v_ref, bias_ref, o_ref, qa_scr, m_scr, acc_scr, s_scr, p_scr, alpha_scr,
                   *, tq, rb):
    qi = pl.program_id(2)
    tk = tq
    m_scr[...] = jnp.full_like(m_scr, NEG_INF)
    acc_scr[...] = jnp.zeros_like(acc_scr)
    for h in range(HPG):
        qa_scr[h, :, 0:HEAD_DIM] = q_ref[:, h * HEAD_DIM:(h + 1) * HEAD_DIM]
        qa_scr[h, :, HEAD_DIM:2 * HEAD_DIM] = bias_ref[...]
    nt = (((1,), (1,)), ((), ()))

    def key_tile(k0, nkeys):
        key_blk = k0 // L_SEL + lax.broadcasted_iota(jnp.int32, (nkeys, LANES), 0) // L_SEL
        onehot = jnp.where(lax.broadcasted_iota(jnp.int32, (nkeys, LANES), 1) == key_blk, 1.0, 0.0).astype(BF16)
        return jnp.concatenate([k_ref[pl.ds(k0, nkeys), :], onehot], axis=1)

    def scores(h, k_aug, r0, r1, nk):
        s_scr[h, r0:r1, 0:nk] = lax.dot_general(qa_scr[h, r0:r1, :], k_aug[0:nk, :], nt,
                                                preferred_element_type=F32)

    def softmax(h, r0, r1, nk, d0):
        for r in range(r0 // rb, r1 // rb):
            rows = slice(r * rb, (r + 1) * rb)
            pieces = []
            if d0 > 0:
                pieces.append(s_scr[h, rows, 0:d0])
            if d0 < nk:
                col = lax.broadcasted_iota(jnp.int32, (rb, nk - d0), 1)
                row = lax.broadcasted_iota(jnp.int32, (rb, nk - d0), 0) + r * rb
                pieces.append(jnp.where(col <= row, s_scr[h, rows, d0:nk], NEG_INF))
            m_prev = m_scr[h, rows, :]
            m_next = m_prev
            for s in pieces:
                m_next = jnp.maximum(m_next, jnp.max(s, axis=-1, keepdims=True))
            c0 = 0
            for s in pieces:
                w = s.shape[1]
                p_scr[h, rows, c0:c0 + w] = jnp.exp2(s - _lane_tile(m_next, w // LANES)).astype(BF16)
                c0 += w
            alpha_scr[h, rows, :] = jnp.exp2(m_prev - m_next)
            m_scr[h, rows, :] = m_next

    def values(h, v_aug, r0, r1, nk):
        acc_scr[h, r0:r1, :] = (_lane_tile(alpha_scr[h, r0:r1, :], 2) * acc_scr[h, r0:r1, :]
                                + jnp.dot(p_scr[h, r0:r1, 0:nk], v_aug[0:nk, :], preferred_element_type=F32))

    def tile(kt, nkeys, diag_last):
        k0 = pl.multiple_of(kt * tk, tk)
        k_aug = key_tile(k0, nkeys)
        v_aug = jnp.concatenate([v_ref[pl.ds(k0, nkeys), :], jnp.ones((nkeys, LANES), BF16)], axis=1)
        if diag_last:
            d0 = nkeys - tk
            parts = [(0, tq // 2, nkeys - tk // 2), (tq // 2, tq, nkeys)]
        else:
            d0 = nkeys
            parts = [(0, tq, nkeys)]
        for h in range(HPG):
            for r0, r1, nk in parts:
                scores(h, k_aug, r0, r1, nk)
        for h in range(HPG):
            for r0, r1, nk in parts:
                softmax(h, r0, r1, nk, d0)
        for h in range(HPG):
            for r0, r1, nk in parts:
                values(h, v_aug, r0, r1, nk)

    def tile_group(j, carry):
        tile(SEL_GROUP * j, SEL_GROUP * tk, False)
        return carry

    n_groups = qi // SEL_GROUP
    lax.fori_loop(0, n_groups, tile_group, 0)
    for run in range(1, SEL_GROUP + 1):
        @pl.when(qi % SEL_GROUP == run - 1)
        def _(run=run):
            tile(n_groups * SEL_GROUP, run * tk, True)

    for h in range(HPG):
        o_ref[:, h * HEAD_DIM:(h + 1) * HEAD_DIM] = (acc_scr[h, :, 0:HEAD_DIM]
                                                     / acc_scr[h, :, HEAD_DIM:2 * HEAD_DIM])


def _sel_attn(qkv, bias, bsz, seq, tq=SEL_TQ, rb=SEL_RB):
    tq = min(tq, seq)
    rb = min(rb, tq)
    nq = seq // tq
    kcol = ATT_WIDTH // HEAD_DIM
    return pl.pallas_call(
        functools.partial(_sel_attn_body, tq=tq, rb=rb),
        out_shape=jax.ShapeDtypeStruct((bsz * seq, ATT_WIDTH), F32),
        grid=(bsz, N_KV, nq),
        in_specs=[pl.BlockSpec((tq, HPG * HEAD_DIM), lambda b, g, i: (b * nq + i, g)),
                  pl.BlockSpec((seq, HEAD_DIM), lambda b, g, i: (b, kcol + g)),
                  pl.BlockSpec((seq, HEAD_DIM), lambda b, g, i: (b, kcol + N_KV + g)),
                  pl.BlockSpec((None, None, tq, LANES), lambda b, g, i: (b, g, i, 0))],
        out_specs=pl.BlockSpec((tq, HPG * HEAD_DIM), lambda b, g, i: (b * nq + i, g)),
        scratch_shapes=[pltpu.VMEM((HPG, tq, 2 * HEAD_DIM), BF16),
                        pltpu.VMEM((HPG, tq, LANES), F32),
                        pltpu.VMEM((HPG, tq, 2 * HEAD_DIM), F32),
                        pltpu.VMEM((HPG, tq, SEL_GROUP * tq), F32),
                        pltpu.VMEM((HPG, tq, SEL_GROUP * tq), BF16),
                        pltpu.VMEM((HPG, tq, LANES), F32)],
        compiler_params=_params(("parallel", "parallel", "arbitrary")),
        name="nsa_sel_attn",
    )(qkv, qkv, qkv, bias)


def _win_attn_body(q_ref, kc_ref, kp_ref, vc_ref, vp_ref, ocmp_ref, osel_ref, gate_ref, o_ref, *, tq):
    qi = pl.program_id(2)
    nt = (((1,), (1,)), ((), ()))
    ones = jnp.ones((tq, LANES), BF16)
    vc_aug = jnp.concatenate([vc_ref[...], ones], axis=1)
    vp_aug = jnp.concatenate([vp_ref[...], ones], axis=1)
    for r in range(tq // ATT_ROWS):
        rows = slice(r * ATT_ROWS, (r + 1) * ATT_ROWS)
        n_cur, p_lo = (r + 1) * ATT_ROWS, r * ATT_ROWS
        row_c = lax.broadcasted_iota(jnp.int32, (ATT_ROWS, n_cur), 0) + r * ATT_ROWS
        ok_cur = lax.broadcasted_iota(jnp.int32, (ATT_ROWS, n_cur), 1) <= row_c
        row_p = lax.broadcasted_iota(jnp.int32, (ATT_ROWS, tq - p_lo), 0) + r * ATT_ROWS
        ok_prev = (lax.broadcasted_iota(jnp.int32, (ATT_ROWS, tq - p_lo), 1) + p_lo > row_p) & (qi > 0)
        gates = jax.nn.sigmoid(gate_ref[rows, :])
        for h in range(HPG):
            sl = slice(h * HEAD_DIM, (h + 1) * HEAD_DIM)
            qh = q_ref[rows, sl]
            s1 = jnp.where(ok_cur, lax.dot_general(qh, kc_ref[0:n_cur, :], nt, preferred_element_type=F32), NEG_INF)
            s0 = jnp.where(ok_prev, lax.dot_general(qh, kp_ref[p_lo:tq, :], nt, preferred_element_type=F32), NEG_INF)
            m = jnp.maximum(jnp.max(s1, axis=-1, keepdims=True), jnp.max(s0, axis=-1, keepdims=True))
            p1 = jnp.exp2(s1 - m).astype(BF16)
            p0 = jnp.exp2(s0 - m).astype(BF16)
            acc = (jnp.dot(p1, vc_aug[0:n_cur, :], preferred_element_type=F32)
                   + jnp.dot(p0, vp_aug[p_lo:tq, :], preferred_element_type=F32))
            ow = acc[:, 0:HEAD_DIM] / acc[:, HEAD_DIM:2 * HEAD_DIM]
            g_cmp = gates[:, h:h + 1]
            g_sel = gates[:, HPG + h:HPG + h + 1]
            g_win = gates[:, 2 * HPG + h:2 * HPG + h + 1]
            o_ref[rows, sl] = (g_cmp * ocmp_ref[rows, sl] + g_sel * osel_ref[rows, sl]
                               + g_win * ow).astype(o_ref.dtype)


def _win_attn_combine(qkv, o_cmp, o_sel, gates, bsz, seq):
    tq = WINDOW
    assert seq % tq == 0
    nq = seq // tq
    kcol = ATT_WIDTH // HEAD_DIM + 2 * N_KV
    qspec = pl.BlockSpec((tq, HPG * HEAD_DIM), lambda b, g, i: (b * nq + i, g))
    cur = lambda c: pl.BlockSpec((tq, HEAD_DIM), lambda b, g, i: (b * nq + i, kcol + c * N_KV + g))
    prev = lambda c: pl.BlockSpec((tq, HEAD_DIM),
                                  lambda b, g, i: (b * nq + jnp.maximum(i - 1, 0), kcol + c * N_KV + g))
    return pl.pallas_call(
        functools.partial(_win_attn_body, tq=tq),
        out_shape=jax.ShapeDtypeStruct((bsz * seq, ATT_WIDTH), BF16),
        grid=(bsz, N_KV, nq),
        in_specs=[qspec, cur(0), prev(0), cur(1), prev(1), qspec, qspec,
                  pl.BlockSpec((None, None, tq, LANES), lambda b, g, i: (b, g, i, 0))],
        out_specs=qspec,
        compiler_params=_params(("parallel", "parallel", "parallel")),
        name="nsa_win_attn_gate",
    )(qkv, qkv, qkv, qkv, qkv, o_cmp, o_sel, gates)


def _merge_body(hn_ref, ya_ref, yb_ref, wga_ref, wgb_ref, wa_ref, wb_ref, o_ref):
    hn = hn_ref[...]
    ga = jnp.dot(hn, wga_ref[...], preferred_element_type=F32)
    gb = jnp.dot(hn, wgb_ref[...], preferred_element_type=F32)
    pa = jnp.dot(ya_ref[...], wa_ref[...], preferred_element_type=F32)
    pb = jnp.dot(yb_ref[...], wb_ref[...], preferred_element_type=F32)
    o_ref[...] = (jax.nn.sigmoid(ga) * pa + jax.nn.sigmoid(gb) * pb).astype(o_ref.dtype)


def _merge(hn, ya, yb, wg, wa, wb, tm=512, tn=512):
    m, d = hn.shape
    ka = ya.shape[1]
    tm = min(tm, m)
    row = lambda k: pl.BlockSpec((tm, k), lambda i, j: (i, 0))
    col = lambda k: pl.BlockSpec((k, tn), lambda i, j: (0, j))
    col_b = pl.BlockSpec((d, tn), lambda i, j: (0, d // tn + j))
    return pl.pallas_call(
        _merge_body,
        out_shape=jax.ShapeDtypeStruct((m, d), BF16),
        grid=(m // tm, d // tn),
        in_specs=[row(d), row(ka), row(ka), col(d), col_b, col(ka), col(ka)],
        out_specs=pl.BlockSpec((tm, tn), lambda i, j: (i, j)),
        compiler_params=_params(("parallel", "arbitrary")),
        name="mixer_merge",
    )(hn, ya, yb, wg, wg, wa, wb)


def _rms(x, g):
    return x * lax.rsqrt(jnp.mean(x * x, axis=-1, keepdims=True) + RMS_EPS) * g


def _res_norm2_body(raw_ref, x_ref, gpost_ref, gpre_ref, h_ref, hn_ref):
    h = x_ref[...] + _rms(raw_ref[...].astype(F32), gpost_ref[...])
    h_ref[...] = h
    hn_ref[...] = _rms(h, gpre_ref[...]).astype(hn_ref.dtype)


def _res_norm2(raw, x, g_post, g_pre, tm=256):
    m, d = x.shape
    tm = min(tm, m)
    rows = pl.BlockSpec((tm, d), lambda i: (i, 0))
    vec = pl.BlockSpec((1, d), lambda i: (0, 0))
    return pl.pallas_call(
        _res_norm2_body,
        out_shape=(jax.ShapeDtypeStruct((m, d), F32), jax.ShapeDtypeStruct((m, d), BF16)),
        grid=(m // tm,),
        in_specs=[rows, rows, vec, vec],
        out_specs=(rows, rows),
        compiler_params=_params(("parallel",)),
        name="residual_norm_prenorm",
    )(raw, x, g_post.reshape(1, d), g_pre.reshape(1, d))


def _res_norm_body(raw_ref, x_ref, g_ref, o_ref):
    o_ref[...] = x_ref[...] + _rms(raw_ref[...].astype(F32), g_ref[...])


def _res_norm(raw, x, g, tm=256):
    m, d = x.shape
    tm = min(tm, m)
    rows = pl.BlockSpec((tm, d), lambda i: (i, 0))
    return pl.pallas_call(
        _res_norm_body,
        out_shape=jax.ShapeDtypeStruct((m, d), F32),
        grid=(m // tm,),
        in_specs=[rows, rows, pl.BlockSpec((1, d), lambda i: (0, 0))],
        out_specs=rows,
        compiler_params=_params(("parallel",)),
        name="residual_norm",
    )(raw, x, g.reshape(1, d))


def _swiglu_body(a_ref, wg_ref, wu_ref, o_ref, wg_scr, wu_scr):
    @pl.when(pl.program_id(1) == 0)
    def _():
        wg_scr[...] = wg_ref[...].astype(BF16)
        wu_scr[...] = wu_ref[...].astype(BF16)

    a = a_ref[...]
    g = jnp.dot(a, wg_scr[...], preferred_element_type=F32)
    u = jnp.dot(a, wu_scr[...], preferred_element_type=F32)
    o_ref[...] = (g * jax.nn.sigmoid(g) * u).astype(o_ref.dtype)


def _swiglu(a, wg, wu, tm=2048, tn=256):
    m, k = a.shape
    n = wg.shape[1]
    tm = min(tm, m)
    assert n % tn == 0
    vmem = 2 * tm * k * 2 + 2 * 2 * k * tn * 4 + 2 * tm * tn * 2 + 2 * k * tn * 2
    assert vmem + V7X_VMEM_SLACK_BYTES <= V7X_VMEM_BYTES
    return pl.pallas_call(
        _swiglu_body,
        out_shape=jax.ShapeDtypeStruct((m, n), BF16),
        grid=(n // tn, m // tm),
        in_specs=[pl.BlockSpec((tm, k), lambda j, i: (i, 0)),
                  pl.BlockSpec((k, tn), lambda j, i: (0, j)),
                  pl.BlockSpec((k, tn), lambda j, i: (0, j))],
        out_specs=pl.BlockSpec((tm, tn), lambda j, i: (i, j)),
        scratch_shapes=[pltpu.VMEM((k, tn), BF16), pltpu.VMEM((k, tn), BF16)],
        compiler_params=_params(("arbitrary", "arbitrary"), vmem=vmem + V7X_VMEM_SLACK_BYTES),
        name="ffn_swiglu",
    )(a, wg, wu)


def kernel(x, norm_mix_pre, w_in, ssm_a_re, ssm_a_im, ssm_log_dt, ssm_b_re, ssm_b_im, ssm_c_re, ssm_c_im, ssm_d, ssm_w_glu, ssm_b_glu, cmp_pe_k, cmp_w1_k, cmp_w2_k, cmp_pe_v, cmp_w1_v, cmp_w2_v, w_proj_a, w_proj_b, w_out, norm_mix_post, norm_ffn_pre, w_ffn_gate, w_ffn_up, w_ffn_down, norm_ffn_post):
    bsz, seq, d = x.shape
    m = bsz * seq
    depth = w_in.shape[0]
    h = x.reshape(m, d)
    o_q, o_kvc, o_kv, o_gn = SSM_WIDTH, SSM_WIDTH + ATT_WIDTH, SSM_WIDTH + ATT_WIDTH + 2 * KV_WIDTH, 7168
    o_ga = o_gn + 3 * N_HEADS
    o_gb = o_ga + D_MODEL
    bf = lambda a: a.astype(BF16)
    tn_in = 512
    nb_u, nb_q = SSM_WIDTH // tn_in, ATT_WIDTH // tn_in
    qk_scale = HEAD_DIM ** -0.5 * LOG2E
    for l in range(depth):
        wt = jnp.swapaxes(w_in[l], 0, 1)
        w_g = _rows_transposed_cast(wt, o_ga, 2 * D_MODEL, BF16)

        hn, gn = _rmsnorm_gates(h, norm_mix_pre[l], wt, o_gn // LANES)
        proj_f32 = _matmul_f32w(hn, wt, SSM_WIDTH + 2 * KV_WIDTH, F32, 1024, tn_in, trans_w=True,
                                col_block=lambda j: jnp.where(j < nb_u, j, j - nb_u + o_kvc // tn_in),
                                name="in_proj_f32")
        qkv = _matmul_f32w(hn, wt, ATT_WIDTH + 4 * KV_WIDTH, BF16, 2048, tn_in, trans_w=True,
                           col_block=lambda j: jnp.where(j < nb_q, j + o_q // tn_in, j - nb_q + o_kv // tn_in),
                           scale_fn=lambda j: jnp.where(j < nb_q, qk_scale, 1.0),
                           name="in_proj_bf16")

        y = _s5_mixer(proj_f32.reshape(bsz, seq, -1), ssm_a_re[l], ssm_a_im[l], ssm_log_dt[l],
                      ssm_b_re[l], ssm_b_im[l], ssm_c_re[l], ssm_c_im[l], ssm_d[l])
        y_a = _glu(y.reshape(m, SSM_WIDTH), bf(ssm_w_glu[l]), ssm_b_glu[l])

        pe = jnp.stack([cmp_pe_k[l], cmp_pe_v[l]])
        w1 = bf(jnp.stack([cmp_w1_k[l], cmp_w1_v[l]]))
        w2 = bf(jnp.stack([cmp_w2_k[l], cmp_w2_v[l]]))
        kcv = _compress(proj_f32, SSM_WIDTH, pe, w1, w2, bsz, seq)
        o_cmp, bias = _cmp_attn(qkv, kcv, bsz, seq)
        o_sel = _sel_attn(qkv, bias, bsz, seq)
        gates = gn[:, :3 * N_HEADS].reshape(bsz, seq, 3, N_KV, HPG).transpose(0, 3, 1, 2, 4)
        gates = jnp.pad(gates.reshape(bsz, N_KV, seq, 3 * HPG), ((0, 0), (0, 0), (0, 0), (0, LANES - 3 * HPG)))
        y_b = _win_attn_combine(qkv, o_cmp, o_sel, gates, bsz, seq)

        merged = _merge(hn, y_a, y_b, w_g, bf(w_proj_a[l]), bf(w_proj_b[l]))
        mix = _matmul_f32w(merged, w_out[l], D_MODEL, BF16, 2048, 512, name="out_proj")
        h, hn2 = _res_norm2(mix, h, norm_mix_post[l], norm_ffn_pre[l])

        act = _swiglu(hn2, w_ffn_gate[l], w_ffn_up[l])
        f = _matmul(act, bf(w_ffn_down[l]), BF16, 512, 512, name="ffn_down")
        h = _res_norm(f, h, norm_ffn_post[l])
    return h.reshape(bsz, seq, d)
```

```python
import functools
import math

import jax
import jax.numpy as jnp
from jax import lax
from jax.experimental import pallas as pl
from jax.experimental.pallas import tpu as pltpu

F32 = jnp.float32
BF16 = jnp.bfloat16

D_MODEL = 4096
SSM_WIDTH = 2048
SSM_GROUP = 16
SSM_GROUPS = 128
SSM_STATE = 64
N_HEADS = 16
HEAD_DIM = 128
N_KV = 4
HPG = 4
ATT_WIDTH = 2048
KV_WIDTH = 512
L_CMP = 32
STRIDE_CMP = 16
L_SEL = 64
N_SEL = 16
WINDOW = 512
D_FF = 11008
RMS_EPS = 1e-6
NEG_INF = -1e30
FORCE_SCORE = 1e9
TAKEN_SCORE = -3e38
LOG2E = math.log2(math.e)

V7X_VMEM_BYTES = 64 * 1024 * 1024
V7X_VMEM_LIMIT_BYTES = 56 * 1024 * 1024
V7X_VMEM_SLACK_BYTES = 4 * 1024 * 1024
LANES = 128
SUBLANES = 8

S5_GT = 16
S5_NC = S5_GT * SSM_GROUP
S5_NP = S5_GT * SSM_STATE
S5_TILES = SSM_GROUPS // S5_GT
S5_SLABS = 2 * S5_NP // LANES

ATT_ROWS = 256
SEL_TQ, SEL_RB = 512, 64
SEL_GROUP = 4


def _params(sem, vmem=V7X_VMEM_LIMIT_BYTES):
    return pltpu.CompilerParams(dimension_semantics=sem, vmem_limit_bytes=vmem)


def _lane_tile(x, n):
    return jnp.concatenate([x] * n, axis=1)


def _rmsnorm_gates_body(x_ref, g_ref, w_ref, hn_ref, gn_ref, wbf_scr):
    @pl.when(pl.program_id(0) == 0)
    def _():
        wbf_scr[...] = w_ref[...].astype(BF16)

    x = x_ref[...]
    ms = jnp.mean(x * x, axis=-1, keepdims=True)
    hn = (x * lax.rsqrt(ms + RMS_EPS) * g_ref[...]).astype(BF16)
    hn_ref[...] = hn
    gn_ref[...] = lax.dot_general(hn, wbf_scr[...], (((1,), (1,)), ((), ())), preferred_element_type=F32)


def _rmsnorm_gates(x, gain, wt, feature_block, tm=256):
    m, d = x.shape
    tm = min(tm, m)
    return pl.pallas_call(
        _rmsnorm_gates_body,
        out_shape=(jax.ShapeDtypeStruct((m, d), BF16), jax.ShapeDtypeStruct((m, LANES), F32)),
        grid=(m // tm,),
        in_specs=[pl.BlockSpec((tm, d), lambda i: (i, 0)),
                  pl.BlockSpec((1, d), lambda i: (0, 0)),
                  pl.BlockSpec((LANES, d), lambda i: (feature_block, 0))],
        out_specs=(pl.BlockSpec((tm, d), lambda i: (i, 0)), pl.BlockSpec((tm, LANES), lambda i: (i, 0))),
        scratch_shapes=[pltpu.VMEM((LANES, d), BF16)],
        compiler_params=_params(("arbitrary",)),
        name="rmsnorm_gates",
    )(x, gain.reshape(1, d), wt)


def _mm_body(a_ref, w_ref, o_ref):
    o_ref[...] = jnp.dot(a_ref[...], w_ref[...], preferred_element_type=F32).astype(o_ref.dtype)


def _matmul(a, w, out_dtype, tm, tn, name="matmul"):
    m, k = a.shape
    n = w.shape[1]
    tm, tn = min(tm, m), min(tn, n)
    assert m % tm == 0 and n % tn == 0
    return pl.pallas_call(
        _mm_body,
        out_shape=jax.ShapeDtypeStruct((m, n), out_dtype),
        grid=(m // tm, n // tn),
        in_specs=[pl.BlockSpec((tm, k), lambda i, j: (i, 0)), pl.BlockSpec((k, tn), lambda i, j: (0, j))],
        out_specs=pl.BlockSpec((tm, tn), lambda i, j: (i, j)),
        compiler_params=_params(("parallel", "arbitrary")),
        name=name,
    )(a, w)


_NT = (((1,), (1,)), ((), ()))


def _mm_f32w_body(a_ref, w_ref, o_ref, wbf_scr, *, scale_fn, trans_w):
    @pl.when(pl.program_id(1) == 0)
    def _():
        wbf_scr[...] = w_ref[...].astype(BF16)

    if trans_w:
        acc = lax.dot_general(a_ref[...], wbf_scr[...], _NT, preferred_element_type=F32)
    else:
        acc = jnp.dot(a_ref[...], wbf_scr[...], preferred_element_type=F32)
    if scale_fn is not None:
        acc = acc * scale_fn(pl.program_id(0))
    o_ref[...] = acc.astype(o_ref.dtype)


def _matmul_f32w(a, w, n_out, out_dtype, tm, tn, col_block=None, scale_fn=None, trans_w=False,
                 name="matmul_f32w"):
    m, k = a.shape
    tm = min(tm, m)
    assert m % tm == 0 and n_out % tn == 0
    col_block = col_block or (lambda j: j)
    if trans_w:
        w_spec = pl.BlockSpec((tn, k), lambda j, i: (col_block(j), 0))
        w_scratch = pltpu.VMEM((tn, k), BF16)
    else:
        w_spec = pl.BlockSpec((k, tn), lambda j, i: (0, col_block(j)))
        w_scratch = pltpu.VMEM((k, tn), BF16)
    vmem = 2 * tm * k * 2 + 2 * k * tn * 4 + 2 * tm * tn * jnp.dtype(out_dtype).itemsize + k * tn * 2
    assert vmem + V7X_VMEM_SLACK_BYTES <= V7X_VMEM_BYTES
    return pl.pallas_call(
        functools.partial(_mm_f32w_body, scale_fn=scale_fn, trans_w=trans_w),
        out_shape=jax.ShapeDtypeStruct((m, n_out), out_dtype),
        grid=(n_out // tn, m // tm),
        in_specs=[pl.BlockSpec((tm, k), lambda j, i: (i, 0)), w_spec],
        out_specs=pl.BlockSpec((tm, tn), lambda j, i: (i, j)),
        scratch_shapes=[w_scratch],
        compiler_params=_params(("arbitrary", "arbitrary"), vmem=vmem + V7X_VMEM_SLACK_BYTES),
        name=name,
    )(a, w)


def _rows_transposed_cast_body(w_ref, o_ref):
    o_ref[...] = w_ref[...].T.astype(o_ref.dtype)


def _rows_transposed_cast(wt, row0, n_rows, out_dtype, tr=512):
    k = wt.shape[1]
    assert row0 % (2 * SUBLANES) == 0 and n_rows % tr == 0
    return pl.pallas_call(
        _rows_transposed_cast_body,
        out_shape=jax.ShapeDtypeStruct((k, n_rows), out_dtype),
        grid=(n_rows // tr,),
        in_specs=[pl.BlockSpec((pl.Element(tr), pl.Element(k)),
                               lambda c: (pl.multiple_of(row0 + c * tr, 2 * SUBLANES), 0))],
        out_specs=pl.BlockSpec((k, tr), lambda c: (0, c)),
        compiler_params=_params(("parallel",)),
        name="gate_weight_cast",
    )(wt)


def _s5_body(u_ref, are_ref, aim_ref, ldt_ref, bre_ref, bim_ref, cre_ref, cim_ref, d_ref, y_ref,
             abar_scr, bbar_scr, cmat_scr, h_scr, st_scr, *, tc, bsz, tps):
    ti = pl.program_id(1)
    half = S5_SLABS // 2

    @pl.when(ti == 0)
    def _():
        row_tile = lax.broadcasted_iota(jnp.int32, (SUBLANES, S5_NP), 0) // bsz
        same_group = (lax.broadcasted_iota(jnp.int32, (S5_NC, S5_NP), 0) // SSM_GROUP
                      == lax.broadcasted_iota(jnp.int32, (S5_NC, S5_NP), 1) // SSM_STATE)
        bdiag = lambda x: jnp.where(same_group, jnp.concatenate([x] * S5_GT, axis=0), 0.0)
        abr8 = jnp.zeros((SUBLANES, S5_NP), F32)
        abi8 = jnp.zeros((SUBLANES, S5_NP), F32)
        for k in range(tps):
            cols = slice(k * S5_NP, (k + 1) * S5_NP)
            ar, ai = are_ref[:, cols], aim_ref[:, cols]
            dt = jnp.exp(ldt_ref[:, cols])
            decay = jnp.exp(dt * ar)
            abr, abi = decay * jnp.cos(dt * ai), decay * jnp.sin(dt * ai)
            den = ar * ar + ai * ai
            zr = ((abr - 1.0) * ar + abi * ai) / den
            zi = (abi * ar - (abr - 1.0) * ai) / den
            abr8 = jnp.where(row_tile == k, abr, abr8)
            abi8 = jnp.where(row_tile == k, abi, abi8)
            br, bi = bdiag(bre_ref[k]), bdiag(bim_ref[k])
            bbar_scr[k, :, 0:S5_NP] = (zr * br - zi * bi).astype(BF16)
            bbar_scr[k, :, S5_NP:2 * S5_NP] = (zr * bi + zi * br).astype(BF16)
            cmat_scr[k, :, 0:S5_NP] = bdiag(cre_ref[k]).astype(BF16)
            cmat_scr[k, :, S5_NP:2 * S5_NP] = (-bdiag(cim_ref[k])).astype(BF16)
        abar_scr[0:SUBLANES, :] = abr8
        abar_scr[SUBLANES:2 * SUBLANES, :] = abi8
        st_scr[...] = jnp.zeros_like(st_scr)

    for k in range(tps):
        cols = slice(k * S5_NC, (k + 1) * S5_NC)
        ub = jnp.concatenate([u_ref[b, :, cols] for b in range(bsz)], axis=0).astype(BF16)
        for s2 in range(S5_SLABS // 2):
            bu = jnp.dot(ub, bbar_scr[k, :, s2 * 2 * LANES:(s2 + 1) * 2 * LANES], preferred_element_type=F32)
            for b in range(bsz):
                for e in range(2):
                    h_scr[2 * s2 + e, pl.ds(k * bsz + b, tc, stride=SUBLANES), :] = bu[b * tc:(b + 1) * tc,
                                                                                      e * LANES:(e + 1) * LANES]

    ar = abar_scr[0:SUBLANES, :]
    ai = abar_scr[SUBLANES:2 * SUBLANES, :]

    def step(t, carry):
        hr, hi = carry
        r0 = pl.multiple_of(t * SUBLANES, SUBLANES)
        bur = jnp.concatenate([h_scr[s, pl.ds(r0, SUBLANES), :] for s in range(half)], axis=1)
        bui = jnp.concatenate([h_scr[half + s, pl.ds(r0, SUBLANES), :] for s in range(half)], axis=1)
        nhr = ar * hr - ai * hi + bur
        nhi = ar * hi + ai * hr + bui
        for s in range(half):
            h_scr[s, pl.ds(r0, SUBLANES), :] = nhr[:, s * LANES:(s + 1) * LANES]
            h_scr[half + s, pl.ds(r0, SUBLANES), :] = nhi[:, s * LANES:(s + 1) * LANES]
        return nhr, nhi

    hr, hi = lax.fori_loop(0, tc, step, (st_scr[0:SUBLANES, :], st_scr[SUBLANES:2 * SUBLANES, :]), unroll=2)
    st_scr[0:SUBLANES, :] = hr
    st_scr[SUBLANES:2 * SUBLANES, :] = hi

    for k in range(tps):
        cols = slice(k * S5_NC, (k + 1) * S5_NC)

        hb = jnp.concatenate(
            [jnp.concatenate([h_scr[s, pl.ds(k * bsz + b, tc, stride=SUBLANES), :] for s in range(S5_SLABS)], axis=1)
             for b in range(bsz)], axis=0).astype(BF16)
        ch = lax.dot_general(hb, cmat_scr[k], _NT, preferred_element_type=F32)
        for b in range(bsz):
            y = ch[b * tc:(b + 1) * tc, :] + d_ref[:, cols] * u_ref[b, :, cols]
            y_ref[b, :, cols] = jax.nn.gelu(y)


def _s5_mixer(u3, a_re, a_im, log_dt, b_re, b_im, c_re, c_im, d_skip, tc=256):
    bsz, seq, _ = u3.shape
    assert SUBLANES % bsz == 0
    tps = min(SUBLANES // bsz, S5_TILES)
    assert tps * bsz == SUBLANES
    tc = min(tc, seq)

    def per_tile_b(b):
        return (b.reshape(S5_TILES, S5_GT, SSM_STATE, SSM_GROUP).transpose(0, 3, 1, 2)
                .reshape(S5_TILES, SSM_GROUP, S5_NP))

    def per_tile_c(c):
        return (c.reshape(S5_TILES, S5_GT, SSM_GROUP, SSM_STATE).transpose(0, 2, 1, 3)
                .reshape(S5_TILES, SSM_GROUP, S5_NP))

    flat = lambda a: a.reshape(1, SSM_GROUPS * SSM_STATE)
    ldt = jnp.repeat(log_dt, SSM_STATE).reshape(1, SSM_GROUPS * SSM_STATE)
    vec_spec = pl.BlockSpec((1, tps * S5_NP), lambda g, t: (0, g))
    bc_spec = pl.BlockSpec((tps, SSM_GROUP, S5_NP), lambda g, t: (g, 0, 0))
    return pl.pallas_call(
        functools.partial(_s5_body, tc=tc, bsz=bsz, tps=tps),
        out_shape=jax.ShapeDtypeStruct((bsz, seq, SSM_WIDTH), F32),
        grid=(S5_TILES // tps, seq // tc),
        in_specs=[
            pl.BlockSpec((bsz, tc, tps * S5_NC), lambda g, t: (0, t, g)),
            vec_spec, vec_spec, vec_spec, bc_spec, bc_spec, bc_spec, bc_spec,
            pl.BlockSpec((1, tps * S5_NC), lambda g, t: (0, g)),
        ],
        out_specs=pl.BlockSpec((bsz, tc, tps * S5_NC), lambda g, t: (0, t, g)),
        scratch_shapes=[
            pltpu.VMEM((2 * SUBLANES, S5_NP), F32),
            pltpu.VMEM((tps, S5_NC, 2 * S5_NP), BF16),
            pltpu.VMEM((tps, S5_NC, 2 * S5_NP), BF16),
            pltpu.VMEM((S5_SLABS, tc * SUBLANES, LANES), F32),
            pltpu.VMEM((2 * SUBLANES, S5_NP), F32),
        ],
        compiler_params=_params(("arbitrary", "arbitrary")),
        name="s5_scan",
    )(u3, flat(a_re), flat(a_im), ldt, per_tile_b(b_re), per_tile_b(b_im), per_tile_c(c_re), per_tile_c(c_im),
      d_skip.reshape(1, SSM_WIDTH))


def _glu_body(y_ref, w_ref, b_ref, o_ref, ybf_scr, *, tn):
    j = pl.program_id(1)

    @pl.when(j == 0)
    def _():
        ybf_scr[...] = y_ref[...].astype(BF16)

    z = jnp.dot(ybf_scr[...], w_ref[...], preferred_element_type=F32) + b_ref[...]
    yt = y_ref[:, pl.ds(pl.multiple_of(j * tn, tn), tn)]
    o_ref[...] = (yt * jax.nn.sigmoid(z)).astype(o_ref.dtype)


def _glu(y, w_bf, bias, tm=1024, tn=1024):
    m, k = y.shape
    tm = min(tm, m)
    return pl.pallas_call(
        functools.partial(_glu_body, tn=tn),
        out_shape=jax.ShapeDtypeStruct((m, k), BF16),
        grid=(m // tm, k // tn),
        in_specs=[pl.BlockSpec((tm, k), lambda i, j: (i, 0)),
                  pl.BlockSpec((k, tn), lambda i, j: (0, j)),
                  pl.BlockSpec((1, tn), lambda i, j: (0, j))],
        out_specs=pl.BlockSpec((tm, tn), lambda i, j: (i, j)),
        scratch_shapes=[pltpu.VMEM((tm, k), BF16)],
        compiler_params=_params(("parallel", "arbitrary")),
        name="s5_glu",
    )(y, w_bf, bias.reshape(1, k))


def _compress_body(x_ref, pe_ref, w1_ref, w2_ref, o_ref, *, ncmp):
    half = L_CMP // 2
    acc_a = jnp.zeros((ncmp, HEAD_DIM), F32)
    acc_b = jnp.zeros((ncmp, HEAD_DIM), F32)
    for r in range(half):
        xr = x_ref[pl.ds(r, ncmp, stride=STRIDE_CMP), :]
        xa = (xr + pe_ref[r:r + 1, :]).astype(BF16)
        xb = (xr + pe_ref[half + r:half + r + 1, :]).astype(BF16)
        acc_a += jnp.dot(xa, w1_ref[r * HEAD_DIM:(r + 1) * HEAD_DIM, :], preferred_element_type=F32)
        acc_b += jnp.dot(xb, w1_ref[(half + r) * HEAD_DIM:(half + r + 1) * HEAD_DIM, :],
                         preferred_element_type=F32)
    pre = acc_a + jnp.concatenate([acc_b[1:], acc_b[:1]], axis=0)
    o_ref[...] = jnp.dot(jax.nn.gelu(pre).astype(BF16), w2_ref[...], preferred_element_type=F32).astype(o_ref.dtype)


def _compress(proj_f32, col0, pe, w1_bf, w2_bf, bsz, seq):
    ncmp = seq // STRIDE_CMP
    cb = col0 // HEAD_DIM
    return pl.pallas_call(
        functools.partial(_compress_body, ncmp=ncmp),
        out_shape=jax.ShapeDtypeStruct((bsz, 2, N_KV, ncmp, HEAD_DIM), BF16),
        grid=(bsz, 2, N_KV),
        in_specs=[pl.BlockSpec((seq, HEAD_DIM), lambda b, w, g: (b, cb + w * N_KV + g)),
                  pl.BlockSpec((None, L_CMP, HEAD_DIM), lambda b, w, g: (w, 0, 0)),
                  pl.BlockSpec((None, L_CMP * HEAD_DIM, HEAD_DIM), lambda b, w, g: (w, 0, 0)),
                  pl.BlockSpec((None, HEAD_DIM, HEAD_DIM), lambda b, w, g: (w, 0, 0))],
        out_specs=pl.BlockSpec((None, None, None, ncmp, HEAD_DIM), lambda b, w, g: (b, w, g, 0, 0)),
        compiler_params=_params(("parallel", "parallel", "parallel")),
        name="nsa_compress",
    )(proj_f32, pe, w1_bf, w2_bf)


def _cmp_attn_body(q_ref, kc_ref, vc_ref, o_ref, bias_ref, *, tq, ncmp, nblk, ntop):
    qi = pl.program_id(2)
    t0 = qi * tq
    tpos = t0 + lax.broadcasted_iota(jnp.int32, (tq, ncmp), 0)
    blk_end = lax.broadcasted_iota(jnp.int32, (tq, ncmp), 1) * STRIDE_CMP + (L_CMP - 1)
    ok = blk_end <= tpos
    okf = ok.astype(F32)
    kc = kc_ref[...]
    vc = vc_ref[...]
    psum = jnp.zeros((tq, ncmp), F32)
    for h in range(HPG):
        qh = q_ref[:, h * HEAD_DIM:(h + 1) * HEAD_DIM]
        s = lax.dot_general(qh, kc, (((1,), (1,)), ((), ())), preferred_element_type=F32)
        s = jnp.where(ok, s, NEG_INF)
        e = jnp.exp2(s - jnp.max(s, axis=-1, keepdims=True))
        p = e / jnp.sum(e, axis=-1, keepdims=True) * okf
        o_ref[:, h * HEAD_DIM:(h + 1) * HEAD_DIM] = jnp.dot(p.astype(BF16), vc, preferred_element_type=F32)
        psum = psum + p

    jj = lax.broadcasted_iota(jnp.int32, (nblk, ncmp), 0)
    nn = lax.broadcasted_iota(jnp.int32, (nblk, ncmp), 1)
    ov = ((nn * STRIDE_CMP < (jj + 1) * L_SEL) & (nn * STRIDE_CMP + L_CMP > jj * L_SEL)).astype(BF16)
    p_hi = psum.astype(BF16)
    p_lo = (psum - p_hi.astype(F32)).astype(BF16)
    nt = (((1,), (1,)), ((), ()))
    imp = (lax.dot_general(ov, p_hi, nt, preferred_element_type=F32)
           + lax.dot_general(ov, p_lo, nt, preferred_element_type=F32))

    jb = lax.broadcasted_iota(jnp.int32, (nblk, tq), 0)
    tt = t0 + lax.broadcasted_iota(jnp.int32, (nblk, tq), 1)
    cur = tt // L_SEL
    allowed = jb * L_SEL <= tt
    forced = (jb == 0) | (jb == cur) | (jb == cur - 1)
    score = jnp.where(forced, FORCE_SCORE, jnp.where(allowed, imp, NEG_INF))
    taken = jnp.zeros((nblk, tq), F32)
    for _ in range(ntop):
        best = jnp.max(score, axis=0, keepdims=True)
        first = jnp.min(jnp.where(score == best, jb, nblk), axis=0, keepdims=True)
        pick = jb == first
        taken = jnp.where(pick, 1.0, taken)
        score = jnp.where(pick, TAKEN_SCORE, score)
    bias_t = jnp.where(taken > 0.5, 0.0, NEG_INF)
    bias_t = jnp.concatenate([bias_t, jnp.zeros((LANES - nblk, tq), F32)], axis=0)
    bias_ref[...] = bias_t.T.astype(bias_ref.dtype)


def _cmp_attn(qkv, kcv, bsz, seq, tq=2048):
    tq = min(tq, seq)
    ncmp = seq // STRIDE_CMP
    nblk = seq // L_SEL
    ntop = min(N_SEL, nblk)
    nq = seq // tq
    return pl.pallas_call(
        functools.partial(_cmp_attn_body, tq=tq, ncmp=ncmp, nblk=nblk, ntop=ntop),
        out_shape=(jax.ShapeDtypeStruct((bsz * seq, ATT_WIDTH), F32),
                   jax.ShapeDtypeStruct((bsz, N_KV, seq, LANES), BF16)),
        grid=(bsz, N_KV, nq),
        in_specs=[pl.BlockSpec((tq, HPG * HEAD_DIM), lambda b, g, i: (b * nq + i, g)),
                  pl.BlockSpec((None, None, None, ncmp, HEAD_DIM), lambda b, g, i: (b, 0, g, 0, 0)),
                  pl.BlockSpec((None, None, None, ncmp, HEAD_DIM), lambda b, g, i: (b, 1, g, 0, 0))],
        out_specs=(pl.BlockSpec((tq, HPG * HEAD_DIM), lambda b, g, i: (b * nq + i, g)),
                   pl.BlockSpec((None, None, tq, LANES), lambda b, g, i: (b, g, i, 0))),
        compiler_params=_params(("parallel", "parallel", "parallel")),
        name="nsa_cmp_attn_topk",
    )(qkv, kcv, kcv)


def _sel_attn_body(q_ref, k_ref, v_ref, bias_ref, o_ref, qa_scr, m_scr, acc_scr, s_scr, p_scr, alpha_scr,
                   *, tq, rb):
    qi = pl.program_id(2)
    tk = tq
    m_scr[...] = jnp.full_like(m_scr, NEG_INF)
    acc_scr[...] = jnp.zeros_like(acc_scr)
    for h in range(HPG):
        qa_scr[h, :, 0:HEAD_DIM] = q_ref[:, h * HEAD_DIM:(h + 1) * HEAD_DIM]
        qa_scr[h, :, HEAD_DIM:2 * HEAD_DIM] = bias_ref[...]
    nt = (((1,), (1,)), ((), ()))

    def key_tile(k0, nkeys):
        key_blk = k0 // L_SEL + lax.broadcasted_iota(jnp.int32, (nkeys, LANES), 0) // L_SEL
        onehot = jnp.where(lax.broadcasted_iota(jnp.int32, (nkeys, LANES), 1) == key_blk, 1.0, 0.0).astype(BF16)
        return jnp.concatenate([k_ref[pl.ds(k0, nkeys), :], onehot], axis=1)

    def scores(h, k_aug, r0, r1, nk):
        s_scr[h, r0:r1, 0:nk] = lax.dot_general(qa_scr[h, r0:r1, :], k_aug[0:nk, :], nt,
                                                preferred_element_type=F32)

    def softmax(h, r0, r1, nk, d0):
        for r in range(r0 // rb, r1 // rb):
            rows = slice(r * rb, (r + 1) * rb)
            pieces = []
            if d0 > 0:
                pieces.append(s_scr[h, rows, 0:d0])
            if d0 < nk:
                col = lax.broadcasted_iota(jnp.int32, (rb, nk - d0), 1)
                row = lax.broadcasted_iota(jnp.int32, (rb, nk - d0), 0) + r * rb
                pieces.append(jnp.where(col <= row, s_scr[h, rows, d0:nk], NEG_INF))
            m_prev = m_scr[h, rows, :]
            m_next = m_prev
            for s in pieces:
                m_next = jnp.maximum(m_next, jnp.max(s, axis=-1, keepdims=True))
            c0 = 0
            for s in pieces:
                w = s.shape[1]
                p_scr[h, rows, c0:c0 + w] = jnp.exp2(s - _lane_tile(m_next, w // LANES)).astype(BF16)
                c0 += w
            alpha_scr[h, rows, :] = jnp.exp2(m_prev - m_next)
            m_scr[h, rows, :] = m_next

    def values(h, v_aug, r0, r1, nk):
        acc_scr[h, r0:r1, :] = (_lane_tile(alpha_scr[h, r0:r1, :], 2) * acc_scr[h, r0:r1, :]
                                + jnp.dot(p_scr[h, r0:r1, 0:nk], v_aug[0:nk, :], preferred_element_type=F32))

    def tile(kt, nkeys, diag_last):
        k0 = pl.multiple_of(kt * tk, tk)
        k_aug = key_tile(k0, nkeys)
        v_aug = jnp.concatenate([v_ref[pl.ds(k0, nkeys), :], jnp.ones((nkeys, LANES), BF16)], axis=1)
        if diag_last:
            d0 = nkeys - tk
            parts = [(0, tq // 2, nkeys - tk // 2), (tq // 2, tq, nkeys)]
        else:
            d0 = nkeys
            parts = [(0, tq, nkeys)]
        for h in range(HPG):
            for r0, r1, nk in parts:
                scores(h, k_aug, r0, r1, nk)
        for h in range(HPG):
            for r0, r1, nk in parts:
                softmax(h, r0, r1, nk, d0)
        for h in range(HPG):
            for r0, r1, nk in parts:
                values(h, v_aug, r0, r1, nk)

    def tile_group(j, carry):
        tile(SEL_GROUP * j, SEL_GROUP * tk, False)
        return carry

    n_groups = qi // SEL_GROUP
    lax.fori_loop(0, n_groups, tile_group, 0)
    for run in range(1, SEL_GROUP + 1):
        @pl.when(qi % SEL_GROUP == run - 1)
        def _(run=run):
            tile(n_groups * SEL_GROUP, run * tk, True)

    for h in range(HPG):
        o_ref[:, h * HEAD_DIM:(h + 1) * HEAD_DIM] = (acc_scr[h, :, 0:HEAD_DIM]
                                                     / acc_scr[h, :, HEAD_DIM:2 * HEAD_DIM])


def _sel_attn(qkv, bias, bsz, seq, tq=SEL_TQ, rb=SEL_RB):
    tq = min(tq, seq)
    rb = min(rb, tq)
    nq = seq // tq
    kcol = ATT_WIDTH // HEAD_DIM
    return pl.pallas_call(
        functools.partial(_sel_attn_body, tq=tq, rb=rb),
        out_shape=jax.ShapeDtypeStruct((bsz * seq, ATT_WIDTH), F32),
        grid=(bsz, N_KV, nq),
        in_specs=[pl.BlockSpec((tq, HPG * HEAD_DIM), lambda b, g, i: (b * nq + i, g)),
                  pl.BlockSpec((seq, HEAD_DIM), lambda b, g, i: (b, kcol + g)),
                  pl.BlockSpec((seq, HEAD_DIM), lambda b, g, i: (b, kcol + N_KV + g)),
                  pl.BlockSpec((None, None, tq, LANES), lambda b, g, i: (b, g, i, 0))],
        out_specs=pl.BlockSpec((tq, HPG * HEAD_DIM), lambda b, g, i: (b * nq + i, g)),
        scratch_shapes=[pltpu.VMEM((HPG, tq, 2 * HEAD_DIM), BF16),
                        pltpu.VMEM((HPG, tq, LANES), F32),
                        pltpu.VMEM((HPG, tq, 2 * HEAD_DIM), F32),
                        pltpu.VMEM((HPG, tq, SEL_GROUP * tq), F32),
                        pltpu.VMEM((HPG, tq, SEL_GROUP * tq), BF16),
                        pltpu.VMEM((HPG, tq, LANES), F32)],
        compiler_params=_params(("parallel", "parallel", "arbitrary")),
        name="nsa_sel_attn",
    )(qkv, qkv, qkv, bias)


def _win_attn_body(q_ref, kc_ref, kp_ref, vc_ref, vp_ref, ocmp_ref, osel_ref, gate_ref, o_ref, *, tq):
    qi = pl.program_id(2)
    nt = (((1,), (1,)), ((), ()))
    ones = jnp.ones((tq, LANES), BF16)
    vc_aug = jnp.concatenate([vc_ref[...], ones], axis=1)
    vp_aug = jnp.concatenate([vp_ref[...], ones], axis=1)
    for r in range(tq // ATT_ROWS):
        rows = slice(r * ATT_ROWS, (r + 1) * ATT_ROWS)
        n_cur, p_lo = (r + 1) * ATT_ROWS, r * ATT_ROWS
        row_c = lax.broadcasted_iota(jnp.int32, (ATT_ROWS, n_cur), 0) + r * ATT_ROWS
        ok_cur = lax.broadcasted_iota(jnp.int32, (ATT_ROWS, n_cur), 1) <= row_c
        row_p = lax.broadcasted_iota(jnp.int32, (ATT_ROWS, tq - p_lo), 0) + r * ATT_ROWS
        ok_prev = (lax.broadcasted_iota(jnp.int32, (ATT_ROWS, tq - p_lo), 1) + p_lo > row_p) & (qi > 0)
        shift = (LANES - HPG * pl.program_id(1)) % LANES
        gates = jax.nn.sigmoid(pltpu.roll(gate_ref[rows, :], shift, 1))
        for h in range(HPG):
            sl = slice(h * HEAD_DIM, (h + 1) * HEAD_DIM)
            qh = q_ref[rows, sl]
            s1 = jnp.where(ok_cur, lax.dot_general(qh, kc_ref[0:n_cur, :], nt, preferred_element_type=F32), NEG_INF)
            s0 = jnp.where(ok_prev, lax.dot_general(qh, kp_ref[p_lo:tq, :], nt, preferred_element_type=F32), NEG_INF)
            m = jnp.maximum(jnp.max(s1, axis=-1, keepdims=True), jnp.max(s0, axis=-1, keepdims=True))
            p1 = jnp.exp2(s1 - m).astype(BF16)
            p0 = jnp.exp2(s0 - m).astype(BF16)
            acc = (jnp.dot(p1, vc_aug[0:n_cur, :], preferred_element_type=F32)
                   + jnp.dot(p0, vp_aug[p_lo:tq, :], preferred_element_type=F32))
            ow = acc[:, 0:HEAD_DIM] / acc[:, HEAD_DIM:2 * HEAD_DIM]
            g_cmp = gates[:, h:h + 1]
            g_sel = gates[:, N_HEADS + h:N_HEADS + h + 1]
            g_win = gates[:, 2 * N_HEADS + h:2 * N_HEADS + h + 1]
            o_ref[rows, sl] = (g_cmp * ocmp_ref[rows, sl] + g_sel * osel_ref[rows, sl]
                               + g_win * ow).astype(o_ref.dtype)


def _win_attn_combine(qkv, o_cmp, o_sel, gates, bsz, seq):
    tq = WINDOW
    assert seq % tq == 0
    nq = seq // tq
    kcol = ATT_WIDTH // HEAD_DIM + 2 * N_KV
    qspec = pl.BlockSpec((tq, HPG * HEAD_DIM), lambda b, g, i: (b * nq + i, g))
    cur = lambda c: pl.BlockSpec((tq, HEAD_DIM), lambda b, g, i: (b * nq + i, kcol + c * N_KV + g))
    prev = lambda c: pl.BlockSpec((tq, HEAD_DIM),
                                  lambda b, g, i: (b * nq + jnp.maximum(i - 1, 0), kcol + c * N_KV + g))
    return pl.pallas_call(
        functools.partial(_win_attn_body, tq=tq),
        out_shape=jax.ShapeDtypeStruct((bsz * seq, ATT_WIDTH), BF16),
        grid=(bsz, N_KV, nq),
        in_specs=[qspec, cur(0), prev(0), cur(1), prev(1), qspec, qspec,
                  pl.BlockSpec((tq, LANES), lambda b, g, i: (b * nq + i, 0))],
        out_specs=qspec,
        compiler_params=_params(("parallel", "parallel", "parallel")),
        name="nsa_win_attn_gate",
    )(qkv, qkv, qkv, qkv, qkv, o_cmp, o_sel, gates)


def _merge_body(hn_ref, ya_ref, yb_ref, wga_ref, wgb_ref, wa_ref, wb_ref, o_ref):
    hn = hn_ref[...]
    ga = jnp.dot(hn, wga_ref[...], preferred_element_type=F32)
    gb = jnp.dot(hn, wgb_ref[...], preferred_element_type=F32)
    pa = jnp.dot(ya_ref[...], wa_ref[...], preferred_element_type=F32)
    pb = jnp.dot(yb_ref[...], wb_ref[...], preferred_element_type=F32)
    o_ref[...] = (jax.nn.sigmoid(ga) * pa + jax.nn.sigmoid(gb) * pb).astype(o_ref.dtype)


def _merge(hn, ya, yb, wg, wa, wb, tm=512, tn=512):
    m, d = hn.shape
    ka = ya.shape[1]
    tm = min(tm, m)
    row = lambda k: pl.BlockSpec((tm, k), lambda i, j: (i, 0))
    col = lambda k: pl.BlockSpec((k, tn), lambda i, j: (0, j))
    col_b = pl.BlockSpec((d, tn), lambda i, j: (0, d // tn + j))
    return pl.pallas_call(
        _merge_body,
        out_shape=jax.ShapeDtypeStruct((m, d), BF16),
        grid=(m // tm, d // tn),
        in_specs=[row(d), row(ka), row(ka), col(d), col_b, col(ka), col(ka)],
        out_specs=pl.BlockSpec((tm, tn), lambda i, j: (i, j)),
        compiler_params=_params(("parallel", "arbitrary")),
        name="mixer_merge",
    )(hn, ya, yb, wg, wg, wa, wb)


def _rms(x, g):
    return x * lax.rsqrt(jnp.mean(x * x, axis=-1, keepdims=True) + RMS_EPS) * g


def _res_norm2_body(raw_ref, x_ref, gpost_ref, gpre_ref, h_ref, hn_ref):
    h = x_ref[...] + _rms(raw_ref[...].astype(F32), gpost_ref[...])
    h_ref[...] = h
    hn_ref[...] = _rms(h, gpre_ref[...]).astype(hn_ref.dtype)


def _res_norm2(raw, x, g_post, g_pre, tm=256):
    m, d = x.shape
    tm = min(tm, m)
    rows = pl.BlockSpec((tm, d), lambda i: (i, 0))
    vec = pl.BlockSpec((1, d), lambda i: (0, 0))
    return pl.pallas_call(
        _res_norm2_body,
        out_shape=(jax.ShapeDtypeStruct((m, d), F32), jax.ShapeDtypeStruct((m, d), BF16)),
        grid=(m // tm,),
        in_specs=[rows, rows, vec, vec],
        out_specs=(rows, rows),
        compiler_params=_params(("parallel",)),
        name="residual_norm_prenorm",
    )(raw, x, g_post.reshape(1, d), g_pre.reshape(1, d))


def _res_norm_body(raw_ref, x_ref, g_ref, o_ref):
    o_ref[...] = x_ref[...] + _rms(raw_ref[...].astype(F32), g_ref[...])


def _res_norm(raw, x, g, tm=256):
    m, d = x.shape
    tm = min(tm, m)
    rows = pl.BlockSpec((tm, d), lambda i: (i, 0))
    return pl.pallas_call(
        _res_norm_body,
        out_shape=jax.ShapeDtypeStruct((m, d), F32),
        grid=(m // tm,),
        in_specs=[rows, rows, pl.BlockSpec((1, d), lambda i: (0, 0))],
        out_specs=rows,
        compiler_params=_params(("parallel",)),
        name="residual_norm",
    )(raw, x, g.reshape(1, d))


def _swiglu_body(a_ref, wg_ref, wu_ref, o_ref, wg_scr, wu_scr):
    @pl.when(pl.program_id(1) == 0)
    def _():
        wg_scr[...] = wg_ref[...].astype(BF16)
        wu_scr[...] = wu_ref[...].astype(BF16)

    a = a_ref[...]
    g = jnp.dot(a, wg_scr[...], preferred_element_type=F32)
    u = jnp.dot(a, wu_scr[...], preferred_element_type=F32)
    o_ref[...] = (g * jax.nn.sigmoid(g) * u).astype(o_ref.dtype)


def _swiglu(a, wg, wu, tm=2048, tn=256):
    m, k = a.shape
    n = wg.shape[1]
    tm = min(tm, m)
    assert n % tn == 0
    vmem = 2 * tm * k * 2 + 2 * 2 * k * tn * 4 + 2 * tm * tn * 2 + 2 * k * tn * 2
    assert vmem + V7X_VMEM_SLACK_BYTES <= V7X_VMEM_BYTES
    return pl.pallas_call(
        _swiglu_body,
        out_shape=jax.ShapeDtypeStruct((m, n), BF16),
        grid=(n // tn, m // tm),
        in_specs=[pl.BlockSpec((tm, k), lambda j, i: (i, 0)),
                  pl.BlockSpec((k, tn), lambda j, i: (0, j)),
                  pl.BlockSpec((k, tn), lambda j, i: (0, j))],
        out_specs=pl.BlockSpec((tm, tn), lambda j, i: (i, j)),
        scratch_shapes=[pltpu.VMEM((k, tn), BF16), pltpu.VMEM((k, tn), BF16)],
        compiler_params=_params(("arbitrary", "arbitrary"), vmem=vmem + V7X_VMEM_SLACK_BYTES),
        name="ffn_swiglu",
    )(a, wg, wu)


def kernel(x, norm_mix_pre, w_in, ssm_a_re, ssm_a_im, ssm_log_dt, ssm_b_re, ssm_b_im, ssm_c_re, ssm_c_im, ssm_d, ssm_w_glu, ssm_b_glu, cmp_pe_k, cmp_w1_k, cmp_w2_k, cmp_pe_v, cmp_w1_v, cmp_w2_v, w_proj_a, w_proj_b, w_out, norm_mix_post, norm_ffn_pre, w_ffn_gate, w_ffn_up, w_ffn_down, norm_ffn_post):
    bsz, seq, d = x.shape
    m = bsz * seq
    depth = w_in.shape[0]
    h = x.reshape(m, d)
    o_q, o_kvc, o_kv, o_gn = SSM_WIDTH, SSM_WIDTH + ATT_WIDTH, SSM_WIDTH + ATT_WIDTH + 2 * KV_WIDTH, 7168
    o_ga = o_gn + 3 * N_HEADS
    o_gb = o_ga + D_MODEL
    bf = lambda a: a.astype(BF16)
    tn_in = 512
    nb_u, nb_q = SSM_WIDTH // tn_in, ATT_WIDTH // tn_in
    qk_scale = HEAD_DIM ** -0.5 * LOG2E
    for l in range(depth):
        wt = jnp.swapaxes(w_in[l], 0, 1)
        w_g = _rows_transposed_cast(wt, o_ga, 2 * D_MODEL, BF16)

        hn, gn = _rmsnorm_gates(h, norm_mix_pre[l], wt, o_gn // LANES)
        proj_f32 = _matmul_f32w(hn, wt, SSM_WIDTH + 2 * KV_WIDTH, F32, 1024, tn_in, trans_w=True,
                                col_block=lambda j: jnp.where(j < nb_u, j, j - nb_u + o_kvc // tn_in),
                                name="in_proj_f32")
        qkv = _matmul_f32w(hn, wt, ATT_WIDTH + 4 * KV_WIDTH, BF16, 2048, tn_in, trans_w=True,
                           col_block=lambda j: jnp.where(j < nb_q, j + o_q // tn_in, j - nb_q + o_kv // tn_in),
                           scale_fn=lambda j: jnp.where(j < nb_q, qk_scale, 1.0),
                           name="in_proj_bf16")

        y = _s5_mixer(proj_f32.reshape(bsz, seq, -1), ssm_a_re[l], ssm_a_im[l], ssm_log_dt[l],
                      ssm_b_re[l], ssm_b_im[l], ssm_c_re[l], ssm_c_im[l], ssm_d[l])
        y_a = _glu(y.reshape(m, SSM_WIDTH), bf(ssm_w_glu[l]), ssm_b_glu[l])

        pe = jnp.stack([cmp_pe_k[l], cmp_pe_v[l]])
        w1 = bf(jnp.stack([cmp_w1_k[l], cmp_w1_v[l]]))
        w2 = bf(jnp.stack([cmp_w2_k[l], cmp_w2_v[l]]))
        kcv = _compress(proj_f32, SSM_WIDTH, pe, w1, w2, bsz, seq)
        o_cmp, bias = _cmp_attn(qkv, kcv, bsz, seq)
        o_sel = _sel_attn(qkv, bias, bsz, seq)
        y_b = _win_attn_combine(qkv, o_cmp, o_sel, gn, bsz, seq)

        merged = _merge(hn, y_a, y_b, w_g, bf(w_proj_a[l]), bf(w_proj_b[l]))
        mix = _matmul_f32w(merged, w_out[l], D_MODEL, BF16, 2048, 512, name="out_proj")
        h, hn2 = _res_norm2(mix, h, norm_mix_post[l], norm_ffn_pre[l])

        act = _swiglu(hn2, w_ffn_gate[l], w_ffn_up[l])
        f = _matmul(act, bf(w_ffn_down[l]), BF16, 512, 512, name="ffn_down")
        h = _res_norm(f, h, norm_ffn_post[l])
    return h.reshape(bsz, seq, d)
```

```python
import functools
import math

import jax
import jax.numpy as jnp
from jax import lax
from jax.experimental import pallas as pl
from jax.experimental.pallas import tpu as pltpu

F32 = jnp.float32
BF16 = jnp.bfloat16

D_MODEL = 4096
SSM_WIDTH = 2048
SSM_GROUP = 16
SSM_GROUPS = 128
SSM_STATE = 64
N_HEADS = 16
HEAD_DIM = 128
N_KV = 4
HPG = 4
ATT_WIDTH = 2048
KV_WIDTH = 512
L_CMP = 32
STRIDE_CMP = 16
L_SEL = 64
N_SEL = 16
WINDOW = 512
RMS_EPS = 1e-6
NEG_INF = -1e30
FORCE_SCORE = 1e9
TAKEN_SCORE = -3e38
LOG2E = math.log2(math.e)

V7X_VMEM_BYTES = 64 * 1024 * 1024
V7X_VMEM_LIMIT_BYTES = 56 * 1024 * 1024
V7X_VMEM_SLACK_BYTES = 4 * 1024 * 1024
LANES = 128
SUBLANES = 8

S5_GT = 16
S5_NC = S5_GT * SSM_GROUP
S5_NP = S5_GT * SSM_STATE
S5_TILES = SSM_GROUPS // S5_GT
S5_SLABS = 2 * S5_NP // LANES

ATT_ROWS = 256
SEL_TQ, SEL_RB = 512, 64
SEL_GROUP = 4


def _params(sem, vmem=V7X_VMEM_LIMIT_BYTES):
    return pltpu.CompilerParams(dimension_semantics=sem, vmem_limit_bytes=vmem)


def _lane_tile(x, n):
    return jnp.concatenate([x] * n, axis=1)


def _rmsnorm_gates_body(x_ref, g_ref, w_ref, hn_ref, gn_ref, wbf_scr):
    @pl.when(pl.program_id(0) == 0)
    def _():
        wbf_scr[...] = w_ref[...].astype(BF16)

    x = x_ref[...]
    ms = jnp.mean(x * x, axis=-1, keepdims=True)
    hn = (x * lax.rsqrt(ms + RMS_EPS) * g_ref[...]).astype(BF16)
    hn_ref[...] = hn
    gn_ref[...] = lax.dot_general(hn, wbf_scr[...], (((1,), (1,)), ((), ())), preferred_element_type=F32)


def _rmsnorm_gates(x, gain, wt, feature_block, tm=256):
    m, d = x.shape
    tm = min(tm, m)
    return pl.pallas_call(
        _rmsnorm_gates_body,
        out_shape=(jax.ShapeDtypeStruct((m, d), BF16), jax.ShapeDtypeStruct((m, LANES), F32)),
        grid=(m // tm,),
        in_specs=[pl.BlockSpec((tm, d), lambda i: (i, 0)),
                  pl.BlockSpec((1, d), lambda i: (0, 0)),
                  pl.BlockSpec((LANES, d), lambda i: (feature_block, 0))],
        out_specs=(pl.BlockSpec((tm, d), lambda i: (i, 0)), pl.BlockSpec((tm, LANES), lambda i: (i, 0))),
        scratch_shapes=[pltpu.VMEM((LANES, d), BF16)],
        compiler_params=_params(("arbitrary",)),
        name="rmsnorm_gates",
    )(x, gain.reshape(1, d), wt)


def _mm_body(a_ref, w_ref, o_ref):
    o_ref[...] = jnp.dot(a_ref[...], w_ref[...], preferred_element_type=F32).astype(o_ref.dtype)


def _matmul(a, w, out_dtype, tm, tn, name="matmul"):
    m, k = a.shape
    n = w.shape[1]
    tm, tn = min(tm, m), min(tn, n)
    assert m % tm == 0 and n % tn == 0
    return pl.pallas_call(
        _mm_body,
        out_shape=jax.ShapeDtypeStruct((m, n), out_dtype),
        grid=(m // tm, n // tn),
        in_specs=[pl.BlockSpec((tm, k), lambda i, j: (i, 0)), pl.BlockSpec((k, tn), lambda i, j: (0, j))],
        out_specs=pl.BlockSpec((tm, tn), lambda i, j: (i, j)),
        compiler_params=_params(("parallel", "arbitrary")),
        name=name,
    )(a, w)


_NT = (((1,), (1,)), ((), ()))


def _mm_f32w_body(a_ref, w_ref, o_ref, wbf_scr, *, scale_fn, trans_w):
    @pl.when(pl.program_id(1) == 0)
    def _():
        wbf_scr[...] = w_ref[...].astype(BF16)

    if trans_w:
        acc = lax.dot_general(a_ref[...], wbf_scr[...], _NT, preferred_element_type=F32)
    else:
        acc = jnp.dot(a_ref[...], wbf_scr[...], preferred_element_type=F32)
    if scale_fn is not None:
        acc = acc * scale_fn(pl.program_id(0))
    o_ref[...] = acc.astype(o_ref.dtype)


def _matmul_f32w(a, w, n_out, out_dtype, tm, tn, col_block=None, scale_fn=None, trans_w=False,
                 name="matmul_f32w"):
    m, k = a.shape
    tm = min(tm, m)
    assert m % tm == 0 and n_out % tn == 0
    col_block = col_block or (lambda j: j)
    if trans_w:
        w_spec = pl.BlockSpec((tn, k), lambda j, i: (col_block(j), 0))
        w_scratch = pltpu.VMEM((tn, k), BF16)
    else:
        w_spec = pl.BlockSpec((k, tn), lambda j, i: (0, col_block(j)))
        w_scratch = pltpu.VMEM((k, tn), BF16)
    vmem = 2 * tm * k * 2 + 2 * k * tn * 4 + 2 * tm * tn * jnp.dtype(out_dtype).itemsize + k * tn * 2
    assert vmem + V7X_VMEM_SLACK_BYTES <= V7X_VMEM_BYTES
    return pl.pallas_call(
        functools.partial(_mm_f32w_body, scale_fn=scale_fn, trans_w=trans_w),
        out_shape=jax.ShapeDtypeStruct((m, n_out), out_dtype),
        grid=(n_out // tn, m // tm),
        in_specs=[pl.BlockSpec((tm, k), lambda j, i: (i, 0)), w_spec],
        out_specs=pl.BlockSpec((tm, tn), lambda j, i: (i, j)),
        scratch_shapes=[w_scratch],
        compiler_params=_params(("arbitrary", "arbitrary"), vmem=vmem + V7X_VMEM_SLACK_BYTES),
        name=name,
    )(a, w)


def _rows_transposed_cast_body(w_ref, o_ref):
    o_ref[...] = w_ref[...].T.astype(o_ref.dtype)


def _rows_transposed_cast(wt, row0, n_rows, out_dtype, tr=512):
    k = wt.shape[1]
    assert row0 % (2 * SUBLANES) == 0 and n_rows % tr == 0
    return pl.pallas_call(
        _rows_transposed_cast_body,
        out_shape=jax.ShapeDtypeStruct((k, n_rows), out_dtype),
        grid=(n_rows // tr,),
        in_specs=[pl.BlockSpec((pl.Element(tr), pl.Element(k)),
                               lambda c: (pl.multiple_of(row0 + c * tr, 2 * SUBLANES), 0))],
        out_specs=pl.BlockSpec((k, tr), lambda c: (0, c)),
        compiler_params=_params(("parallel",)),
        name="gate_weight_cast",
    )(wt)


def _s5_body(u_ref, are_ref, aim_ref, ldt_ref, bre_ref, bim_ref, cre_ref, cim_ref, d_ref, y_ref,
             abar_scr, bbar_scr, cmat_scr, h_scr, st_scr, *, tc, bsz, tps):
    ti = pl.program_id(1)
    half = S5_SLABS // 2

    @pl.when(ti == 0)
    def _():
        row_tile = lax.broadcasted_iota(jnp.int32, (SUBLANES, S5_NP), 0) // bsz
        same_group = (lax.broadcasted_iota(jnp.int32, (S5_NC, S5_NP), 0) // SSM_GROUP
                      == lax.broadcasted_iota(jnp.int32, (S5_NC, S5_NP), 1) // SSM_STATE)
        bdiag = lambda x: jnp.where(same_group, jnp.concatenate([x] * S5_GT, axis=0), 0.0)
        abr8 = jnp.zeros((SUBLANES, S5_NP), F32)
        abi8 = jnp.zeros((SUBLANES, S5_NP), F32)
        for k in range(tps):
            cols = slice(k * S5_NP, (k + 1) * S5_NP)
            ar, ai = are_ref[:, cols], aim_ref[:, cols]
            dt = jnp.exp(ldt_ref[:, cols])
            decay = jnp.exp(dt * ar)
            abr, abi = decay * jnp.cos(dt * ai), decay * jnp.sin(dt * ai)
            den = ar * ar + ai * ai
            zr = ((abr - 1.0) * ar + abi * ai) / den
            zi = (abi * ar - (abr - 1.0) * ai) / den
            abr8 = jnp.where(row_tile == k, abr, abr8)
            abi8 = jnp.where(row_tile == k, abi, abi8)
            br, bi = bdiag(bre_ref[k]), bdiag(bim_ref[k])
            bbar_scr[k, :, 0:S5_NP] = (zr * br - zi * bi).astype(BF16)
            bbar_scr[k, :, S5_NP:2 * S5_NP] = (zr * bi + zi * br).astype(BF16)
            cmat_scr[k, :, 0:S5_NP] = bdiag(cre_ref[k]).astype(BF16)
            cmat_scr[k, :, S5_NP:2 * S5_NP] = (-bdiag(cim_ref[k])).astype(BF16)
        abar_scr[0:SUBLANES, :] = abr8
        abar_scr[SUBLANES:2 * SUBLANES, :] = abi8
        st_scr[...] = jnp.zeros_like(st_scr)

    for k in range(tps):
        cols = slice(k * S5_NC, (k + 1) * S5_NC)
        ub = jnp.concatenate([u_ref[b, :, cols] for b in range(bsz)], axis=0).astype(BF16)
        for s2 in range(S5_SLABS // 2):
            bu = jnp.dot(ub, bbar_scr[k, :, s2 * 2 * LANES:(s2 + 1) * 2 * LANES], preferred_element_type=F32)
            for b in range(bsz):
                for e in range(2):
                    h_scr[2 * s2 + e, pl.ds(k * bsz + b, tc, stride=SUBLANES), :] = bu[b * tc:(b + 1) * tc,
                                                                                      e * LANES:(e + 1) * LANES]

    ar = abar_scr[0:SUBLANES, :]
    ai = abar_scr[SUBLANES:2 * SUBLANES, :]

    def step(t, carry):
        hr, hi = carry
        r0 = pl.multiple_of(t * SUBLANES, SUBLANES)
        bur = jnp.concatenate([h_scr[s, pl.ds(r0, SUBLANES), :] for s in range(half)], axis=1)
        bui = jnp.concatenate([h_scr[half + s, pl.ds(r0, SUBLANES), :] for s in range(half)], axis=1)
        nhr = ar * hr - ai * hi + bur
        nhi = ar * hi + ai * hr + bui
        for s in range(half):
            h_scr[s, pl.ds(r0, SUBLANES), :] = nhr[:, s * LANES:(s + 1) * LANES]
            h_scr[half + s, pl.ds(r0, SUBLANES), :] = nhi[:, s * LANES:(s + 1) * LANES]
        return nhr, nhi

    hr, hi = lax.fori_loop(0, tc, step, (st_scr[0:SUBLANES, :], st_scr[SUBLANES:2 * SUBLANES, :]), unroll=2)
    st_scr[0:SUBLANES, :] = hr
    st_scr[SUBLANES:2 * SUBLANES, :] = hi

    for k in range(tps):
        cols = slice(k * S5_NC, (k + 1) * S5_NC)

        hb = jnp.concatenate(
            [jnp.concatenate([h_scr[s, pl.ds(k * bsz + b, tc, stride=SUBLANES), :] for s in range(S5_SLABS)], axis=1)
             for b in range(bsz)], axis=0).astype(BF16)
        ch = lax.dot_general(hb, cmat_scr[k], _NT, preferred_element_type=F32)
        for b in range(bsz):
            y = ch[b * tc:(b + 1) * tc, :] + d_ref[:, cols] * u_ref[b, :, cols]
            y_ref[b, :, cols] = jax.nn.gelu(y)


def _s5_mixer(u3, a_re, a_im, log_dt, b_re, b_im, c_re, c_im, d_skip, tc=256):
    bsz, seq, _ = u3.shape
    assert SUBLANES % bsz == 0
    tps = min(SUBLANES // bsz, S5_TILES)
    assert tps * bsz == SUBLANES
    tc = min(tc, seq)

    def per_tile_b(b):
        return (b.reshape(S5_TILES, S5_GT, SSM_STATE, SSM_GROUP).transpose(0, 3, 1, 2)
                .reshape(S5_TILES, SSM_GROUP, S5_NP))

    def per_tile_c(c):
        return (c.reshape(S5_TILES, S5_GT, SSM_GROUP, SSM_STATE).transpose(0, 2, 1, 3)
                .reshape(S5_TILES, SSM_GROUP, S5_NP))

    flat = lambda a: a.reshape(1, SSM_GROUPS * SSM_STATE)
    ldt = jnp.repeat(log_dt, SSM_STATE).reshape(1, SSM_GROUPS * SSM_STATE)
    vec_spec = pl.BlockSpec((1, tps * S5_NP), lambda g, t: (0, g))
    bc_spec = pl.BlockSpec((tps, SSM_GROUP, S5_NP), lambda g, t: (g, 0, 0))
    return pl.pallas_call(
        functools.partial(_s5_body, tc=tc, bsz=bsz, tps=tps),
        out_shape=jax.ShapeDtypeStruct((bsz, seq, SSM_WIDTH), F32),
        grid=(S5_TILES // tps, seq // tc),
        in_specs=[
            pl.BlockSpec((bsz, tc, tps * S5_NC), lambda g, t: (0, t, g)),
            vec_spec, vec_spec, vec_spec, bc_spec, bc_spec, bc_spec, bc_spec,
            pl.BlockSpec((1, tps * S5_NC), lambda g, t: (0, g)),
        ],
        out_specs=pl.BlockSpec((bsz, tc, tps * S5_NC), lambda g, t: (0, t, g)),
        scratch_shapes=[
            pltpu.VMEM((2 * SUBLANES, S5_NP), F32),
            pltpu.VMEM((tps, S5_NC, 2 * S5_NP), BF16),
            pltpu.VMEM((tps, S5_NC, 2 * S5_NP), BF16),
            pltpu.VMEM((S5_SLABS, tc * SUBLANES, LANES), F32),
            pltpu.VMEM((2 * SUBLANES, S5_NP), F32),
        ],
        compiler_params=_params(("arbitrary", "arbitrary")),
        name="s5_scan",
    )(u3, flat(a_re), flat(a_im), ldt, per_tile_b(b_re), per_tile_b(b_im), per_tile_c(c_re), per_tile_c(c_im),
      d_skip.reshape(1, SSM_WIDTH))


def _glu_body(y_ref, w_ref, b_ref, o_ref, ybf_scr, *, tn):
    j = pl.program_id(1)

    @pl.when(j == 0)
    def _():
        ybf_scr[...] = y_ref[...].astype(BF16)

    z = jnp.dot(ybf_scr[...], w_ref[...], preferred_element_type=F32) + b_ref[...]
    yt = y_ref[:, pl.ds(pl.multiple_of(j * tn, tn), tn)]
    o_ref[...] = (yt * jax.nn.sigmoid(z)).astype(o_ref.dtype)


def _glu(y, w_bf, bias, tm=1024, tn=1024):
    m, k = y.shape
    tm = min(tm, m)
    return pl.pallas_call(
        functools.partial(_glu_body, tn=tn),
        out_shape=jax.ShapeDtypeStruct((m, k), BF16),
        grid=(m // tm, k // tn),
        in_specs=[pl.BlockSpec((tm, k), lambda i, j: (i, 0)),
                  pl.BlockSpec((k, tn), lambda i, j: (0, j)),
                  pl.BlockSpec((1, tn), lambda i, j: (0, j))],
        out_specs=pl.BlockSpec((tm, tn), lambda i, j: (i, j)),
        scratch_shapes=[pltpu.VMEM((tm, k), BF16)],
        compiler_params=_params(("parallel", "arbitrary")),
        name="s5_glu",
    )(y, w_bf, bias.reshape(1, k))


def _compress_body(x_ref, pe_ref, w1_ref, w2_ref, o_ref, *, ncmp):
    half = L_CMP // 2
    acc_a = jnp.zeros((ncmp, HEAD_DIM), F32)
    acc_b = jnp.zeros((ncmp, HEAD_DIM), F32)
    for r in range(half):
        xr = x_ref[pl.ds(r, ncmp, stride=STRIDE_CMP), :]
        xa = (xr + pe_ref[r:r + 1, :]).astype(BF16)
        xb = (xr + pe_ref[half + r:half + r + 1, :]).astype(BF16)
        acc_a += jnp.dot(xa, w1_ref[r * HEAD_DIM:(r + 1) * HEAD_DIM, :], preferred_element_type=F32)
        acc_b += jnp.dot(xb, w1_ref[(half + r) * HEAD_DIM:(half + r + 1) * HEAD_DIM, :],
                         preferred_element_type=F32)
    pre = acc_a + jnp.concatenate([acc_b[1:], acc_b[:1]], axis=0)
    o_ref[...] = jnp.dot(jax.nn.gelu(pre).astype(BF16), w2_ref[...], preferred_element_type=F32).astype(o_ref.dtype)


def _compress(proj_f32, col0, pe, w1_bf, w2_bf, bsz, seq):
    ncmp = seq // STRIDE_CMP
    cb = col0 // HEAD_DIM
    return pl.pallas_call(
        functools.partial(_compress_body, ncmp=ncmp),
        out_shape=jax.ShapeDtypeStruct((bsz, 2, N_KV, ncmp, HEAD_DIM), BF16),
        grid=(bsz, 2, N_KV),
        in_specs=[pl.BlockSpec((seq, HEAD_DIM), lambda b, w, g: (b, cb + w * N_KV + g)),
                  pl.BlockSpec((None, L_CMP, HEAD_DIM), lambda b, w, g: (w, 0, 0)),
                  pl.BlockSpec((None, L_CMP * HEAD_DIM, HEAD_DIM), lambda b, w, g: (w, 0, 0)),
                  pl.BlockSpec((None, HEAD_DIM, HEAD_DIM), lambda b, w, g: (w, 0, 0))],
        out_specs=pl.BlockSpec((None, None, None, ncmp, HEAD_DIM), lambda b, w, g: (b, w, g, 0, 0)),
        compiler_params=_params(("parallel", "parallel", "parallel")),
        name="nsa_compress",
    )(proj_f32, pe, w1_bf, w2_bf)


def _cmp_attn_body(q_ref, kc_ref, vc_ref, o_ref, bias_ref, *, tq, ncmp, nblk, ntop):
    qi = pl.program_id(2)
    t0 = qi * tq
    tpos = t0 + lax.broadcasted_iota(jnp.int32, (tq, ncmp), 0)
    blk_end = lax.broadcasted_iota(jnp.int32, (tq, ncmp), 1) * STRIDE_CMP + (L_CMP - 1)
    ok = blk_end <= tpos
    okf = ok.astype(F32)
    kc = kc_ref[...]
    vc = vc_ref[...]
    psum = jnp.zeros((tq, ncmp), F32)
    for h in range(HPG):
        qh = q_ref[:, h * HEAD_DIM:(h + 1) * HEAD_DIM]
        s = lax.dot_general(qh, kc, (((1,), (1,)), ((), ())), preferred_element_type=F32)
        s = jnp.where(ok, s, NEG_INF)
        e = jnp.exp2(s - jnp.max(s, axis=-1, keepdims=True))
        p = e / jnp.sum(e, axis=-1, keepdims=True) * okf
        o_ref[:, h * HEAD_DIM:(h + 1) * HEAD_DIM] = jnp.dot(p.astype(BF16), vc, preferred_element_type=F32)
        psum = psum + p

    jj = lax.broadcasted_iota(jnp.int32, (nblk, ncmp), 0)
    nn = lax.broadcasted_iota(jnp.int32, (nblk, ncmp), 1)
    ov = ((nn * STRIDE_CMP < (jj + 1) * L_SEL) & (nn * STRIDE_CMP + L_CMP > jj * L_SEL)).astype(BF16)
    p_hi = psum.astype(BF16)
    p_lo = (psum - p_hi.astype(F32)).astype(BF16)
    nt = (((1,), (1,)), ((), ()))
    imp = (lax.dot_general(ov, p_hi, nt, preferred_element_type=F32)
           + lax.dot_general(ov, p_lo, nt, preferred_element_type=F32))

    jb = lax.broadcasted_iota(jnp.int32, (nblk, tq), 0)
    tt = t0 + lax.broadcasted_iota(jnp.int32, (nblk, tq), 1)
    cur = tt // L_SEL
    allowed = jb * L_SEL <= tt
    forced = (jb == 0) | (jb == cur) | (jb == cur - 1)
    score = jnp.where(forced, FORCE_SCORE, jnp.where(allowed, imp, NEG_INF))
    taken = jnp.zeros((nblk, tq), F32)
    for _ in range(ntop):
        best = jnp.max(score, axis=0, keepdims=True)
        first = jnp.min(jnp.where(score == best, jb, nblk), axis=0, keepdims=True)
        pick = jb == first
        taken = jnp.where(pick, 1.0, taken)
        score = jnp.where(pick, TAKEN_SCORE, score)
    bias_t = jnp.where(taken > 0.5, 0.0, NEG_INF)
    bias_t = jnp.concatenate([bias_t, jnp.zeros((LANES - nblk, tq), F32)], axis=0)
    bias_ref[...] = bias_t.T.astype(bias_ref.dtype)


def _cmp_attn(qkv, kcv, bsz, seq, tq=2048):
    tq = min(tq, seq)
    ncmp = seq // STRIDE_CMP
    nblk = seq // L_SEL
    ntop = min(N_SEL, nblk)
    nq = seq // tq
    return pl.pallas_call(
        functools.partial(_cmp_attn_body, tq=tq, ncmp=ncmp, nblk=nblk, ntop=ntop),
        out_shape=(jax.ShapeDtypeStruct((bsz * seq, ATT_WIDTH), F32),
                   jax.ShapeDtypeStruct((bsz, N_KV, seq, LANES), BF16)),
        grid=(bsz, N_KV, nq),
        in_specs=[pl.BlockSpec((tq, HPG * HEAD_DIM), lambda b, g, i: (b * nq + i, g)),
                  pl.BlockSpec((None, None, None, ncmp, HEAD_DIM), lambda b, g, i: (b, 0, g, 0, 0)),
                  pl.BlockSpec((None, None, None, ncmp, HEAD_DIM), lambda b, g, i: (b, 1, g, 0, 0))],
        out_specs=(pl.BlockSpec((tq, HPG * HEAD_DIM), lambda b, g, i: (b * nq + i, g)),
                   pl.BlockSpec((None, None, tq, LANES), lambda b, g, i: (b, g, i, 0))),
        compiler_params=_params(("parallel", "parallel", "parallel")),
        name="nsa_cmp_attn_topk",
    )(qkv, kcv, kcv)


def _sel_attn_body(q_ref, k_ref, v_ref, bias_ref, o_ref, qa_scr, m_scr, acc_scr, s_scr, p_scr, alpha_scr,
                   *, tq, rb):
    qi = pl.program_id(2)
    tk = tq
    m_scr[...] = jnp.full_like(m_scr, NEG_INF)
    acc_scr[...] = jnp.zeros_like(acc_scr)
    for h in range(HPG):
        qa_scr[h, :, 0:HEAD_DIM] = q_ref[:, h * HEAD_DIM:(h + 1) * HEAD_DIM]
        qa_scr[h, :, HEAD_DIM:2 * HEAD_DIM] = bias_ref[...]
    nt = (((1,), (1,)), ((), ()))

    def key_tile(k0, nkeys):
        key_blk = k0 // L_SEL + lax.broadcasted_iota(jnp.int32, (nkeys, LANES), 0) // L_SEL
        onehot = jnp.where(lax.broadcasted_iota(jnp.int32, (nkeys, LANES), 1) == key_blk, 1.0, 0.0).astype(BF16)
        return jnp.concatenate([k_ref[pl.ds(k0, nkeys), :], onehot], axis=1)

    def scores(h, k_aug, r0, r1, nk):
        s_scr[h, r0:r1, 0:nk] = lax.dot_general(qa_scr[h, r0:r1, :], k_aug[0:nk, :], nt,
                                                preferred_element_type=F32)

    def softmax(h, r0, r1, nk, d0):
        for r in range(r0 // rb, r1 // rb):
            rows = slice(r * rb, (r + 1) * rb)
            pieces = []
            if d0 > 0:
                pieces.append(s_scr[h, rows, 0:d0])
            if d0 < nk:
                col = lax.broadcasted_iota(jnp.int32, (rb, nk - d0), 1)
                row = lax.broadcasted_iota(jnp.int32, (rb, nk - d0), 0) + r * rb
                pieces.append(jnp.where(col <= row, s_scr[h, rows, d0:nk], NEG_INF))
            m_prev = m_scr[h, rows, :]
            m_next = m_prev
            for s in pieces:
                m_next = jnp.maximum(m_next, jnp.max(s, axis=-1, keepdims=True))
            c0 = 0
            for s in pieces:
                w = s.shape[1]
                p_scr[h, rows, c0:c0 + w] = jnp.exp2(s - _lane_tile(m_next, w // LANES)).astype(BF16)
                c0 += w
            alpha_scr[h, rows, :] = jnp.exp2(m_prev - m_next)
            m_scr[h, rows, :] = m_next

    def values(h, v_aug, r0, r1, nk):
        acc_scr[h, r0:r1, :] = (_lane_tile(alpha_scr[h, r0:r1, :], 2) * acc_scr[h, r0:r1, :]
                                + jnp.dot(p_scr[h, r0:r1, 0:nk], v_aug[0:nk, :], preferred_element_type=F32))

    def tile(kt, nkeys, diag_last):
        k0 = pl.multiple_of(kt * tk, tk)
        k_aug = key_tile(k0, nkeys)
        v_aug = jnp.concatenate([v_ref[pl.ds(k0, nkeys), :], jnp.ones((nkeys, LANES), BF16)], axis=1)
        if diag_last:
            d0 = nkeys - tk
            parts = [(0, tq // 2, nkeys - tk // 2), (tq // 2, tq, nkeys)]
        else:
            d0 = nkeys
            parts = [(0, tq, nkeys)]
        for h in range(HPG):
            for r0, r1, nk in parts:
                scores(h, k_aug, r0, r1, nk)
        for h in range(HPG):
            for r0, r1, nk in parts:
                softmax(h, r0, r1, nk, d0)
        for h in range(HPG):
            for r0, r1, nk in parts:
                values(h, v_aug, r0, r1, nk)

    def tile_group(j, carry):
        tile(SEL_GROUP * j, SEL_GROUP * tk, False)
        return carry

    n_groups = qi // SEL_GROUP
    lax.fori_loop(0, n_groups, tile_group, 0)
    for run in range(1, SEL_GROUP + 1):
        @pl.when(qi % SEL_GROUP == run - 1)
        def _(run=run):
            tile(n_groups * SEL_GROUP, run * tk, True)

    for h in range(HPG):
        o_ref[:, h * HEAD_DIM:(h + 1) * HEAD_DIM] = (acc_scr[h, :, 0:HEAD_DIM]
                                                     / acc_scr[h, :, HEAD_DIM:2 * HEAD_DIM])


def _sel_attn(qkv, bias, bsz, seq, tq=SEL_TQ, rb=SEL_RB):
    tq = min(tq, seq)
    rb = min(rb, tq)
    nq = seq // tq
    kcol = ATT_WIDTH // HEAD_DIM
    return pl.pallas_call(
        functools.partial(_sel_attn_body, tq=tq, rb=rb),
        out_shape=jax.ShapeDtypeStruct((bsz * seq, ATT_WIDTH), F32),
        grid=(bsz, N_KV, nq),
        in_specs=[pl.BlockSpec((tq, HPG * HEAD_DIM), lambda b, g, i: (b * nq + i, g)),
                  pl.BlockSpec((seq, HEAD_DIM), lambda b, g, i: (b, kcol + g)),
                  pl.BlockSpec((seq, HEAD_DIM), lambda b, g, i: (b, kcol + N_KV + g)),
                  pl.BlockSpec((None, None, tq, LANES), lambda b, g, i: (b, g, i, 0))],
        out_specs=pl.BlockSpec((tq, HPG * HEAD_DIM), lambda b, g, i: (b * nq + i, g)),
        scratch_shapes=[pltpu.VMEM((HPG, tq, 2 * HEAD_DIM), BF16),
                        pltpu.VMEM((HPG, tq, LANES), F32),
                        pltpu.VMEM((HPG, tq, 2 * HEAD_DIM), F32),
                        pltpu.VMEM((HPG, tq, SEL_GROUP * tq), F32),
                        pltpu.VMEM((HPG, tq, SEL_GROUP * tq), BF16),
                        pltpu.VMEM((HPG, tq, LANES), F32)],
        compiler_params=_params(("parallel", "parallel", "arbitrary")),
        name="nsa_sel_attn",
    )(qkv, qkv, qkv, bias)


def _win_attn_body(q_ref, kc_ref, kp_ref, vc_ref, vp_ref, ocmp_ref, osel_ref, gate_ref, o_ref, *, tq):
    qi = pl.program_id(2)
    nt = (((1,), (1,)), ((), ()))
    ones = jnp.ones((tq, LANES), BF16)
    vc_aug = jnp.concatenate([vc_ref[...], ones], axis=1)
    vp_aug = jnp.concatenate([vp_ref[...], ones], axis=1)
    for r in range(tq // ATT_ROWS):
        rows = slice(r * ATT_ROWS, (r + 1) * ATT_ROWS)
        n_cur, p_lo = (r + 1) * ATT_ROWS, r * ATT_ROWS
        row_c = lax.broadcasted_iota(jnp.int32, (ATT_ROWS, n_cur), 0) + r * ATT_ROWS
        ok_cur = lax.broadcasted_iota(jnp.int32, (ATT_ROWS, n_cur), 1) <= row_c
        row_p = lax.broadcasted_iota(jnp.int32, (ATT_ROWS, tq - p_lo), 0) + r * ATT_ROWS
        ok_prev = (lax.broadcasted_iota(jnp.int32, (ATT_ROWS, tq - p_lo), 1) + p_lo > row_p) & (qi > 0)
        shift = (LANES - HPG * pl.program_id(1)) % LANES
        gates = jax.nn.sigmoid(pltpu.roll(gate_ref[rows, :], shift, 1))
        for h in range(HPG):
            sl = slice(h * HEAD_DIM, (h + 1) * HEAD_DIM)
            qh = q_ref[rows, sl]
            s1 = jnp.where(ok_cur, lax.dot_general(qh, kc_ref[0:n_cur, :], nt, preferred_element_type=F32), NEG_INF)
            s0 = jnp.where(ok_prev, lax.dot_general(qh, kp_ref[p_lo:tq, :], nt, preferred_element_type=F32), NEG_INF)
            m = jnp.maximum(jnp.max(s1, axis=-1, keepdims=True), jnp.max(s0, axis=-1, keepdims=True))
            p1 = jnp.exp2(s1 - m).astype(BF16)
            p0 = jnp.exp2(s0 - m).astype(BF16)
            acc = (jnp.dot(p1, vc_aug[0:n_cur, :], preferred_element_type=F32)
                   + jnp.dot(p0, vp_aug[p_lo:tq, :], preferred_element_type=F32))
            ow = acc[:, 0:HEAD_DIM] / acc[:, HEAD_DIM:2 * HEAD_DIM]
            g_cmp = gates[:, h:h + 1]
            g_sel = gates[:, N_HEADS + h:N_HEADS + h + 1]
            g_win = gates[:, 2 * N_HEADS + h:2 * N_HEADS + h + 1]
            o_ref[rows, sl] = (g_cmp * ocmp_ref[rows, sl] + g_sel * osel_ref[rows, sl]
                               + g_win * ow).astype(o_ref.dtype)


def _win_attn_combine(qkv, o_cmp, o_sel, gates, bsz, seq):
    tq = WINDOW
    assert seq % tq == 0
    nq = seq // tq
    kcol = ATT_WIDTH // HEAD_DIM + 2 * N_KV
    qspec = pl.BlockSpec((tq, HPG * HEAD_DIM), lambda b, g, i: (b * nq + i, g))
    cur = lambda c: pl.BlockSpec((tq, HEAD_DIM), lambda b, g, i: (b * nq + i, kcol + c * N_KV + g))
    prev = lambda c: pl.BlockSpec((tq, HEAD_DIM),
                                  lambda b, g, i: (b * nq + jnp.maximum(i - 1, 0), kcol + c * N_KV + g))
    return pl.pallas_call(
        functools.partial(_win_attn_body, tq=tq),
        out_shape=jax.ShapeDtypeStruct((bsz * seq, ATT_WIDTH), BF16),
        grid=(bsz, N_KV, nq),
        in_specs=[qspec, cur(0), prev(0), cur(1), prev(1), qspec, qspec,
                  pl.BlockSpec((tq, LANES), lambda b, g, i: (b * nq + i, 0))],
        out_specs=qspec,
        compiler_params=_params(("parallel", "parallel", "parallel")),
        name="nsa_win_attn_gate",
    )(qkv, qkv, qkv, qkv, qkv, o_cmp, o_sel, gates)


def _merge_body(hn_ref, ya_ref, yb_ref, wga_ref, wgb_ref, wa_ref, wb_ref, o_ref):
    hn = hn_ref[...]
    ga = jnp.dot(hn, wga_ref[...], preferred_element_type=F32)
    gb = jnp.dot(hn, wgb_ref[...], preferred_element_type=F32)
    pa = jnp.dot(ya_ref[...], wa_ref[...], preferred_element_type=F32)
    pb = jnp.dot(yb_ref[...], wb_ref[...], preferred_element_type=F32)
    o_ref[...] = (jax.nn.sigmoid(ga) * pa + jax.nn.sigmoid(gb) * pb).astype(o_ref.dtype)


def _merge(hn, ya, yb, wg, wa, wb, tm=512, tn=512):
    m, d = hn.shape
    ka = ya.shape[1]
    tm = min(tm, m)
    row = lambda k: pl.BlockSpec((tm, k), lambda i, j: (i, 0))
    col = lambda k: pl.BlockSpec((k, tn), lambda i, j: (0, j))
    col_b = pl.BlockSpec((d, tn), lambda i, j: (0, d // tn + j))
    return pl.pallas_call(
        _merge_body,
        out_shape=jax.ShapeDtypeStruct((m, d), BF16),
        grid=(m // tm, d // tn),
        in_specs=[row(d), row(ka), row(ka), col(d), col_b, col(ka), col(ka)],
        out_specs=pl.BlockSpec((tm, tn), lambda i, j: (i, j)),
        compiler_params=_params(("parallel", "arbitrary")),
        name="mixer_merge",
    )(hn, ya, yb, wg, wg, wa, wb)


def _rms(x, g):
    return x * lax.rsqrt(jnp.mean(x * x, axis=-1, keepdims=True) + RMS_EPS) * g


def _res_norm2_body(raw_ref, x_ref, gpost_ref, gpre_ref, h_ref, hn_ref):
    h = x_ref[...] + _rms(raw_ref[...].astype(F32), gpost_ref[...])
    h_ref[...] = h
    hn_ref[...] = _rms(h, gpre_ref[...]).astype(hn_ref.dtype)


def _res_norm2(raw, x, g_post, g_pre, tm=256):
    m, d = x.shape
    tm = min(tm, m)
    rows = pl.BlockSpec((tm, d), lambda i: (i, 0))
    vec = pl.BlockSpec((1, d), lambda i: (0, 0))
    return pl.pallas_call(
        _res_norm2_body,
        out_shape=(jax.ShapeDtypeStruct((m, d), F32), jax.ShapeDtypeStruct((m, d), BF16)),
        grid=(m // tm,),
        in_specs=[rows, rows, vec, vec],
        out_specs=(rows, rows),
        compiler_params=_params(("parallel",)),
        name="residual_norm_prenorm",
    )(raw, x, g_post.reshape(1, d), g_pre.reshape(1, d))


def _res_norm_body(raw_ref, x_ref, g_ref, o_ref):
    o_ref[...] = x_ref[...] + _rms(raw_ref[...].astype(F32), g_ref[...])


def _res_norm(raw, x, g, tm=256):
    m, d = x.shape
    tm = min(tm, m)
    rows = pl.BlockSpec((tm, d), lambda i: (i, 0))
    return pl.pallas_call(
        _res_norm_body,
        out_shape=jax.ShapeDtypeStruct((m, d), F32),
        grid=(m // tm,),
        in_specs=[rows, rows, pl.BlockSpec((1, d), lambda i: (0, 0))],
        out_specs=rows,
        compiler_params=_params(("parallel",)),
        name="residual_norm",
    )(raw, x, g.reshape(1, d))


def _swiglu_body(a_ref, wg_ref, wu_ref, o_ref, wg_scr, wu_scr):
    @pl.when(pl.program_id(1) == 0)
    def _():
        wg_scr[...] = wg_ref[...].astype(BF16)
        wu_scr[...] = wu_ref[...].astype(BF16)

    a = a_ref[...]
    g = jnp.dot(a, wg_scr[...], preferred_element_type=F32)
    u = jnp.dot(a, wu_scr[...], preferred_element_type=F32)
    o_ref[...] = (g * jax.nn.sigmoid(g) * u).astype(o_ref.dtype)


def _swiglu(a, wg, wu, tm=2048, tn=256):
    m, k = a.shape
    n = wg.shape[1]
    tm = min(tm, m)
    assert n % tn == 0
    vmem = 2 * tm * k * 2 + 2 * 2 * k * tn * 4 + 2 * tm * tn * 2 + 2 * k * tn * 2
    assert vmem + V7X_VMEM_SLACK_BYTES <= V7X_VMEM_BYTES
    return pl.pallas_call(
        _swiglu_body,
        out_shape=jax.ShapeDtypeStruct((m, n), BF16),
        grid=(n // tn, m // tm),
        in_specs=[pl.BlockSpec((tm, k), lambda j, i: (i, 0)),
                  pl.BlockSpec((k, tn), lambda j, i: (0, j)),
                  pl.BlockSpec((k, tn), lambda j, i: (0, j))],
        out_specs=pl.BlockSpec((tm, tn), lambda j, i: (i, j)),
        scratch_shapes=[pltpu.VMEM((k, tn), BF16), pltpu.VMEM((k, tn), BF16)],
        compiler_params=_params(("arbitrary", "arbitrary"), vmem=vmem + V7X_VMEM_SLACK_BYTES),
        name="ffn_swiglu",
    )(a, wg, wu)


def kernel(x, norm_mix_pre, w_in, ssm_a_re, ssm_a_im, ssm_log_dt, ssm_b_re, ssm_b_im, ssm_c_re, ssm_c_im, ssm_d, ssm_w_glu, ssm_b_glu, cmp_pe_k, cmp_w1_k, cmp_w2_k, cmp_pe_v, cmp_w1_v, cmp_w2_v, w_proj_a, w_proj_b, w_out, norm_mix_post, norm_ffn_pre, w_ffn_gate, w_ffn_up, w_ffn_down, norm_ffn_post):
    bsz, seq, d = x.shape
    m = bsz * seq
    depth = w_in.shape[0]
    h = x.reshape(m, d)
    o_q = SSM_WIDTH
    o_kvc = o_q + ATT_WIDTH
    o_kv = o_kvc + 2 * KV_WIDTH
    o_gn = o_kv + 4 * KV_WIDTH
    o_ga = o_gn + 3 * N_HEADS
    assert o_gn % LANES == 0 and o_ga + 2 * D_MODEL == w_in.shape[2]
    bf = lambda a: a.astype(BF16)
    tn_in = 512
    nb_u, nb_q = SSM_WIDTH // tn_in, ATT_WIDTH // tn_in
    qk_scale = HEAD_DIM ** -0.5 * LOG2E
    for l in range(depth):
        wt = jnp.swapaxes(w_in[l], 0, 1)
        w_g = _rows_transposed_cast(wt, o_ga, 2 * D_MODEL, BF16)

        hn, gn = _rmsnorm_gates(h, norm_mix_pre[l], wt, o_gn // LANES)
        proj_f32 = _matmul_f32w(hn, wt, SSM_WIDTH + 2 * KV_WIDTH, F32, 1024, tn_in, trans_w=True,
                                col_block=lambda j: jnp.where(j < nb_u, j, j - nb_u + o_kvc // tn_in),
                                name="in_proj_f32")
        qkv = _matmul_f32w(hn, wt, ATT_WIDTH + 4 * KV_WIDTH, BF16, 2048, tn_in, trans_w=True,
                           col_block=lambda j: jnp.where(j < nb_q, j + o_q // tn_in, j - nb_q + o_kv // tn_in),
                           scale_fn=lambda j: jnp.where(j < nb_q, qk_scale, 1.0),
                           name="in_proj_bf16")

        y = _s5_mixer(proj_f32.reshape(bsz, seq, -1), ssm_a_re[l], ssm_a_im[l], ssm_log_dt[l],
                      ssm_b_re[l], ssm_b_im[l], ssm_c_re[l], ssm_c_im[l], ssm_d[l])
        y_a = _glu(y.reshape(m, SSM_WIDTH), bf(ssm_w_glu[l]), ssm_b_glu[l])

        pe = jnp.stack([cmp_pe_k[l], cmp_pe_v[l]])
        w1 = bf(jnp.stack([cmp_w1_k[l], cmp_w1_v[l]]))
        w2 = bf(jnp.stack([cmp_w2_k[l], cmp_w2_v[l]]))
        kcv = _compress(proj_f32, SSM_WIDTH, pe, w1, w2, bsz, seq)
        o_cmp, bias = _cmp_attn(qkv, kcv, bsz, seq)
        o_sel = _sel_attn(qkv, bias, bsz, seq)
        y_b = _win_attn_combine(qkv, o_cmp, o_sel, gn, bsz, seq)

        merged = _merge(hn, y_a, y_b, w_g, bf(w_proj_a[l]), bf(w_proj_b[l]))
        mix = _matmul_f32w(merged, w_out[l], D_MODEL, BF16, 2048, 512, name="out_proj")
        h, hn2 = _res_norm2(mix, h, norm_mix_post[l], norm_ffn_pre[l])

        act = _swiglu(hn2, w_ffn_gate[l], w_ffn_up[l])
        f = _matmul(act, bf(w_ffn_down[l]), BF16, 512, 512, name="ffn_down")
        h = _res_norm(f, h, norm_ffn_post[l])
    return h.reshape(bsz, seq, d)
```

```python
import functools
import math

import jax
import jax.numpy as jnp
from jax import lax
from jax.experimental import pallas as pl
from jax.experimental.pallas import tpu as pltpu

F32 = jnp.float32
BF16 = jnp.bfloat16

D_MODEL = 4096
SSM_WIDTH = 2048
SSM_GROUP = 16
SSM_GROUPS = 128
SSM_STATE = 64
N_HEADS = 16
HEAD_DIM = 128
N_KV = 4
HPG = 4
ATT_WIDTH = 2048
KV_WIDTH = 512
L_CMP = 32
STRIDE_CMP = 16
L_SEL = 64
N_SEL = 16
WINDOW = 512
RMS_EPS = 1e-6
NEG_INF = -1e30
FORCE_SCORE = 1e9
TAKEN_SCORE = -3e38
LOG2E = math.log2(math.e)

V7X_VMEM_BYTES = 64 * 1024 * 1024
V7X_VMEM_LIMIT_BYTES = 56 * 1024 * 1024
V7X_VMEM_SLACK_BYTES = 4 * 1024 * 1024
LANES = 128
SUBLANES = 8

S5_GT = 16
S5_NC = S5_GT * SSM_GROUP
S5_NP = S5_GT * SSM_STATE
S5_TILES = SSM_GROUPS // S5_GT
S5_SLABS = 2 * S5_NP // LANES

ATT_ROWS = 256
SEL_TQ, SEL_RB = 512, 64
SEL_GROUP = 4


def _params(sem, vmem=V7X_VMEM_LIMIT_BYTES):
    return pltpu.CompilerParams(dimension_semantics=sem, vmem_limit_bytes=vmem)


def _lane_tile(x, n):
    return jnp.concatenate([x] * n, axis=1)


def _rmsnorm_gates_body(x_ref, g_ref, w_ref, hn_ref, gn_ref, wbf_scr):
    @pl.when(pl.program_id(0) == 0)
    def _():
        wbf_scr[...] = w_ref[...].astype(BF16)

    x = x_ref[...]
    ms = jnp.mean(x * x, axis=-1, keepdims=True)
    hn = (x * lax.rsqrt(ms + RMS_EPS) * g_ref[...]).astype(BF16)
    hn_ref[...] = hn
    gn_ref[...] = lax.dot_general(hn, wbf_scr[...], (((1,), (1,)), ((), ())), preferred_element_type=F32)


def _rmsnorm_gates(x, gain, wt, feature_block, tm=256):
    m, d = x.shape
    tm = min(tm, m)
    return pl.pallas_call(
        _rmsnorm_gates_body,
        out_shape=(jax.ShapeDtypeStruct((m, d), BF16), jax.ShapeDtypeStruct((m, LANES), F32)),
        grid=(m // tm,),
        in_specs=[pl.BlockSpec((tm, d), lambda i: (i, 0)),
                  pl.BlockSpec((1, d), lambda i: (0, 0)),
                  pl.BlockSpec((LANES, d), lambda i: (feature_block, 0))],
        out_specs=(pl.BlockSpec((tm, d), lambda i: (i, 0)), pl.BlockSpec((tm, LANES), lambda i: (i, 0))),
        scratch_shapes=[pltpu.VMEM((LANES, d), BF16)],
        compiler_params=_params(("arbitrary",)),
        name="rmsnorm_gates",
    )(x, gain.reshape(1, d), wt)


def _mm_body(a_ref, w_ref, o_ref):
    o_ref[...] = jnp.dot(a_ref[...], w_ref[...], preferred_element_type=F32).astype(o_ref.dtype)


def _matmul(a, w, out_dtype, tm, tn, name="matmul"):
    m, k = a.shape
    n = w.shape[1]
    tm, tn = min(tm, m), min(tn, n)
    assert m % tm == 0 and n % tn == 0
    return pl.pallas_call(
        _mm_body,
        out_shape=jax.ShapeDtypeStruct((m, n), out_dtype),
        grid=(m // tm, n // tn),
        in_specs=[pl.BlockSpec((tm, k), lambda i, j: (i, 0)), pl.BlockSpec((k, tn), lambda i, j: (0, j))],
        out_specs=pl.BlockSpec((tm, tn), lambda i, j: (i, j)),
        compiler_params=_params(("parallel", "arbitrary")),
        name=name,
    )(a, w)


_NT = (((1,), (1,)), ((), ()))


def _mm_f32w_body(a_ref, w_ref, o_ref, wbf_scr, *, scale_fn, trans_w):
    @pl.when(pl.program_id(1) == 0)
    def _():
        wbf_scr[...] = w_ref[...].astype(BF16)

    if trans_w:
        acc = lax.dot_general(a_ref[...], wbf_scr[...], _NT, preferred_element_type=F32)
    else:
        acc = jnp.dot(a_ref[...], wbf_scr[...], preferred_element_type=F32)
    if scale_fn is not None:
        acc = acc * scale_fn(pl.program_id(0))
    o_ref[...] = acc.astype(o_ref.dtype)


def _matmul_f32w(a, w, n_out, out_dtype, tm, tn, col_block=None, scale_fn=None, trans_w=False,
                 name="matmul_f32w"):
    m, k = a.shape
    tm = min(tm, m)
    assert m % tm == 0 and n_out % tn == 0
    col_block = col_block or (lambda j: j)
    if trans_w:
        w_spec = pl.BlockSpec((tn, k), lambda j, i: (col_block(j), 0))
        w_scratch = pltpu.VMEM((tn, k), BF16)
    else:
        w_spec = pl.BlockSpec((k, tn), lambda j, i: (0, col_block(j)))
        w_scratch = pltpu.VMEM((k, tn), BF16)
    vmem = 2 * tm * k * 2 + 2 * k * tn * 4 + 2 * tm * tn * jnp.dtype(out_dtype).itemsize + k * tn * 2
    assert vmem + V7X_VMEM_SLACK_BYTES <= V7X_VMEM_BYTES
    return pl.pallas_call(
        functools.partial(_mm_f32w_body, scale_fn=scale_fn, trans_w=trans_w),
        out_shape=jax.ShapeDtypeStruct((m, n_out), out_dtype),
        grid=(n_out // tn, m // tm),
        in_specs=[pl.BlockSpec((tm, k), lambda j, i: (i, 0)), w_spec],
        out_specs=pl.BlockSpec((tm, tn), lambda j, i: (i, j)),
        scratch_shapes=[w_scratch],
        compiler_params=_params(("arbitrary", "arbitrary"), vmem=vmem + V7X_VMEM_SLACK_BYTES),
        name=name,
    )(a, w)


def _rows_transposed_cast_body(w_ref, o_ref):
    o_ref[...] = w_ref[...].T.astype(o_ref.dtype)


def _rows_transposed_cast(wt, row0, n_rows, out_dtype, tr=512):
    k = wt.shape[1]
    assert row0 % (2 * SUBLANES) == 0 and n_rows % tr == 0
    return pl.pallas_call(
        _rows_transposed_cast_body,
        out_shape=jax.ShapeDtypeStruct((k, n_rows), out_dtype),
        grid=(n_rows // tr,),
        in_specs=[pl.BlockSpec((pl.Element(tr), pl.Element(k)),
                               lambda c: (pl.multiple_of(row0 + c * tr, 2 * SUBLANES), 0))],
        out_specs=pl.BlockSpec((k, tr), lambda c: (0, c)),
        compiler_params=_params(("parallel",)),
        name="gate_weight_cast",
    )(wt)


def _s5_body(u_ref, are_ref, aim_ref, ldt_ref, bre_ref, bim_ref, cre_ref, cim_ref, d_ref, y_ref,
             abar_scr, bbar_scr, cmat_scr, h_scr, st_scr, *, tc, bsz, tps):
    ti = pl.program_id(1)
    half = S5_SLABS // 2

    @pl.when(ti == 0)
    def _():
        row_tile = lax.broadcasted_iota(jnp.int32, (SUBLANES, S5_NP), 0) // bsz
        same_group = (lax.broadcasted_iota(jnp.int32, (S5_NC, S5_NP), 0) // SSM_GROUP
                      == lax.broadcasted_iota(jnp.int32, (S5_NC, S5_NP), 1) // SSM_STATE)
        bdiag = lambda x: jnp.where(same_group, jnp.concatenate([x] * S5_GT, axis=0), 0.0)
        abr8 = jnp.zeros((SUBLANES, S5_NP), F32)
        abi8 = jnp.zeros((SUBLANES, S5_NP), F32)
        for k in range(tps):
            cols = slice(k * S5_NP, (k + 1) * S5_NP)
            ar, ai = are_ref[:, cols], aim_ref[:, cols]
            dt = jnp.exp(ldt_ref[:, cols])
            decay = jnp.exp(dt * ar)
            abr, abi = decay * jnp.cos(dt * ai), decay * jnp.sin(dt * ai)
            den = ar * ar + ai * ai
            zr = ((abr - 1.0) * ar + abi * ai) / den
            zi = (abi * ar - (abr - 1.0) * ai) / den
            abr8 = jnp.where(row_tile == k, abr, abr8)
            abi8 = jnp.where(row_tile == k, abi, abi8)
            br, bi = bdiag(bre_ref[k]), bdiag(bim_ref[k])
            bbar_scr[k, :, 0:S5_NP] = (zr * br - zi * bi).astype(BF16)
            bbar_scr[k, :, S5_NP:2 * S5_NP] = (zr * bi + zi * br).astype(BF16)
            cmat_scr[k, :, 0:S5_NP] = bdiag(cre_ref[k]).astype(BF16)
            cmat_scr[k, :, S5_NP:2 * S5_NP] = (-bdiag(cim_ref[k])).astype(BF16)
        abar_scr[0:SUBLANES, :] = abr8
        abar_scr[SUBLANES:2 * SUBLANES, :] = abi8
        st_scr[...] = jnp.zeros_like(st_scr)

    for k in range(tps):
        cols = slice(k * S5_NC, (k + 1) * S5_NC)
        ub = jnp.concatenate([u_ref[b, :, cols] for b in range(bsz)], axis=0).astype(BF16)
        for s2 in range(S5_SLABS // 2):
            bu = jnp.dot(ub, bbar_scr[k, :, s2 * 2 * LANES:(s2 + 1) * 2 * LANES], preferred_element_type=F32)
            for b in range(bsz):
                for e in range(2):
                    h_scr[2 * s2 + e, pl.ds(k * bsz + b, tc, stride=SUBLANES), :] = bu[b * tc:(b + 1) * tc,
                                                                                      e * LANES:(e + 1) * LANES]

    ar = abar_scr[0:SUBLANES, :]
    ai = abar_scr[SUBLANES:2 * SUBLANES, :]

    def step(t, carry):
        hr, hi = carry
        r0 = pl.multiple_of(t * SUBLANES, SUBLANES)
        bur = jnp.concatenate([h_scr[s, pl.ds(r0, SUBLANES), :] for s in range(half)], axis=1)
        bui = jnp.concatenate([h_scr[half + s, pl.ds(r0, SUBLANES), :] for s in range(half)], axis=1)
        nhr = ar * hr - ai * hi + bur
        nhi = ar * hi + ai * hr + bui
        for s in range(half):
            h_scr[s, pl.ds(r0, SUBLANES), :] = nhr[:, s * LANES:(s + 1) * LANES]
            h_scr[half + s, pl.ds(r0, SUBLANES), :] = nhi[:, s * LANES:(s + 1) * LANES]
        return nhr, nhi

    hr, hi = lax.fori_loop(0, tc, step, (st_scr[0:SUBLANES, :], st_scr[SUBLANES:2 * SUBLANES, :]), unroll=2)
    st_scr[0:SUBLANES, :] = hr
    st_scr[SUBLANES:2 * SUBLANES, :] = hi

    for k in range(tps):
        cols = slice(k * S5_NC, (k + 1) * S5_NC)

        hb = jnp.concatenate(
            [jnp.concatenate([h_scr[s, pl.ds(k * bsz + b, tc, stride=SUBLANES), :] for s in range(S5_SLABS)], axis=1)
             for b in range(bsz)], axis=0).astype(BF16)
        ch = lax.dot_general(hb, cmat_scr[k], _NT, preferred_element_type=F32)
        for b in range(bsz):
            y = ch[b * tc:(b + 1) * tc, :] + d_ref[:, cols] * u_ref[b, :, cols]
            y_ref[b, :, cols] = jax.nn.gelu(y)


def _s5_mixer(u3, a_re, a_im, log_dt, b_re, b_im, c_re, c_im, d_skip, tc=256):
    bsz, seq, _ = u3.shape
    assert SUBLANES % bsz == 0
    tps = min(SUBLANES // bsz, S5_TILES)
    assert tps * bsz == SUBLANES
    tc = min(tc, seq)

    def per_tile_b(b):
        return (b.reshape(S5_TILES, S5_GT, SSM_STATE, SSM_GROUP).transpose(0, 3, 1, 2)
                .reshape(S5_TILES, SSM_GROUP, S5_NP))

    def per_tile_c(c):
        return (c.reshape(S5_TILES, S5_GT, SSM_GROUP, SSM_STATE).transpose(0, 2, 1, 3)
                .reshape(S5_TILES, SSM_GROUP, S5_NP))

    flat = lambda a: a.reshape(1, SSM_GROUPS * SSM_STATE)
    ldt = jnp.repeat(log_dt, SSM_STATE).reshape(1, SSM_GROUPS * SSM_STATE)
    vec_spec = pl.BlockSpec((1, tps * S5_NP), lambda g, t: (0, g))
    bc_spec = pl.BlockSpec((tps, SSM_GROUP, S5_NP), lambda g, t: (g, 0, 0))
    return pl.pallas_call(
        functools.partial(_s5_body, tc=tc, bsz=bsz, tps=tps),
        out_shape=jax.ShapeDtypeStruct((bsz, seq, SSM_WIDTH), F32),
        grid=(S5_TILES // tps, seq // tc),
        in_specs=[
            pl.BlockSpec((bsz, tc, tps * S5_NC), lambda g, t: (0, t, g)),
            vec_spec, vec_spec, vec_spec, bc_spec, bc_spec, bc_spec, bc_spec,
            pl.BlockSpec((1, tps * S5_NC), lambda g, t: (0, g)),
        ],
        out_specs=pl.BlockSpec((bsz, tc, tps * S5_NC), lambda g, t: (0, t, g)),
        scratch_shapes=[
            pltpu.VMEM((2 * SUBLANES, S5_NP), F32),
            pltpu.VMEM((tps, S5_NC, 2 * S5_NP), BF16),
            pltpu.VMEM((tps, S5_NC, 2 * S5_NP), BF16),
            pltpu.VMEM((S5_SLABS, tc * SUBLANES, LANES), F32),
            pltpu.VMEM((2 * SUBLANES, S5_NP), F32),
        ],
        compiler_params=_params(("arbitrary", "arbitrary")),
        name="s5_scan",
    )(u3, flat(a_re), flat(a_im), ldt, per_tile_b(b_re), per_tile_b(b_im), per_tile_c(c_re), per_tile_c(c_im),
      d_skip.reshape(1, SSM_WIDTH))


def _glu_body(y_ref, w_ref, b_ref, o_ref, ybf_scr, *, tn):
    j = pl.program_id(1)

    @pl.when(j == 0)
    def _():
        ybf_scr[...] = y_ref[...].astype(BF16)

    z = jnp.dot(ybf_scr[...], w_ref[...], preferred_element_type=F32) + b_ref[...]
    yt = y_ref[:, pl.ds(pl.multiple_of(j * tn, tn), tn)]
    o_ref[...] = (yt * jax.nn.sigmoid(z)).astype(o_ref.dtype)


def _glu(y, w_bf, bias, tm=1024, tn=1024):
    m, k = y.shape
    tm = min(tm, m)
    return pl.pallas_call(
        functools.partial(_glu_body, tn=tn),
        out_shape=jax.ShapeDtypeStruct((m, k), BF16),
        grid=(m // tm, k // tn),
        in_specs=[pl.BlockSpec((tm, k), lambda i, j: (i, 0)),
                  pl.BlockSpec((k, tn), lambda i, j: (0, j)),
                  pl.BlockSpec((1, tn), lambda i, j: (0, j))],
        out_specs=pl.BlockSpec((tm, tn), lambda i, j: (i, j)),
        scratch_shapes=[pltpu.VMEM((tm, k), BF16)],
        compiler_params=_params(("parallel", "arbitrary")),
        name="s5_glu",
    )(y, w_bf, bias.reshape(1, k))


def _compress_body(x_ref, pe_ref, w1_ref, w2_ref, o_ref, *, ncmp):
    half = L_CMP // 2
    acc_a = jnp.zeros((ncmp, HEAD_DIM), F32)
    acc_b = jnp.zeros((ncmp, HEAD_DIM), F32)
    for r in range(half):
        xr = x_ref[pl.ds(r, ncmp, stride=STRIDE_CMP), :]
        xa = (xr + pe_ref[r:r + 1, :]).astype(BF16)
        xb = (xr + pe_ref[half + r:half + r + 1, :]).astype(BF16)
        acc_a += jnp.dot(xa, w1_ref[r * HEAD_DIM:(r + 1) * HEAD_DIM, :], preferred_element_type=F32)
        acc_b += jnp.dot(xb, w1_ref[(half + r) * HEAD_DIM:(half + r + 1) * HEAD_DIM, :],
                         preferred_element_type=F32)
    pre = acc_a + jnp.concatenate([acc_b[1:], acc_b[:1]], axis=0)
    o_ref[...] = jnp.dot(jax.nn.gelu(pre).astype(BF16), w2_ref[...], preferred_element_type=F32).astype(o_ref.dtype)


def _compress(proj_f32, col0, pe, w1_bf, w2_bf, bsz, seq):
    ncmp = seq // STRIDE_CMP
    cb = col0 // HEAD_DIM
    return pl.pallas_call(
        functools.partial(_compress_body, ncmp=ncmp),
        out_shape=jax.ShapeDtypeStruct((bsz, 2, N_KV, ncmp, HEAD_DIM), BF16),
        grid=(bsz, 2, N_KV),
        in_specs=[pl.BlockSpec((seq, HEAD_DIM), lambda b, w, g: (b, cb + w * N_KV + g)),
                  pl.BlockSpec((None, L_CMP, HEAD_DIM), lambda b, w, g: (w, 0, 0)),
                  pl.BlockSpec((None, L_CMP * HEAD_DIM, HEAD_DIM), lambda b, w, g: (w, 0, 0)),
                  pl.BlockSpec((None, HEAD_DIM, HEAD_DIM), lambda b, w, g: (w, 0, 0))],
        out_specs=pl.BlockSpec((None, None, None, ncmp, HEAD_DIM), lambda b, w, g: (b, w, g, 0, 0)),
        compiler_params=_params(("parallel", "parallel", "parallel")),
        name="nsa_compress",
    )(proj_f32, pe, w1_bf, w2_bf)


def _cmp_attn_body(q_ref, kc_ref, vc_ref, o_ref, bias_ref, *, tq, ncmp, nblk, ntop):
    qi = pl.program_id(2)
    t0 = qi * tq
    tpos = t0 + lax.broadcasted_iota(jnp.int32, (tq, ncmp), 0)
    blk_end = lax.broadcasted_iota(jnp.int32, (tq, ncmp), 1) * STRIDE_CMP + (L_CMP - 1)
    ok = blk_end <= tpos
    okf = ok.astype(F32)
    kc = kc_ref[...]
    vc = vc_ref[...]
    psum = jnp.zeros((tq, ncmp), F32)
    for h in range(HPG):
        qh = q_ref[:, h * HEAD_DIM:(h + 1) * HEAD_DIM]
        s = lax.dot_general(qh, kc, (((1,), (1,)), ((), ())), preferred_element_type=F32)
        s = jnp.where(ok, s, NEG_INF)
        e = jnp.exp2(s - jnp.max(s, axis=-1, keepdims=True))
        p = e / jnp.sum(e, axis=-1, keepdims=True) * okf
        o_ref[:, h * HEAD_DIM:(h + 1) * HEAD_DIM] = jnp.dot(p.astype(BF16), vc, preferred_element_type=F32)
        psum = psum + p

    jj = lax.broadcasted_iota(jnp.int32, (nblk, ncmp), 0)
    nn = lax.broadcasted_iota(jnp.int32, (nblk, ncmp), 1)
    ov = ((nn * STRIDE_CMP < (jj + 1) * L_SEL) & (nn * STRIDE_CMP + L_CMP > jj * L_SEL)).astype(BF16)
    p_hi = psum.astype(BF16)
    p_lo = (psum - p_hi.astype(F32)).astype(BF16)
    nt = (((1,), (1,)), ((), ()))
    imp = (lax.dot_general(ov, p_hi, nt, preferred_element_type=F32)
           + lax.dot_general(ov, p_lo, nt, preferred_element_type=F32))

    jb = lax.broadcasted_iota(jnp.int32, (nblk, tq), 0)
    tt = t0 + lax.broadcasted_iota(jnp.int32, (nblk, tq), 1)
    cur = tt // L_SEL
    allowed = jb * L_SEL <= tt
    forced = (jb == 0) | (jb == cur) | (jb == cur - 1)
    score = jnp.where(forced, FORCE_SCORE, jnp.where(allowed, imp, NEG_INF))
    taken = jnp.zeros((nblk, tq), F32)
    for _ in range(ntop):
        best = jnp.max(score, axis=0, keepdims=True)
        first = jnp.min(jnp.where(score == best, jb, nblk), axis=0, keepdims=True)
        pick = jb == first
        taken = jnp.where(pick, 1.0, taken)
        score = jnp.where(pick, TAKEN_SCORE, score)
    bias_t = jnp.where(taken > 0.5, 0.0, NEG_INF)
    bias_t = jnp.concatenate([bias_t, jnp.zeros((LANES - nblk, tq), F32)], axis=0)
    bias_ref[...] = bias_t.T.astype(bias_ref.dtype)


def _cmp_attn(qkv, kcv, bsz, seq, tq=2048):
    tq = min(tq, seq)
    ncmp = seq // STRIDE_CMP
    nblk = seq // L_SEL
    ntop = min(N_SEL, nblk)
    nq = seq // tq
    return pl.pallas_call(
        functools.partial(_cmp_attn_body, tq=tq, ncmp=ncmp, nblk=nblk, ntop=ntop),
        out_shape=(jax.ShapeDtypeStruct((bsz * seq, ATT_WIDTH), F32),
                   jax.ShapeDtypeStruct((bsz, N_KV, seq, LANES), BF16)),
        grid=(bsz, N_KV, nq),
        in_specs=[pl.BlockSpec((tq, HPG * HEAD_DIM), lambda b, g, i: (b * nq + i, g)),
                  pl.BlockSpec((None, None, None, ncmp, HEAD_DIM), lambda b, g, i: (b, 0, g, 0, 0)),
                  pl.BlockSpec((None, None, None, ncmp, HEAD_DIM), lambda b, g, i: (b, 1, g, 0, 0))],
        out_specs=(pl.BlockSpec((tq, HPG * HEAD_DIM), lambda b, g, i: (b * nq + i, g)),
                   pl.BlockSpec((None, None, tq, LANES), lambda b, g, i: (b, g, i, 0))),
        compiler_params=_params(("parallel", "parallel", "parallel")),
        name="nsa_cmp_attn_topk",
    )(qkv, kcv, kcv)


def _sel_attn_body(q_ref, k_ref, v_ref, bias_ref, o_ref, qa_scr, m_scr, acc_scr, s_scr, p_scr, alpha_scr,
                   *, tq, rb):
    qi = pl.program_id(2)
    tk = tq
    m_scr[...] = jnp.full_like(m_scr, NEG_INF)
    acc_scr[...] = jnp.zeros_like(acc_scr)
    for h in range(HPG):
        qa_scr[h, :, 0:HEAD_DIM] = q_ref[:, h * HEAD_DIM:(h + 1) * HEAD_DIM]
        qa_scr[h, :, HEAD_DIM:2 * HEAD_DIM] = bias_ref[...]
    nt = (((1,), (1,)), ((), ()))

    def key_tile(k0, nkeys):
        key_blk = k0 // L_SEL + lax.broadcasted_iota(jnp.int32, (nkeys, LANES), 0) // L_SEL
        onehot = jnp.where(lax.broadcasted_iota(jnp.int32, (nkeys, LANES), 1) == key_blk, 1.0, 0.0).astype(BF16)
        return jnp.concatenate([k_ref[pl.ds(k0, nkeys), :], onehot], axis=1)

    def scores(h, k_aug, r0, r1, nk):
        s_scr[h, r0:r1, 0:nk] = lax.dot_general(qa_scr[h, r0:r1, :], k_aug[0:nk, :], nt,
                                                preferred_element_type=F32)

    def softmax(h, r0, r1, nk, d0):
        for r in range(r0 // rb, r1 // rb):
            rows = slice(r * rb, (r + 1) * rb)
            pieces = []
            if d0 > 0:
                pieces.append(s_scr[h, rows, 0:d0])
            if d0 < nk:
                col = lax.broadcasted_iota(jnp.int32, (rb, nk - d0), 1)
                row = lax.broadcasted_iota(jnp.int32, (rb, nk - d0), 0) + r * rb
                pieces.append(jnp.where(col <= row, s_scr[h, rows, d0:nk], NEG_INF))
            m_prev = m_scr[h, rows, :]
            m_next = m_prev
            for s in pieces:
                m_next = jnp.maximum(m_next, jnp.max(s, axis=-1, keepdims=True))
            c0 = 0
            for s in pieces:
                w = s.shape[1]
                p_scr[h, rows, c0:c0 + w] = jnp.exp2(s - _lane_tile(m_next, w // LANES)).astype(BF16)
                c0 += w
            alpha_scr[h, rows, :] = jnp.exp2(m_prev - m_next)
            m_scr[h, rows, :] = m_next

    def values(h, v_aug, r0, r1, nk):
        acc_scr[h, r0:r1, :] = (_lane_tile(alpha_scr[h, r0:r1, :], 2) * acc_scr[h, r0:r1, :]
                                + jnp.dot(p_scr[h, r0:r1, 0:nk], v_aug[0:nk, :], preferred_element_type=F32))

    def tile(kt, nkeys, diag_last):
        k0 = pl.multiple_of(kt * tk, tk)
        k_aug = key_tile(k0, nkeys)
        v_aug = jnp.concatenate([v_ref[pl.ds(k0, nkeys), :], jnp.ones((nkeys, LANES), BF16)], axis=1)
        if diag_last:
            d0 = nkeys - tk
            parts = [(0, tq // 2, nkeys - tk // 2), (tq // 2, tq, nkeys)]
        else:
            d0 = nkeys
            parts = [(0, tq, nkeys)]
        for h in range(HPG):
            for r0, r1, nk in parts:
                scores(h, k_aug, r0, r1, nk)
        for h in range(HPG):
            for r0, r1, nk in parts:
                softmax(h, r0, r1, nk, d0)
        for h in range(HPG):
            for r0, r1, nk in parts:
                values(h, v_aug, r0, r1, nk)

    def tile_group(j, carry):
        tile(SEL_GROUP * j, SEL_GROUP * tk, False)
        return carry

    n_groups = qi // SEL_GROUP
    lax.fori_loop(0, n_groups, tile_group, 0)
    for run in range(1, SEL_GROUP + 1):
        @pl.when(qi % SEL_GROUP == run - 1)
        def _(run=run):
            tile(n_groups * SEL_GROUP, run * tk, True)

    for h in range(HPG):
        o_ref[:, h * HEAD_DIM:(h + 1) * HEAD_DIM] = (acc_scr[h, :, 0:HEAD_DIM]
                                                     / acc_scr[h, :, HEAD_DIM:2 * HEAD_DIM])


def _sel_attn(qkv, bias, bsz, seq, tq=SEL_TQ, rb=SEL_RB):
    tq = min(tq, seq)
    rb = min(rb, tq)
    nq = seq // tq
    kcol = ATT_WIDTH // HEAD_DIM
    return pl.pallas_call(
        functools.partial(_sel_attn_body, tq=tq, rb=rb),
        out_shape=jax.ShapeDtypeStruct((bsz * seq, ATT_WIDTH), F32),
        grid=(bsz, N_KV, nq),
        in_specs=[pl.BlockSpec((tq, HPG * HEAD_DIM), lambda b, g, i: (b * nq + i, g)),
                  pl.BlockSpec((seq, HEAD_DIM), lambda b, g, i: (b, kcol + g)),
                  pl.BlockSpec((seq, HEAD_DIM), lambda b, g, i: (b, kcol + N_KV + g)),
                  pl.BlockSpec((None, None, tq, LANES), lambda b, g, i: (b, g, i, 0))],
        out_specs=pl.BlockSpec((tq, HPG * HEAD_DIM), lambda b, g, i: (b * nq + i, g)),
        scratch_shapes=[pltpu.VMEM((HPG, tq, 2 * HEAD_DIM), BF16),
                        pltpu.VMEM((HPG, tq, LANES), F32),
                        pltpu.VMEM((HPG, tq, 2 * HEAD_DIM), F32),
                        pltpu.VMEM((HPG, tq, SEL_GROUP * tq), F32),
                        pltpu.VMEM((HPG, tq, SEL_GROUP * tq), BF16),
                        pltpu.VMEM((HPG, tq, LANES), F32)],
        compiler_params=_params(("parallel", "parallel", "arbitrary")),
        name="nsa_sel_attn",
    )(qkv, qkv, qkv, bias)


def _win_attn_body(q_ref, kc_ref, kp_ref, vc_ref, vp_ref, ocmp_ref, osel_ref, gate_ref, o_ref, *, tq):
    qi = pl.program_id(2)
    nt = (((1,), (1,)), ((), ()))
    ones = jnp.ones((tq, LANES), BF16)
    vc_aug = jnp.concatenate([vc_ref[...], ones], axis=1)
    vp_aug = jnp.concatenate([vp_ref[...], ones], axis=1)
    for r in range(tq // ATT_ROWS):
        rows = slice(r * ATT_ROWS, (r + 1) * ATT_ROWS)
        n_cur, p_lo = (r + 1) * ATT_ROWS, r * ATT_ROWS
        row_c = lax.broadcasted_iota(jnp.int32, (ATT_ROWS, n_cur), 0) + r * ATT_ROWS
        ok_cur = lax.broadcasted_iota(jnp.int32, (ATT_ROWS, n_cur), 1) <= row_c
        row_p = lax.broadcasted_iota(jnp.int32, (ATT_ROWS, tq - p_lo), 0) + r * ATT_ROWS
        ok_prev = (lax.broadcasted_iota(jnp.int32, (ATT_ROWS, tq - p_lo), 1) + p_lo > row_p) & (qi > 0)
        shift = (LANES - HPG * pl.program_id(1)) % LANES
        gates = jax.nn.sigmoid(pltpu.roll(gate_ref[rows, :], shift, 1))
        for h in range(HPG):
            sl = slice(h * HEAD_DIM, (h + 1) * HEAD_DIM)
            qh = q_ref[rows, sl]
            s1 = jnp.where(ok_cur, lax.dot_general(qh, kc_ref[0:n_cur, :], nt, preferred_element_type=F32), NEG_INF)
            s0 = jnp.where(ok_prev, lax.dot_general(qh, kp_ref[p_lo:tq, :], nt, preferred_element_type=F32), NEG_INF)
            m = jnp.maximum(jnp.max(s1, axis=-1, keepdims=True), jnp.max(s0, axis=-1, keepdims=True))
            p1 = jnp.exp2(s1 - m).astype(BF16)
            p0 = jnp.exp2(s0 - m).astype(BF16)
            acc = (jnp.dot(p1, vc_aug[0:n_cur, :], preferred_element_type=F32)
                   + jnp.dot(p0, vp_aug[p_lo:tq, :], preferred_element_type=F32))
            ow = acc[:, 0:HEAD_DIM] / acc[:, HEAD_DIM:2 * HEAD_DIM]
            g_cmp = gates[:, h:h + 1]
            g_sel = gates[:, N_HEADS + h:N_HEADS + h + 1]
            g_win = gates[:, 2 * N_HEADS + h:2 * N_HEADS + h + 1]
            o_ref[rows, sl] = (g_cmp * ocmp_ref[rows, sl] + g_sel * osel_ref[rows, sl]
                               + g_win * ow).astype(o_ref.dtype)


def _win_attn_combine(qkv, o_cmp, o_sel, gates, bsz, seq):
    tq = WINDOW
    assert seq % tq == 0
    nq = seq // tq
    kcol = ATT_WIDTH // HEAD_DIM + 2 * N_KV
    qspec = pl.BlockSpec((tq, HPG * HEAD_DIM), lambda b, g, i: (b * nq + i, g))
    cur = lambda c: pl.BlockSpec((tq, HEAD_DIM), lambda b, g, i: (b * nq + i, kcol + c * N_KV + g))
    prev = lambda c: pl.BlockSpec((tq, HEAD_DIM),
                                  lambda b, g, i: (b * nq + jnp.maximum(i - 1, 0), kcol + c * N_KV + g))
    return pl.pallas_call(
        functools.partial(_win_attn_body, tq=tq),
        out_shape=jax.ShapeDtypeStruct((bsz * seq, ATT_WIDTH), BF16),
        grid=(bsz, N_KV, nq),
        in_specs=[qspec, cur(0), prev(0), cur(1), prev(1), qspec, qspec,
                  pl.BlockSpec((tq, LANES), lambda b, g, i: (b * nq + i, 0))],
        out_specs=qspec,
        compiler_params=_params(("parallel", "parallel", "parallel")),
        name="nsa_win_attn_gate",
    )(qkv, qkv, qkv, qkv, qkv, o_cmp, o_sel, gates)


def _merge_body(hn_ref, ya_ref, yb_ref, wga_ref, wgb_ref, wa_ref, wb_ref, o_ref):
    hn = hn_ref[...]
    ga = jnp.dot(hn, wga_ref[...], preferred_element_type=F32)
    gb = jnp.dot(hn, wgb_ref[...], preferred_element_type=F32)
    pa = jnp.dot(ya_ref[...], wa_ref[...], preferred_element_type=F32)
    pb = jnp.dot(yb_ref[...], wb_ref[...], preferred_element_type=F32)
    o_ref[...] = (jax.nn.sigmoid(ga) * pa + jax.nn.sigmoid(gb) * pb).astype(o_ref.dtype)


def _merge(hn, ya, yb, wg, wa, wb, tm=512, tn=512):
    m, d = hn.shape
    ka = ya.shape[1]
    tm = min(tm, m)
    row = lambda k: pl.BlockSpec((tm, k), lambda i, j: (i, 0))
    col = lambda k: pl.BlockSpec((k, tn), lambda i, j: (0, j))
    col_b = pl.BlockSpec((d, tn), lambda i, j: (0, d // tn + j))
    return pl.pallas_call(
        _merge_body,
        out_shape=jax.ShapeDtypeStruct((m, d), BF16),
        grid=(m // tm, d // tn),
        in_specs=[row(d), row(ka), row(ka), col(d), col_b, col(ka), col(ka)],
        out_specs=pl.BlockSpec((tm, tn), lambda i, j: (i, j)),
        compiler_params=_params(("parallel", "arbitrary")),
        name="mixer_merge",
    )(hn, ya, yb, wg, wg, wa, wb)


def _rms(x, g):
    return x * lax.rsqrt(jnp.mean(x * x, axis=-1, keepdims=True) + RMS_EPS) * g


def _res_norm2_body(raw_ref, x_ref, gpost_ref, gpre_ref, h_ref, hn_ref):
    h = x_ref[...] + _rms(raw_ref[...].astype(F32), gpost_ref[...])
    h_ref[...] = h
    hn_ref[...] = _rms(h, gpre_ref[...]).astype(hn_ref.dtype)


def _res_norm2(raw, x, g_post, g_pre, tm=256):
    m, d = x.shape
    tm = min(tm, m)
    rows = pl.BlockSpec((tm, d), lambda i: (i, 0))
    vec = pl.BlockSpec((1, d), lambda i: (0, 0))
    return pl.pallas_call(
        _res_norm2_body,
        out_shape=(jax.ShapeDtypeStruct((m, d), F32), jax.ShapeDtypeStruct((m, d), BF16)),
        grid=(m // tm,),
        in_specs=[rows, rows, vec, vec],
        out_specs=(rows, rows),
        compiler_params=_params(("parallel",)),
        name="residual_norm_prenorm",
    )(raw, x, g_post.reshape(1, d), g_pre.reshape(1, d))


def _res_norm_body(raw_ref, x_ref, g_ref, o_ref):
    o_ref[...] = x_ref[...] + _rms(raw_ref[...].astype(F32), g_ref[...])


def _res_norm(raw, x, g, tm=256):
    m, d = x.shape
    tm = min(tm, m)
    rows = pl.BlockSpec((tm, d), lambda i: (i, 0))
    return pl.pallas_call(
        _res_norm_body,
        out_shape=jax.ShapeDtypeStruct((m, d), F32),
        grid=(m // tm,),
        in_specs=[rows, rows, pl.BlockSpec((1, d), lambda i: (0, 0))],
        out_specs=rows,
        compiler_params=_params(("parallel",)),
        name="residual_norm",
    )(raw, x, g.reshape(1, d))


def _swiglu_body(a_ref, wg_ref, wu_ref, wd_ref, o_ref, wd_bf_ref, wg_scr, wu_scr):
    @pl.when(pl.program_id(1) == 0)
    def _():
        wg_scr[...] = wg_ref[...].astype(BF16)
        wu_scr[...] = wu_ref[...].astype(BF16)

    a = a_ref[...]
    g = jnp.dot(a, wg_scr[...], preferred_element_type=F32)
    u = jnp.dot(a, wu_scr[...], preferred_element_type=F32)
    o_ref[...] = (g * jax.nn.sigmoid(g) * u).astype(o_ref.dtype)
    wd_bf_ref[...] = wd_ref[...].astype(BF16)


def _swiglu(a, wg, wu, wd, tm=2048, tn=256):
    m, k = a.shape
    n = wg.shape[1]
    tm = min(tm, m)
    assert n % tn == 0
    n_i = m // tm
    steps = (n // tn) * n_i
    slab = wd.shape[0] // steps
    assert slab * steps == wd.shape[0] and slab % (2 * SUBLANES) == 0
    vmem = (2 * tm * k * 2 + 2 * 2 * k * tn * 4 + 2 * tm * tn * 2 + 2 * k * tn * 2
            + 2 * slab * wd.shape[1] * (4 + 2))
    assert vmem + V7X_VMEM_SLACK_BYTES <= V7X_VMEM_BYTES
    return pl.pallas_call(
        _swiglu_body,
        out_shape=(jax.ShapeDtypeStruct((m, n), BF16), jax.ShapeDtypeStruct(wd.shape, BF16)),
        grid=(n // tn, n_i),
        in_specs=[pl.BlockSpec((tm, k), lambda j, i: (i, 0)),
                  pl.BlockSpec((k, tn), lambda j, i: (0, j)),
                  pl.BlockSpec((k, tn), lambda j, i: (0, j)),
                  pl.BlockSpec((slab, wd.shape[1]), lambda j, i: (j * n_i + i, 0))],
        out_specs=(pl.BlockSpec((tm, tn), lambda j, i: (i, j)),
                   pl.BlockSpec((slab, wd.shape[1]), lambda j, i: (j * n_i + i, 0))),
        scratch_shapes=[pltpu.VMEM((k, tn), BF16), pltpu.VMEM((k, tn), BF16)],
        compiler_params=_params(("arbitrary", "arbitrary"), vmem=vmem + V7X_VMEM_SLACK_BYTES),
        name="ffn_swiglu",
    )(a, wg, wu, wd)


def kernel(x, norm_mix_pre, w_in, ssm_a_re, ssm_a_im, ssm_log_dt, ssm_b_re, ssm_b_im, ssm_c_re, ssm_c_im, ssm_d, ssm_w_glu, ssm_b_glu, cmp_pe_k, cmp_w1_k, cmp_w2_k, cmp_pe_v, cmp_w1_v, cmp_w2_v, w_proj_a, w_proj_b, w_out, norm_mix_post, norm_ffn_pre, w_ffn_gate, w_ffn_up, w_ffn_down, norm_ffn_post):
    bsz, seq, d = x.shape
    m = bsz * seq
    depth = w_in.shape[0]
    h = x.reshape(m, d)
    o_q = SSM_WIDTH
    o_kvc = o_q + ATT_WIDTH
    o_kv = o_kvc + 2 * KV_WIDTH
    o_gn = o_kv + 4 * KV_WIDTH
    o_ga = o_gn + 3 * N_HEADS
    assert o_gn % LANES == 0 and o_ga + 2 * D_MODEL == w_in.shape[2]
    bf = lambda a: a.astype(BF16)
    tn_in = 512
    nb_u, nb_q = SSM_WIDTH // tn_in, ATT_WIDTH // tn_in
    qk_scale = HEAD_DIM ** -0.5 * LOG2E
    for l in range(depth):
        wt = jnp.swapaxes(w_in[l], 0, 1)
        w_g = _rows_transposed_cast(wt, o_ga, 2 * D_MODEL, BF16)

        hn, gn = _rmsnorm_gates(h, norm_mix_pre[l], wt, o_gn // LANES)
        proj_f32 = _matmul_f32w(hn, wt, SSM_WIDTH + 2 * KV_WIDTH, F32, 1024, tn_in, trans_w=True,
                                col_block=lambda j: jnp.where(j < nb_u, j, j - nb_u + o_kvc // tn_in),
                                name="in_proj_f32")
        qkv = _matmul_f32w(hn, wt, ATT_WIDTH + 4 * KV_WIDTH, BF16, 2048, tn_in, trans_w=True,
                           col_block=lambda j: jnp.where(j < nb_q, j + o_q // tn_in, j - nb_q + o_kv // tn_in),
                           scale_fn=lambda j: jnp.where(j < nb_q, qk_scale, 1.0),
                           name="in_proj_bf16")

        y = _s5_mixer(proj_f32.reshape(bsz, seq, -1), ssm_a_re[l], ssm_a_im[l], ssm_log_dt[l],
                      ssm_b_re[l], ssm_b_im[l], ssm_c_re[l], ssm_c_im[l], ssm_d[l])
        y_a = _glu(y.reshape(m, SSM_WIDTH), bf(ssm_w_glu[l]), ssm_b_glu[l])

        pe = jnp.stack([cmp_pe_k[l], cmp_pe_v[l]])
        w1 = bf(jnp.stack([cmp_w1_k[l], cmp_w1_v[l]]))
        w2 = bf(jnp.stack([cmp_w2_k[l], cmp_w2_v[l]]))
        kcv = _compress(proj_f32, SSM_WIDTH, pe, w1, w2, bsz, seq)
        o_cmp, bias = _cmp_attn(qkv, kcv, bsz, seq)
        o_sel = _sel_attn(qkv, bias, bsz, seq)
        y_b = _win_attn_combine(qkv, o_cmp, o_sel, gn, bsz, seq)

        merged = _merge(hn, y_a, y_b, w_g, bf(w_proj_a[l]), bf(w_proj_b[l]))
        mix = _matmul_f32w(merged, w_out[l], D_MODEL, BF16, 2048, 512, name="out_proj")
        h, hn2 = _res_norm2(mix, h, norm_mix_post[l], norm_ffn_pre[l])

        act, w_down = _swiglu(hn2, w_ffn_gate[l], w_ffn_up[l], w_ffn_down[l])
        f = _matmul(act, w_down, BF16, 512, 512, name="ffn_down")
        h = _res_norm(f, h, norm_ffn_post[l])
    return h.reshape(bsz, seq, d)
```

```python
import functools
import math

import jax
import jax.numpy as jnp
from jax import lax
from jax.experimental import pallas as pl
from jax.experimental.pallas import tpu as pltpu

F32 = jnp.float32
BF16 = jnp.bfloat16

D_MODEL = 4096
SSM_WIDTH = 2048
SSM_GROUP = 16
SSM_GROUPS = 128
SSM_STATE = 64
N_HEADS = 16
HEAD_DIM = 128
N_KV = 4
HPG = 4
ATT_WIDTH = 2048
KV_WIDTH = 512
L_CMP = 32
STRIDE_CMP = 16
L_SEL = 64
N_SEL = 16
WINDOW = 512
RMS_EPS = 1e-6
NEG_INF = -1e30
FORCE_SCORE = 1e9
TAKEN_SCORE = -3e38
LOG2E = math.log2(math.e)

V7X_VMEM_BYTES = 64 * 1024 * 1024
V7X_VMEM_LIMIT_BYTES = 56 * 1024 * 1024
V7X_VMEM_SLACK_BYTES = 4 * 1024 * 1024
LANES = 128
SUBLANES = 8

S5_GT = 16
S5_NC = S5_GT * SSM_GROUP
S5_NP = S5_GT * SSM_STATE
S5_TILES = SSM_GROUPS // S5_GT
S5_SLABS = 2 * S5_NP // LANES

ATT_ROWS = 256
SEL_TQ, SEL_RB = 512, 64
SEL_GROUP = 4


def _params(sem, vmem=V7X_VMEM_LIMIT_BYTES):
    return pltpu.CompilerParams(dimension_semantics=sem, vmem_limit_bytes=vmem)


def _lane_tile(x, n):
    return jnp.concatenate([x] * n, axis=1)


def _rmsnorm_gates_body(x_ref, g_ref, w_ref, hn_ref, gn_ref, wbf_scr):
    @pl.when(pl.program_id(0) == 0)
    def _():
        wbf_scr[...] = w_ref[...].astype(BF16)

    x = x_ref[...]
    ms = jnp.mean(x * x, axis=-1, keepdims=True)
    hn = (x * lax.rsqrt(ms + RMS_EPS) * g_ref[...]).astype(BF16)
    hn_ref[...] = hn
    gn_ref[...] = lax.dot_general(hn, wbf_scr[...], (((1,), (1,)), ((), ())), preferred_element_type=F32)


def _rmsnorm_gates(x, gain, wt, feature_block, tm=256):
    m, d = x.shape
    tm = min(tm, m)
    return pl.pallas_call(
        _rmsnorm_gates_body,
        out_shape=(jax.ShapeDtypeStruct((m, d), BF16), jax.ShapeDtypeStruct((m, LANES), F32)),
        grid=(m // tm,),
        in_specs=[pl.BlockSpec((tm, d), lambda i: (i, 0)),
                  pl.BlockSpec((1, d), lambda i: (0, 0)),
                  pl.BlockSpec((LANES, d), lambda i: (feature_block, 0))],
        out_specs=(pl.BlockSpec((tm, d), lambda i: (i, 0)), pl.BlockSpec((tm, LANES), lambda i: (i, 0))),
        scratch_shapes=[pltpu.VMEM((LANES, d), BF16)],
        compiler_params=_params(("arbitrary",)),
        name="rmsnorm_gates",
    )(x, gain.reshape(1, d), wt)


def _mm_body(a_ref, w_ref, o_ref):
    o_ref[...] = jnp.dot(a_ref[...], w_ref[...], preferred_element_type=F32).astype(o_ref.dtype)


def _matmul(a, w, out_dtype, tm, tn, name="matmul"):
    m, k = a.shape
    n = w.shape[1]
    tm, tn = min(tm, m), min(tn, n)
    assert m % tm == 0 and n % tn == 0
    return pl.pallas_call(
        _mm_body,
        out_shape=jax.ShapeDtypeStruct((m, n), out_dtype),
        grid=(m // tm, n // tn),
        in_specs=[pl.BlockSpec((tm, k), lambda i, j: (i, 0)), pl.BlockSpec((k, tn), lambda i, j: (0, j))],
        out_specs=pl.BlockSpec((tm, tn), lambda i, j: (i, j)),
        compiler_params=_params(("parallel", "arbitrary")),
        name=name,
    )(a, w)


_NT = (((1,), (1,)), ((), ()))


def _mm_f32w_body(a_ref, w_ref, o_ref, wbf_scr, *, scale_fn, trans_w):
    @pl.when(pl.program_id(1) == 0)
    def _():
        wbf_scr[...] = w_ref[...].astype(BF16)

    if trans_w:
        acc = lax.dot_general(a_ref[...], wbf_scr[...], _NT, preferred_element_type=F32)
    else:
        acc = jnp.dot(a_ref[...], wbf_scr[...], preferred_element_type=F32)
    if scale_fn is not None:
        acc = acc * scale_fn(pl.program_id(0))
    o_ref[...] = acc.astype(o_ref.dtype)


def _matmul_f32w(a, w, n_out, out_dtype, tm, tn, col_block=None, scale_fn=None, trans_w=False,
                 name="matmul_f32w"):
    m, k = a.shape
    tm = min(tm, m)
    assert m % tm == 0 and n_out % tn == 0
    col_block = col_block or (lambda j: j)
    if trans_w:
        w_spec = pl.BlockSpec((tn, k), lambda j, i: (col_block(j), 0))
        w_scratch = pltpu.VMEM((tn, k), BF16)
    else:
        w_spec = pl.BlockSpec((k, tn), lambda j, i: (0, col_block(j)))
        w_scratch = pltpu.VMEM((k, tn), BF16)
    vmem = 2 * tm * k * 2 + 2 * k * tn * 4 + 2 * tm * tn * jnp.dtype(out_dtype).itemsize + k * tn * 2
    assert vmem + V7X_VMEM_SLACK_BYTES <= V7X_VMEM_BYTES
    return pl.pallas_call(
        functools.partial(_mm_f32w_body, scale_fn=scale_fn, trans_w=trans_w),
        out_shape=jax.ShapeDtypeStruct((m, n_out), out_dtype),
        grid=(n_out // tn, m // tm),
        in_specs=[pl.BlockSpec((tm, k), lambda j, i: (i, 0)), w_spec],
        out_specs=pl.BlockSpec((tm, tn), lambda j, i: (i, j)),
        scratch_shapes=[w_scratch],
        compiler_params=_params(("arbitrary", "arbitrary"), vmem=vmem + V7X_VMEM_SLACK_BYTES),
        name=name,
    )(a, w)


def _s5_body(u_ref, are_ref, aim_ref, ldt_ref, bre_ref, bim_ref, cre_ref, cim_ref, d_ref, y_ref,
             abar_scr, bbar_scr, cmat_scr, h_scr, st_scr, *, tc, bsz, tps):
    ti = pl.program_id(1)
    half = S5_SLABS // 2

    @pl.when(ti == 0)
    def _():
        row_tile = lax.broadcasted_iota(jnp.int32, (SUBLANES, S5_NP), 0) // bsz
        same_group = (lax.broadcasted_iota(jnp.int32, (S5_NC, S5_NP), 0) // SSM_GROUP
                      == lax.broadcasted_iota(jnp.int32, (S5_NC, S5_NP), 1) // SSM_STATE)
        bdiag = lambda x: jnp.where(same_group, jnp.concatenate([x] * S5_GT, axis=0), 0.0)
        abr8 = jnp.zeros((SUBLANES, S5_NP), F32)
        abi8 = jnp.zeros((SUBLANES, S5_NP), F32)
        for k in range(tps):
            cols = slice(k * S5_NP, (k + 1) * S5_NP)
            ar, ai = are_ref[:, cols], aim_ref[:, cols]
            dt = jnp.exp(ldt_ref[:, cols])
            decay = jnp.exp(dt * ar)
            abr, abi = decay * jnp.cos(dt * ai), decay * jnp.sin(dt * ai)
            den = ar * ar + ai * ai
            zr = ((abr - 1.0) * ar + abi * ai) / den
            zi = (abi * ar - (abr - 1.0) * ai) / den
            abr8 = jnp.where(row_tile == k, abr, abr8)
            abi8 = jnp.where(row_tile == k, abi, abi8)
            br, bi = bdiag(bre_ref[k]), bdiag(bim_ref[k])
            bbar_scr[k, :, 0:S5_NP] = (zr * br - zi * bi).astype(BF16)
            bbar_scr[k, :, S5_NP:2 * S5_NP] = (zr * bi + zi * br).astype(BF16)
            cmat_scr[k, :, 0:S5_NP] = bdiag(cre_ref[k]).astype(BF16)
            cmat_scr[k, :, S5_NP:2 * S5_NP] = (-bdiag(cim_ref[k])).astype(BF16)
        abar_scr[0:SUBLANES, :] = abr8
        abar_scr[SUBLANES:2 * SUBLANES, :] = abi8
        st_scr[...] = jnp.zeros_like(st_scr)

    for k in range(tps):
        cols = slice(k * S5_NC, (k + 1) * S5_NC)
        ub = jnp.concatenate([u_ref[b, :, cols] for b in range(bsz)], axis=0).astype(BF16)
        for s2 in range(S5_SLABS // 2):
            bu = jnp.dot(ub, bbar_scr[k, :, s2 * 2 * LANES:(s2 + 1) * 2 * LANES], preferred_element_type=F32)
            for b in range(bsz):
                for e in range(2):
                    h_scr[2 * s2 + e, pl.ds(k * bsz + b, tc, stride=SUBLANES), :] = bu[b * tc:(b + 1) * tc,
                                                                                      e * LANES:(e + 1) * LANES]

    ar = abar_scr[0:SUBLANES, :]
    ai = abar_scr[SUBLANES:2 * SUBLANES, :]

    def step(t, carry):
        hr, hi = carry
        r0 = pl.multiple_of(t * SUBLANES, SUBLANES)
        bur = jnp.concatenate([h_scr[s, pl.ds(r0, SUBLANES), :] for s in range(half)], axis=1)
        bui = jnp.concatenate([h_scr[half + s, pl.ds(r0, SUBLANES), :] for s in range(half)], axis=1)
        nhr = ar * hr - ai * hi + bur
        nhi = ar * hi + ai * hr + bui
        for s in range(half):
            h_scr[s, pl.ds(r0, SUBLANES), :] = nhr[:, s * LANES:(s + 1) * LANES]
            h_scr[half + s, pl.ds(r0, SUBLANES), :] = nhi[:, s * LANES:(s + 1) * LANES]
        return nhr, nhi

    hr, hi = lax.fori_loop(0, tc, step, (st_scr[0:SUBLANES, :], st_scr[SUBLANES:2 * SUBLANES, :]), unroll=2)
    st_scr[0:SUBLANES, :] = hr
    st_scr[SUBLANES:2 * SUBLANES, :] = hi

    for k in range(tps):
        cols = slice(k * S5_NC, (k + 1) * S5_NC)

        hb = jnp.concatenate(
            [jnp.concatenate([h_scr[s, pl.ds(k * bsz + b, tc, stride=SUBLANES), :] for s in range(S5_SLABS)], axis=1)
             for b in range(bsz)], axis=0).astype(BF16)
        ch = lax.dot_general(hb, cmat_scr[k], _NT, preferred_element_type=F32)
        for b in range(bsz):
            y = ch[b * tc:(b + 1) * tc, :] + d_ref[:, cols] * u_ref[b, :, cols]
            y_ref[b, :, cols] = jax.nn.gelu(y)


def _s5_mixer(u3, a_re, a_im, log_dt, b_re, b_im, c_re, c_im, d_skip, tc=256):
    bsz, seq, _ = u3.shape
    assert SUBLANES % bsz == 0
    tps = min(SUBLANES // bsz, S5_TILES)
    assert tps * bsz == SUBLANES
    tc = min(tc, seq)

    def per_tile_b(b):
        return (b.reshape(S5_TILES, S5_GT, SSM_STATE, SSM_GROUP).transpose(0, 3, 1, 2)
                .reshape(S5_TILES, SSM_GROUP, S5_NP))

    def per_tile_c(c):
        return (c.reshape(S5_TILES, S5_GT, SSM_GROUP, SSM_STATE).transpose(0, 2, 1, 3)
                .reshape(S5_TILES, SSM_GROUP, S5_NP))

    flat = lambda a: a.reshape(1, SSM_GROUPS * SSM_STATE)
    ldt = jnp.repeat(log_dt, SSM_STATE).reshape(1, SSM_GROUPS * SSM_STATE)
    vec_spec = pl.BlockSpec((1, tps * S5_NP), lambda g, t: (0, g))
    bc_spec = pl.BlockSpec((tps, SSM_GROUP, S5_NP), lambda g, t: (g, 0, 0))
    return pl.pallas_call(
        functools.partial(_s5_body, tc=tc, bsz=bsz, tps=tps),
        out_shape=jax.ShapeDtypeStruct((bsz, seq, SSM_WIDTH), F32),
        grid=(S5_TILES // tps, seq // tc),
        in_specs=[
            pl.BlockSpec((bsz, tc, tps * S5_NC), lambda g, t: (0, t, g)),
            vec_spec, vec_spec, vec_spec, bc_spec, bc_spec, bc_spec, bc_spec,
            pl.BlockSpec((1, tps * S5_NC), lambda g, t: (0, g)),
        ],
        out_specs=pl.BlockSpec((bsz, tc, tps * S5_NC), lambda g, t: (0, t, g)),
        scratch_shapes=[
            pltpu.VMEM((2 * SUBLANES, S5_NP), F32),
            pltpu.VMEM((tps, S5_NC, 2 * S5_NP), BF16),
            pltpu.VMEM((tps, S5_NC, 2 * S5_NP), BF16),
            pltpu.VMEM((S5_SLABS, tc * SUBLANES, LANES), F32),
            pltpu.VMEM((2 * SUBLANES, S5_NP), F32),
        ],
        compiler_params=_params(("arbitrary", "arbitrary")),
        name="s5_scan",
    )(u3, flat(a_re), flat(a_im), ldt, per_tile_b(b_re), per_tile_b(b_im), per_tile_c(c_re), per_tile_c(c_im),
      d_skip.reshape(1, SSM_WIDTH))


def _glu_body(y_ref, w_ref, b_ref, o_ref, ybf_scr, *, tn):
    j = pl.program_id(1)

    @pl.when(j == 0)
    def _():
        ybf_scr[...] = y_ref[...].astype(BF16)

    z = jnp.dot(ybf_scr[...], w_ref[...], preferred_element_type=F32) + b_ref[...]
    yt = y_ref[:, pl.ds(pl.multiple_of(j * tn, tn), tn)]
    o_ref[...] = (yt * jax.nn.sigmoid(z)).astype(o_ref.dtype)


def _glu(y, w_bf, bias, tm=1024, tn=1024):
    m, k = y.shape
    tm = min(tm, m)
    return pl.pallas_call(
        functools.partial(_glu_body, tn=tn),
        out_shape=jax.ShapeDtypeStruct((m, k), BF16),
        grid=(m // tm, k // tn),
        in_specs=[pl.BlockSpec((tm, k), lambda i, j: (i, 0)),
                  pl.BlockSpec((k, tn), lambda i, j: (0, j)),
                  pl.BlockSpec((1, tn), lambda i, j: (0, j))],
        out_specs=pl.BlockSpec((tm, tn), lambda i, j: (i, j)),
        scratch_shapes=[pltpu.VMEM((tm, k), BF16)],
        compiler_params=_params(("parallel", "arbitrary")),
        name="s5_glu",
    )(y, w_bf, bias.reshape(1, k))


def _compress_body(x_ref, pe_ref, w1_ref, w2_ref, o_ref, *, ncmp):
    half = L_CMP // 2
    acc_a = jnp.zeros((ncmp, HEAD_DIM), F32)
    acc_b = jnp.zeros((ncmp, HEAD_DIM), F32)
    for r in range(half):
        xr = x_ref[pl.ds(r, ncmp, stride=STRIDE_CMP), :]
        xa = (xr + pe_ref[r:r + 1, :]).astype(BF16)
        xb = (xr + pe_ref[half + r:half + r + 1, :]).astype(BF16)
        acc_a += jnp.dot(xa, w1_ref[r * HEAD_DIM:(r + 1) * HEAD_DIM, :], preferred_element_type=F32)
        acc_b += jnp.dot(xb, w1_ref[(half + r) * HEAD_DIM:(half + r + 1) * HEAD_DIM, :],
                         preferred_element_type=F32)
    pre = acc_a + jnp.concatenate([acc_b[1:], acc_b[:1]], axis=0)
    o_ref[...] = jnp.dot(jax.nn.gelu(pre).astype(BF16), w2_ref[...], preferred_element_type=F32).astype(o_ref.dtype)


def _compress(proj_f32, col0, pe, w1_bf, w2_bf, bsz, seq):
    ncmp = seq // STRIDE_CMP
    cb = col0 // HEAD_DIM
    return pl.pallas_call(
        functools.partial(_compress_body, ncmp=ncmp),
        out_shape=jax.ShapeDtypeStruct((bsz, 2, N_KV, ncmp, HEAD_DIM), BF16),
        grid=(bsz, 2, N_KV),
        in_specs=[pl.BlockSpec((seq, HEAD_DIM), lambda b, w, g: (b, cb + w * N_KV + g)),
                  pl.BlockSpec((None, L_CMP, HEAD_DIM), lambda b, w, g: (w, 0, 0)),
                  pl.BlockSpec((None, L_CMP * HEAD_DIM, HEAD_DIM), lambda b, w, g: (w, 0, 0)),
                  pl.BlockSpec((None, HEAD_DIM, HEAD_DIM), lambda b, w, g: (w, 0, 0))],
        out_specs=pl.BlockSpec((None, None, None, ncmp, HEAD_DIM), lambda b, w, g: (b, w, g, 0, 0)),
        compiler_params=_params(("parallel", "parallel", "parallel")),
        name="nsa_compress",
    )(proj_f32, pe, w1_bf, w2_bf)


def _cmp_attn_body(q_ref, kc_ref, vc_ref, o_ref, bias_ref, *, tq, ncmp, nblk, ntop):
    qi = pl.program_id(2)
    t0 = qi * tq
    tpos = t0 + lax.broadcasted_iota(jnp.int32, (tq, ncmp), 0)
    blk_end = lax.broadcasted_iota(jnp.int32, (tq, ncmp), 1) * STRIDE_CMP + (L_CMP - 1)
    ok = blk_end <= tpos
    okf = ok.astype(F32)
    kc = kc_ref[...]
    vc = vc_ref[...]
    psum = jnp.zeros((tq, ncmp), F32)
    for h in range(HPG):
        qh = q_ref[:, h * HEAD_DIM:(h + 1) * HEAD_DIM]
        s = lax.dot_general(qh, kc, (((1,), (1,)), ((), ())), preferred_element_type=F32)
        s = jnp.where(ok, s, NEG_INF)
        e = jnp.exp2(s - jnp.max(s, axis=-1, keepdims=True))
        p = e / jnp.sum(e, axis=-1, keepdims=True) * okf
        o_ref[:, h * HEAD_DIM:(h + 1) * HEAD_DIM] = jnp.dot(p.astype(BF16), vc, preferred_element_type=F32)
        psum = psum + p

    jj = lax.broadcasted_iota(jnp.int32, (nblk, ncmp), 0)
    nn = lax.broadcasted_iota(jnp.int32, (nblk, ncmp), 1)
    ov = ((nn * STRIDE_CMP < (jj + 1) * L_SEL) & (nn * STRIDE_CMP + L_CMP > jj * L_SEL)).astype(BF16)
    p_hi = psum.astype(BF16)
    p_lo = (psum - p_hi.astype(F32)).astype(BF16)
    nt = (((1,), (1,)), ((), ()))
    imp = (lax.dot_general(ov, p_hi, nt, preferred_element_type=F32)
           + lax.dot_general(ov, p_lo, nt, preferred_element_type=F32))

    jb = lax.broadcasted_iota(jnp.int32, (nblk, tq), 0)
    tt = t0 + lax.broadcasted_iota(jnp.int32, (nblk, tq), 1)
    cur = tt // L_SEL
    allowed = jb * L_SEL <= tt
    forced = (jb == 0) | (jb == cur) | (jb == cur - 1)
    score = jnp.where(forced, FORCE_SCORE, jnp.where(allowed, imp, NEG_INF))
    taken = jnp.zeros((nblk, tq), F32)
    for _ in range(ntop):
        best = jnp.max(score, axis=0, keepdims=True)
        first = jnp.min(jnp.where(score == best, jb, nblk), axis=0, keepdims=True)
        pick = jb == first
        taken = jnp.where(pick, 1.0, taken)
        score = jnp.where(pick, TAKEN_SCORE, score)
    bias_t = jnp.where(taken > 0.5, 0.0, NEG_INF)
    bias_t = jnp.concatenate([bias_t, jnp.zeros((LANES - nblk, tq), F32)], axis=0)
    bias_ref[...] = bias_t.T.astype(bias_ref.dtype)


def _cmp_attn(qkv, kcv, bsz, seq, tq=2048):
    tq = min(tq, seq)
    ncmp = seq // STRIDE_CMP
    nblk = seq // L_SEL
    ntop = min(N_SEL, nblk)
    nq = seq // tq
    return pl.pallas_call(
        functools.partial(_cmp_attn_body, tq=tq, ncmp=ncmp, nblk=nblk, ntop=ntop),
        out_shape=(jax.ShapeDtypeStruct((bsz * seq, ATT_WIDTH), F32),
                   jax.ShapeDtypeStruct((bsz, N_KV, seq, LANES), BF16)),
        grid=(bsz, N_KV, nq),
        in_specs=[pl.BlockSpec((tq, HPG * HEAD_DIM), lambda b, g, i: (b * nq + i, g)),
                  pl.BlockSpec((None, None, None, ncmp, HEAD_DIM), lambda b, g, i: (b, 0, g, 0, 0)),
                  pl.BlockSpec((None, None, None, ncmp, HEAD_DIM), lambda b, g, i: (b, 1, g, 0, 0))],
        out_specs=(pl.BlockSpec((tq, HPG * HEAD_DIM), lambda b, g, i: (b * nq + i, g)),
                   pl.BlockSpec((None, None, tq, LANES), lambda b, g, i: (b, g, i, 0))),
        compiler_params=_params(("parallel", "parallel", "parallel")),
        name="nsa_cmp_attn_topk",
    )(qkv, kcv, kcv)


def _sel_attn_body(q_ref, k_ref, v_ref, bias_ref, wg_ref, o_ref, wg_bf_ref, qa_scr, m_scr, acc_scr, s_scr, p_scr,
                   alpha_scr, *, tq, rb):
    qi = pl.program_id(2)
    tk = tq
    wg_bf_ref[...] = wg_ref[...].astype(BF16)
    m_scr[...] = jnp.full_like(m_scr, NEG_INF)
    acc_scr[...] = jnp.zeros_like(acc_scr)
    for h in range(HPG):
        qa_scr[h, :, 0:HEAD_DIM] = q_ref[:, h * HEAD_DIM:(h + 1) * HEAD_DIM]
        qa_scr[h, :, HEAD_DIM:2 * HEAD_DIM] = bias_ref[...]
    nt = (((1,), (1,)), ((), ()))

    def key_tile(k0, nkeys):
        key_blk = k0 // L_SEL + lax.broadcasted_iota(jnp.int32, (nkeys, LANES), 0) // L_SEL
        onehot = jnp.where(lax.broadcasted_iota(jnp.int32, (nkeys, LANES), 1) == key_blk, 1.0, 0.0).astype(BF16)
        return jnp.concatenate([k_ref[pl.ds(k0, nkeys), :], onehot], axis=1)

    def scores(h, k_aug, r0, r1, nk):
        s_scr[h, r0:r1, 0:nk] = lax.dot_general(qa_scr[h, r0:r1, :], k_aug[0:nk, :], nt,
                                                preferred_element_type=F32)

    def softmax(h, r0, r1, nk, d0):
        for r in range(r0 // rb, r1 // rb):
            rows = slice(r * rb, (r + 1) * rb)
            pieces = []
            if d0 > 0:
                pieces.append(s_scr[h, rows, 0:d0])
            if d0 < nk:
                col = lax.broadcasted_iota(jnp.int32, (rb, nk - d0), 1)
                row = lax.broadcasted_iota(jnp.int32, (rb, nk - d0), 0) + r * rb
                pieces.append(jnp.where(col <= row, s_scr[h, rows, d0:nk], NEG_INF))
            m_prev = m_scr[h, rows, :]
            m_next = m_prev
            for s in pieces:
                m_next = jnp.maximum(m_next, jnp.max(s, axis=-1, keepdims=True))
            c0 = 0
            for s in pieces:
                w = s.shape[1]
                p_scr[h, rows, c0:c0 + w] = jnp.exp2(s - _lane_tile(m_next, w // LANES)).astype(BF16)
                c0 += w
            alpha_scr[h, rows, :] = jnp.exp2(m_prev - m_next)
            m_scr[h, rows, :] = m_next

    def values(h, v_aug, r0, r1, nk):
        acc_scr[h, r0:r1, :] = (_lane_tile(alpha_scr[h, r0:r1, :], 2) * acc_scr[h, r0:r1, :]
                                + jnp.dot(p_scr[h, r0:r1, 0:nk], v_aug[0:nk, :], preferred_element_type=F32))

    def tile(kt, nkeys, diag_last):
        k0 = pl.multiple_of(kt * tk, tk)
        k_aug = key_tile(k0, nkeys)
        v_aug = jnp.concatenate([v_ref[pl.ds(k0, nkeys), :], jnp.ones((nkeys, LANES), BF16)], axis=1)
        if diag_last:
            d0 = nkeys - tk
            parts = [(0, tq // 2, nkeys - tk // 2), (tq // 2, tq, nkeys)]
        else:
            d0 = nkeys
            parts = [(0, tq, nkeys)]
        for h in range(HPG):
            for r0, r1, nk in parts:
                scores(h, k_aug, r0, r1, nk)
        for h in range(HPG):
            for r0, r1, nk in parts:
                softmax(h, r0, r1, nk, d0)
        for h in range(HPG):
            for r0, r1, nk in parts:
                values(h, v_aug, r0, r1, nk)

    def tile_group(j, carry):
        tile(SEL_GROUP * j, SEL_GROUP * tk, False)
        return carry

    n_groups = qi // SEL_GROUP
    lax.fori_loop(0, n_groups, tile_group, 0)
    for run in range(1, SEL_GROUP + 1):
        @pl.when(qi % SEL_GROUP == run - 1)
        def _(run=run):
            tile(n_groups * SEL_GROUP, run * tk, True)

    for h in range(HPG):
        o_ref[:, h * HEAD_DIM:(h + 1) * HEAD_DIM] = (acc_scr[h, :, 0:HEAD_DIM]
                                                     / acc_scr[h, :, HEAD_DIM:2 * HEAD_DIM])


def _sel_attn(qkv, bias, wt, gate_row0, gate_rows, bsz, seq, tq=SEL_TQ, rb=SEL_RB):
    tq = min(tq, seq)
    rb = min(rb, tq)
    nq = seq // tq
    kcol = ATT_WIDTH // HEAD_DIM
    steps = bsz * N_KV * nq
    slab = gate_rows // steps
    d = wt.shape[1]
    assert slab * steps == gate_rows and slab % (2 * SUBLANES) == 0 and gate_row0 % (2 * SUBLANES) == 0
    step = lambda b, g, i: (b * N_KV + g) * nq + i
    return pl.pallas_call(
        functools.partial(_sel_attn_body, tq=tq, rb=rb),
        out_shape=(jax.ShapeDtypeStruct((bsz * seq, ATT_WIDTH), F32),
                   jax.ShapeDtypeStruct((gate_rows, d), BF16)),
        grid=(bsz, N_KV, nq),
        in_specs=[pl.BlockSpec((tq, HPG * HEAD_DIM), lambda b, g, i: (b * nq + i, g)),
                  pl.BlockSpec((seq, HEAD_DIM), lambda b, g, i: (b, kcol + g)),
                  pl.BlockSpec((seq, HEAD_DIM), lambda b, g, i: (b, kcol + N_KV + g)),
                  pl.BlockSpec((None, None, tq, LANES), lambda b, g, i: (b, g, i, 0)),
                  pl.BlockSpec((pl.Element(slab), pl.Element(d)),
                               lambda b, g, i: (pl.multiple_of(gate_row0 + step(b, g, i) * slab, 2 * SUBLANES), 0))],
        out_specs=(pl.BlockSpec((tq, HPG * HEAD_DIM), lambda b, g, i: (b * nq + i, g)),
                   pl.BlockSpec((slab, d), lambda b, g, i: (step(b, g, i), 0))),
        scratch_shapes=[pltpu.VMEM((HPG, tq, 2 * HEAD_DIM), BF16),
                        pltpu.VMEM((HPG, tq, LANES), F32),
                        pltpu.VMEM((HPG, tq, 2 * HEAD_DIM), F32),
                        pltpu.VMEM((HPG, tq, SEL_GROUP * tq), F32),
                        pltpu.VMEM((HPG, tq, SEL_GROUP * tq), BF16),
                        pltpu.VMEM((HPG, tq, LANES), F32)],
        compiler_params=_params(("parallel", "parallel", "arbitrary")),
        name="nsa_sel_attn",
    )(qkv, qkv, qkv, bias, wt)


def _win_attn_body(q_ref, kc_ref, kp_ref, vc_ref, vp_ref, ocmp_ref, osel_ref, gate_ref, o_ref, *, tq):
    qi = pl.program_id(2)
    nt = (((1,), (1,)), ((), ()))
    ones = jnp.ones((tq, LANES), BF16)
    vc_aug = jnp.concatenate([vc_ref[...], ones], axis=1)
    vp_aug = jnp.concatenate([vp_ref[...], ones], axis=1)
    for r in range(tq // ATT_ROWS):
        rows = slice(r * ATT_ROWS, (r + 1) * ATT_ROWS)
        n_cur, p_lo = (r + 1) * ATT_ROWS, r * ATT_ROWS
        row_c = lax.broadcasted_iota(jnp.int32, (ATT_ROWS, n_cur), 0) + r * ATT_ROWS
        ok_cur = lax.broadcasted_iota(jnp.int32, (ATT_ROWS, n_cur), 1) <= row_c
        row_p = lax.broadcasted_iota(jnp.int32, (ATT_ROWS, tq - p_lo), 0) + r * ATT_ROWS
        ok_prev = (lax.broadcasted_iota(jnp.int32, (ATT_ROWS, tq - p_lo), 1) + p_lo > row_p) & (qi > 0)
        shift = (LANES - HPG * pl.program_id(1)) % LANES
        gates = jax.nn.sigmoid(pltpu.roll(gate_ref[rows, :], shift, 1))
        for h in range(HPG):
            sl = slice(h * HEAD_DIM, (h + 1) * HEAD_DIM)
            qh = q_ref[rows, sl]
            s1 = jnp.where(ok_cur, lax.dot_general(qh, kc_ref[0:n_cur, :], nt, preferred_element_type=F32), NEG_INF)
            s0 = jnp.where(ok_prev, lax.dot_general(qh, kp_ref[p_lo:tq, :], nt, preferred_element_type=F32), NEG_INF)
            m = jnp.maximum(jnp.max(s1, axis=-1, keepdims=True), jnp.max(s0, axis=-1, keepdims=True))
            p1 = jnp.exp2(s1 - m).astype(BF16)
            p0 = jnp.exp2(s0 - m).astype(BF16)
            acc = (jnp.dot(p1, vc_aug[0:n_cur, :], preferred_element_type=F32)
                   + jnp.dot(p0, vp_aug[p_lo:tq, :], preferred_element_type=F32))
            ow = acc[:, 0:HEAD_DIM] / acc[:, HEAD_DIM:2 * HEAD_DIM]
            g_cmp = gates[:, h:h + 1]
            g_sel = gates[:, N_HEADS + h:N_HEADS + h + 1]
            g_win = gates[:, 2 * N_HEADS + h:2 * N_HEADS + h + 1]
            o_ref[rows, sl] = (g_cmp * ocmp_ref[rows, sl] + g_sel * osel_ref[rows, sl]
                               + g_win * ow).astype(o_ref.dtype)


def _win_attn_combine(qkv, o_cmp, o_sel, gates, bsz, seq):
    tq = WINDOW
    assert seq % tq == 0
    nq = seq // tq
    kcol = ATT_WIDTH // HEAD_DIM + 2 * N_KV
    qspec = pl.BlockSpec((tq, HPG * HEAD_DIM), lambda b, g, i: (b * nq + i, g))
    cur = lambda c: pl.BlockSpec((tq, HEAD_DIM), lambda b, g, i: (b * nq + i, kcol + c * N_KV + g))
    prev = lambda c: pl.BlockSpec((tq, HEAD_DIM),
                                  lambda b, g, i: (b * nq + jnp.maximum(i - 1, 0), kcol + c * N_KV + g))
    return pl.pallas_call(
        functools.partial(_win_attn_body, tq=tq),
        out_shape=jax.ShapeDtypeStruct((bsz * seq, ATT_WIDTH), BF16),
        grid=(bsz, N_KV, nq),
        in_specs=[qspec, cur(0), prev(0), cur(1), prev(1), qspec, qspec,
                  pl.BlockSpec((tq, LANES), lambda b, g, i: (b * nq + i, 0))],
        out_specs=qspec,
        compiler_params=_params(("parallel", "parallel", "parallel")),
        name="nsa_win_attn_gate",
    )(qkv, qkv, qkv, qkv, qkv, o_cmp, o_sel, gates)


def _merge_body(hn_ref, ya_ref, yb_ref, wga_ref, wgb_ref, wa_ref, wb_ref, o_ref):
    hn = hn_ref[...]
    ga = lax.dot_general(hn, wga_ref[...], _NT, preferred_element_type=F32)
    gb = lax.dot_general(hn, wgb_ref[...], _NT, preferred_element_type=F32)
    pa = jnp.dot(ya_ref[...], wa_ref[...], preferred_element_type=F32)
    pb = jnp.dot(yb_ref[...], wb_ref[...], preferred_element_type=F32)
    o_ref[...] = (jax.nn.sigmoid(ga) * pa + jax.nn.sigmoid(gb) * pb).astype(o_ref.dtype)


def _merge(hn, ya, yb, wg, wa, wb, tm=512, tn=512):
    m, d = hn.shape
    ka = ya.shape[1]
    tm = min(tm, m)
    row = lambda k: pl.BlockSpec((tm, k), lambda i, j: (i, 0))
    col = lambda k: pl.BlockSpec((k, tn), lambda i, j: (0, j))
    gate_a = pl.BlockSpec((tn, d), lambda i, j: (j, 0))
    gate_b = pl.BlockSpec((tn, d), lambda i, j: (d // tn + j, 0))
    return pl.pallas_call(
        _merge_body,
        out_shape=jax.ShapeDtypeStruct((m, d), BF16),
        grid=(m // tm, d // tn),
        in_specs=[row(d), row(ka), row(ka), gate_a, gate_b, col(ka), col(ka)],
        out_specs=pl.BlockSpec((tm, tn), lambda i, j: (i, j)),
        compiler_params=_params(("parallel", "arbitrary")),
        name="mixer_merge",
    )(hn, ya, yb, wg, wg, wa, wb)


def _rms(x, g):
    return x * lax.rsqrt(jnp.mean(x * x, axis=-1, keepdims=True) + RMS_EPS) * g


def _res_norm2_body(raw_ref, x_ref, gpost_ref, gpre_ref, h_ref, hn_ref):
    h = x_ref[...] + _rms(raw_ref[...].astype(F32), gpost_ref[...])
    h_ref[...] = h
    hn_ref[...] = _rms(h, gpre_ref[...]).astype(hn_ref.dtype)


def _res_norm2(raw, x, g_post, g_pre, tm=256):
    m, d = x.shape
    tm = min(tm, m)
    rows = pl.BlockSpec((tm, d), lambda i: (i, 0))
    vec = pl.BlockSpec((1, d), lambda i: (0, 0))
    return pl.pallas_call(
        _res_norm2_body,
        out_shape=(jax.ShapeDtypeStruct((m, d), F32), jax.ShapeDtypeStruct((m, d), BF16)),
        grid=(m // tm,),
        in_specs=[rows, rows, vec, vec],
        out_specs=(rows, rows),
        compiler_params=_params(("parallel",)),
        name="residual_norm_prenorm",
    )(raw, x, g_post.reshape(1, d), g_pre.reshape(1, d))


def _res_norm_body(raw_ref, x_ref, g_ref, o_ref):
    o_ref[...] = x_ref[...] + _rms(raw_ref[...].astype(F32), g_ref[...])


def _res_norm(raw, x, g, tm=256):
    m, d = x.shape
    tm = min(tm, m)
    rows = pl.BlockSpec((tm, d), lambda i: (i, 0))
    return pl.pallas_call(
        _res_norm_body,
        out_shape=jax.ShapeDtypeStruct((m, d), F32),
        grid=(m // tm,),
        in_specs=[rows, rows, pl.BlockSpec((1, d), lambda i: (0, 0))],
        out_specs=rows,
        compiler_params=_params(("parallel",)),
        name="residual_norm",
    )(raw, x, g.reshape(1, d))


def _swiglu_body(a_ref, wg_ref, wu_ref, wd_ref, o_ref, wd_bf_ref, wg_scr, wu_scr):
    @pl.when(pl.program_id(1) == 0)
    def _():
        wg_scr[...] = wg_ref[...].astype(BF16)
        wu_scr[...] = wu_ref[...].astype(BF16)

    a = a_ref[...]
    g = jnp.dot(a, wg_scr[...], preferred_element_type=F32)
    u = jnp.dot(a, wu_scr[...], preferred_element_type=F32)
    o_ref[...] = (g * jax.nn.sigmoid(g) * u).astype(o_ref.dtype)
    wd_bf_ref[...] = wd_ref[...].astype(BF16)


def _swiglu(a, wg, wu, wd, tm=2048, tn=256):
    m, k = a.shape
    n = wg.shape[1]
    tm = min(tm, m)
    assert n % tn == 0
    n_i = m // tm
    steps = (n // tn) * n_i
    slab = wd.shape[0] // steps
    assert slab * steps == wd.shape[0] and slab % (2 * SUBLANES) == 0
    vmem = (2 * tm * k * 2 + 2 * 2 * k * tn * 4 + 2 * tm * tn * 2 + 2 * k * tn * 2
            + 2 * slab * wd.shape[1] * (4 + 2))
    assert vmem + V7X_VMEM_SLACK_BYTES <= V7X_VMEM_BYTES
    return pl.pallas_call(
        _swiglu_body,
        out_shape=(jax.ShapeDtypeStruct((m, n), BF16), jax.ShapeDtypeStruct(wd.shape, BF16)),
        grid=(n // tn, n_i),
        in_specs=[pl.BlockSpec((tm, k), lambda j, i: (i, 0)),
                  pl.BlockSpec((k, tn), lambda j, i: (0, j)),
                  pl.BlockSpec((k, tn), lambda j, i: (0, j)),
                  pl.BlockSpec((slab, wd.shape[1]), lambda j, i: (j * n_i + i, 0))],
        out_specs=(pl.BlockSpec((tm, tn), lambda j, i: (i, j)),
                   pl.BlockSpec((slab, wd.shape[1]), lambda j, i: (j * n_i + i, 0))),
        scratch_shapes=[pltpu.VMEM((k, tn), BF16), pltpu.VMEM((k, tn), BF16)],
        compiler_params=_params(("arbitrary", "arbitrary"), vmem=vmem + V7X_VMEM_SLACK_BYTES),
        name="ffn_swiglu",
    )(a, wg, wu, wd)


def kernel(x, norm_mix_pre, w_in, ssm_a_re, ssm_a_im, ssm_log_dt, ssm_b_re, ssm_b_im, ssm_c_re, ssm_c_im, ssm_d, ssm_w_glu, ssm_b_glu, cmp_pe_k, cmp_w1_k, cmp_w2_k, cmp_pe_v, cmp_w1_v, cmp_w2_v, w_proj_a, w_proj_b, w_out, norm_mix_post, norm_ffn_pre, w_ffn_gate, w_ffn_up, w_ffn_down, norm_ffn_post):
    bsz, seq, d = x.shape
    m = bsz * seq
    depth = w_in.shape[0]
    h = x.reshape(m, d)
    o_q = SSM_WIDTH
    o_kvc = o_q + ATT_WIDTH
    o_kv = o_kvc + 2 * KV_WIDTH
    o_gn = o_kv + 4 * KV_WIDTH
    o_ga = o_gn + 3 * N_HEADS
    assert o_gn % LANES == 0 and o_ga + 2 * D_MODEL == w_in.shape[2]
    bf = lambda a: a.astype(BF16)
    tn_in = 512
    nb_u, nb_q = SSM_WIDTH // tn_in, ATT_WIDTH // tn_in
    qk_scale = HEAD_DIM ** -0.5 * LOG2E
    for l in range(depth):
        wt = jnp.swapaxes(w_in[l], 0, 1)

        hn, gn = _rmsnorm_gates(h, norm_mix_pre[l], wt, o_gn // LANES)
        proj_f32 = _matmul_f32w(hn, wt, SSM_WIDTH + 2 * KV_WIDTH, F32, 1024, tn_in, trans_w=True,
                                col_block=lambda j: jnp.where(j < nb_u, j, j - nb_u + o_kvc // tn_in),
                                name="in_proj_f32")
        qkv = _matmul_f32w(hn, wt, ATT_WIDTH + 4 * KV_WIDTH, BF16, 2048, tn_in, trans_w=True,
                           col_block=lambda j: jnp.where(j < nb_q, j + o_q // tn_in, j - nb_q + o_kv // tn_in),
                           scale_fn=lambda j: jnp.where(j < nb_q, qk_scale, 1.0),
                           name="in_proj_bf16")

        y = _s5_mixer(proj_f32.reshape(bsz, seq, -1), ssm_a_re[l], ssm_a_im[l], ssm_log_dt[l],
                      ssm_b_re[l], ssm_b_im[l], ssm_c_re[l], ssm_c_im[l], ssm_d[l])
        y_a = _glu(y.reshape(m, SSM_WIDTH), bf(ssm_w_glu[l]), ssm_b_glu[l])

        pe = jnp.stack([cmp_pe_k[l], cmp_pe_v[l]])
        w1 = bf(jnp.stack([cmp_w1_k[l], cmp_w1_v[l]]))
        w2 = bf(jnp.stack([cmp_w2_k[l], cmp_w2_v[l]]))
        kcv = _compress(proj_f32, SSM_WIDTH, pe, w1, w2, bsz, seq)
        o_cmp, bias = _cmp_attn(qkv, kcv, bsz, seq)
        o_sel, w_g = _sel_attn(qkv, bias, wt, o_ga, 2 * D_MODEL, bsz, seq)
        y_b = _win_attn_combine(qkv, o_cmp, o_sel, gn, bsz, seq)

        merged = _merge(hn, y_a, y_b, w_g, bf(w_proj_a[l]), bf(w_proj_b[l]))
        mix = _matmul_f32w(merged, w_out[l], D_MODEL, BF16, 2048, 512, name="out_proj")
        h, hn2 = _res_norm2(mix, h, norm_mix_post[l], norm_ffn_pre[l])

        act, w_down = _swiglu(hn2, w_ffn_gate[l], w_ffn_up[l], w_ffn_down[l])
        f = _matmul(act, w_down, BF16, 512, 512, name="ffn_down")
        h = _res_norm(f, h, norm_ffn_post[l])
    return h.reshape(bsz, seq, d)
```

```python
import functools
import math

import jax
import jax.numpy as jnp
from jax import lax
from jax.experimental import pallas as pl
from jax.experimental.pallas import tpu as pltpu

F32 = jnp.float32
BF16 = jnp.bfloat16

D_MODEL = 4096
SSM_WIDTH = 2048
SSM_GROUP = 16
SSM_GROUPS = 128
SSM_STATE = 64
N_HEADS = 16
HEAD_DIM = 128
N_KV = 4
HPG = 4
ATT_WIDTH = 2048
KV_WIDTH = 512
L_CMP = 32
STRIDE_CMP = 16
L_SEL = 64
N_SEL = 16
WINDOW = 512
RMS_EPS = 1e-6
NEG_INF = -1e30
FORCE_SCORE = 1e9
TAKEN_SCORE = -3e38
LOG2E = math.log2(math.e)

V7X_VMEM_BYTES = 64 * 1024 * 1024
V7X_VMEM_LIMIT_BYTES = 56 * 1024 * 1024
V7X_VMEM_SLACK_BYTES = 4 * 1024 * 1024
LANES = 128
SUBLANES = 8

S5_GT = 16
S5_NC = S5_GT * SSM_GROUP
S5_NP = S5_GT * SSM_STATE
S5_TILES = SSM_GROUPS // S5_GT
S5_SLABS = 2 * S5_NP // LANES

ATT_ROWS = 256
SEL_TQ, SEL_RB = 512, 64
SEL_GROUP = 4


def _params(sem, vmem=V7X_VMEM_LIMIT_BYTES):
    return pltpu.CompilerParams(dimension_semantics=sem, vmem_limit_bytes=vmem)


def _lane_tile(x, n):
    return jnp.concatenate([x] * n, axis=1)


def _rmsnorm_gates_body(x_ref, g_ref, w_ref, hn_ref, gn_ref, wbf_scr):
    @pl.when(pl.program_id(0) == 0)
    def _():
        wbf_scr[...] = w_ref[...].astype(BF16)

    x = x_ref[...]
    ms = jnp.mean(x * x, axis=-1, keepdims=True)
    hn = (x * lax.rsqrt(ms + RMS_EPS) * g_ref[...]).astype(BF16)
    hn_ref[...] = hn
    gn_ref[...] = lax.dot_general(hn, wbf_scr[...], (((1,), (1,)), ((), ())), preferred_element_type=F32)


def _rmsnorm_gates(x, gain, wt, feature_block, tm=256):
    m, d = x.shape
    tm = min(tm, m)
    return pl.pallas_call(
        _rmsnorm_gates_body,
        out_shape=(jax.ShapeDtypeStruct((m, d), BF16), jax.ShapeDtypeStruct((m, LANES), F32)),
        grid=(m // tm,),
        in_specs=[pl.BlockSpec((tm, d), lambda i: (i, 0)),
                  pl.BlockSpec((1, d), lambda i: (0, 0)),
                  pl.BlockSpec((LANES, d), lambda i: (feature_block, 0))],
        out_specs=(pl.BlockSpec((tm, d), lambda i: (i, 0)), pl.BlockSpec((tm, LANES), lambda i: (i, 0))),
        scratch_shapes=[pltpu.VMEM((LANES, d), BF16)],
        compiler_params=_params(("arbitrary",)),
        name="rmsnorm_gates",
    )(x, gain.reshape(1, d), wt)


def _mm_body(a_ref, w_ref, o_ref):
    o_ref[...] = jnp.dot(a_ref[...], w_ref[...], preferred_element_type=F32).astype(o_ref.dtype)


def _matmul(a, w, out_dtype, tm, tn, name="matmul"):
    m, k = a.shape
    n = w.shape[1]
    tm, tn = min(tm, m), min(tn, n)
    assert m % tm == 0 and n % tn == 0
    return pl.pallas_call(
        _mm_body,
        out_shape=jax.ShapeDtypeStruct((m, n), out_dtype),
        grid=(m // tm, n // tn),
        in_specs=[pl.BlockSpec((tm, k), lambda i, j: (i, 0)), pl.BlockSpec((k, tn), lambda i, j: (0, j))],
        out_specs=pl.BlockSpec((tm, tn), lambda i, j: (i, j)),
        compiler_params=_params(("parallel", "arbitrary")),
        name=name,
    )(a, w)


_NT = (((1,), (1,)), ((), ()))


def _mm_f32w_body(a_ref, w_ref, o_ref, wbf_scr, *, scale_fn, trans_w):
    @pl.when(pl.program_id(1) == 0)
    def _():
        wbf_scr[...] = w_ref[...].astype(BF16)

    if trans_w:
        acc = lax.dot_general(a_ref[...], wbf_scr[...], _NT, preferred_element_type=F32)
    else:
        acc = jnp.dot(a_ref[...], wbf_scr[...], preferred_element_type=F32)
    if scale_fn is not None:
        acc = acc * scale_fn(pl.program_id(0))
    o_ref[...] = acc.astype(o_ref.dtype)


def _matmul_f32w(a, w, n_out, out_dtype, tm, tn, col_block=None, scale_fn=None, trans_w=False,
                 name="matmul_f32w"):
    m, k = a.shape
    tm = min(tm, m)
    assert m % tm == 0 and n_out % tn == 0
    col_block = col_block or (lambda j: j)
    if trans_w:
        w_spec = pl.BlockSpec((tn, k), lambda j, i: (col_block(j), 0))
        w_scratch = pltpu.VMEM((tn, k), BF16)
    else:
        w_spec = pl.BlockSpec((k, tn), lambda j, i: (0, col_block(j)))
        w_scratch = pltpu.VMEM((k, tn), BF16)
    vmem = 2 * tm * k * 2 + 2 * k * tn * 4 + 2 * tm * tn * jnp.dtype(out_dtype).itemsize + k * tn * 2
    assert vmem + V7X_VMEM_SLACK_BYTES <= V7X_VMEM_BYTES
    return pl.pallas_call(
        functools.partial(_mm_f32w_body, scale_fn=scale_fn, trans_w=trans_w),
        out_shape=jax.ShapeDtypeStruct((m, n_out), out_dtype),
        grid=(n_out // tn, m // tm),
        in_specs=[pl.BlockSpec((tm, k), lambda j, i: (i, 0)), w_spec],
        out_specs=pl.BlockSpec((tm, tn), lambda j, i: (i, j)),
        scratch_shapes=[w_scratch],
        compiler_params=_params(("arbitrary", "arbitrary"), vmem=vmem + V7X_VMEM_SLACK_BYTES),
        name=name,
    )(a, w)


def _s5_body(u_ref, are_ref, aim_ref, ldt_ref, bre_ref, bim_ref, cre_ref, cim_ref, d_ref, y_ref,
             abar_scr, bbar_scr, cmat_scr, h_scr, st_scr, *, tc, bsz, tps):
    ti = pl.program_id(1)
    half = S5_SLABS // 2

    @pl.when(ti == 0)
    def _():
        row_tile = lax.broadcasted_iota(jnp.int32, (SUBLANES, S5_NP), 0) // bsz
        same_group = (lax.broadcasted_iota(jnp.int32, (S5_NC, S5_NP), 0) // SSM_GROUP
                      == lax.broadcasted_iota(jnp.int32, (S5_NC, S5_NP), 1) // SSM_STATE)
        bdiag = lambda x: jnp.where(same_group, jnp.concatenate([x] * S5_GT, axis=0), 0.0)
        abr8 = jnp.zeros((SUBLANES, S5_NP), F32)
        abi8 = jnp.zeros((SUBLANES, S5_NP), F32)
        for k in range(tps):
            cols = slice(k * S5_NP, (k + 1) * S5_NP)
            ar, ai = are_ref[:, cols], aim_ref[:, cols]
            dt = jnp.exp(ldt_ref[:, cols])
            decay = jnp.exp(dt * ar)
            abr, abi = decay * jnp.cos(dt * ai), decay * jnp.sin(dt * ai)
            den = ar * ar + ai * ai
            zr = ((abr - 1.0) * ar + abi * ai) / den
            zi = (abi * ar - (abr - 1.0) * ai) / den
            abr8 = jnp.where(row_tile == k, abr, abr8)
            abi8 = jnp.where(row_tile == k, abi, abi8)
            br, bi = bdiag(bre_ref[k]), bdiag(bim_ref[k])
            bbar_scr[k, :, 0:S5_NP] = (zr * br - zi * bi).astype(BF16)
            bbar_scr[k, :, S5_NP:2 * S5_NP] = (zr * bi + zi * br).astype(BF16)
            cmat_scr[k, :, 0:S5_NP] = bdiag(cre_ref[k]).astype(BF16)
            cmat_scr[k, :, S5_NP:2 * S5_NP] = (-bdiag(cim_ref[k])).astype(BF16)
        abar_scr[0:SUBLANES, :] = abr8
        abar_scr[SUBLANES:2 * SUBLANES, :] = abi8
        st_scr[...] = jnp.zeros_like(st_scr)

    for k in range(tps):
        cols = slice(k * S5_NC, (k + 1) * S5_NC)
        ub = jnp.concatenate([u_ref[b, :, cols] for b in range(bsz)], axis=0).astype(BF16)
        for s2 in range(S5_SLABS // 2):
            bu = jnp.dot(ub, bbar_scr[k, :, s2 * 2 * LANES:(s2 + 1) * 2 * LANES], preferred_element_type=F32)
            for b in range(bsz):
                for e in range(2):
                    h_scr[2 * s2 + e, pl.ds(k * bsz + b, tc, stride=SUBLANES), :] = bu[b * tc:(b + 1) * tc,
                                                                                      e * LANES:(e + 1) * LANES]

    ar = abar_scr[0:SUBLANES, :]
    ai = abar_scr[SUBLANES:2 * SUBLANES, :]

    def step(t, carry):
        hr, hi = carry
        r0 = pl.multiple_of(t * SUBLANES, SUBLANES)
        bur = jnp.concatenate([h_scr[s, pl.ds(r0, SUBLANES), :] for s in range(half)], axis=1)
        bui = jnp.concatenate([h_scr[half + s, pl.ds(r0, SUBLANES), :] for s in range(half)], axis=1)
        nhr = ar * hr - ai * hi + bur
        nhi = ar * hi + ai * hr + bui
        for s in range(half):
            h_scr[s, pl.ds(r0, SUBLANES), :] = nhr[:, s * LANES:(s + 1) * LANES]
            h_scr[half + s, pl.ds(r0, SUBLANES), :] = nhi[:, s * LANES:(s + 1) * LANES]
        return nhr, nhi

    hr, hi = lax.fori_loop(0, tc, step, (st_scr[0:SUBLANES, :], st_scr[SUBLANES:2 * SUBLANES, :]), unroll=8)
    st_scr[0:SUBLANES, :] = hr
    st_scr[SUBLANES:2 * SUBLANES, :] = hi

    for k in range(tps):
        cols = slice(k * S5_NC, (k + 1) * S5_NC)

        hb = jnp.concatenate(
            [jnp.concatenate([h_scr[s, pl.ds(k * bsz + b, tc, stride=SUBLANES), :] for s in range(S5_SLABS)], axis=1)
             for b in range(bsz)], axis=0).astype(BF16)
        ch = lax.dot_general(hb, cmat_scr[k], _NT, preferred_element_type=F32)
        for b in range(bsz):
            y = ch[b * tc:(b + 1) * tc, :] + d_ref[:, cols] * u_ref[b, :, cols]
            y_ref[b, :, cols] = jax.nn.gelu(y)


def _s5_mixer(u3, a_re, a_im, log_dt, b_re, b_im, c_re, c_im, d_skip, tc=256):
    bsz, seq, _ = u3.shape
    assert SUBLANES % bsz == 0
    tps = min(SUBLANES // bsz, S5_TILES)
    assert tps * bsz == SUBLANES
    tc = min(tc, seq)

    def per_tile_b(b):
        return (b.reshape(S5_TILES, S5_GT, SSM_STATE, SSM_GROUP).transpose(0, 3, 1, 2)
                .reshape(S5_TILES, SSM_GROUP, S5_NP))

    def per_tile_c(c):
        return (c.reshape(S5_TILES, S5_GT, SSM_GROUP, SSM_STATE).transpose(0, 2, 1, 3)
                .reshape(S5_TILES, SSM_GROUP, S5_NP))

    flat = lambda a: a.reshape(1, SSM_GROUPS * SSM_STATE)
    ldt = jnp.repeat(log_dt, SSM_STATE).reshape(1, SSM_GROUPS * SSM_STATE)
    vec_spec = pl.BlockSpec((1, tps * S5_NP), lambda g, t: (0, g))
    bc_spec = pl.BlockSpec((tps, SSM_GROUP, S5_NP), lambda g, t: (g, 0, 0))
    return pl.pallas_call(
        functools.partial(_s5_body, tc=tc, bsz=bsz, tps=tps),
        out_shape=jax.ShapeDtypeStruct((bsz, seq, SSM_WIDTH), F32),
        grid=(S5_TILES // tps, seq // tc),
        in_specs=[
            pl.BlockSpec((bsz, tc, tps * S5_NC), lambda g, t: (0, t, g)),
            vec_spec, vec_spec, vec_spec, bc_spec, bc_spec, bc_spec, bc_spec,
            pl.BlockSpec((1, tps * S5_NC), lambda g, t: (0, g)),
        ],
        out_specs=pl.BlockSpec((bsz, tc, tps * S5_NC), lambda g, t: (0, t, g)),
        scratch_shapes=[
            pltpu.VMEM((2 * SUBLANES, S5_NP), F32),
            pltpu.VMEM((tps, S5_NC, 2 * S5_NP), BF16),
            pltpu.VMEM((tps, S5_NC, 2 * S5_NP), BF16),
            pltpu.VMEM((S5_SLABS, tc * SUBLANES, LANES), F32),
            pltpu.VMEM((2 * SUBLANES, S5_NP), F32),
        ],
        compiler_params=_params(("arbitrary", "arbitrary")),
        name="s5_scan",
    )(u3, flat(a_re), flat(a_im), ldt, per_tile_b(b_re), per_tile_b(b_im), per_tile_c(c_re), per_tile_c(c_im),
      d_skip.reshape(1, SSM_WIDTH))


def _glu_body(y_ref, w_ref, b_ref, o_ref, ybf_scr, *, tn):
    j = pl.program_id(1)

    @pl.when(j == 0)
    def _():
        ybf_scr[...] = y_ref[...].astype(BF16)

    z = jnp.dot(ybf_scr[...], w_ref[...], preferred_element_type=F32) + b_ref[...]
    yt = y_ref[:, pl.ds(pl.multiple_of(j * tn, tn), tn)]
    o_ref[...] = (yt * jax.nn.sigmoid(z)).astype(o_ref.dtype)


def _glu(y, w_bf, bias, tm=1024, tn=1024):
    m, k = y.shape
    tm = min(tm, m)
    return pl.pallas_call(
        functools.partial(_glu_body, tn=tn),
        out_shape=jax.ShapeDtypeStruct((m, k), BF16),
        grid=(m // tm, k // tn),
        in_specs=[pl.BlockSpec((tm, k), lambda i, j: (i, 0)),
                  pl.BlockSpec((k, tn), lambda i, j: (0, j)),
                  pl.BlockSpec((1, tn), lambda i, j: (0, j))],
        out_specs=pl.BlockSpec((tm, tn), lambda i, j: (i, j)),
        scratch_shapes=[pltpu.VMEM((tm, k), BF16)],
        compiler_params=_params(("parallel", "arbitrary")),
        name="s5_glu",
    )(y, w_bf, bias.reshape(1, k))


def _compress_body(x_ref, pe_ref, w1_ref, w2_ref, o_ref, *, ncmp):
    half = L_CMP // 2
    acc_a = jnp.zeros((ncmp, HEAD_DIM), F32)
    acc_b = jnp.zeros((ncmp, HEAD_DIM), F32)
    for r in range(half):
        xr = x_ref[pl.ds(r, ncmp, stride=STRIDE_CMP), :]
        xa = (xr + pe_ref[r:r + 1, :]).astype(BF16)
        xb = (xr + pe_ref[half + r:half + r + 1, :]).astype(BF16)
        acc_a += jnp.dot(xa, w1_ref[r * HEAD_DIM:(r + 1) * HEAD_DIM, :], preferred_element_type=F32)
        acc_b += jnp.dot(xb, w1_ref[(half + r) * HEAD_DIM:(half + r + 1) * HEAD_DIM, :],
                         preferred_element_type=F32)
    pre = acc_a + jnp.concatenate([acc_b[1:], acc_b[:1]], axis=0)
    o_ref[...] = jnp.dot(jax.nn.gelu(pre).astype(BF16), w2_ref[...], preferred_element_type=F32).astype(o_ref.dtype)


def _compress(proj_f32, col0, pe, w1_bf, w2_bf, bsz, seq):
    ncmp = seq // STRIDE_CMP
    cb = col0 // HEAD_DIM
    return pl.pallas_call(
        functools.partial(_compress_body, ncmp=ncmp),
        out_shape=jax.ShapeDtypeStruct((bsz, 2, N_KV, ncmp, HEAD_DIM), BF16),
        grid=(bsz, 2, N_KV),
        in_specs=[pl.BlockSpec((seq, HEAD_DIM), lambda b, w, g: (b, cb + w * N_KV + g)),
                  pl.BlockSpec((None, L_CMP, HEAD_DIM), lambda b, w, g: (w, 0, 0)),
                  pl.BlockSpec((None, L_CMP * HEAD_DIM, HEAD_DIM), lambda b, w, g: (w, 0, 0)),
                  pl.BlockSpec((None, HEAD_DIM, HEAD_DIM), lambda b, w, g: (w, 0, 0))],
        out_specs=pl.BlockSpec((None, None, None, ncmp, HEAD_DIM), lambda b, w, g: (b, w, g, 0, 0)),
        compiler_params=_params(("parallel", "parallel", "parallel")),
        name="nsa_compress",
    )(proj_f32, pe, w1_bf, w2_bf)


def _cmp_attn_body(q_ref, kc_ref, vc_ref, o_ref, bias_ref, *, tq, ncmp, nblk, ntop):
    qi = pl.program_id(2)
    t0 = qi * tq
    tpos = t0 + lax.broadcasted_iota(jnp.int32, (tq, ncmp), 0)
    blk_end = lax.broadcasted_iota(jnp.int32, (tq, ncmp), 1) * STRIDE_CMP + (L_CMP - 1)
    ok = blk_end <= tpos
    okf = ok.astype(F32)
    kc = kc_ref[...]
    vc = vc_ref[...]
    psum = jnp.zeros((tq, ncmp), F32)
    for h in range(HPG):
        qh = q_ref[:, h * HEAD_DIM:(h + 1) * HEAD_DIM]
        s = lax.dot_general(qh, kc, (((1,), (1,)), ((), ())), preferred_element_type=F32)
        s = jnp.where(ok, s, NEG_INF)
        e = jnp.exp2(s - jnp.max(s, axis=-1, keepdims=True))
        p = e / jnp.sum(e, axis=-1, keepdims=True) * okf
        o_ref[:, h * HEAD_DIM:(h + 1) * HEAD_DIM] = jnp.dot(p.astype(BF16), vc, preferred_element_type=F32)
        psum = psum + p

    jj = lax.broadcasted_iota(jnp.int32, (nblk, ncmp), 0)
    nn = lax.broadcasted_iota(jnp.int32, (nblk, ncmp), 1)
    ov = ((nn * STRIDE_CMP < (jj + 1) * L_SEL) & (nn * STRIDE_CMP + L_CMP > jj * L_SEL)).astype(BF16)
    p_hi = psum.astype(BF16)
    p_lo = (psum - p_hi.astype(F32)).astype(BF16)
    nt = (((1,), (1,)), ((), ()))
    imp = (lax.dot_general(ov, p_hi, nt, preferred_element_type=F32)
           + lax.dot_general(ov, p_lo, nt, preferred_element_type=F32))

    jb = lax.broadcasted_iota(jnp.int32, (nblk, tq), 0)
    tt = t0 + lax.broadcasted_iota(jnp.int32, (nblk, tq), 1)
    cur = tt // L_SEL
    allowed = jb * L_SEL <= tt
    forced = (jb == 0) | (jb == cur) | (jb == cur - 1)
    score = jnp.where(forced, FORCE_SCORE, jnp.where(allowed, imp, NEG_INF))
    taken = jnp.zeros((nblk, tq), F32)
    for _ in range(ntop):
        best = jnp.max(score, axis=0, keepdims=True)
        first = jnp.min(jnp.where(score == best, jb, nblk), axis=0, keepdims=True)
        pick = jb == first
        taken = jnp.where(pick, 1.0, taken)
        score = jnp.where(pick, TAKEN_SCORE, score)
    bias_t = jnp.where(taken > 0.5, 0.0, NEG_INF)
    bias_t = jnp.concatenate([bias_t, jnp.zeros((LANES - nblk, tq), F32)], axis=0)
    bias_ref[...] = bias_t.T.astype(bias_ref.dtype)


def _cmp_attn(qkv, kcv, bsz, seq, tq=2048):
    tq = min(tq, seq)
    ncmp = seq // STRIDE_CMP
    nblk = seq // L_SEL
    ntop = min(N_SEL, nblk)
    nq = seq // tq
    return pl.pallas_call(
        functools.partial(_cmp_attn_body, tq=tq, ncmp=ncmp, nblk=nblk, ntop=ntop),
        out_shape=(jax.ShapeDtypeStruct((bsz * seq, ATT_WIDTH), F32),
                   jax.ShapeDtypeStruct((bsz, N_KV, seq, LANES), BF16)),
        grid=(bsz, N_KV, nq),
        in_specs=[pl.BlockSpec((tq, HPG * HEAD_DIM), lambda b, g, i: (b * nq + i, g)),
                  pl.BlockSpec((None, None, None, ncmp, HEAD_DIM), lambda b, g, i: (b, 0, g, 0, 0)),
                  pl.BlockSpec((None, None, None, ncmp, HEAD_DIM), lambda b, g, i: (b, 1, g, 0, 0))],
        out_specs=(pl.BlockSpec((tq, HPG * HEAD_DIM), lambda b, g, i: (b * nq + i, g)),
                   pl.BlockSpec((None, None, tq, LANES), lambda b, g, i: (b, g, i, 0))),
        compiler_params=_params(("parallel", "parallel", "parallel")),
        name="nsa_cmp_attn_topk",
    )(qkv, kcv, kcv)


def _sel_attn_body(q_ref, k_ref, v_ref, bias_ref, wg_ref, o_ref, wg_bf_ref, qa_scr, m_scr, acc_scr, s_scr, p_scr,
                   alpha_scr, *, tq, rb):
    qi = pl.program_id(2)
    tk = tq
    wg_bf_ref[...] = wg_ref[...].astype(BF16)
    m_scr[...] = jnp.full_like(m_scr, NEG_INF)
    acc_scr[...] = jnp.zeros_like(acc_scr)
    for h in range(HPG):
        qa_scr[h, :, 0:HEAD_DIM] = q_ref[:, h * HEAD_DIM:(h + 1) * HEAD_DIM]
        qa_scr[h, :, HEAD_DIM:2 * HEAD_DIM] = bias_ref[...]
    nt = (((1,), (1,)), ((), ()))

    def key_tile(k0, nkeys):
        key_blk = k0 // L_SEL + lax.broadcasted_iota(jnp.int32, (nkeys, LANES), 0) // L_SEL
        onehot = jnp.where(lax.broadcasted_iota(jnp.int32, (nkeys, LANES), 1) == key_blk, 1.0, 0.0).astype(BF16)
        return jnp.concatenate([k_ref[pl.ds(k0, nkeys), :], onehot], axis=1)

    def scores(h, k_aug, r0, r1, nk):
        s_scr[h, r0:r1, 0:nk] = lax.dot_general(qa_scr[h, r0:r1, :], k_aug[0:nk, :], nt,
                                                preferred_element_type=F32)

    def softmax(h, r0, r1, nk, d0):
        for r in range(r0 // rb, r1 // rb):
            rows = slice(r * rb, (r + 1) * rb)
            pieces = []
            if d0 > 0:
                pieces.append(s_scr[h, rows, 0:d0])
            if d0 < nk:
                col = lax.broadcasted_iota(jnp.int32, (rb, nk - d0), 1)
                row = lax.broadcasted_iota(jnp.int32, (rb, nk - d0), 0) + r * rb
                pieces.append(jnp.where(col <= row, s_scr[h, rows, d0:nk], NEG_INF))
            m_prev = m_scr[h, rows, :]
            m_next = m_prev
            for s in pieces:
                m_next = jnp.maximum(m_next, jnp.max(s, axis=-1, keepdims=True))
            c0 = 0
            for s in pieces:
                w = s.shape[1]
                p_scr[h, rows, c0:c0 + w] = jnp.exp2(s - _lane_tile(m_next, w // LANES)).astype(BF16)
                c0 += w
            alpha_scr[h, rows, :] = jnp.exp2(m_prev - m_next)
            m_scr[h, rows, :] = m_next

    def values(h, v_aug, r0, r1, nk):
        acc_scr[h, r0:r1, :] = (_lane_tile(alpha_scr[h, r0:r1, :], 2) * acc_scr[h, r0:r1, :]
                                + jnp.dot(p_scr[h, r0:r1, 0:nk], v_aug[0:nk, :], preferred_element_type=F32))

    def tile(kt, nkeys, diag_last):
        k0 = pl.multiple_of(kt * tk, tk)
        k_aug = key_tile(k0, nkeys)
        v_aug = jnp.concatenate([v_ref[pl.ds(k0, nkeys), :], jnp.ones((nkeys, LANES), BF16)], axis=1)
        if diag_last:
            d0 = nkeys - tk
            parts = [(0, tq // 2, nkeys - tk // 2), (tq // 2, tq, nkeys)]
        else:
            d0 = nkeys
            parts = [(0, tq, nkeys)]
        for h in range(HPG):
            for r0, r1, nk in parts:
                scores(h, k_aug, r0, r1, nk)
        for h in range(HPG):
            for r0, r1, nk in parts:
                softmax(h, r0, r1, nk, d0)
        for h in range(HPG):
            for r0, r1, nk in parts:
                values(h, v_aug, r0, r1, nk)

    def tile_group(j, carry):
        tile(SEL_GROUP * j, SEL_GROUP * tk, False)
        return carry

    n_groups = qi // SEL_GROUP
    lax.fori_loop(0, n_groups, tile_group, 0)
    for run in range(1, SEL_GROUP + 1):
        @pl.when(qi % SEL_GROUP == run - 1)
        def _(run=run):
            tile(n_groups * SEL_GROUP, run * tk, True)

    for h in range(HPG):
        o_ref[:, h * HEAD_DIM:(h + 1) * HEAD_DIM] = (acc_scr[h, :, 0:HEAD_DIM]
                                                     / acc_scr[h, :, HEAD_DIM:2 * HEAD_DIM])


def _sel_attn(qkv, bias, wt, gate_row0, gate_rows, bsz, seq, tq=SEL_TQ, rb=SEL_RB):
    tq = min(tq, seq)
    rb = min(rb, tq)
    nq = seq // tq
    kcol = ATT_WIDTH // HEAD_DIM
    steps = bsz * N_KV * nq
    slab = gate_rows // steps
    d = wt.shape[1]
    assert slab * steps == gate_rows and slab % (2 * SUBLANES) == 0 and gate_row0 % (2 * SUBLANES) == 0
    step = lambda b, g, i: (b * N_KV + g) * nq + i
    return pl.pallas_call(
        functools.partial(_sel_attn_body, tq=tq, rb=rb),
        out_shape=(jax.ShapeDtypeStruct((bsz * seq, ATT_WIDTH), F32),
                   jax.ShapeDtypeStruct((gate_rows, d), BF16)),
        grid=(bsz, N_KV, nq),
        in_specs=[pl.BlockSpec((tq, HPG * HEAD_DIM), lambda b, g, i: (b * nq + i, g)),
                  pl.BlockSpec((seq, HEAD_DIM), lambda b, g, i: (b, kcol + g)),
                  pl.BlockSpec((seq, HEAD_DIM), lambda b, g, i: (b, kcol + N_KV + g)),
                  pl.BlockSpec((None, None, tq, LANES), lambda b, g, i: (b, g, i, 0)),
                  pl.BlockSpec((pl.Element(slab), pl.Element(d)),
                               lambda b, g, i: (pl.multiple_of(gate_row0 + step(b, g, i) * slab, 2 * SUBLANES), 0))],
        out_specs=(pl.BlockSpec((tq, HPG * HEAD_DIM), lambda b, g, i: (b * nq + i, g)),
                   pl.BlockSpec((slab, d), lambda b, g, i: (step(b, g, i), 0))),
        scratch_shapes=[pltpu.VMEM((HPG, tq, 2 * HEAD_DIM), BF16),
                        pltpu.VMEM((HPG, tq, LANES), F32),
                        pltpu.VMEM((HPG, tq, 2 * HEAD_DIM), F32),
                        pltpu.VMEM((HPG, tq, SEL_GROUP * tq), F32),
                        pltpu.VMEM((HPG, tq, SEL_GROUP * tq), BF16),
                        pltpu.VMEM((HPG, tq, LANES), F32)],
        compiler_params=_params(("parallel", "parallel", "arbitrary")),
        name="nsa_sel_attn",
    )(qkv, qkv, qkv, bias, wt)


def _win_attn_body(q_ref, kc_ref, kp_ref, vc_ref, vp_ref, ocmp_ref, osel_ref, gate_ref, o_ref, *, tq):
    qi = pl.program_id(2)
    nt = (((1,), (1,)), ((), ()))
    ones = jnp.ones((tq, LANES), BF16)
    vc_aug = jnp.concatenate([vc_ref[...], ones], axis=1)
    vp_aug = jnp.concatenate([vp_ref[...], ones], axis=1)
    for r in range(tq // ATT_ROWS):
        rows = slice(r * ATT_ROWS, (r + 1) * ATT_ROWS)
        n_cur, p_lo = (r + 1) * ATT_ROWS, r * ATT_ROWS
        row_c = lax.broadcasted_iota(jnp.int32, (ATT_ROWS, n_cur), 0) + r * ATT_ROWS
        ok_cur = lax.broadcasted_iota(jnp.int32, (ATT_ROWS, n_cur), 1) <= row_c
        row_p = lax.broadcasted_iota(jnp.int32, (ATT_ROWS, tq - p_lo), 0) + r * ATT_ROWS
        ok_prev = (lax.broadcasted_iota(jnp.int32, (ATT_ROWS, tq - p_lo), 1) + p_lo > row_p) & (qi > 0)
        shift = (LANES - HPG * pl.program_id(1)) % LANES
        gates = jax.nn.sigmoid(pltpu.roll(gate_ref[rows, :], shift, 1))
        for h in range(HPG):
            sl = slice(h * HEAD_DIM, (h + 1) * HEAD_DIM)
            qh = q_ref[rows, sl]
            s1 = jnp.where(ok_cur, lax.dot_general(qh, kc_ref[0:n_cur, :], nt, preferred_element_type=F32), NEG_INF)
            s0 = jnp.where(ok_prev, lax.dot_general(qh, kp_ref[p_lo:tq, :], nt, preferred_element_type=F32), NEG_INF)
            m = jnp.maximum(jnp.max(s1, axis=-1, keepdims=True), jnp.max(s0, axis=-1, keepdims=True))
            p1 = jnp.exp2(s1 - m).astype(BF16)
            p0 = jnp.exp2(s0 - m).astype(BF16)
            acc = (jnp.dot(p1, vc_aug[0:n_cur, :], preferred_element_type=F32)
                   + jnp.dot(p0, vp_aug[p_lo:tq, :], preferred_element_type=F32))
            ow = acc[:, 0:HEAD_DIM] / acc[:, HEAD_DIM:2 * HEAD_DIM]
            g_cmp = gates[:, h:h + 1]
            g_sel = gates[:, N_HEADS + h:N_HEADS + h + 1]
            g_win = gates[:, 2 * N_HEADS + h:2 * N_HEADS + h + 1]
            o_ref[rows, sl] = (g_cmp * ocmp_ref[rows, sl] + g_sel * osel_ref[rows, sl]
                               + g_win * ow).astype(o_ref.dtype)


def _win_attn_combine(qkv, o_cmp, o_sel, gates, bsz, seq):
    tq = WINDOW
    assert seq % tq == 0
    nq = seq // tq
    kcol = ATT_WIDTH // HEAD_DIM + 2 * N_KV
    qspec = pl.BlockSpec((tq, HPG * HEAD_DIM), lambda b, g, i: (b * nq + i, g))
    cur = lambda c: pl.BlockSpec((tq, HEAD_DIM), lambda b, g, i: (b * nq + i, kcol + c * N_KV + g))
    prev = lambda c: pl.BlockSpec((tq, HEAD_DIM),
                                  lambda b, g, i: (b * nq + jnp.maximum(i - 1, 0), kcol + c * N_KV + g))
    return pl.pallas_call(
        functools.partial(_win_attn_body, tq=tq),
        out_shape=jax.ShapeDtypeStruct((bsz * seq, ATT_WIDTH), BF16),
        grid=(bsz, N_KV, nq),
        in_specs=[qspec, cur(0), prev(0), cur(1), prev(1), qspec, qspec,
                  pl.BlockSpec((tq, LANES), lambda b, g, i: (b * nq + i, 0))],
        out_specs=qspec,
        compiler_params=_params(("parallel", "parallel", "parallel")),
        name="nsa_win_attn_gate",
    )(qkv, qkv, qkv, qkv, qkv, o_cmp, o_sel, gates)


def _merge_body(hn_ref, ya_ref, yb_ref, wga_ref, wgb_ref, wa_ref, wb_ref, o_ref):
    hn = hn_ref[...]
    ga = lax.dot_general(hn, wga_ref[...], _NT, preferred_element_type=F32)
    gb = lax.dot_general(hn, wgb_ref[...], _NT, preferred_element_type=F32)
    pa = jnp.dot(ya_ref[...], wa_ref[...], preferred_element_type=F32)
    pb = jnp.dot(yb_ref[...], wb_ref[...], preferred_element_type=F32)
    o_ref[...] = (jax.nn.sigmoid(ga) * pa + jax.nn.sigmoid(gb) * pb).astype(o_ref.dtype)


def _merge(hn, ya, yb, wg, wa, wb, tm=512, tn=512):
    m, d = hn.shape
    ka = ya.shape[1]
    tm = min(tm, m)
    row = lambda k: pl.BlockSpec((tm, k), lambda i, j: (i, 0))
    col = lambda k: pl.BlockSpec((k, tn), lambda i, j: (0, j))
    gate_a = pl.BlockSpec((tn, d), lambda i, j: (j, 0))
    gate_b = pl.BlockSpec((tn, d), lambda i, j: (d // tn + j, 0))
    return pl.pallas_call(
        _merge_body,
        out_shape=jax.ShapeDtypeStruct((m, d), BF16),
        grid=(m // tm, d // tn),
        in_specs=[row(d), row(ka), row(ka), gate_a, gate_b, col(ka), col(ka)],
        out_specs=pl.BlockSpec((tm, tn), lambda i, j: (i, j)),
        compiler_params=_params(("parallel", "arbitrary")),
        name="mixer_merge",
    )(hn, ya, yb, wg, wg, wa, wb)


def _rms(x, g):
    return x * lax.rsqrt(jnp.mean(x * x, axis=-1, keepdims=True) + RMS_EPS) * g


def _res_norm2_body(raw_ref, x_ref, gpost_ref, gpre_ref, h_ref, hn_ref):
    h = x_ref[...] + _rms(raw_ref[...].astype(F32), gpost_ref[...])
    h_ref[...] = h
    hn_ref[...] = _rms(h, gpre_ref[...]).astype(hn_ref.dtype)


def _res_norm2(raw, x, g_post, g_pre, tm=256):
    m, d = x.shape
    tm = min(tm, m)
    rows = pl.BlockSpec((tm, d), lambda i: (i, 0))
    vec = pl.BlockSpec((1, d), lambda i: (0, 0))
    return pl.pallas_call(
        _res_norm2_body,
        out_shape=(jax.ShapeDtypeStruct((m, d), F32), jax.ShapeDtypeStruct((m, d), BF16)),
        grid=(m // tm,),
        in_specs=[rows, rows, vec, vec],
        out_specs=(rows, rows),
        compiler_params=_params(("parallel",)),
        name="residual_norm_prenorm",
    )(raw, x, g_post.reshape(1, d), g_pre.reshape(1, d))


def _res_norm_body(raw_ref, x_ref, g_ref, o_ref):
    o_ref[...] = x_ref[...] + _rms(raw_ref[...].astype(F32), g_ref[...])


def _res_norm(raw, x, g, tm=256):
    m, d = x.shape
    tm = min(tm, m)
    rows = pl.BlockSpec((tm, d), lambda i: (i, 0))
    return pl.pallas_call(
        _res_norm_body,
        out_shape=jax.ShapeDtypeStruct((m, d), F32),
        grid=(m // tm,),
        in_specs=[rows, rows, pl.BlockSpec((1, d), lambda i: (0, 0))],
        out_specs=rows,
        compiler_params=_params(("parallel",)),
        name="residual_norm",
    )(raw, x, g.reshape(1, d))


def _swiglu_body(a_ref, wg_ref, wu_ref, wd_ref, o_ref, wd_bf_ref, wg_scr, wu_scr):
    @pl.when(pl.program_id(1) == 0)
    def _():
        wg_scr[...] = wg_ref[...].astype(BF16)
        wu_scr[...] = wu_ref[...].astype(BF16)

    a = a_ref[...]
    g = jnp.dot(a, wg_scr[...], preferred_element_type=F32)
    u = jnp.dot(a, wu_scr[...], preferred_element_type=F32)
    o_ref[...] = (g * jax.nn.sigmoid(g) * u).astype(o_ref.dtype)
    wd_bf_ref[...] = wd_ref[...].astype(BF16)


def _swiglu(a, wg, wu, wd, tm=2048, tn=256):
    m, k = a.shape
    n = wg.shape[1]
    tm = min(tm, m)
    assert n % tn == 0
    n_i = m // tm
    steps = (n // tn) * n_i
    slab = wd.shape[0] // steps
    assert slab * steps == wd.shape[0] and slab % (2 * SUBLANES) == 0
    vmem = (2 * tm * k * 2 + 2 * 2 * k * tn * 4 + 2 * tm * tn * 2 + 2 * k * tn * 2
            + 2 * slab * wd.shape[1] * (4 + 2))
    assert vmem + V7X_VMEM_SLACK_BYTES <= V7X_VMEM_BYTES
    return pl.pallas_call(
        _swiglu_body,
        out_shape=(jax.ShapeDtypeStruct((m, n), BF16), jax.ShapeDtypeStruct(wd.shape, BF16)),
        grid=(n // tn, n_i),
        in_specs=[pl.BlockSpec((tm, k), lambda j, i: (i, 0)),
                  pl.BlockSpec((k, tn), lambda j, i: (0, j)),
                  pl.BlockSpec((k, tn), lambda j, i: (0, j)),
                  pl.BlockSpec((slab, wd.shape[1]), lambda j, i: (j * n_i + i, 0))],
        out_specs=(pl.BlockSpec((tm, tn), lambda j, i: (i, j)),
                   pl.BlockSpec((slab, wd.shape[1]), lambda j, i: (j * n_i + i, 0))),
        scratch_shapes=[pltpu.VMEM((k, tn), BF16), pltpu.VMEM((k, tn), BF16)],
        compiler_params=_params(("arbitrary", "arbitrary"), vmem=vmem + V7X_VMEM_SLACK_BYTES),
        name="ffn_swiglu",
    )(a, wg, wu, wd)


def kernel(x, norm_mix_pre, w_in, ssm_a_re, ssm_a_im, ssm_log_dt, ssm_b_re, ssm_b_im, ssm_c_re, ssm_c_im, ssm_d, ssm_w_glu, ssm_b_glu, cmp_pe_k, cmp_w1_k, cmp_w2_k, cmp_pe_v, cmp_w1_v, cmp_w2_v, w_proj_a, w_proj_b, w_out, norm_mix_post, norm_ffn_pre, w_ffn_gate, w_ffn_up, w_ffn_down, norm_ffn_post):
    bsz, seq, d = x.shape
    m = bsz * seq
    depth = w_in.shape[0]
    h = x.reshape(m, d)
    o_q = SSM_WIDTH
    o_kvc = o_q + ATT_WIDTH
    o_kv = o_kvc + 2 * KV_WIDTH
    o_gn = o_kv + 4 * KV_WIDTH
    o_ga = o_gn + 3 * N_HEADS
    assert o_gn % LANES == 0 and o_ga + 2 * D_MODEL == w_in.shape[2]
    bf = lambda a: a.astype(BF16)
    tn_in = 512
    nb_u, nb_q = SSM_WIDTH // tn_in, ATT_WIDTH // tn_in
    qk_scale = HEAD_DIM ** -0.5 * LOG2E
    for l in range(depth):
        wt = jnp.swapaxes(w_in[l], 0, 1)

        hn, gn = _rmsnorm_gates(h, norm_mix_pre[l], wt, o_gn // LANES)
        proj_f32 = _matmul_f32w(hn, wt, SSM_WIDTH + 2 * KV_WIDTH, F32, 1024, tn_in, trans_w=True,
                                col_block=lambda j: jnp.where(j < nb_u, j, j - nb_u + o_kvc // tn_in),
                                name="in_proj_f32")
        qkv = _matmul_f32w(hn, wt, ATT_WIDTH + 4 * KV_WIDTH, BF16, 2048, tn_in, trans_w=True,
                           col_block=lambda j: jnp.where(j < nb_q, j + o_q // tn_in, j - nb_q + o_kv // tn_in),
                           scale_fn=lambda j: jnp.where(j < nb_q, qk_scale, 1.0),
                           name="in_proj_bf16")

        y = _s5_mixer(proj_f32.reshape(bsz, seq, -1), ssm_a_re[l], ssm_a_im[l], ssm_log_dt[l],
                      ssm_b_re[l], ssm_b_im[l], ssm_c_re[l], ssm_c_im[l], ssm_d[l])
        y_a = _glu(y.reshape(m, SSM_WIDTH), bf(ssm_w_glu[l]), ssm_b_glu[l])

        pe = jnp.stack([cmp_pe_k[l], cmp_pe_v[l]])
        w1 = bf(jnp.stack([cmp_w1_k[l], cmp_w1_v[l]]))
        w2 = bf(jnp.stack([cmp_w2_k[l], cmp_w2_v[l]]))
        kcv = _compress(proj_f32, SSM_WIDTH, pe, w1, w2, bsz, seq)
        o_cmp, bias = _cmp_attn(qkv, kcv, bsz, seq)
        o_sel, w_g = _sel_attn(qkv, bias, wt, o_ga, 2 * D_MODEL, bsz, seq)
        y_b = _win_attn_combine(qkv, o_cmp, o_sel, gn, bsz, seq)

        merged = _merge(hn, y_a, y_b, w_g, bf(w_proj_a[l]), bf(w_proj_b[l]))
        mix = _matmul_f32w(merged, w_out[l], D_MODEL, BF16, 2048, 512, name="out_proj")
        h, hn2 = _res_norm2(mix, h, norm_mix_post[l], norm_ffn_pre[l])

        act, w_down = _swiglu(hn2, w_ffn_gate[l], w_ffn_up[l], w_ffn_down[l])
        f = _matmul(act, w_down, BF16, 512, 512, name="ffn_down")
        h = _res_norm(f, h, norm_ffn_post[l])
    return h.reshape(bsz, seq, d)
```
